```python
import math
import jax, jax.numpy as jnp
from jax import lax
import numpy as np

D_MODEL = 2048
BATCH = 16
SEQ = 2048
DEPTH = 1
DEC_BATCH = 4
DEC_SEQ = 4096
PAST_LEN = 128

PLE_DIM = 256
NORM_EPS = 1e-6
HY_CH = D_MODEL // 2
SHORT_K = 3
FILT_BANDS = 16
FILT_EMB = 1 + 2 * FILT_BANDS
FILT_ORDER = 64
FILT_TARGET = 1e-2
FILT_FAST_PCT = 0.3
FILT_SLOW_PCT = 1.5
FILT_MAX_DECAY = math.log(FILT_TARGET) / FILT_FAST_PCT
FILT_MIN_DECAY = math.log(FILT_TARGET) / FILT_SLOW_PCT
ATT_WIDTH = D_MODEL // 2
N_HEADS = 8
HEAD_DIM = ATT_WIDTH // (2 * N_HEADS)
ROPE_DIM = HEAD_DIM // 4
ROPE_THETA = 500000.0
Q_BLOCK = 128
D_FF = ((8 * D_MODEL + 3 * 256 - 1) // (3 * 256)) * 256
HY_COLS = 3 * HY_CH
QK_COLS = ATT_WIDTH
V_COLS = ATT_WIDTH
GATE_COLS = 2 * D_MODEL
IN_COLS = HY_COLS + 2 * QK_COLS + V_COLS + GATE_COLS
SPLITS = (HY_COLS, HY_COLS + QK_COLS, HY_COLS + 2 * QK_COLS, HY_COLS + 2 * QK_COLS + V_COLS)

kernel_name = 'hyena_diffattn_gated_encoder'


def _rmsnorm(x, g):
    xf = x.astype(jnp.float32)
    y = xf * lax.rsqrt(jnp.mean(xf * xf, axis=-1, keepdims=True) + NORM_EPS)
    return (y * g.astype(jnp.float32)).astype(x.dtype)


def _rope(x):
    L = x.shape[1]
    inv = ROPE_THETA ** (-jnp.arange(0, ROPE_DIM, 2, dtype=jnp.float32) / ROPE_DIM)
    ang = jnp.arange(L, dtype=jnp.float32)[:, None] * inv[None, :]
    ang = jnp.concatenate([ang, ang], axis=-1)[None, :, None, None, :]
    xr = x[..., :ROPE_DIM].astype(jnp.float32)
    x1, x2 = jnp.split(xr, 2, axis=-1)
    rot = jnp.concatenate([-x2, x1], axis=-1)
    xr = xr * jnp.cos(ang) + rot * jnp.sin(ang)
    return jnp.concatenate([xr.astype(x.dtype), x[..., ROPE_DIM:]], axis=-1)


def _hyena_filters(L, w1, b1, w2, b2, freq, w3):
    f32 = jnp.float32
    t = jnp.linspace(0.0, 1.0, L, dtype=f32)[:, None]
    wpos = 2.0 * math.pi * jnp.arange(L, dtype=f32) / L
    bands = jnp.linspace(1e-4, FILT_BANDS - 1, FILT_BANDS, dtype=f32)
    ang = wpos[:, None] * bands[None, :]
    emb = jnp.concatenate([t, jnp.cos(ang), -jnp.sin(ang)], axis=-1)
    fr = freq.astype(f32)
    hdn = jnp.sin(fr * (emb @ w1.astype(f32) + b1.astype(f32)))
    hdn = jnp.sin(fr * (hdn @ w2.astype(f32) + b2.astype(f32)))
    h = (hdn @ w3.astype(f32)).reshape(L, 2, HY_CH)
    deltas = jnp.abs(jnp.linspace(FILT_MIN_DECAY, FILT_MAX_DECAY, HY_CH, dtype=f32))
    h = h * jnp.exp(-t * deltas[None, :])[:, None, :]
    h = h / (jnp.sum(jnp.abs(h), axis=(0, 1), keepdims=True) + NORM_EPS)
    return h[:, 0], h[:, 1]


def _long_conv(u, h_fwd, h_bwd, d):
    B, L, C = u.shape
    k = jnp.concatenate([h_fwd, jnp.zeros((1, C), jnp.float32), h_bwd[1:][::-1]], axis=0)
    uf32 = u.astype(jnp.float32)
    uf = jnp.fft.rfft(uf32, n=2 * L, axis=1)
    kf = jnp.fft.rfft(k, axis=0)
    y = jnp.fft.irfft(uf * kf[None], n=2 * L, axis=1)[:, :L]
    return (y + uf32 * d.astype(jnp.float32)).astype(u.dtype)


def _hyena_branch(u, conv_w, conv_b, w1, b1, w2, b2, freq, w3, hyena_d):
    L = u.shape[1]
    pad = SHORT_K // 2
    up = jnp.pad(u, ((0, 0), (pad, pad), (0, 0)))
    uc = conv_b
    for j in range(SHORT_K):
        uc = uc + up[:, j:j + L] * conv_w[j]
    x0, x1, v = jnp.split(uc, 3, axis=-1)
    h_fwd, h_bwd = _hyena_filters(L, w1, b1, w2, b2, freq, w3)
    return x0 * _long_conv(x1 * v, h_fwd, h_bwd, hyena_d)


def _diff_attention(q, k, v, lam_q1, lam_k1, lam_q2, lam_k2, g_subln, lam_init):
    B, L, _ = q.shape
    f32 = jnp.float32
    q = _rope(q.reshape(B, L, N_HEADS, 2, HEAD_DIM))
    k = _rope(k.reshape(B, L, N_HEADS, 2, HEAD_DIM))
    v = v.reshape(B, L, N_HEADS, 2 * HEAD_DIM)
    lam = (jnp.exp(jnp.sum(lam_q1.astype(f32) * lam_k1.astype(f32)))
           - jnp.exp(jnp.sum(lam_q2.astype(f32) * lam_k2.astype(f32))) + lam_init)
    scale = HEAD_DIM ** -0.5
    nb = L // Q_BLOCK
    qb = jnp.moveaxis(q.reshape(B, nb, Q_BLOCK, N_HEADS, 2, HEAD_DIM), 1, 0)

    def block(qi):
        s = jnp.einsum('bqhcd,bkhcd->bchqk', qi, k).astype(f32) * scale
        a = jax.nn.softmax(s, axis=-1)
        w = a[:, 0] - lam * a[:, 1]
        return jnp.einsum('bhqk,bkhe->bqhe', w.astype(v.dtype), v)

    o = lax.map(block, qb)
    o = jnp.moveaxis(o, 0, 1).reshape(B, L, N_HEADS, 2 * HEAD_DIM)
    o = _rmsnorm(o, g_subln) * (1.0 - lam_init)
    return o.reshape(B, L, ATT_WIDTH)


def _layer(x, p, lam_init, g_mix_pre, g_mix_post, g_ffn_pre, g_ffn_post, g_ple, w_in, b_gate,
           conv_w, conv_b, filt_w1, filt_b1, filt_w2, filt_b2, filt_freq, filt_w3, hyena_d,
           lam_q1, lam_k1, lam_q2, lam_k2, g_subln, w_hy_out, w_att_out, w_out,
           w_ffn_in, w_ffn_out, w_ple_in, w_ple_gate):
    B, L, _ = x.shape
    h = _rmsnorm(x, g_mix_pre)
    z = h @ w_in
    u_hy, q, k, v, g_logit = jnp.split(z, SPLITS, axis=-1)
    gates = jax.nn.sigmoid(g_logit.reshape(B, L, 2, D_MODEL) + b_gate)
    y_hy = _hyena_branch(u_hy, conv_w, conv_b, filt_w1, filt_b1, filt_w2, filt_b2,
                         filt_freq, filt_w3, hyena_d)
    y_att = _diff_attention(q, k, v, lam_q1, lam_k1, lam_q2, lam_k2, g_subln, lam_init)
    m = gates[:, :, 0] * (y_hy @ w_hy_out) + gates[:, :, 1] * (y_att @ w_att_out)
    x = x + _rmsnorm(m @ w_out, g_mix_post)
    h = _rmsnorm(x, g_ffn_pre)
    f_gate, f_up = jnp.split(h @ w_ffn_in, 2, axis=-1)
    x = x + _rmsnorm((jax.nn.silu(f_gate) * f_up) @ w_ffn_out, g_ffn_post)
    e_gate = jax.nn.sigmoid(_rmsnorm(x, g_ple) @ w_ple_gate)
    return x + (p @ w_ple_in) * e_gate


def _trunk(x, p, weights):
    for i in range(DEPTH):
        lam_init = 0.8 - 0.6 * math.exp(-0.3 * i)
        x = _layer(x, p[i], lam_init, *[w[i] for w in weights])
    return x


def setup_inputs(seed: int = 0) -> dict:
    key = jax.random.key(seed)
    ks = jax.random.split(key, 32)

    def nrm(k, shape, scale):
        return jax.random.normal(k, shape, jnp.float32) * scale

    def gain(k, n):
        return 1.0 + nrm(k, (DEPTH, n), 0.02)

    return {
        'x_prompt': nrm(ks[0], (BATCH, SEQ, D_MODEL), 1.0),
        'x_sample': nrm(ks[1], (DEC_BATCH, DEC_SEQ, D_MODEL), 1.0),
        'p_prompt': nrm(ks[2], (DEPTH, BATCH, SEQ, PLE_DIM), 1.0),
        'p_sample': nrm(ks[3], (DEPTH, DEC_BATCH, DEC_SEQ, PLE_DIM), 1.0),
        'g_mix_pre': gain(ks[4], D_MODEL),
        'g_mix_post': gain(ks[5], D_MODEL),
        'g_ffn_pre': gain(ks[6], D_MODEL),
        'g_ffn_post': gain(ks[7], D_MODEL),
        'g_ple': gain(ks[8], D_MODEL),
        'w_in': nrm(ks[9], (DEPTH, D_MODEL, IN_COLS), D_MODEL ** -0.5),
        'b_gate': nrm(ks[10], (DEPTH, 2, D_MODEL), 0.1),
        'conv_w': nrm(ks[11], (DEPTH, SHORT_K, HY_COLS), SHORT_K ** -0.5),
        'conv_b': nrm(ks[12], (DEPTH, HY_COLS), 0.02),
        'filt_w1': nrm(ks[13], (DEPTH, FILT_EMB, FILT_ORDER), FILT_EMB ** -0.5),
        'filt_b1': nrm(ks[14], (DEPTH, FILT_ORDER), 0.1),
        'filt_w2': nrm(ks[15], (DEPTH, FILT_ORDER, FILT_ORDER), FILT_ORDER ** -0.5),
        'filt_b2': nrm(ks[16], (DEPTH, FILT_ORDER), 0.1),
        'filt_freq': 1.0 + nrm(ks[17], (DEPTH, FILT_ORDER), 0.1),
        'filt_w3': nrm(ks[18], (DEPTH, FILT_ORDER, 2 * HY_CH), FILT_ORDER ** -0.5),
        'hyena_d': nrm(ks[19], (DEPTH, HY_CH), 0.5),
        'lam_q1': nrm(ks[20], (DEPTH, HEAD_DIM), 0.1),
        'lam_k1': nrm(ks[21], (DEPTH, HEAD_DIM), 0.1),
        'lam_q2': nrm(ks[22], (DEPTH, HEAD_DIM), 0.1),
        'lam_k2': nrm(ks[23], (DEPTH, HEAD_DIM), 0.1),
        'g_subln': gain(ks[24], 2 * HEAD_DIM),
        'w_hy_out': nrm(ks[25], (DEPTH, HY_CH, D_MODEL), HY_CH ** -0.5),
        'w_att_out': nrm(ks[26], (DEPTH, ATT_WIDTH, D_MODEL), ATT_WIDTH ** -0.5),
        'w_out': nrm(ks[27], (DEPTH, D_MODEL, D_MODEL), D_MODEL ** -0.5),
        'w_ffn_in': nrm(ks[28], (DEPTH, D_MODEL, 2 * D_FF), D_MODEL ** -0.5),
        'w_ffn_out': nrm(ks[29], (DEPTH, D_FF, D_MODEL), D_FF ** -0.5),
        'w_ple_in': nrm(ks[30], (DEPTH, PLE_DIM, D_MODEL), PLE_DIM ** -0.5),
        'w_ple_gate': nrm(ks[31], (DEPTH, D_MODEL, D_MODEL), D_MODEL ** -0.5),
    }


def reference(x_prompt, x_sample, p_prompt, p_sample, g_mix_pre, g_mix_post, g_ffn_pre,
              g_ffn_post, g_ple, w_in, b_gate, conv_w, conv_b, filt_w1, filt_b1, filt_w2,
              filt_b2, filt_freq, filt_w3, hyena_d, lam_q1, lam_k1, lam_q2, lam_k2, g_subln,
              w_hy_out, w_att_out, w_out, w_ffn_in, w_ffn_out, w_ple_in, w_ple_gate):
    weights = (g_mix_pre, g_mix_post, g_ffn_pre, g_ffn_post, g_ple, w_in, b_gate,
               conv_w, conv_b, filt_w1, filt_b1, filt_w2, filt_b2, filt_freq, filt_w3, hyena_d,
               lam_q1, lam_k1, lam_q2, lam_k2, g_subln, w_hy_out, w_att_out, w_out,
               w_ffn_in, w_ffn_out, w_ple_in, w_ple_gate)
    y_prompt = _trunk(x_prompt, p_prompt, weights)
    y_sample = _trunk(x_sample, p_sample, weights)
    return (y_prompt, y_sample)
```

```python
import functools
import math

import jax
import jax.numpy as jnp
from jax import lax
from jax.experimental import pallas as pl
from jax.experimental.pallas import tpu as pltpu

F32 = jnp.float32
BF16 = jnp.bfloat16

D_MODEL = 2048
PLE_DIM = 256
NORM_EPS = 1e-6
HY_CH = D_MODEL // 2
FILT_BANDS = 16
FILT_EMB = 1 + 2 * FILT_BANDS
FILT_ORDER = 64
FILT_MAX_DECAY = math.log(1e-2) / 0.3
FILT_MIN_DECAY = math.log(1e-2) / 1.5
ATT_WIDTH = D_MODEL // 2
N_HEADS = 8
HEAD_DIM = ATT_WIDTH // (2 * N_HEADS)
HEAD_W = 2 * HEAD_DIM
ROPE_DIM = HEAD_DIM // 4
ROPE_THETA = 500000.0
D_FF = ((8 * D_MODEL + 3 * 256 - 1) // (3 * 256)) * 256
HY_COLS = 3 * HY_CH
GATE_COL0 = HY_COLS + 3 * ATT_WIDTH
IN_COLS = GATE_COL0 + 2 * D_MODEL
LANES = 128
LOG2E = math.log2(math.e)
MIB = 2 ** 20


def _cparams(sem, vmem_mib):
    return pltpu.CompilerParams(dimension_semantics=sem, vmem_limit_bytes=vmem_mib * MIB)


def _const_spec(shape):
    nd = len(shape)
    return pl.BlockSpec(shape, lambda *_: (0,) * nd, pipeline_mode=pl.Buffered(1))


def _rms(x, g):
    ms = jnp.mean(x * x, axis=-1, keepdims=True)
    return x * lax.rsqrt(ms + NORM_EPS) * g


def _sigmoid(x):
    return 1.0 / (1.0 + jnp.exp(-x))


ROW_CHUNK = 128


def _for_row_chunks(n_rows, body):
    def step(c, carry):
        body(pl.ds(pl.multiple_of(c * ROW_CHUNK, ROW_CHUNK), ROW_CHUNK))
        return carry
    lax.fori_loop(0, n_rows // ROW_CHUNK, step, 0)


def _rms_rows_to(h_ref, x_ref, g_ref):
    def body(rows):
        h_ref[rows, :] = _rms(x_ref[rows, :], g_ref[...]).astype(h_ref.dtype)
    _for_row_chunks(x_ref.shape[0], body)


def _inproj_kernel(x_ref, g_ref, w_ref, b_ref, o_ref, h_ref, *, n_plain):
    j = pl.program_id(1)

    @pl.when(j == 0)
    def _():
        _rms_rows_to(h_ref, x_ref, g_ref)

    acc = jnp.dot(h_ref[...], w_ref[...], preferred_element_type=F32)

    @pl.when(j < n_plain)
    def _():
        o_ref[...] = acc.astype(BF16)

    @pl.when(j >= n_plain)
    def _():
        o_ref[...] = _sigmoid(acc + b_ref[...]).astype(BF16)


def _in_proj(x2d, g, w_bf, b_flat):
    m, d = x2d.shape
    n = w_bf.shape[1]
    tm = min(1024, m)
    tn = 1024
    n_plain = GATE_COL0 // tn
    return pl.pallas_call(
        functools.partial(_inproj_kernel, n_plain=n_plain),
        out_shape=jax.ShapeDtypeStruct((m, n), BF16),
        grid=(m // tm, n // tn),
        in_specs=[
            pl.BlockSpec((tm, d), lambda i, j: (i, 0)),
            pl.BlockSpec((1, d), lambda i, j: (0, 0)),
            pl.BlockSpec((d, tn), lambda i, j: (0, j)),
            pl.BlockSpec((1, tn), lambda i, j: (0, jnp.maximum(j - n_plain, 0))),
        ],
        out_specs=pl.BlockSpec((tm, tn), lambda i, j: (i, j)),
        scratch_shapes=[pltpu.VMEM((tm, d), BF16)],
        compiler_params=_cparams(("parallel", "arbitrary"), 48),
        name="in_proj",
    )(x2d, g, w_bf, b_flat)


def _filt_mlp_kernel(emb_ref, w1_ref, b1_ref, w2_ref, b2_ref, fr_ref, w3_ref, dl_ref, ht_ref, sum_ref):
    i = pl.program_id(0)
    hp = lax.Precision.HIGHEST
    emb = emb_ref[...]
    fr = fr_ref[...]
    h1 = jnp.sin(fr * (jnp.dot(emb, w1_ref[...], precision=hp, preferred_element_type=F32) + b1_ref[...]))
    h2 = jnp.sin(fr * (jnp.dot(h1, w2_ref[...], precision=hp, preferred_element_type=F32) + b2_ref[...]))
    h = jnp.dot(h2, w3_ref[...], precision=hp, preferred_element_type=F32)
    dec = jnp.exp(-emb[:, 0:1] * dl_ref[...])
    ht = jnp.concatenate([h[:, :HY_CH] * dec, h[:, HY_CH:] * dec], axis=1).T
    ht_ref[...] = ht
    ab = jnp.abs(ht)
    part = ab[:, 0:LANES]
    for c in range(1, ab.shape[1] // LANES):
        part = part + ab[:, c * LANES:(c + 1) * LANES]

    @pl.when(i == 0)
    def _():
        sum_ref[...] = part

    @pl.when(i > 0)
    def _():
        sum_ref[...] += part


def _filt_mlp(seq, w1, b1, w2, b2, freq, w3):
    t = jnp.linspace(0.0, 1.0, seq, dtype=F32)[:, None]
    wpos = 2.0 * math.pi * jnp.arange(seq, dtype=F32) / seq
    bands = jnp.linspace(1e-4, FILT_BANDS - 1, FILT_BANDS, dtype=F32)
    ang = wpos[:, None] * bands[None, :]
    emb = jnp.concatenate([t, jnp.cos(ang), -jnp.sin(ang)], axis=-1)
    emb = jnp.pad(emb, ((0, 0), (0, LANES - FILT_EMB)))
    pad_o = LANES - FILT_ORDER
    w1p = jnp.pad(w1, ((0, LANES - FILT_EMB), (0, pad_o)))
    w2p = jnp.pad(w2, ((0, pad_o), (0, pad_o)))
    w3p = jnp.pad(w3, ((0, pad_o), (0, 0)))
    b1p = jnp.pad(b1[None, :], ((0, 0), (0, pad_o)))
    b2p = jnp.pad(b2[None, :], ((0, 0), (0, pad_o)))
    frp = jnp.pad(freq[None, :], ((0, 0), (0, pad_o)))
    deltas = jnp.abs(jnp.linspace(FILT_MIN_DECAY, FILT_MAX_DECAY, HY_CH, dtype=F32))[None, :]
    tl = min(512, seq)
    return pl.pallas_call(
        _filt_mlp_kernel,
        out_shape=(jax.ShapeDtypeStruct((2 * HY_CH, seq), F32),
                   jax.ShapeDtypeStruct((2 * HY_CH, LANES), F32)),
        grid=(seq // tl,),
        in_specs=[
            pl.BlockSpec((tl, LANES), lambda i: (i, 0)),
            _const_spec((LANES, LANES)), _const_spec((1, LANES)),
            _const_spec((LANES, LANES)), _const_spec((1, LANES)),
            _const_spec((1, LANES)), _const_spec((LANES, 2 * HY_CH)),
            _const_spec((1, HY_CH)),
        ],
        out_specs=(pl.BlockSpec((2 * HY_CH, tl), lambda i: (0, i)),
                   pl.BlockSpec((2 * HY_CH, LANES), lambda i: (0, 0))),
        compiler_params=_cparams(("arbitrary",), 48),
        name="filt_mlp",
    )(emb, w1p, b1p, w2p, b2p, frp, w3p, deltas)


def _split_dot(s, m):
    hi = s.astype(BF16)
    lo = (s - hi.astype(F32)).astype(BF16)
    return jnp.dot(hi, m, preferred_element_type=F32) + jnp.dot(lo, m, preferred_element_type=F32)


def _filt_spec_kernel(hf_ref, hb_ref, sf_ref, sb_ref, q_ref, a1_ref, u1_ref, v2_ref, kl_ref, *, tf, seq):
    f = pl.program_id(1)
    norm = jnp.sum(sf_ref[...] + sb_ref[...], axis=1, keepdims=True) + NORM_EPS
    hf = hf_ref[...] / norm
    lane = lax.broadcasted_iota(jnp.int32, hf.shape, 1)
    hb = jnp.where(lane == 0, 0.0, hb_ref[...] / norm)
    s1 = hf + hb
    s2 = hf - hb
    c = _split_dot(s1, q_ref[:, :tf]) * (1.0 / seq)
    col = lax.broadcasted_iota(jnp.int32, c.shape, 1) + f * tf
    u1_ref[...] = jnp.where(col == 0, 0.5 * c, c)
    v2_ref[...] = _split_dot(s2, q_ref[:, tf:]) * (1.0 / seq)

    @pl.when(f == 0)
    def _():
        kl_ref[...] = _split_dot(s1, a1_ref[...]) * (0.5 / seq)


def _filt_spec(ht, sums, qmat, a1, tf):
    seq = ht.shape[1]
    tc = 256
    nc = HY_CH // tc
    return pl.pallas_call(
        functools.partial(_filt_spec_kernel, tf=tf, seq=seq),
        out_shape=(jax.ShapeDtypeStruct((HY_CH, seq), F32),
                   jax.ShapeDtypeStruct((HY_CH, seq), F32),
                   jax.ShapeDtypeStruct((HY_CH, LANES), F32)),
        grid=(nc, seq // tf),
        in_specs=[
            pl.BlockSpec((tc, seq), lambda c, f: (c, 0)),
            pl.BlockSpec((tc, seq), lambda c, f: (nc + c, 0)),
            pl.BlockSpec((tc, LANES), lambda c, f: (c, 0)),
            pl.BlockSpec((tc, LANES), lambda c, f: (nc + c, 0)),
            pl.BlockSpec((seq, 2 * tf), lambda c, f: (0, f)),
            pl.BlockSpec((seq, LANES), lambda c, f: (0, 0)),
        ],
        out_specs=(pl.BlockSpec((tc, tf), lambda c, f: (c, f)),
                   pl.BlockSpec((tc, tf), lambda c, f: (c, f)),
                   pl.BlockSpec((tc, LANES), lambda c, f: (c, 0))),
        compiler_params=_cparams(("parallel", "arbitrary"), 48),
        name="filt_spec",
    )(ht, ht, sums, sums, qmat, a1)


def _hyprep_kernel(u0_ref, u1_ref, u2_ref, w0_ref, w1_ref, w2_ref, b0_ref, b1_ref, b2_ref, x0_ref, wt_ref):
    seq = u0_ref.shape[0]
    row = lax.broadcasted_iota(jnp.int32, u0_ref.shape, 0)

    def sconv(u_ref, w_ref, b_ref):
        u = u_ref[...].astype(F32)
        um = jnp.where(row == 0, 0.0, pltpu.roll(u, 1, 0))
        up = jnp.where(row == seq - 1, 0.0, pltpu.roll(u, seq - 1, 0))
        w = w_ref[...]
        return b_ref[...] + um * w[0:1] + u * w[1:2] + up * w[2:3]

    x0_ref[...] = sconv(u0_ref, w0_ref, b0_ref).astype(BF16)
    prod = sconv(u1_ref, w1_ref, b1_ref) * sconv(u2_ref, w2_ref, b2_ref)
    wt_ref[...] = prod.T.astype(BF16)


def _hy_prep(z3, conv_w, conv_b):
    b, seq, _ = z3.shape
    tc = LANES
    nc = HY_CH // tc
    uspec = lambda g: pl.BlockSpec((None, seq, tc), lambda i, c: (i, 0, g * nc + c))
    wspec = lambda g: pl.BlockSpec((3, tc), lambda i, c: (0, g * nc + c))
    bspec = lambda g: pl.BlockSpec((1, tc), lambda i, c: (0, g * nc + c))
    return pl.pallas_call(
        _hyprep_kernel,
        out_shape=(jax.ShapeDtypeStruct((b, seq, HY_CH), BF16),
                   jax.ShapeDtypeStruct((b, HY_CH, seq), BF16)),
        grid=(b, nc),
        in_specs=[uspec(0), uspec(1), uspec(2), wspec(0), wspec(1), wspec(2), bspec(0), bspec(1), bspec(2)],
        out_specs=(pl.BlockSpec((None, seq, tc), lambda i, c: (i, 0, c)),
                   pl.BlockSpec((None, tc, seq), lambda i, c: (i, c, 0))),
        compiler_params=_cparams(("parallel", "parallel"), 48),
        name="hy_prep",
    )(z3, z3, z3, conv_w, conv_w, conv_w, conv_b, conv_b, conv_b)


def _conv_kernel(wt_ref, q_ref, r_ref, u1_ref, v2_ref, a1_ref, a2_ref, kl_ref, d_ref, o_ref, acc_ref, *, tf):
    f = pl.program_id(2)
    wt = wt_ref[...]

    @pl.when(f == 0)
    def _():
        xl = jnp.dot(wt, a1_ref[...], preferred_element_type=F32)
        yl = (xl * kl_ref[...]).astype(BF16)
        acc_ref[...] = jnp.dot(yl, a2_ref[...], preferred_element_type=F32)

    ap = jnp.dot(wt, q_ref[...], preferred_element_type=F32)
    a = ap[:, :tf]
    p = ap[:, tf:]
    u1 = u1_ref[...]
    v2 = v2_ref[...]
    z = jnp.concatenate([a * u1 - p * v2, p * u1 + a * v2], axis=1).astype(BF16)
    acc_ref[...] += jnp.dot(z, r_ref[...], preferred_element_type=F32)

    @pl.when(f == pl.num_programs(2) - 1)
    def _():
        y = acc_ref[...] + d_ref[:, 0:1] * wt.astype(F32)
        o_ref[...] = y.T.astype(BF16)


def _long_conv(wt, qmat, rmat, u1, v2, a1, a2, kl, dcol, tf):
    b, _, seq = wt.shape
    tc = 512
    return pl.pallas_call(
        functools.partial(_conv_kernel, tf=tf),
        out_shape=jax.ShapeDtypeStruct((b, seq, HY_CH), BF16),
        grid=(b, HY_CH // tc, seq // tf),
        in_specs=[
            pl.BlockSpec((None, tc, seq), lambda i, c, f: (i, c, 0)),
            pl.BlockSpec((seq, 2 * tf), lambda i, c, f: (0, f)),
            pl.BlockSpec((2 * tf, seq), lambda i, c, f: (f, 0)),
            pl.BlockSpec((tc, tf), lambda i, c, f: (c, f)),
            pl.BlockSpec((tc, tf), lambda i, c, f: (c, f)),
            _const_spec((seq, LANES)),
            _const_spec((LANES, seq)),
            pl.BlockSpec((tc, LANES), lambda i, c, f: (c, 0)),
            pl.BlockSpec((tc, LANES), lambda i, c, f: (c, 0)),
        ],
        out_specs=pl.BlockSpec((None, seq, tc), lambda i, c, f: (i, 0, c)),
        scratch_shapes=[pltpu.VMEM((tc, seq), F32)],
        compiler_params=_cparams(("parallel", "parallel", "arbitrary"), 56),
        name="long_conv",
    )(wt, qmat, rmat, u1, v2, a1, a2, kl, dcol)


def _dft_tables(seq, tf):
    idx = jnp.arange(seq, dtype=jnp.int32)
    prod = (idx[:, None] * idx[None, :]) % (2 * seq)
    ang = prod.astype(F32) * (math.pi / seq)
    nf = seq // tf
    cm = jnp.cos(ang).astype(BF16).reshape(seq, nf, 1, tf)
    sm = jnp.sin(ang).astype(BF16).reshape(seq, nf, 1, tf)
    qmat = jnp.concatenate([cm, sm], axis=2).reshape(seq, 2 * seq)
    rmat = qmat.T
    alt = (1.0 - 2.0 * (idx % 2).astype(F32)).astype(BF16)
    a1 = jnp.zeros((seq, LANES), BF16).at[:, 0].set(alt)
    return qmat, rmat, a1, a1.T


def _attn_kernel(lam_ref, q_ref, k_ref, v_ref, cos_ref, sa_ref, sb_ref, gs_ref, o_ref, krot_ref, *, tq, lam_init):
    qi = pl.program_id(2)

    def rope(x, c, sa, sb):
        return x * c + pltpu.roll(x, LANES - ROPE_DIM // 2, 1) * sa + pltpu.roll(x, ROPE_DIM // 2, 1) * sb

    @pl.when(qi == 0)
    def _():
        krot_ref[...] = rope(k_ref[...].astype(F32), cos_ref[...], sa_ref[...], sb_ref[...]).astype(BF16)

    rows = pl.ds(pl.multiple_of(qi * tq, tq), tq)
    q = rope(q_ref[...].astype(F32), cos_ref[rows, :], sa_ref[rows, :], sb_ref[rows, :])
    q = q * (HEAD_DIM ** -0.5 * LOG2E)
    lane = lax.broadcasted_iota(jnp.int32, q.shape, 1)
    qq = jnp.concatenate([jnp.where(lane < HEAD_DIM, q, 0.0), jnp.where(lane >= HEAD_DIM, q, 0.0)], axis=0)
    s = lax.dot_general(qq.astype(BF16), krot_ref[...], (((1,), (1,)), ((), ())),
                        preferred_element_type=F32)
    p = jnp.exp2(s - jnp.max(s, axis=-1, keepdims=True))
    r = 1.0 / jnp.sum(p, axis=-1, keepdims=True)
    lp = lam_ref[...]
    lam = (jnp.exp(jnp.sum(lp[0:1] * lp[1:2], axis=-1, keepdims=True))
           - jnp.exp(jnp.sum(lp[2:3] * lp[3:4], axis=-1, keepdims=True)) + lam_init)
    w = p[:tq] * r[:tq] - p[tq:] * (lam * r[tq:])
    o = jnp.dot(w.astype(BF16), v_ref[...], preferred_element_type=F32)
    o_ref[...] = (_rms(o, gs_ref[...]) * (1.0 - lam_init)).astype(BF16)


def _attention(z3, lam_params, g_subln, lam_init):
    b, seq, _ = z3.shape
    tq = min(256, seq)
    inv = ROPE_THETA ** (-jnp.arange(0, ROPE_DIM, 2, dtype=F32) / ROPE_DIM)
    ang = jnp.arange(seq, dtype=F32)[:, None] * inv[None, :]
    c8, s8 = jnp.cos(ang), jnp.sin(ang)
    half = ROPE_DIM // 2
    rest = HEAD_DIM - ROPE_DIM
    one, zero = jnp.ones((seq, rest), F32), jnp.zeros((seq, rest), F32)
    z8 = jnp.zeros((seq, half), F32)
    cos_t = jnp.tile(jnp.concatenate([c8, c8, one], axis=1), (1, 2))
    sa_t = jnp.tile(jnp.concatenate([-s8, z8, zero], axis=1), (1, 2))
    sb_t = jnp.tile(jnp.concatenate([z8, s8, zero], axis=1), (1, 2))
    qb, kb, vb = (HY_COLS // HEAD_W, (HY_COLS + ATT_WIDTH) // HEAD_W, (HY_COLS + 2 * ATT_WIDTH) // HEAD_W)
    return pl.pallas_call(
        functools.partial(_attn_kernel, tq=tq, lam_init=lam_init),
        out_shape=jax.ShapeDtypeStruct((b, seq, ATT_WIDTH), BF16),
        grid=(b, N_HEADS, seq // tq),
        in_specs=[
            _const_spec((4, HEAD_DIM)),
            pl.BlockSpec((None, tq, HEAD_W), lambda i, h, t: (i, t, qb + h)),
            pl.BlockSpec((None, seq, HEAD_W), lambda i, h, t: (i, 0, kb + h)),
            pl.BlockSpec((None, seq, HEAD_W), lambda i, h, t: (i, 0, vb + h)),
            _const_spec((seq, HEAD_W)), _const_spec((seq, HEAD_W)), _const_spec((seq, HEAD_W)),
            _const_spec((1, HEAD_W)),
        ],
        out_specs=pl.BlockSpec((None, tq, HEAD_W), lambda i, h, t: (i, t, h)),
        scratch_shapes=[pltpu.VMEM((seq, HEAD_W), BF16)],
        compiler_params=_cparams(("parallel", "parallel", "arbitrary"), 56),
        name="attention",
    )(lam_params, z3, z3, z3, cos_t, sa_t, sb_t, g_subln)


def _merge_kernel(yc_ref, x0_ref, ya_ref, g0_ref, g1_ref, x_ref, why_ref, wat_ref, wo_ref, gp_ref, o_ref):
    yh = (x0_ref[...].astype(F32) * yc_ref[...].astype(F32)).astype(BF16)
    a = jnp.dot(yh, why_ref[...], preferred_element_type=F32)
    b = jnp.dot(ya_ref[...], wat_ref[...], preferred_element_type=F32)
    m = g0_ref[...].astype(F32) * a + g1_ref[...].astype(F32) * b
    r = jnp.dot(m.astype(BF16), wo_ref[...], preferred_element_type=F32)
    o_ref[...] = x_ref[...] + _rms(r, gp_ref[...])


def _merge(yconv, x0c, yatt, z, x2d, w_hy, w_att, w_out, g_post):
    m, d = x2d.shape
    tm = min(256, m)
    row = lambda width, blk=0: pl.BlockSpec((tm, width), lambda i: (i, blk))
    gate_blk = GATE_COL0 // d
    return pl.pallas_call(
        _merge_kernel,
        out_shape=jax.ShapeDtypeStruct((m, d), F32),
        grid=(m // tm,),
        in_specs=[row(HY_CH), row(HY_CH), row(ATT_WIDTH), row(d, gate_blk), row(d, gate_blk + 1), row(d),
                  _const_spec((HY_CH, d)), _const_spec((ATT_WIDTH, d)), _const_spec((d, d)), _const_spec((1, d))],
        out_specs=row(d),
        compiler_params=_cparams(("parallel",), 56),
        name="merge",
    )(yconv, x0c, yatt, z, z, x2d, w_hy, w_att, w_out, g_post)


def _ffn_kernel(x_ref, p_ref, gpre_ref, wg_ref, wu_ref, wo_ref, gpost_ref, gple_ref, wpg_ref, wpi_ref,
                o_ref, h_ref, acc_ref):
    j = pl.program_id(1)

    @pl.when(j == 0)
    def _():
        _rms_rows_to(h_ref, x_ref, gpre_ref)

    h = h_ref[...]
    gate = jnp.dot(h, wg_ref[...], preferred_element_type=F32)
    up = jnp.dot(h, wu_ref[...], preferred_element_type=F32)
    act = (gate * _sigmoid(gate) * up).astype(BF16)
    part = jnp.dot(act, wo_ref[...], preferred_element_type=F32)

    @pl.when(j == 0)
    def _():
        acc_ref[...] = part

    @pl.when(j > 0)
    def _():
        acc_ref[...] += part

    @pl.when(j == pl.num_programs(1) - 1)
    def _():
        def body(rows):
            x2 = x_ref[rows, :] + _rms(acc_ref[rows, :], gpost_ref[...])
            e = jnp.dot(_rms(x2, gple_ref[...]).astype(BF16), wpg_ref[...], preferred_element_type=F32)
            pe = jnp.dot(p_ref[rows, :].astype(BF16), wpi_ref[...], preferred_element_type=F32)
            o_ref[rows, :] = x2 + pe * _sigmoid(e)
        _for_row_chunks(x_ref.shape[0], body)


def _ffn_ple(x1, p2d, g_pre, w_in, w_out, g_post, g_ple, w_pg, w_pi):
    m, d = x1.shape
    tm = min(512, m)
    tf = 512
    nff = D_FF // tf
    return pl.pallas_call(
        _ffn_kernel,
        out_shape=jax.ShapeDtypeStruct((m, d), F32),
        grid=(m // tm, nff),
        in_specs=[
            pl.BlockSpec((tm, d), lambda i, j: (i, 0)),
            pl.BlockSpec((tm, PLE_DIM), lambda i, j: (i, 0)),
            _const_spec((1, d)),
            pl.BlockSpec((d, tf), lambda i, j: (0, j)),
            pl.BlockSpec((d, tf), lambda i, j: (0, nff + j)),
            pl.BlockSpec((tf, d), lambda i, j: (j, 0)),
            _const_spec((1, d)), _const_spec((1, d)),
            _const_spec((d, d)), _const_spec((PLE_DIM, d)),
        ],
        out_specs=pl.BlockSpec((tm, d), lambda i, j: (i, 0)),
        scratch_shapes=[pltpu.VMEM((tm, d), BF16), pltpu.VMEM((tm, d), F32)],
        compiler_params=_cparams(("parallel", "arbitrary"), 56),
        name="ffn_ple",
    )(x1, p2d, g_pre, w_in, w_in, w_out, g_post, g_ple, w_pg, w_pi)


def _layer(x, p, lam_init, wts):
    b, seq, d = x.shape
    x2d = x.reshape(b * seq, d)
    z = _in_proj(x2d, wts["g_mix_pre"], wts["w_in"], wts["b_gate"])
    z3 = z.reshape(b, seq, IN_COLS)

    tf = min(256, seq)
    qmat, rmat, a1, a2 = _dft_tables(seq, tf)
    ht, sums = _filt_mlp(seq, wts["filt_w1"], wts["filt_b1"], wts["filt_w2"], wts["filt_b2"],
                         wts["filt_freq"], wts["filt_w3"])
    u1, v2, kl = _filt_spec(ht, sums, qmat, a1, tf)
    x0c, wt = _hy_prep(z3, wts["conv_w"], wts["conv_b"])
    yconv = _long_conv(wt, qmat, rmat, u1, v2, a1, a2, kl, wts["hyena_d"], tf)

    yatt = _attention(z3, wts["lam"], wts["g_subln"], lam_init)

    x1 = _merge(yconv.reshape(b * seq, HY_CH), x0c.reshape(b * seq, HY_CH), yatt.reshape(b * seq, ATT_WIDTH),
                z, x2d, wts["w_hy_out"], wts["w_att_out"], wts["w_out"], wts["g_mix_post"])
    y = _ffn_ple(x1, p.reshape(b * seq, PLE_DIM), wts["g_ffn_pre"], wts["w_ffn_in"], wts["w_ffn_out"],
                 wts["g_ffn_post"], wts["g_ple"], wts["w_ple_gate"], wts["w_ple_in"])
    return y.reshape(b, seq, d)


def kernel(x_prompt, x_sample, p_prompt, p_sample, g_mix_pre, g_mix_post, g_ffn_pre, g_ffn_post, g_ple, w_in, b_gate, conv_w, conv_b, filt_w1, filt_b1, filt_w2, filt_b2, filt_freq, filt_w3, hyena_d, lam_q1, lam_k1, lam_q2, lam_k2, g_subln, w_hy_out, w_att_out, w_out, w_ffn_in, w_ffn_out, w_ple_in, w_ple_gate):
    depth = w_in.shape[0]
    xs = [x_prompt, x_sample]
    ps = [p_prompt, p_sample]
    for i in range(depth):
        lam_init = 0.8 - 0.6 * math.exp(-0.3 * i)
        wts = dict(
            g_mix_pre=g_mix_pre[i][None, :], g_mix_post=g_mix_post[i][None, :],
            g_ffn_pre=g_ffn_pre[i][None, :], g_ffn_post=g_ffn_post[i][None, :], g_ple=g_ple[i][None, :],
            w_in=w_in[i].astype(BF16), b_gate=b_gate[i].reshape(1, 2 * D_MODEL),
            conv_w=conv_w[i], conv_b=conv_b[i][None, :],
            filt_w1=filt_w1[i], filt_b1=filt_b1[i], filt_w2=filt_w2[i], filt_b2=filt_b2[i],
            filt_freq=filt_freq[i], filt_w3=filt_w3[i],
            hyena_d=jnp.broadcast_to(hyena_d[i][:, None], (HY_CH, LANES)),
            lam=jnp.stack([lam_q1[i], lam_k1[i], lam_q2[i], lam_k2[i]]),
            g_subln=g_subln[i][None, :],
            w_hy_out=w_hy_out[i].astype(BF16), w_att_out=w_att_out[i].astype(BF16),
            w_out=w_out[i].astype(BF16), w_ffn_in=w_ffn_in[i].astype(BF16),
            w_ffn_out=w_ffn_out[i].astype(BF16), w_ple_in=w_ple_in[i].astype(BF16),
            w_ple_gate=w_ple_gate[i].astype(BF16),
        )
        xs = [_layer(x, p[i], lam_init, wts) for x, p in zip(xs, ps)]
    return (xs[0], xs[1])
```

```python
import functools
import math

import jax
import jax.numpy as jnp
from jax import lax
from jax.experimental import pallas as pl
from jax.experimental.pallas import tpu as pltpu

F32 = jnp.float32
BF16 = jnp.bfloat16

D_MODEL = 2048
PLE_DIM = 256
NORM_EPS = 1e-6
HY_CH = D_MODEL // 2
FILT_BANDS = 16
FILT_EMB = 1 + 2 * FILT_BANDS
FILT_ORDER = 64
FILT_MAX_DECAY = math.log(1e-2) / 0.3
FILT_MIN_DECAY = math.log(1e-2) / 1.5
ATT_WIDTH = D_MODEL // 2
N_HEADS = 8
HEAD_DIM = ATT_WIDTH // (2 * N_HEADS)
HEAD_W = 2 * HEAD_DIM
ROPE_DIM = HEAD_DIM // 4
ROPE_THETA = 500000.0
D_FF = ((8 * D_MODEL + 3 * 256 - 1) // (3 * 256)) * 256
HY_COLS = 3 * HY_CH
GATE_COL0 = HY_COLS + 3 * ATT_WIDTH
IN_COLS = GATE_COL0 + 2 * D_MODEL
LANES = 128
LOG2E = math.log2(math.e)
MIB = 2 ** 20


def _cparams(sem, vmem_mib):
    return pltpu.CompilerParams(dimension_semantics=sem, vmem_limit_bytes=vmem_mib * MIB)


def _const_spec(shape):
    nd = len(shape)
    return pl.BlockSpec(shape, lambda *_: (0,) * nd, pipeline_mode=pl.Buffered(1))


def _rms(x, g):
    ms = jnp.mean(x * x, axis=-1, keepdims=True)
    return x * lax.rsqrt(ms + NORM_EPS) * g


def _sigmoid(x):
    return 1.0 / (1.0 + jnp.exp(-x))


ROW_CHUNK = 128


def _for_row_chunks(n_rows, body, chunk=ROW_CHUNK):
    chunk = min(chunk, n_rows)

    def step(c, carry):
        body(pl.ds(pl.multiple_of(c * chunk, chunk), chunk))
        return carry
    lax.fori_loop(0, n_rows // chunk, step, 0)


def _rms_rows_to(h_ref, x_ref, g_ref):
    def body(rows):
        h_ref[rows, :] = _rms(x_ref[rows, :], g_ref[...]).astype(h_ref.dtype)
    _for_row_chunks(x_ref.shape[0], body)


def _inproj_kernel(x_ref, g_ref, w_ref, b_ref, o_ref, h_ref, *, n_plain):
    j = pl.program_id(1)

    @pl.when(j == 0)
    def _():
        _rms_rows_to(h_ref, x_ref, g_ref)

    acc = jnp.dot(h_ref[...], w_ref[...], preferred_element_type=F32)
    o_ref[...] = jnp.where(j >= n_plain, _sigmoid(acc + b_ref[...]), acc).astype(BF16)


def _in_proj(x2d, g, w_bf, b_flat):
    m, d = x2d.shape
    n = w_bf.shape[1]
    tm = min(1024, m)
    tn = 1024
    n_plain = GATE_COL0 // tn
    return pl.pallas_call(
        functools.partial(_inproj_kernel, n_plain=n_plain),
        out_shape=jax.ShapeDtypeStruct((m, n), BF16),
        grid=(m // tm, n // tn),
        in_specs=[
            pl.BlockSpec((tm, d), lambda i, j: (i, 0)),
            pl.BlockSpec((1, d), lambda i, j: (0, 0)),
            pl.BlockSpec((d, tn), lambda i, j: (0, j)),
            pl.BlockSpec((1, tn), lambda i, j: (0, jnp.maximum(j - n_plain, 0))),
        ],
        out_specs=pl.BlockSpec((tm, tn), lambda i, j: (i, j)),
        scratch_shapes=[pltpu.VMEM((tm, d), BF16)],
        compiler_params=_cparams(("parallel", "arbitrary"), 48),
        name="in_proj",
    )(x2d, g, w_bf, b_flat)


def _filt_mlp_kernel(emb_ref, w1_ref, b1_ref, w2_ref, b2_ref, fr_ref, w3_ref, dl_ref, ht_ref, sum_ref):
    i = pl.program_id(0)
    hp = lax.Precision.HIGHEST
    emb = emb_ref[...]
    fr = fr_ref[...]
    h1 = jnp.sin(fr * (jnp.dot(emb, w1_ref[...], precision=hp, preferred_element_type=F32) + b1_ref[...]))
    h2 = jnp.sin(fr * (jnp.dot(h1, w2_ref[...], precision=hp, preferred_element_type=F32) + b2_ref[...]))
    h = jnp.dot(h2, w3_ref[...], precision=hp, preferred_element_type=F32)
    dec = jnp.exp(-emb[:, 0:1] * dl_ref[...])
    ht = jnp.concatenate([h[:, :HY_CH] * dec, h[:, HY_CH:] * dec], axis=1).T
    ht_ref[...] = ht
    ab = jnp.abs(ht)
    part = ab[:, 0:LANES]
    for c in range(1, ab.shape[1] // LANES):
        part = part + ab[:, c * LANES:(c + 1) * LANES]

    @pl.when(i == 0)
    def _():
        sum_ref[...] = part

    @pl.when(i > 0)
    def _():
        sum_ref[...] += part


def _filt_mlp(seq, w1, b1, w2, b2, freq, w3):
    t = jnp.linspace(0.0, 1.0, seq, dtype=F32)[:, None]
    wpos = 2.0 * math.pi * jnp.arange(seq, dtype=F32) / seq
    bands = jnp.linspace(1e-4, FILT_BANDS - 1, FILT_BANDS, dtype=F32)
    ang = wpos[:, None] * bands[None, :]
    emb = jnp.concatenate([t, jnp.cos(ang), -jnp.sin(ang)], axis=-1)
    emb = jnp.pad(emb, ((0, 0), (0, LANES - FILT_EMB)))
    pad_o = LANES - FILT_ORDER
    w1p = jnp.pad(w1, ((0, LANES - FILT_EMB), (0, pad_o)))
    w2p = jnp.pad(w2, ((0, pad_o), (0, pad_o)))
    w3p = jnp.pad(w3, ((0, pad_o), (0, 0)))
    b1p = jnp.pad(b1[None, :], ((0, 0), (0, pad_o)))
    b2p = jnp.pad(b2[None, :], ((0, 0), (0, pad_o)))
    frp = jnp.pad(freq[None, :], ((0, 0), (0, pad_o)))
    deltas = jnp.abs(jnp.linspace(FILT_MIN_DECAY, FILT_MAX_DECAY, HY_CH, dtype=F32))[None, :]
    tl = min(512, seq)
    return pl.pallas_call(
        _filt_mlp_kernel,
        out_shape=(jax.ShapeDtypeStruct((2 * HY_CH, seq), F32),
                   jax.ShapeDtypeStruct((2 * HY_CH, LANES), F32)),
        grid=(seq // tl,),
        in_specs=[
            pl.BlockSpec((tl, LANES), lambda i: (i, 0)),
            _const_spec((LANES, LANES)), _const_spec((1, LANES)),
            _const_spec((LANES, LANES)), _const_spec((1, LANES)),
            _const_spec((1, LANES)), _const_spec((LANES, 2 * HY_CH)),
            _const_spec((1, HY_CH)),
        ],
        out_specs=(pl.BlockSpec((2 * HY_CH, tl), lambda i: (0, i)),
                   pl.BlockSpec((2 * HY_CH, LANES), lambda i: (0, 0))),
        compiler_params=_cparams(("arbitrary",), 48),
        name="filt_mlp",
    )(emb, w1p, b1p, w2p, b2p, frp, w3p, deltas)


def _split_dot(s, m):
    hi = s.astype(BF16)
    lo = (s - hi.astype(F32)).astype(BF16)
    return jnp.dot(hi, m, preferred_element_type=F32) + jnp.dot(lo, m, preferred_element_type=F32)


def _filt_spec_kernel(hf_ref, hb_ref, sf_ref, sb_ref, q_ref, a1_ref, u1_ref, v2_ref, kl_ref, *, tf, seq):
    f = pl.program_id(1)
    norm = jnp.sum(sf_ref[...] + sb_ref[...], axis=1, keepdims=True) + NORM_EPS
    hf = hf_ref[...] / norm
    lane = lax.broadcasted_iota(jnp.int32, hf.shape, 1)
    hb = jnp.where(lane == 0, 0.0, hb_ref[...] / norm)
    s1 = hf + hb
    s2 = hf - hb
    c = _split_dot(s1, q_ref[:, :tf]) * (1.0 / seq)
    col = lax.broadcasted_iota(jnp.int32, c.shape, 1) + f * tf
    u1_ref[...] = jnp.where(col == 0, 0.5 * c, c)
    v2_ref[...] = _split_dot(s2, q_ref[:, tf:]) * (1.0 / seq)

    @pl.when(f == 0)
    def _():
        kl_ref[...] = _split_dot(s1, a1_ref[...]) * (0.5 / seq)


def _filt_spec(ht, sums, qmat, a1, tf):
    seq = ht.shape[1]
    tc = 256
    nc = HY_CH // tc
    return pl.pallas_call(
        functools.partial(_filt_spec_kernel, tf=tf, seq=seq),
        out_shape=(jax.ShapeDtypeStruct((HY_CH, seq), F32),
                   jax.ShapeDtypeStruct((HY_CH, seq), F32),
                   jax.ShapeDtypeStruct((HY_CH, LANES), F32)),
        grid=(nc, seq // tf),
        in_specs=[
            pl.BlockSpec((tc, seq), lambda c, f: (c, 0)),
            pl.BlockSpec((tc, seq), lambda c, f: (nc + c, 0)),
            pl.BlockSpec((tc, LANES), lambda c, f: (c, 0)),
            pl.BlockSpec((tc, LANES), lambda c, f: (nc + c, 0)),
            pl.BlockSpec((seq, 2 * tf), lambda c, f: (0, f)),
            pl.BlockSpec((seq, LANES), lambda c, f: (0, 0)),
        ],
        out_specs=(pl.BlockSpec((tc, tf), lambda c, f: (c, f)),
                   pl.BlockSpec((tc, tf), lambda c, f: (c, f)),
                   pl.BlockSpec((tc, LANES), lambda c, f: (c, 0))),
        compiler_params=_cparams(("parallel", "arbitrary"), 48),
        name="filt_spec",
    )(ht, ht, sums, sums, qmat, a1)


def _hyprep_kernel(u0_ref, u1_ref, u2_ref, w0_ref, w1_ref, w2_ref, b0_ref, b1_ref, b2_ref, x0_ref, wt_ref):
    seq = u0_ref.shape[0]
    row = lax.broadcasted_iota(jnp.int32, u0_ref.shape, 0)

    def sconv(u_ref, w_ref, b_ref):
        u = u_ref[...].astype(F32)
        um = jnp.where(row == 0, 0.0, pltpu.roll(u, 1, 0))
        up = jnp.where(row == seq - 1, 0.0, pltpu.roll(u, seq - 1, 0))
        w = w_ref[...]
        return b_ref[...] + um * w[0:1] + u * w[1:2] + up * w[2:3]

    x0_ref[...] = sconv(u0_ref, w0_ref, b0_ref).astype(BF16)
    prod = sconv(u1_ref, w1_ref, b1_ref) * sconv(u2_ref, w2_ref, b2_ref)
    wt_ref[...] = prod.T.astype(BF16)


def _hy_prep(z3, conv_w, conv_b):
    b, seq, _ = z3.shape
    tc = LANES
    nc = HY_CH // tc
    uspec = lambda g: pl.BlockSpec((None, seq, tc), lambda i, c: (i, 0, g * nc + c))
    wspec = lambda g: pl.BlockSpec((3, tc), lambda i, c: (0, g * nc + c))
    bspec = lambda g: pl.BlockSpec((1, tc), lambda i, c: (0, g * nc + c))
    return pl.pallas_call(
        _hyprep_kernel,
        out_shape=(jax.ShapeDtypeStruct((b, seq, HY_CH), BF16),
                   jax.ShapeDtypeStruct((b, HY_CH, seq), BF16)),
        grid=(b, nc),
        in_specs=[uspec(0), uspec(1), uspec(2), wspec(0), wspec(1), wspec(2), bspec(0), bspec(1), bspec(2)],
        out_specs=(pl.BlockSpec((None, seq, tc), lambda i, c: (i, 0, c)),
                   pl.BlockSpec((None, tc, seq), lambda i, c: (i, c, 0))),
        compiler_params=_cparams(("parallel", "parallel"), 48),
        name="hy_prep",
    )(z3, z3, z3, conv_w, conv_w, conv_w, conv_b, conv_b, conv_b)


def _conv_kernel(wt_ref, q_ref, r_ref, u1_ref, v2_ref, a1_ref, a2_ref, kl_ref, d_ref, o_ref, acc_ref, *, tf):
    f = pl.program_id(2)
    wt = wt_ref[...]

    @pl.when(f == 0)
    def _():
        xl = jnp.dot(wt, a1_ref[...], preferred_element_type=F32)
        yl = (xl * kl_ref[...]).astype(BF16)
        acc_ref[...] = jnp.dot(yl, a2_ref[...], preferred_element_type=F32)

    ap = jnp.dot(wt, q_ref[...], preferred_element_type=F32)
    a = ap[:, :tf]
    p = ap[:, tf:]
    u1 = u1_ref[...]
    v2 = v2_ref[...]
    z = jnp.concatenate([a * u1 - p * v2, p * u1 + a * v2], axis=1).astype(BF16)
    acc_ref[...] += jnp.dot(z, r_ref[...], preferred_element_type=F32)

    @pl.when(f == pl.num_programs(2) - 1)
    def _():
        y = acc_ref[...] + d_ref[:, 0:1] * wt.astype(F32)
        o_ref[...] = y.T.astype(BF16)


def _long_conv(wt, qmat, rmat, u1, v2, a1, a2, kl, dcol, tf):
    b, _, seq = wt.shape
    tc = 512
    return pl.pallas_call(
        functools.partial(_conv_kernel, tf=tf),
        out_shape=jax.ShapeDtypeStruct((b, seq, HY_CH), BF16),
        grid=(b, HY_CH // tc, seq // tf),
        in_specs=[
            pl.BlockSpec((None, tc, seq), lambda i, c, f: (i, c, 0)),
            pl.BlockSpec((seq, 2 * tf), lambda i, c, f: (0, f)),
            pl.BlockSpec((2 * tf, seq), lambda i, c, f: (f, 0)),
            pl.BlockSpec((tc, tf), lambda i, c, f: (c, f)),
            pl.BlockSpec((tc, tf), lambda i, c, f: (c, f)),
            _const_spec((seq, LANES)),
            _const_spec((LANES, seq)),
            pl.BlockSpec((tc, LANES), lambda i, c, f: (c, 0)),
            pl.BlockSpec((tc, LANES), lambda i, c, f: (c, 0)),
        ],
        out_specs=pl.BlockSpec((None, seq, tc), lambda i, c, f: (i, 0, c)),
        scratch_shapes=[pltpu.VMEM((tc, seq), F32)],
        compiler_params=_cparams(("parallel", "parallel", "arbitrary"), 56),
        name="long_conv",
    )(wt, qmat, rmat, u1, v2, a1, a2, kl, dcol)


def _dft_tables(seq, tf):
    idx = jnp.arange(seq, dtype=jnp.int32)
    prod = (idx[:, None] * idx[None, :]) % (2 * seq)
    ang = prod.astype(F32) * (math.pi / seq)
    nf = seq // tf
    cm = jnp.cos(ang).astype(BF16).reshape(seq, nf, 1, tf)
    sm = jnp.sin(ang).astype(BF16).reshape(seq, nf, 1, tf)
    qmat = jnp.concatenate([cm, sm], axis=2).reshape(seq, 2 * seq)
    rmat = qmat.T
    alt = (1.0 - 2.0 * (idx % 2).astype(F32)).astype(BF16)
    a1 = jnp.zeros((seq, LANES), BF16).at[:, 0].set(alt)
    return qmat, rmat, a1, a1.T


KV_CHUNK = 256


def _attn_kernel(lam_ref, q_ref, k_ref, v_ref, cos_ref, sa_ref, sb_ref, gs_ref, o_ref, krot_ref, vt_ref, s_ref,
                 *, tq, lam_init):
    qi = pl.program_id(2)

    def rope(x, c, sa, sb):
        return x * c + pltpu.roll(x, LANES - ROPE_DIM // 2, 1) * sa + pltpu.roll(x, ROPE_DIM // 2, 1) * sb

    seq = k_ref.shape[0]
    ck = min(KV_CHUNK, seq)
    n_chunks = seq // ck

    @pl.when(qi == 0)
    def _():
        krot_ref[...] = rope(k_ref[...].astype(F32), cos_ref[...], sa_ref[...], sb_ref[...]).astype(BF16)
        vt_ref[...] = v_ref[...].astype(F32).T.astype(BF16)

    rows = pl.ds(pl.multiple_of(qi * tq, tq), tq)
    q = rope(q_ref[...].astype(F32), cos_ref[rows, :], sa_ref[rows, :], sb_ref[rows, :])
    q = q * (HEAD_DIM ** -0.5 * LOG2E)
    lane = lax.broadcasted_iota(jnp.int32, q.shape, 1)
    qq = jnp.concatenate([jnp.where(lane < HEAD_DIM, q, 0.0), jnp.where(lane >= HEAD_DIM, q, 0.0)], axis=0)
    qq = qq.astype(BF16)

    m8 = jnp.full((8, 2 * tq), -jnp.inf, F32)
    for c in range(n_chunks):
        s_c = lax.dot_general(krot_ref[c * ck:(c + 1) * ck, :], qq, (((1,), (1,)), ((), ())),
                              preferred_element_type=F32)
        s_ref[c * ck:(c + 1) * ck, :] = s_c
        m8 = jnp.maximum(m8, jnp.max(s_c.reshape(ck // 8, 8, 2 * tq), axis=0))
    m = jnp.max(m8, axis=0, keepdims=True)

    l8 = jnp.zeros((8, 2 * tq), F32)
    ot = jnp.zeros((HEAD_W, 2 * tq), F32)
    for c in range(n_chunks):
        p_c = jnp.exp2(s_ref[c * ck:(c + 1) * ck, :] - m)
        l8 = l8 + jnp.sum(p_c.reshape(ck // 8, 8, 2 * tq), axis=0)
        ot = ot + jnp.dot(vt_ref[:, c * ck:(c + 1) * ck], p_c.astype(BF16), preferred_element_type=F32)
    l = jnp.sum(l8, axis=0, keepdims=True)

    lp = lam_ref[...]
    lam = (jnp.exp(jnp.sum(lp[0:1] * lp[1:2], axis=-1, keepdims=True))
           - jnp.exp(jnp.sum(lp[2:3] * lp[3:4], axis=-1, keepdims=True)) + lam_init)
    r = 1.0 / l
    o = (ot[:, :tq] * r[:, :tq] - ot[:, tq:] * (lam * r[:, tq:])).T
    o_ref[...] = (_rms(o, gs_ref[...]) * (1.0 - lam_init)).astype(BF16)


def _attention(z3, lam_params, g_subln, lam_init):
    b, seq, _ = z3.shape
    tq = min(256, seq)
    inv = ROPE_THETA ** (-jnp.arange(0, ROPE_DIM, 2, dtype=F32) / ROPE_DIM)
    ang = jnp.arange(seq, dtype=F32)[:, None] * inv[None, :]
    c8, s8 = jnp.cos(ang), jnp.sin(ang)
    half = ROPE_DIM // 2
    rest = HEAD_DIM - ROPE_DIM
    one, zero = jnp.ones((seq, rest), F32), jnp.zeros((seq, rest), F32)
    z8 = jnp.zeros((seq, half), F32)
    cos_t = jnp.tile(jnp.concatenate([c8, c8, one], axis=1), (1, 2))
    sa_t = jnp.tile(jnp.concatenate([-s8, z8, zero], axis=1), (1, 2))
    sb_t = jnp.tile(jnp.concatenate([z8, s8, zero], axis=1), (1, 2))
    qb, kb, vb = (HY_COLS // HEAD_W, (HY_COLS + ATT_WIDTH) // HEAD_W, (HY_COLS + 2 * ATT_WIDTH) // HEAD_W)
    return pl.pallas_call(
        functools.partial(_attn_kernel, tq=tq, lam_init=lam_init),
        out_shape=jax.ShapeDtypeStruct((b, seq, ATT_WIDTH), BF16),
        grid=(b, N_HEADS, seq // tq),
        in_specs=[
            _const_spec((4, HEAD_DIM)),
            pl.BlockSpec((None, tq, HEAD_W), lambda i, h, t: (i, t, qb + h)),
            pl.BlockSpec((None, seq, HEAD_W), lambda i, h, t: (i, 0, kb + h)),
            pl.BlockSpec((None, seq, HEAD_W), lambda i, h, t: (i, 0, vb + h)),
            _const_spec((seq, HEAD_W)), _const_spec((seq, HEAD_W)), _const_spec((seq, HEAD_W)),
            _const_spec((1, HEAD_W)),
        ],
        out_specs=pl.BlockSpec((None, tq, HEAD_W), lambda i, h, t: (i, t, h)),
        scratch_shapes=[pltpu.VMEM((seq, HEAD_W), BF16), pltpu.VMEM((HEAD_W, seq), BF16),
                        pltpu.VMEM((seq, 2 * tq), F32)],
        compiler_params=_cparams(("parallel", "parallel", "arbitrary"), 56),
        name="attention",
    )(lam_params, z3, z3, z3, cos_t, sa_t, sb_t, g_subln)


def _merge_kernel(yc_ref, x0_ref, ya_ref, g0_ref, g1_ref, x_ref, why_ref, wat_ref, wo_ref, gp_ref, o_ref):
    yh = (x0_ref[...].astype(F32) * yc_ref[...].astype(F32)).astype(BF16)
    a = jnp.dot(yh, why_ref[...], preferred_element_type=F32)
    b = jnp.dot(ya_ref[...], wat_ref[...], preferred_element_type=F32)
    m = g0_ref[...].astype(F32) * a + g1_ref[...].astype(F32) * b
    r = jnp.dot(m.astype(BF16), wo_ref[...], preferred_element_type=F32)
    o_ref[...] = x_ref[...] + _rms(r, gp_ref[...])


def _merge(yconv, x0c, yatt, z, x2d, w_hy, w_att, w_out, g_post):
    m, d = x2d.shape
    tm = min(256, m)
    row = lambda width, blk=0: pl.BlockSpec((tm, width), lambda i: (i, blk))
    gate_blk = GATE_COL0 // d
    return pl.pallas_call(
        _merge_kernel,
        out_shape=jax.ShapeDtypeStruct((m, d), F32),
        grid=(m // tm,),
        in_specs=[row(HY_CH), row(HY_CH), row(ATT_WIDTH), row(d, gate_blk), row(d, gate_blk + 1), row(d),
                  _const_spec((HY_CH, d)), _const_spec((ATT_WIDTH, d)), _const_spec((d, d)), _const_spec((1, d))],
        out_specs=row(d),
        compiler_params=_cparams(("parallel",), 56),
        name="merge",
    )(yconv, x0c, yatt, z, z, x2d, w_hy, w_att, w_out, g_post)


def _ffn_kernel(x_ref, p_ref, gpre_ref, wg_ref, wu_ref, wo_ref, gpost_ref, gple_ref, wpg_ref, wpi_ref,
                o_ref, h_ref, acc_ref):
    j = pl.program_id(1)

    @pl.when(j == 0)
    def _():
        _rms_rows_to(h_ref, x_ref, gpre_ref)
        acc_ref[...] = jnp.zeros_like(acc_ref)

    h = h_ref[...]
    gate = jnp.dot(h, wg_ref[...], preferred_element_type=F32)
    up = jnp.dot(h, wu_ref[...], preferred_element_type=F32)
    act = (gate * _sigmoid(gate) * up).astype(BF16)
    acc_ref[...] += jnp.dot(act, wo_ref[...], preferred_element_type=F32)

    @pl.when(j == pl.num_programs(1) - 1)
    def _():
        def body(rows):
            x2 = x_ref[rows, :] + _rms(acc_ref[rows, :], gpost_ref[...])
            e = jnp.dot(_rms(x2, gple_ref[...]).astype(BF16), wpg_ref[...], preferred_element_type=F32)
            pe = jnp.dot(p_ref[rows, :].astype(BF16), wpi_ref[...], preferred_element_type=F32)
            o_ref[rows, :] = x2 + pe * _sigmoid(e)
        _for_row_chunks(x_ref.shape[0], body, chunk=2 * ROW_CHUNK)


def _ffn_ple(x1, p2d, g_pre, w_in, w_out, g_post, g_ple, w_pg, w_pi):
    m, d = x1.shape
    tm = min(512, m)
    tf = 512
    nff = D_FF // tf
    return pl.pallas_call(
        _ffn_kernel,
        out_shape=jax.ShapeDtypeStruct((m, d), F32),
        grid=(m // tm, nff),
        in_specs=[
            pl.BlockSpec((tm, d), lambda i, j: (i, 0)),
            pl.BlockSpec((tm, PLE_DIM), lambda i, j: (i, 0)),
            _const_spec((1, d)),
            pl.BlockSpec((d, tf), lambda i, j: (0, j)),
            pl.BlockSpec((d, tf), lambda i, j: (0, nff + j)),
            pl.BlockSpec((tf, d), lambda i, j: (j, 0)),
            _const_spec((1, d)), _const_spec((1, d)),
            _const_spec((d, d)), _const_spec((PLE_DIM, d)),
        ],
        out_specs=pl.BlockSpec((tm, d), lambda i, j: (i, 0)),
        scratch_shapes=[pltpu.VMEM((tm, d), BF16), pltpu.VMEM((tm, d), F32)],
        compiler_params=_cparams(("parallel", "arbitrary"), 56),
        name="ffn_ple",
    )(x1, p2d, g_pre, w_in, w_in, w_out, g_post, g_ple, w_pg, w_pi)


def _layer(x, p, lam_init, wts):
    b, seq, d = x.shape
    x2d = x.reshape(b * seq, d)
    z = _in_proj(x2d, wts["g_mix_pre"], wts["w_in"], wts["b_gate"])
    z3 = z.reshape(b, seq, IN_COLS)

    tf = min(256, seq)
    qmat, rmat, a1, a2 = _dft_tables(seq, tf)
    ht, sums = _filt_mlp(seq, wts["filt_w1"], wts["filt_b1"], wts["filt_w2"], wts["filt_b2"],
                         wts["filt_freq"], wts["filt_w3"])
    u1, v2, kl = _filt_spec(ht, sums, qmat, a1, tf)
    x0c, wt = _hy_prep(z3, wts["conv_w"], wts["conv_b"])
    yconv = _long_conv(wt, qmat, rmat, u1, v2, a1, a2, kl, wts["hyena_d"], tf)

    yatt = _attention(z3, wts["lam"], wts["g_subln"], lam_init)

    x1 = _merge(yconv.reshape(b * seq, HY_CH), x0c.reshape(b * seq, HY_CH), yatt.reshape(b * seq, ATT_WIDTH),
                z, x2d, wts["w_hy_out"], wts["w_att_out"], wts["w_out"], wts["g_mix_post"])
    y = _ffn_ple(x1, p.reshape(b * seq, PLE_DIM), wts["g_ffn_pre"], wts["w_ffn_in"], wts["w_ffn_out"],
                 wts["g_ffn_post"], wts["g_ple"], wts["w_ple_gate"], wts["w_ple_in"])
    return y.reshape(b, seq, d)


def kernel(x_prompt, x_sample, p_prompt, p_sample, g_mix_pre, g_mix_post, g_ffn_pre, g_ffn_post, g_ple, w_in, b_gate, conv_w, conv_b, filt_w1, filt_b1, filt_w2, filt_b2, filt_freq, filt_w3, hyena_d, lam_q1, lam_k1, lam_q2, lam_k2, g_subln, w_hy_out, w_att_out, w_out, w_ffn_in, w_ffn_out, w_ple_in, w_ple_gate):
    depth = w_in.shape[0]
    xs = [x_prompt, x_sample]
    ps = [p_prompt, p_sample]
    for i in range(depth):
        lam_init = 0.8 - 0.6 * math.exp(-0.3 * i)
        wts = dict(
            g_mix_pre=g_mix_pre[i][None, :], g_mix_post=g_mix_post[i][None, :],
            g_ffn_pre=g_ffn_pre[i][None, :], g_ffn_post=g_ffn_post[i][None, :], g_ple=g_ple[i][None, :],
            w_in=w_in[i].astype(BF16), b_gate=b_gate[i].reshape(1, 2 * D_MODEL),
            conv_w=conv_w[i], conv_b=conv_b[i][None, :],
            filt_w1=filt_w1[i], filt_b1=filt_b1[i], filt_w2=filt_w2[i], filt_b2=filt_b2[i],
            filt_freq=filt_freq[i], filt_w3=filt_w3[i],
            hyena_d=jnp.broadcast_to(hyena_d[i][:, None], (HY_CH, LANES)),
            lam=jnp.stack([lam_q1[i], lam_k1[i], lam_q2[i], lam_k2[i]]),
            g_subln=g_subln[i][None, :],
            w_hy_out=w_hy_out[i].astype(BF16), w_att_out=w_att_out[i].astype(BF16),
            w_out=w_out[i].astype(BF16), w_ffn_in=w_ffn_in[i].astype(BF16),
            w_ffn_out=w_ffn_out[i].astype(BF16), w_ple_in=w_ple_in[i].astype(BF16),
            w_ple_gate=w_ple_gate[i].astype(BF16),
        )
        xs = [_layer(x, p[i], lam_init, wts) for x, p in zip(xs, ps)]
    return (xs[0], xs[1])
```

```python
import functools
import math

import jax
import jax.numpy as jnp
from jax import lax
from jax.experimental import pallas as pl
from jax.experimental.pallas import tpu as pltpu

F32 = jnp.float32
BF16 = jnp.bfloat16

D_MODEL = 2048
PLE_DIM = 256
NORM_EPS = 1e-6
HY_CH = D_MODEL // 2
FILT_BANDS = 16
FILT_EMB = 1 + 2 * FILT_BANDS
FILT_ORDER = 64
FILT_MAX_DECAY = math.log(1e-2) / 0.3
FILT_MIN_DECAY = math.log(1e-2) / 1.5
ATT_WIDTH = D_MODEL // 2
N_HEADS = 8
HEAD_DIM = ATT_WIDTH // (2 * N_HEADS)
HEAD_W = 2 * HEAD_DIM
ROPE_DIM = HEAD_DIM // 4
ROPE_THETA = 500000.0
D_FF = ((8 * D_MODEL + 3 * 256 - 1) // (3 * 256)) * 256
HY_COLS = 3 * HY_CH
GATE_COL0 = HY_COLS + 3 * ATT_WIDTH
IN_COLS = GATE_COL0 + 2 * D_MODEL
LANES = 128
LOG2E = math.log2(math.e)
MIB = 2 ** 20


def _cparams(sem, vmem_mib):
    return pltpu.CompilerParams(dimension_semantics=sem, vmem_limit_bytes=vmem_mib * MIB)


def _const_spec(shape):
    nd = len(shape)
    return pl.BlockSpec(shape, lambda *_: (0,) * nd, pipeline_mode=pl.Buffered(1))


def _rms(x, g):
    ms = jnp.mean(x * x, axis=-1, keepdims=True)
    return x * lax.rsqrt(ms + NORM_EPS) * g


def _sigmoid(x):
    return 1.0 / (1.0 + jnp.exp(-x))


ROW_CHUNK = 128


def _for_row_chunks(n_rows, body, chunk=ROW_CHUNK):
    chunk = min(chunk, n_rows)

    def step(c, carry):
        body(pl.ds(pl.multiple_of(c * chunk, chunk), chunk))
        return carry
    lax.fori_loop(0, n_rows // chunk, step, 0)


def _rms_rows_to(h_ref, x_ref, g_ref):
    def body(rows):
        h_ref[rows, :] = _rms(x_ref[rows, :], g_ref[...]).astype(h_ref.dtype)
    _for_row_chunks(x_ref.shape[0], body)


def _inproj_kernel(x_ref, g_ref, w_ref, b_ref, o_ref, h_ref, *, n_plain):
    j = pl.program_id(1)

    @pl.when(j == 0)
    def _():
        _rms_rows_to(h_ref, x_ref, g_ref)

    acc = jnp.dot(h_ref[...], w_ref[...], preferred_element_type=F32)
    o_ref[...] = jnp.where(j >= n_plain, _sigmoid(acc + b_ref[...]), acc).astype(BF16)


def _in_proj(x2d, g, w_bf, b_flat):
    m, d = x2d.shape
    n = w_bf.shape[1]
    tm = min(1024, m)
    tn = 1024
    n_plain = GATE_COL0 // tn
    return pl.pallas_call(
        functools.partial(_inproj_kernel, n_plain=n_plain),
        out_shape=jax.ShapeDtypeStruct((m, n), BF16),
        grid=(m // tm, n // tn),
        in_specs=[
            pl.BlockSpec((tm, d), lambda i, j: (i, 0)),
            pl.BlockSpec((1, d), lambda i, j: (0, 0)),
            pl.BlockSpec((d, tn), lambda i, j: (0, j)),
            pl.BlockSpec((1, tn), lambda i, j: (0, jnp.maximum(j - n_plain, 0))),
        ],
        out_specs=pl.BlockSpec((tm, tn), lambda i, j: (i, j)),
        scratch_shapes=[pltpu.VMEM((tm, d), BF16)],
        compiler_params=_cparams(("parallel", "arbitrary"), 48),
        name="in_proj",
    )(x2d, g, w_bf, b_flat)


def _filt_mlp_kernel(emb_ref, w1_ref, b1_ref, w2_ref, b2_ref, fr_ref, w3_ref, dl_ref, ht_ref, sum_ref):
    i = pl.program_id(0)
    hp = lax.Precision.HIGHEST
    emb = emb_ref[...]
    fr = fr_ref[...]
    h1 = jnp.sin(fr * (jnp.dot(emb, w1_ref[...], precision=hp, preferred_element_type=F32) + b1_ref[...]))
    h2 = jnp.sin(fr * (jnp.dot(h1, w2_ref[...], precision=hp, preferred_element_type=F32) + b2_ref[...]))
    h = jnp.dot(h2, w3_ref[...], precision=hp, preferred_element_type=F32)
    dec = jnp.exp(-emb[:, 0:1] * dl_ref[...])
    ht = jnp.concatenate([h[:, :HY_CH] * dec, h[:, HY_CH:] * dec], axis=1).T
    ht_ref[...] = ht
    ab = jnp.abs(ht)
    part = ab[:, 0:LANES]
    for c in range(1, ab.shape[1] // LANES):
        part = part + ab[:, c * LANES:(c + 1) * LANES]

    @pl.when(i == 0)
    def _():
        sum_ref[...] = part

    @pl.when(i > 0)
    def _():
        sum_ref[...] += part


def _filt_mlp(seq, w1, b1, w2, b2, freq, w3):
    t = jnp.linspace(0.0, 1.0, seq, dtype=F32)[:, None]
    wpos = 2.0 * math.pi * jnp.arange(seq, dtype=F32) / seq
    bands = jnp.linspace(1e-4, FILT_BANDS - 1, FILT_BANDS, dtype=F32)
    ang = wpos[:, None] * bands[None, :]
    emb = jnp.concatenate([t, jnp.cos(ang), -jnp.sin(ang)], axis=-1)
    emb = jnp.pad(emb, ((0, 0), (0, LANES - FILT_EMB)))
    pad_o = LANES - FILT_ORDER
    w1p = jnp.pad(w1, ((0, LANES - FILT_EMB), (0, pad_o)))
    w2p = jnp.pad(w2, ((0, pad_o), (0, pad_o)))
    w3p = jnp.pad(w3, ((0, pad_o), (0, 0)))
    b1p = jnp.pad(b1[None, :], ((0, 0), (0, pad_o)))
    b2p = jnp.pad(b2[None, :], ((0, 0), (0, pad_o)))
    frp = jnp.pad(freq[None, :], ((0, 0), (0, pad_o)))
    deltas = jnp.abs(jnp.linspace(FILT_MIN_DECAY, FILT_MAX_DECAY, HY_CH, dtype=F32))[None, :]
    tl = min(512, seq)
    return pl.pallas_call(
        _filt_mlp_kernel,
        out_shape=(jax.ShapeDtypeStruct((2 * HY_CH, seq), F32),
                   jax.ShapeDtypeStruct((2 * HY_CH, LANES), F32)),
        grid=(seq // tl,),
        in_specs=[
            pl.BlockSpec((tl, LANES), lambda i: (i, 0)),
            _const_spec((LANES, LANES)), _const_spec((1, LANES)),
            _const_spec((LANES, LANES)), _const_spec((1, LANES)),
            _const_spec((1, LANES)), _const_spec((LANES, 2 * HY_CH)),
            _const_spec((1, HY_CH)),
        ],
        out_specs=(pl.BlockSpec((2 * HY_CH, tl), lambda i: (0, i)),
                   pl.BlockSpec((2 * HY_CH, LANES), lambda i: (0, 0))),
        compiler_params=_cparams(("arbitrary",), 48),
        name="filt_mlp",
    )(emb, w1p, b1p, w2p, b2p, frp, w3p, deltas)


def _split_dot(s, m):
    hi = s.astype(BF16)
    lo = (s - hi.astype(F32)).astype(BF16)
    return jnp.dot(hi, m, preferred_element_type=F32) + jnp.dot(lo, m, preferred_element_type=F32)


def _filt_spec_kernel(hf_ref, hb_ref, sf_ref, sb_ref, q_ref, a1_ref, u1_ref, v2_ref, kl_ref, *, tf, seq):
    f = pl.program_id(1)
    norm = jnp.sum(sf_ref[...] + sb_ref[...], axis=1, keepdims=True) + NORM_EPS
    hf = hf_ref[...] / norm
    lane = lax.broadcasted_iota(jnp.int32, hf.shape, 1)
    hb = jnp.where(lane == 0, 0.0, hb_ref[...] / norm)
    s1 = hf + hb
    s2 = hf - hb
    c = _split_dot(s1, q_ref[:, :tf]) * (1.0 / seq)
    col = lax.broadcasted_iota(jnp.int32, c.shape, 1) + f * tf
    u1_ref[...] = jnp.where(col == 0, 0.5 * c, c)
    v2_ref[...] = _split_dot(s2, q_ref[:, tf:]) * (1.0 / seq)

    @pl.when(f == 0)
    def _():
        kl_ref[...] = _split_dot(s1, a1_ref[...]) * (0.5 / seq)


def _filt_spec(ht, sums, qmat, a1, tf):
    seq = ht.shape[1]
    tc = 256
    nc = HY_CH // tc
    return pl.pallas_call(
        functools.partial(_filt_spec_kernel, tf=tf, seq=seq),
        out_shape=(jax.ShapeDtypeStruct((HY_CH, seq), F32),
                   jax.ShapeDtypeStruct((HY_CH, seq), F32),
                   jax.ShapeDtypeStruct((HY_CH, LANES), F32)),
        grid=(nc, seq // tf),
        in_specs=[
            pl.BlockSpec((tc, seq), lambda c, f: (c, 0)),
            pl.BlockSpec((tc, seq), lambda c, f: (nc + c, 0)),
            pl.BlockSpec((tc, LANES), lambda c, f: (c, 0)),
            pl.BlockSpec((tc, LANES), lambda c, f: (nc + c, 0)),
            pl.BlockSpec((seq, 2 * tf), lambda c, f: (0, f)),
            pl.BlockSpec((seq, LANES), lambda c, f: (0, 0)),
        ],
        out_specs=(pl.BlockSpec((tc, tf), lambda c, f: (c, f)),
                   pl.BlockSpec((tc, tf), lambda c, f: (c, f)),
                   pl.BlockSpec((tc, LANES), lambda c, f: (c, 0))),
        compiler_params=_cparams(("parallel", "arbitrary"), 48),
        name="filt_spec",
    )(ht, ht, sums, sums, qmat, a1)


def _hyprep_kernel(u0_ref, u1_ref, u2_ref, w0_ref, w1_ref, w2_ref, b0_ref, b1_ref, b2_ref, x0_ref, wt_ref):
    seq = u0_ref.shape[0]
    row = lax.broadcasted_iota(jnp.int32, u0_ref.shape, 0)

    def sconv(u_ref, w_ref, b_ref):
        u = u_ref[...].astype(F32)
        um = jnp.where(row == 0, 0.0, pltpu.roll(u, 1, 0))
        up = jnp.where(row == seq - 1, 0.0, pltpu.roll(u, seq - 1, 0))
        w = w_ref[...]
        return b_ref[...] + um * w[0:1] + u * w[1:2] + up * w[2:3]

    x0_ref[...] = sconv(u0_ref, w0_ref, b0_ref).astype(BF16)
    prod = sconv(u1_ref, w1_ref, b1_ref) * sconv(u2_ref, w2_ref, b2_ref)
    wt_ref[...] = prod.T.astype(BF16)


def _hy_prep(z3, conv_w, conv_b):
    b, seq, _ = z3.shape
    tc = LANES
    nc = HY_CH // tc
    uspec = lambda g: pl.BlockSpec((None, seq, tc), lambda i, c: (i, 0, g * nc + c))
    wspec = lambda g: pl.BlockSpec((3, tc), lambda i, c: (0, g * nc + c))
    bspec = lambda g: pl.BlockSpec((1, tc), lambda i, c: (0, g * nc + c))
    return pl.pallas_call(
        _hyprep_kernel,
        out_shape=(jax.ShapeDtypeStruct((b, seq, HY_CH), BF16),
                   jax.ShapeDtypeStruct((b, HY_CH, seq), BF16)),
        grid=(b, nc),
        in_specs=[uspec(0), uspec(1), uspec(2), wspec(0), wspec(1), wspec(2), bspec(0), bspec(1), bspec(2)],
        out_specs=(pl.BlockSpec((None, seq, tc), lambda i, c: (i, 0, c)),
                   pl.BlockSpec((None, tc, seq), lambda i, c: (i, c, 0))),
        compiler_params=_cparams(("parallel", "parallel"), 48),
        name="hy_prep",
    )(z3, z3, z3, conv_w, conv_w, conv_w, conv_b, conv_b, conv_b)


def _conv_kernel(wt_ref, q_ref, r_ref, u1_ref, v2_ref, a1_ref, a2_ref, kl_ref, d_ref, o_ref, acc_ref, *, tf):
    f = pl.program_id(2)
    wt = wt_ref[...]

    @pl.when(f == 0)
    def _():
        xl = jnp.dot(wt, a1_ref[...], preferred_element_type=F32)
        yl = (xl * kl_ref[...]).astype(BF16)
        acc_ref[...] = jnp.dot(yl, a2_ref[...], preferred_element_type=F32)

    ap = jnp.dot(wt, q_ref[...], preferred_element_type=F32)
    a = ap[:, :tf]
    p = ap[:, tf:]
    u1 = u1_ref[...]
    v2 = v2_ref[...]
    z = jnp.concatenate([a * u1 - p * v2, p * u1 + a * v2], axis=1).astype(BF16)
    acc_ref[...] += jnp.dot(z, r_ref[...], preferred_element_type=F32)

    @pl.when(f == pl.num_programs(2) - 1)
    def _():
        y = acc_ref[...] + d_ref[:, 0:1] * wt.astype(F32)
        o_ref[...] = y.T.astype(BF16)


def _long_conv(wt, qmat, rmat, u1, v2, a1, a2, kl, dcol, tf):
    b, _, seq = wt.shape
    tc = 512
    return pl.pallas_call(
        functools.partial(_conv_kernel, tf=tf),
        out_shape=jax.ShapeDtypeStruct((b, seq, HY_CH), BF16),
        grid=(b, HY_CH // tc, seq // tf),
        in_specs=[
            pl.BlockSpec((None, tc, seq), lambda i, c, f: (i, c, 0)),
            pl.BlockSpec((seq, 2 * tf), lambda i, c, f: (0, f)),
            pl.BlockSpec((2 * tf, seq), lambda i, c, f: (f, 0)),
            pl.BlockSpec((tc, tf), lambda i, c, f: (c, f)),
            pl.BlockSpec((tc, tf), lambda i, c, f: (c, f)),
            _const_spec((seq, LANES)),
            _const_spec((LANES, seq)),
            pl.BlockSpec((tc, LANES), lambda i, c, f: (c, 0)),
            pl.BlockSpec((tc, LANES), lambda i, c, f: (c, 0)),
        ],
        out_specs=pl.BlockSpec((None, seq, tc), lambda i, c, f: (i, 0, c)),
        scratch_shapes=[pltpu.VMEM((tc, seq), F32)],
        compiler_params=_cparams(("parallel", "parallel", "arbitrary"), 56),
        name="long_conv",
    )(wt, qmat, rmat, u1, v2, a1, a2, kl, dcol)


def _dft_tables(seq, tf):
    idx = jnp.arange(seq, dtype=jnp.int32)
    nf = seq // tf

    def trig(freqs):
        ang = ((freqs[:, None] * idx[None, :]) % (2 * seq)).astype(F32) * (math.pi / seq)
        return jnp.cos(ang), jnp.sin(ang)

    ch, sh = trig(jnp.arange(nf, dtype=jnp.int32) * tf)
    cl, sl = trig(jnp.arange(tf, dtype=jnp.int32))
    r_cos = ch[:, None, :] * cl[None, :, :] - sh[:, None, :] * sl[None, :, :]
    r_sin = sh[:, None, :] * cl[None, :, :] + ch[:, None, :] * sl[None, :, :]
    rmat = jnp.stack([r_cos, r_sin], axis=1).astype(BF16).reshape(2 * seq, seq)
    cht, sht, clt, slt = ch.T, sh.T, cl.T, sl.T
    q_cos = cht[:, :, None] * clt[:, None, :] - sht[:, :, None] * slt[:, None, :]
    q_sin = sht[:, :, None] * clt[:, None, :] + cht[:, :, None] * slt[:, None, :]
    qmat = jnp.stack([q_cos, q_sin], axis=2).astype(BF16).reshape(seq, 2 * seq)
    alt = (1.0 - 2.0 * (idx % 2).astype(F32)).astype(BF16)
    a1 = jnp.zeros((seq, LANES), BF16).at[:, 0].set(alt)
    return qmat, rmat, a1, a1.T


KV_CHUNK = 256


def _attn_kernel(lam_ref, q_ref, k_ref, v_ref, cos_ref, sa_ref, sb_ref, gs_ref, o_ref, krot_ref, vt_ref,
                 sa_buf, sb_buf, ma_buf, mb_buf,
                 *, tq, lam_init):
    t = pl.program_id(2)
    n_q = pl.num_programs(2) - 1
    seq = k_ref.shape[0]
    ck = min(KV_CHUNK, seq)
    n_chunks = seq // ck

    def rope(x, c, sa, sb):
        return x * c + pltpu.roll(x, LANES - ROPE_DIM // 2, 1) * sa + pltpu.roll(x, ROPE_DIM // 2, 1) * sb

    buf_a = (sa_buf, ma_buf)
    buf_b = (sb_buf, mb_buf)

    def tile_work(score, finish):
        if score is not None:
            rows = pl.ds(pl.multiple_of(t * tq, tq), tq)
            q = rope(q_ref[...].astype(F32), cos_ref[rows, :], sa_ref[rows, :], sb_ref[rows, :])
            q = q * (HEAD_DIM ** -0.5 * LOG2E)
            lane = lax.broadcasted_iota(jnp.int32, q.shape, 1)
            qq = jnp.concatenate([jnp.where(lane < HEAD_DIM, q, 0.0), jnp.where(lane >= HEAD_DIM, q, 0.0)], axis=0)
            qq = qq.astype(BF16)
            m8 = jnp.full((8, 2 * tq), -jnp.inf, F32)
        if finish is not None:
            m = jnp.max(finish[1][...], axis=0, keepdims=True)
            l8 = jnp.zeros((8, 2 * tq), F32)
            ot = jnp.zeros((HEAD_W, 2 * tq), F32)
        for c in range(n_chunks):
            kv = slice(c * ck, (c + 1) * ck)
            if finish is not None:
                p_c = jnp.exp2(finish[0][kv, :] - m)
                l8 = l8 + jnp.sum(p_c.reshape(ck // 8, 8, 2 * tq), axis=0)
                ot = ot + jnp.dot(vt_ref[:, kv], p_c.astype(BF16), preferred_element_type=F32)
            if score is not None:
                s_c = lax.dot_general(krot_ref[kv, :], qq, (((1,), (1,)), ((), ())),
                                      preferred_element_type=F32)
                score[0][kv, :] = s_c
                m8 = jnp.maximum(m8, jnp.max(s_c.reshape(ck // 8, 8, 2 * tq), axis=0))
        if score is not None:
            score[1][...] = m8
        if finish is not None:
            l = jnp.sum(l8, axis=0, keepdims=True)
            lp = lam_ref[...]
            lam = (jnp.exp(jnp.sum(lp[0:1] * lp[1:2], axis=-1, keepdims=True))
                   - jnp.exp(jnp.sum(lp[2:3] * lp[3:4], axis=-1, keepdims=True)) + lam_init)
            r = 1.0 / l
            o = (ot[:, :tq] * r[:, :tq] - ot[:, tq:] * (lam * r[:, tq:])).T
            o_ref[...] = (_rms(o, gs_ref[...]) * (1.0 - lam_init)).astype(BF16)

    @pl.when(t == 0)
    def _():
        krot_ref[...] = rope(k_ref[...].astype(F32), cos_ref[...], sa_ref[...], sb_ref[...]).astype(BF16)
        vt_ref[...] = v_ref[...].astype(F32).T.astype(BF16)
        tile_work(buf_a, None)

    steady = jnp.logical_and(t > 0, t < n_q)

    @pl.when(jnp.logical_and(steady, t % 2 == 1))
    def _():
        tile_work(buf_b, buf_a)

    @pl.when(jnp.logical_and(steady, t % 2 == 0))
    def _():
        tile_work(buf_a, buf_b)

    @pl.when(jnp.logical_and(t == n_q, t % 2 == 1))
    def _():
        tile_work(None, buf_a)

    @pl.when(jnp.logical_and(t == n_q, t % 2 == 0))
    def _():
        tile_work(None, buf_b)


def _attention(z3, lam_params, g_subln, lam_init):
    b, seq, _ = z3.shape
    tq = min(256, seq)
    n_q = seq // tq
    inv = ROPE_THETA ** (-jnp.arange(0, ROPE_DIM, 2, dtype=F32) / ROPE_DIM)
    ang = jnp.arange(seq, dtype=F32)[:, None] * inv[None, :]
    c8, s8 = jnp.cos(ang), jnp.sin(ang)
    half = ROPE_DIM // 2
    rest = HEAD_DIM - ROPE_DIM
    one, zero = jnp.ones((seq, rest), F32), jnp.zeros((seq, rest), F32)
    z8 = jnp.zeros((seq, half), F32)
    cos_t = jnp.tile(jnp.concatenate([c8, c8, one], axis=1), (1, 2))
    sa_t = jnp.tile(jnp.concatenate([-s8, z8, zero], axis=1), (1, 2))
    sb_t = jnp.tile(jnp.concatenate([z8, s8, zero], axis=1), (1, 2))
    qb, kb, vb = (HY_COLS // HEAD_W, (HY_COLS + ATT_WIDTH) // HEAD_W, (HY_COLS + 2 * ATT_WIDTH) // HEAD_W)
    return pl.pallas_call(
        functools.partial(_attn_kernel, tq=tq, lam_init=lam_init),
        out_shape=jax.ShapeDtypeStruct((b, seq, ATT_WIDTH), BF16),
        grid=(b, N_HEADS, n_q + 1),
        in_specs=[
            _const_spec((4, HEAD_DIM)),
            pl.BlockSpec((None, tq, HEAD_W), lambda i, h, t: (i, jnp.minimum(t, n_q - 1), qb + h)),
            pl.BlockSpec((None, seq, HEAD_W), lambda i, h, t: (i, 0, kb + h)),
            pl.BlockSpec((None, seq, HEAD_W), lambda i, h, t: (i, 0, vb + h)),
            _const_spec((seq, HEAD_W)), _const_spec((seq, HEAD_W)), _const_spec((seq, HEAD_W)),
            _const_spec((1, HEAD_W)),
        ],
        out_specs=pl.BlockSpec((None, tq, HEAD_W), lambda i, h, t: (i, jnp.maximum(t - 1, 0), h)),
        scratch_shapes=[pltpu.VMEM((seq, HEAD_W), BF16), pltpu.VMEM((HEAD_W, seq), BF16),
                        pltpu.VMEM((seq, 2 * tq), F32), pltpu.VMEM((seq, 2 * tq), F32),
                        pltpu.VMEM((8, 2 * tq), F32), pltpu.VMEM((8, 2 * tq), F32)],
        compiler_params=_cparams(("parallel", "parallel", "arbitrary"), 56),
        name="attention",
    )(lam_params, z3, z3, z3, cos_t, sa_t, sb_t, g_subln)


def _merge_kernel(yc_ref, x0_ref, ya_ref, g0_ref, g1_ref, x_ref, why_ref, wat_ref, wo_ref, gp_ref, o_ref):
    yh = (x0_ref[...].astype(F32) * yc_ref[...].astype(F32)).astype(BF16)
    a = jnp.dot(yh, why_ref[...], preferred_element_type=F32)
    b = jnp.dot(ya_ref[...], wat_ref[...], preferred_element_type=F32)
    m = g0_ref[...].astype(F32) * a + g1_ref[...].astype(F32) * b
    r = jnp.dot(m.astype(BF16), wo_ref[...], preferred_element_type=F32)
    o_ref[...] = x_ref[...] + _rms(r, gp_ref[...])


def _merge(yconv, x0c, yatt, z, x2d, w_hy, w_att, w_out, g_post):
    m, d = x2d.shape
    tm = min(256, m)
    row = lambda width, blk=0: pl.BlockSpec((tm, width), lambda i: (i, blk))
    gate_blk = GATE_COL0 // d
    return pl.pallas_call(
        _merge_kernel,
        out_shape=jax.ShapeDtypeStruct((m, d), F32),
        grid=(m // tm,),
        in_specs=[row(HY_CH), row(HY_CH), row(ATT_WIDTH), row(d, gate_blk), row(d, gate_blk + 1), row(d),
                  _const_spec((HY_CH, d)), _const_spec((ATT_WIDTH, d)), _const_spec((d, d)), _const_spec((1, d))],
        out_specs=row(d),
        compiler_params=_cparams(("parallel",), 56),
        name="merge",
    )(yconv, x0c, yatt, z, z, x2d, w_hy, w_att, w_out, g_post)


def _ffn_kernel(x_ref, p_ref, gpre_ref, wg_ref, wu_ref, wo_ref, gpost_ref, gple_ref, wpg_ref, wpi_ref,
                o_ref, h_ref, acc_ref):
    j = pl.program_id(1)

    @pl.when(j == 0)
    def _():
        _rms_rows_to(h_ref, x_ref, gpre_ref)
        acc_ref[...] = jnp.zeros_like(acc_ref)

    h = h_ref[...]
    gate = jnp.dot(h, wg_ref[...], preferred_element_type=F32)
    up = jnp.dot(h, wu_ref[...], preferred_element_type=F32)
    act = (gate * _sigmoid(gate) * up).astype(BF16)
    acc_ref[...] += jnp.dot(act, wo_ref[...], preferred_element_type=F32)

    @pl.when(j == pl.num_programs(1) - 1)
    def _():
        def body(rows):
            x2 = x_ref[rows, :] + _rms(acc_ref[rows, :], gpost_ref[...])
            e = jnp.dot(_rms(x2, gple_ref[...]).astype(BF16), wpg_ref[...], preferred_element_type=F32)
            pe = jnp.dot(p_ref[rows, :].astype(BF16), wpi_ref[...], preferred_element_type=F32)
            o_ref[rows, :] = x2 + pe * _sigmoid(e)
        _for_row_chunks(x_ref.shape[0], body, chunk=2 * ROW_CHUNK)


def _ffn_ple(x1, p2d, g_pre, w_in, w_out, g_post, g_ple, w_pg, w_pi):
    m, d = x1.shape
    tm = min(512, m)
    tf = 512
    nff = D_FF // tf
    return pl.pallas_call(
        _ffn_kernel,
        out_shape=jax.ShapeDtypeStruct((m, d), F32),
        grid=(m // tm, nff),
        in_specs=[
            pl.BlockSpec((tm, d), lambda i, j: (i, 0)),
            pl.BlockSpec((tm, PLE_DIM), lambda i, j: (i, 0)),
            _const_spec((1, d)),
            pl.BlockSpec((d, tf), lambda i, j: (0, j)),
            pl.BlockSpec((d, tf), lambda i, j: (0, nff + j)),
            pl.BlockSpec((tf, d), lambda i, j: (j, 0)),
            _const_spec((1, d)), _const_spec((1, d)),
            _const_spec((d, d)), _const_spec((PLE_DIM, d)),
        ],
        out_specs=pl.BlockSpec((tm, d), lambda i, j: (i, 0)),
        scratch_shapes=[pltpu.VMEM((tm, d), BF16), pltpu.VMEM((tm, d), F32)],
        compiler_params=_cparams(("parallel", "arbitrary"), 56),
        name="ffn_ple",
    )(x1, p2d, g_pre, w_in, w_in, w_out, g_post, g_ple, w_pg, w_pi)


def _layer(x, p, lam_init, wts):
    b, seq, d = x.shape
    x2d = x.reshape(b * seq, d)
    z = _in_proj(x2d, wts["g_mix_pre"], wts["w_in"], wts["b_gate"])
    z3 = z.reshape(b, seq, IN_COLS)

    tf = min(256, seq)
    qmat, rmat, a1, a2 = _dft_tables(seq, tf)
    ht, sums = _filt_mlp(seq, wts["filt_w1"], wts["filt_b1"], wts["filt_w2"], wts["filt_b2"],
                         wts["filt_freq"], wts["filt_w3"])
    u1, v2, kl = _filt_spec(ht, sums, qmat, a1, tf)
    x0c, wt = _hy_prep(z3, wts["conv_w"], wts["conv_b"])
    yconv = _long_conv(wt, qmat, rmat, u1, v2, a1, a2, kl, wts["hyena_d"], tf)

    yatt = _attention(z3, wts["lam"], wts["g_subln"], lam_init)

    x1 = _merge(yconv.reshape(b * seq, HY_CH), x0c.reshape(b * seq, HY_CH), yatt.reshape(b * seq, ATT_WIDTH),
                z, x2d, wts["w_hy_out"], wts["w_att_out"], wts["w_out"], wts["g_mix_post"])
    y = _ffn_ple(x1, p.reshape(b * seq, PLE_DIM), wts["g_ffn_pre"], wts["w_ffn_in"], wts["w_ffn_out"],
                 wts["g_ffn_post"], wts["g_ple"], wts["w_ple_gate"], wts["w_ple_in"])
    return y.reshape(b, seq, d)


def kernel(x_prompt, x_sample, p_prompt, p_sample, g_mix_pre, g_mix_post, g_ffn_pre, g_ffn_post, g_ple, w_in, b_gate, conv_w, conv_b, filt_w1, filt_b1, filt_w2, filt_b2, filt_freq, filt_w3, hyena_d, lam_q1, lam_k1, lam_q2, lam_k2, g_subln, w_hy_out, w_att_out, w_out, w_ffn_in, w_ffn_out, w_ple_in, w_ple_gate):
    depth = w_in.shape[0]
    xs = [x_prompt, x_sample]
    ps = [p_prompt, p_sample]
    for i in range(depth):
        lam_init = 0.8 - 0.6 * math.exp(-0.3 * i)
        wts = dict(
            g_mix_pre=g_mix_pre[i][None, :], g_mix_post=g_mix_post[i][None, :],
            g_ffn_pre=g_ffn_pre[i][None, :], g_ffn_post=g_ffn_post[i][None, :], g_ple=g_ple[i][None, :],
            w_in=w_in[i].astype(BF16), b_gate=b_gate[i].reshape(1, 2 * D_MODEL),
            conv_w=conv_w[i], conv_b=conv_b[i][None, :],
            filt_w1=filt_w1[i], filt_b1=filt_b1[i], filt_w2=filt_w2[i], filt_b2=filt_b2[i],
            filt_freq=filt_freq[i], filt_w3=filt_w3[i],
            hyena_d=jnp.broadcast_to(hyena_d[i][:, None], (HY_CH, LANES)),
            lam=jnp.stack([lam_q1[i], lam_k1[i], lam_q2[i], lam_k2[i]]),
            g_subln=g_subln[i][None, :],
            w_hy_out=w_hy_out[i].astype(BF16), w_att_out=w_att_out[i].astype(BF16),
            w_out=w_out[i].astype(BF16), w_ffn_in=w_ffn_in[i].astype(BF16),
            w_ffn_out=w_ffn_out[i].astype(BF16), w_ple_in=w_ple_in[i].astype(BF16),
            w_ple_gate=w_ple_gate[i].astype(BF16),
        )
        xs = [_layer(x, p[i], lam_init, wts) for x, p in zip(xs, ps)]
    return (xs[0], xs[1])
```

```python
import functools
import math

import jax
import jax.numpy as jnp
from jax import lax
from jax.experimental import pallas as pl
from jax.experimental.pallas import tpu as pltpu

F32 = jnp.float32
BF16 = jnp.bfloat16

D_MODEL = 2048
PLE_DIM = 256
NORM_EPS = 1e-6
HY_CH = D_MODEL // 2
FILT_BANDS = 16
FILT_EMB = 1 + 2 * FILT_BANDS
FILT_ORDER = 64
FILT_MAX_DECAY = math.log(1e-2) / 0.3
FILT_MIN_DECAY = math.log(1e-2) / 1.5
ATT_WIDTH = D_MODEL // 2
N_HEADS = 8
HEAD_DIM = ATT_WIDTH // (2 * N_HEADS)
HEAD_W = 2 * HEAD_DIM
ROPE_DIM = HEAD_DIM // 4
ROPE_THETA = 500000.0
D_FF = ((8 * D_MODEL + 3 * 256 - 1) // (3 * 256)) * 256
HY_COLS = 3 * HY_CH
GATE_COL0 = HY_COLS + 3 * ATT_WIDTH
IN_COLS = GATE_COL0 + 2 * D_MODEL
LANES = 128
LOG2E = math.log2(math.e)
MIB = 2 ** 20


def _cparams(sem, vmem_mib):
    return pltpu.CompilerParams(dimension_semantics=sem, vmem_limit_bytes=vmem_mib * MIB)


def _const_spec(shape):
    nd = len(shape)
    return pl.BlockSpec(shape, lambda *_: (0,) * nd, pipeline_mode=pl.Buffered(1))


def _rms(x, g):
    ms = jnp.mean(x * x, axis=-1, keepdims=True)
    return x * lax.rsqrt(ms + NORM_EPS) * g


def _sigmoid(x):
    return 1.0 / (1.0 + jnp.exp(-x))


ROW_CHUNK = 128


def _for_row_chunks(n_rows, body, chunk=ROW_CHUNK):
    chunk = min(chunk, n_rows)

    def step(c, carry):
        body(pl.ds(pl.multiple_of(c * chunk, chunk), chunk))
        return carry
    lax.fori_loop(0, n_rows // chunk, step, 0)


def _rms_rows_to(h_ref, x_ref, g_ref):
    def body(rows):
        h_ref[rows, :] = _rms(x_ref[rows, :], g_ref[...]).astype(h_ref.dtype)
    _for_row_chunks(x_ref.shape[0], body)


def _inproj_kernel(x_ref, g_ref, w_ref, b_ref, o_ref, h_ref, *, n_plain):
    j = pl.program_id(1)

    @pl.when(j == 0)
    def _():
        _rms_rows_to(h_ref, x_ref, g_ref)

    acc = jnp.dot(h_ref[...], w_ref[...], preferred_element_type=F32)
    o_ref[...] = jnp.where(j >= n_plain, _sigmoid(acc + b_ref[...]), acc).astype(BF16)


def _in_proj(x2d, g, w_bf, b_flat):
    m, d = x2d.shape
    n = w_bf.shape[1]
    tm = min(1024, m)
    tn = 1024
    n_plain = GATE_COL0 // tn
    return pl.pallas_call(
        functools.partial(_inproj_kernel, n_plain=n_plain),
        out_shape=jax.ShapeDtypeStruct((m, n), BF16),
        grid=(m // tm, n // tn),
        in_specs=[
            pl.BlockSpec((tm, d), lambda i, j: (i, 0)),
            pl.BlockSpec((1, d), lambda i, j: (0, 0)),
            pl.BlockSpec((d, tn), lambda i, j: (0, j)),
            pl.BlockSpec((1, tn), lambda i, j: (0, jnp.maximum(j - n_plain, 0))),
        ],
        out_specs=pl.BlockSpec((tm, tn), lambda i, j: (i, j)),
        scratch_shapes=[pltpu.VMEM((tm, d), BF16)],
        compiler_params=_cparams(("parallel", "arbitrary"), 48),
        name="in_proj",
    )(x2d, g, w_bf, b_flat)


def _filt_mlp_kernel(emb_ref, w1_ref, b1_ref, w2_ref, b2_ref, fr_ref, w3_ref, dl_ref, ht_ref, sum_ref):
    i = pl.program_id(0)
    hp = lax.Precision.HIGHEST
    emb = emb_ref[...]
    fr = fr_ref[...]
    h1 = jnp.sin(fr * (jnp.dot(emb, w1_ref[...], precision=hp, preferred_element_type=F32) + b1_ref[...]))
    h2 = jnp.sin(fr * (jnp.dot(h1, w2_ref[...], precision=hp, preferred_element_type=F32) + b2_ref[...]))
    h = jnp.dot(h2, w3_ref[...], precision=hp, preferred_element_type=F32)
    dec = jnp.exp(-emb[:, 0:1] * dl_ref[...])
    ht = jnp.concatenate([h[:, :HY_CH] * dec, h[:, HY_CH:] * dec], axis=1).T
    ht_ref[...] = ht
    ab = jnp.abs(ht)
    part = ab[:, 0:LANES]
    for c in range(1, ab.shape[1] // LANES):
        part = part + ab[:, c * LANES:(c + 1) * LANES]

    @pl.when(i == 0)
    def _():
        sum_ref[...] = part

    @pl.when(i > 0)
    def _():
        sum_ref[...] += part


def _filt_mlp(seq, w1, b1, w2, b2, freq, w3):
    t = jnp.linspace(0.0, 1.0, seq, dtype=F32)[:, None]
    wpos = 2.0 * math.pi * jnp.arange(seq, dtype=F32) / seq
    bands = jnp.linspace(1e-4, FILT_BANDS - 1, FILT_BANDS, dtype=F32)
    ang = wpos[:, None] * bands[None, :]
    emb = jnp.concatenate([t, jnp.cos(ang), -jnp.sin(ang)], axis=-1)
    emb = jnp.pad(emb, ((0, 0), (0, LANES - FILT_EMB)))
    pad_o = LANES - FILT_ORDER
    w1p = jnp.pad(w1, ((0, LANES - FILT_EMB), (0, pad_o)))
    w2p = jnp.pad(w2, ((0, pad_o), (0, pad_o)))
    w3p = jnp.pad(w3, ((0, pad_o), (0, 0)))
    b1p = jnp.pad(b1[None, :], ((0, 0), (0, pad_o)))
    b2p = jnp.pad(b2[None, :], ((0, 0), (0, pad_o)))
    frp = jnp.pad(freq[None, :], ((0, 0), (0, pad_o)))
    deltas = jnp.abs(jnp.linspace(FILT_MIN_DECAY, FILT_MAX_DECAY, HY_CH, dtype=F32))[None, :]
    tl = min(512, seq)
    return pl.pallas_call(
        _filt_mlp_kernel,
        out_shape=(jax.ShapeDtypeStruct((2 * HY_CH, seq), F32),
                   jax.ShapeDtypeStruct((2 * HY_CH, LANES), F32)),
        grid=(seq // tl,),
        in_specs=[
            pl.BlockSpec((tl, LANES), lambda i: (i, 0)),
            _const_spec((LANES, LANES)), _const_spec((1, LANES)),
            _const_spec((LANES, LANES)), _const_spec((1, LANES)),
            _const_spec((1, LANES)), _const_spec((LANES, 2 * HY_CH)),
            _const_spec((1, HY_CH)),
        ],
        out_specs=(pl.BlockSpec((2 * HY_CH, tl), lambda i: (0, i)),
                   pl.BlockSpec((2 * HY_CH, LANES), lambda i: (0, 0))),
        compiler_params=_cparams(("arbitrary",), 48),
        name="filt_mlp",
    )(emb, w1p, b1p, w2p, b2p, frp, w3p, deltas)


def _dot_t(a, b):
    return lax.dot_general(a, b, (((1,), (1,)), ((), ())), preferred_element_type=F32)


def _filt_spec_kernel(hf_ref, hb_ref, sf_ref, sb_ref, r_ref, a1_ref, u1_ref, v2_ref, kl_ref, *, tf, seq):
    f = pl.program_id(1)
    norm = jnp.sum(sf_ref[...] + sb_ref[...], axis=1, keepdims=True) + NORM_EPS
    hf = hf_ref[...] / norm
    lane = lax.broadcasted_iota(jnp.int32, hf.shape, 1)
    hb = jnp.where(lane == 0, 0.0, hb_ref[...] / norm)
    s1 = (hf + hb).astype(BF16)
    s2 = (hf - hb).astype(BF16)
    c = _dot_t(s1, r_ref[:tf, :]) * (1.0 / seq)
    col = lax.broadcasted_iota(jnp.int32, c.shape, 1) + f * tf
    u1_ref[...] = jnp.where(col == 0, 0.5 * c, c)
    v2_ref[...] = _dot_t(s2, r_ref[tf:, :]) * (1.0 / seq)

    @pl.when(f == 0)
    def _():
        kl_ref[...] = jnp.dot(s1, a1_ref[...], preferred_element_type=F32) * (0.5 / seq)


def _filt_spec(ht, sums, rmat, a1, tf):
    seq = ht.shape[1]
    tc = 256
    nc = HY_CH // tc
    return pl.pallas_call(
        functools.partial(_filt_spec_kernel, tf=tf, seq=seq),
        out_shape=(jax.ShapeDtypeStruct((HY_CH, seq), F32),
                   jax.ShapeDtypeStruct((HY_CH, seq), F32),
                   jax.ShapeDtypeStruct((HY_CH, LANES), F32)),
        grid=(nc, seq // tf),
        in_specs=[
            pl.BlockSpec((tc, seq), lambda c, f: (c, 0)),
            pl.BlockSpec((tc, seq), lambda c, f: (nc + c, 0)),
            pl.BlockSpec((tc, LANES), lambda c, f: (c, 0)),
            pl.BlockSpec((tc, LANES), lambda c, f: (nc + c, 0)),
            pl.BlockSpec((2 * tf, seq), lambda c, f: (f, 0)),
            pl.BlockSpec((seq, LANES), lambda c, f: (0, 0)),
        ],
        out_specs=(pl.BlockSpec((tc, tf), lambda c, f: (c, f)),
                   pl.BlockSpec((tc, tf), lambda c, f: (c, f)),
                   pl.BlockSpec((tc, LANES), lambda c, f: (c, 0))),
        compiler_params=_cparams(("parallel", "arbitrary"), 48),
        name="filt_spec",
    )(ht, ht, sums, sums, rmat, a1)


def _hyprep_kernel(u0_ref, u1_ref, u2_ref, w0_ref, w1_ref, w2_ref, b0_ref, b1_ref, b2_ref, x0_ref, wt_ref):
    seq = u0_ref.shape[0]
    row = lax.broadcasted_iota(jnp.int32, u0_ref.shape, 0)

    def sconv(u_ref, w_ref, b_ref):
        u = u_ref[...].astype(F32)
        um = jnp.where(row == 0, 0.0, pltpu.roll(u, 1, 0))
        up = jnp.where(row == seq - 1, 0.0, pltpu.roll(u, seq - 1, 0))
        w = w_ref[...]
        return b_ref[...] + um * w[0:1] + u * w[1:2] + up * w[2:3]

    x0_ref[...] = sconv(u0_ref, w0_ref, b0_ref).astype(BF16)
    prod = sconv(u1_ref, w1_ref, b1_ref) * sconv(u2_ref, w2_ref, b2_ref)
    wt_ref[...] = prod.T.astype(BF16)


def _hy_prep(z3, conv_w, conv_b):
    b, seq, _ = z3.shape
    tc = LANES
    nc = HY_CH // tc
    uspec = lambda g: pl.BlockSpec((None, seq, tc), lambda i, c: (i, 0, g * nc + c))
    wspec = lambda g: pl.BlockSpec((3, tc), lambda i, c: (0, g * nc + c))
    bspec = lambda g: pl.BlockSpec((1, tc), lambda i, c: (0, g * nc + c))
    return pl.pallas_call(
        _hyprep_kernel,
        out_shape=(jax.ShapeDtypeStruct((b, seq, HY_CH), BF16),
                   jax.ShapeDtypeStruct((b, HY_CH, seq), BF16)),
        grid=(b, nc),
        in_specs=[uspec(0), uspec(1), uspec(2), wspec(0), wspec(1), wspec(2), bspec(0), bspec(1), bspec(2)],
        out_specs=(pl.BlockSpec((None, seq, tc), lambda i, c: (i, 0, c)),
                   pl.BlockSpec((None, tc, seq), lambda i, c: (i, c, 0))),
        compiler_params=_cparams(("parallel", "parallel"), 48),
        name="hy_prep",
    )(z3, z3, z3, conv_w, conv_w, conv_w, conv_b, conv_b, conv_b)


def _conv_kernel(wt_ref, r_ref, u1_ref, v2_ref, a1_ref, a2_ref, kl_ref, d_ref, o_ref, acc_ref, *, tf):
    f = pl.program_id(2)
    wt = wt_ref[...]

    @pl.when(f == 0)
    def _():
        xl = jnp.dot(wt, a1_ref[...], preferred_element_type=F32)
        yl = (xl * kl_ref[...]).astype(BF16)
        acc_ref[...] = jnp.dot(yl, a2_ref[...], preferred_element_type=F32)

    r = r_ref[...]
    ap = _dot_t(wt, r)
    a = ap[:, :tf]
    p = ap[:, tf:]
    u1 = u1_ref[...]
    v2 = v2_ref[...]
    z = jnp.concatenate([a * u1 - p * v2, p * u1 + a * v2], axis=1).astype(BF16)
    acc_ref[...] += jnp.dot(z, r, preferred_element_type=F32)

    @pl.when(f == pl.num_programs(2) - 1)
    def _():
        y = acc_ref[...] + d_ref[:, 0:1] * wt.astype(F32)
        o_ref[...] = y.T.astype(BF16)


def _long_conv(wt, rmat, u1, v2, a1, a2, kl, dcol, tf):
    b, _, seq = wt.shape
    tc = 512
    return pl.pallas_call(
        functools.partial(_conv_kernel, tf=tf),
        out_shape=jax.ShapeDtypeStruct((b, seq, HY_CH), BF16),
        grid=(b, HY_CH // tc, seq // tf),
        in_specs=[
            pl.BlockSpec((None, tc, seq), lambda i, c, f: (i, c, 0)),
            pl.BlockSpec((2 * tf, seq), lambda i, c, f: (f, 0)),
            pl.BlockSpec((tc, tf), lambda i, c, f: (c, f)),
            pl.BlockSpec((tc, tf), lambda i, c, f: (c, f)),
            _const_spec((seq, LANES)),
            _const_spec((LANES, seq)),
            pl.BlockSpec((tc, LANES), lambda i, c, f: (c, 0)),
            pl.BlockSpec((tc, LANES), lambda i, c, f: (c, 0)),
        ],
        out_specs=pl.BlockSpec((None, seq, tc), lambda i, c, f: (i, 0, c)),
        scratch_shapes=[pltpu.VMEM((tc, seq), F32)],
        compiler_params=_cparams(("parallel", "parallel", "arbitrary"), 56),
        name="long_conv",
    )(wt, rmat, u1, v2, a1, a2, kl, dcol)


def _dft_gen_kernel(ch_ref, sh_ref, cl_ref, sl_ref, r_ref, *, tf):
    ch, sh = ch_ref[...], sh_ref[...]
    cl, sl = cl_ref[...], sl_ref[...]
    r_ref[:tf, :] = (ch * cl - sh * sl).astype(BF16)
    r_ref[tf:, :] = (sh * cl + ch * sl).astype(BF16)


def _dft_tables(seq, tf):
    idx = jnp.arange(seq, dtype=jnp.int32)
    nf = seq // tf

    def trig(freqs):
        ang = ((freqs[:, None] * idx[None, :]) % (2 * seq)).astype(F32) * (math.pi / seq)
        return jnp.cos(ang), jnp.sin(ang)

    ch, sh = trig(jnp.arange(nf, dtype=jnp.int32) * tf)
    cl, sl = trig(jnp.arange(tf, dtype=jnp.int32))
    base_spec = pl.BlockSpec((None, 1, seq), lambda f: (f, 0, 0))
    rmat = pl.pallas_call(
        functools.partial(_dft_gen_kernel, tf=tf),
        out_shape=jax.ShapeDtypeStruct((2 * seq, seq), BF16),
        grid=(nf,),
        in_specs=[base_spec, base_spec, _const_spec((tf, seq)), _const_spec((tf, seq))],
        out_specs=pl.BlockSpec((2 * tf, seq), lambda f: (f, 0)),
        compiler_params=_cparams(("parallel",), 32),
        name="dft_tables",
    )(ch[:, None, :], sh[:, None, :], cl, sl)
    lane = lax.broadcasted_iota(jnp.int32, (seq, LANES), 1)
    alt = (1.0 - 2.0 * (idx % 2).astype(F32))[:, None]
    a1 = jnp.where(lane == 0, alt, 0.0).astype(BF16)
    return rmat, a1, a1.T


KV_CHUNK = 256


def _attn_kernel(lam_ref, q_ref, k_ref, v_ref, cos_ref, sa_ref, sb_ref, gs_ref, o_ref, krot_ref, vt_ref,
                 sa_buf, sb_buf, ma_buf, mb_buf,
                 *, tq, lam_init):
    t = pl.program_id(2)
    n_q = pl.num_programs(2) - 1
    seq = k_ref.shape[0]
    ck = min(KV_CHUNK, seq)
    n_chunks = seq // ck

    def rope(x, c, sa, sb):
        return x * c + pltpu.roll(x, LANES - ROPE_DIM // 2, 1) * sa + pltpu.roll(x, ROPE_DIM // 2, 1) * sb

    buf_a = (sa_buf, ma_buf)
    buf_b = (sb_buf, mb_buf)

    def tile_work(score, finish):
        if score is not None:
            rows = pl.ds(pl.multiple_of(t * tq, tq), tq)
            q = rope(q_ref[...].astype(F32), cos_ref[rows, :], sa_ref[rows, :], sb_ref[rows, :])
            q = q * (HEAD_DIM ** -0.5 * LOG2E)
            lane = lax.broadcasted_iota(jnp.int32, q.shape, 1)
            qq = jnp.concatenate([jnp.where(lane < HEAD_DIM, q, 0.0), jnp.where(lane >= HEAD_DIM, q, 0.0)], axis=0)
            qq = qq.astype(BF16)
            m8 = jnp.full((8, 2 * tq), -jnp.inf, F32)
        if finish is not None:
            m = jnp.max(finish[1][...], axis=0, keepdims=True)
            l8 = jnp.zeros((8, 2 * tq), F32)
            ot = jnp.zeros((HEAD_W, 2 * tq), F32)
        for c in range(n_chunks):
            kv = slice(c * ck, (c + 1) * ck)
            if finish is not None:
                p_c = jnp.exp2(finish[0][kv, :] - m)
                l8 = l8 + jnp.sum(p_c.reshape(ck // 8, 8, 2 * tq), axis=0)
                ot = ot + jnp.dot(vt_ref[:, kv], p_c.astype(BF16), preferred_element_type=F32)
            if score is not None:
                s_c = lax.dot_general(krot_ref[kv, :], qq, (((1,), (1,)), ((), ())),
                                      preferred_element_type=F32)
                score[0][kv, :] = s_c
                m8 = jnp.maximum(m8, jnp.max(s_c.reshape(ck // 8, 8, 2 * tq), axis=0))
        if score is not None:
            score[1][...] = m8
        if finish is not None:
            l = jnp.sum(l8, axis=0, keepdims=True)
            lp = lam_ref[...]
            lam = (jnp.exp(jnp.sum(lp[0:1] * lp[1:2], axis=-1, keepdims=True))
                   - jnp.exp(jnp.sum(lp[2:3] * lp[3:4], axis=-1, keepdims=True)) + lam_init)
            r = 1.0 / l
            o = (ot[:, :tq] * r[:, :tq] - ot[:, tq:] * (lam * r[:, tq:])).T
            o_ref[...] = (_rms(o, gs_ref[...]) * (1.0 - lam_init)).astype(BF16)

    @pl.when(t == 0)
    def _():
        krot_ref[...] = rope(k_ref[...].astype(F32), cos_ref[...], sa_ref[...], sb_ref[...]).astype(BF16)
        vt_ref[...] = v_ref[...].astype(F32).T.astype(BF16)
        tile_work(buf_a, None)

    steady = jnp.logical_and(t > 0, t < n_q)

    @pl.when(jnp.logical_and(steady, t % 2 == 1))
    def _():
        tile_work(buf_b, buf_a)

    @pl.when(jnp.logical_and(steady, t % 2 == 0))
    def _():
        tile_work(buf_a, buf_b)

    @pl.when(jnp.logical_and(t == n_q, t % 2 == 1))
    def _():
        tile_work(None, buf_a)

    @pl.when(jnp.logical_and(t == n_q, t % 2 == 0))
    def _():
        tile_work(None, buf_b)


def _attention(z3, lam_params, g_subln, lam_init):
    b, seq, _ = z3.shape
    tq = min(256, seq)
    n_q = seq // tq
    inv = ROPE_THETA ** (-jnp.arange(0, ROPE_DIM, 2, dtype=F32) / ROPE_DIM)
    ang = jnp.arange(seq, dtype=F32)[:, None] * inv[None, :]
    c8, s8 = jnp.cos(ang), jnp.sin(ang)
    half = ROPE_DIM // 2
    rest = HEAD_DIM - ROPE_DIM
    one, zero = jnp.ones((seq, rest), F32), jnp.zeros((seq, rest), F32)
    z8 = jnp.zeros((seq, half), F32)
    cos_t = jnp.tile(jnp.concatenate([c8, c8, one], axis=1), (1, 2))
    sa_t = jnp.tile(jnp.concatenate([-s8, z8, zero], axis=1), (1, 2))
    sb_t = jnp.tile(jnp.concatenate([z8, s8, zero], axis=1), (1, 2))
    qb, kb, vb = (HY_COLS // HEAD_W, (HY_COLS + ATT_WIDTH) // HEAD_W, (HY_COLS + 2 * ATT_WIDTH) // HEAD_W)
    return pl.pallas_call(
        functools.partial(_attn_kernel, tq=tq, lam_init=lam_init),
        out_shape=jax.ShapeDtypeStruct((b, seq, ATT_WIDTH), BF16),
        grid=(b, N_HEADS, n_q + 1),
        in_specs=[
            _const_spec((4, HEAD_DIM)),
            pl.BlockSpec((None, tq, HEAD_W), lambda i, h, t: (i, jnp.minimum(t, n_q - 1), qb + h)),
            pl.BlockSpec((None, seq, HEAD_W), lambda i, h, t: (i, 0, kb + h)),
            pl.BlockSpec((None, seq, HEAD_W), lambda i, h, t: (i, 0, vb + h)),
            _const_spec((seq, HEAD_W)), _const_spec((seq, HEAD_W)), _const_spec((seq, HEAD_W)),
            _const_spec((1, HEAD_W)),
        ],
        out_specs=pl.BlockSpec((None, tq, HEAD_W), lambda i, h, t: (i, jnp.maximum(t - 1, 0), h)),
        scratch_shapes=[pltpu.VMEM((seq, HEAD_W), BF16), pltpu.VMEM((HEAD_W, seq), BF16),
                        pltpu.VMEM((seq, 2 * tq), F32), pltpu.VMEM((seq, 2 * tq), F32),
                        pltpu.VMEM((8, 2 * tq), F32), pltpu.VMEM((8, 2 * tq), F32)],
        compiler_params=_cparams(("parallel", "parallel", "arbitrary"), 56),
        name="attention",
    )(lam_params, z3, z3, z3, cos_t, sa_t, sb_t, g_subln)


def _merge_kernel(yc_ref, x0_ref, ya_ref, g0_ref, g1_ref, x_ref, why_ref, wat_ref, wo_ref, gp_ref, o_ref):
    yh = (x0_ref[...].astype(F32) * yc_ref[...].astype(F32)).astype(BF16)
    a = jnp.dot(yh, why_ref[...], preferred_element_type=F32)
    b = jnp.dot(ya_ref[...], wat_ref[...], preferred_element_type=F32)
    m = g0_ref[...].astype(F32) * a + g1_ref[...].astype(F32) * b
    r = jnp.dot(m.astype(BF16), wo_ref[...], preferred_element_type=F32)
    o_ref[...] = x_ref[...] + _rms(r, gp_ref[...])


def _merge(yconv, x0c, yatt, z, x2d, w_hy, w_att, w_out, g_post):
    m, d = x2d.shape
    tm = min(256, m)
    row = lambda width, blk=0: pl.BlockSpec((tm, width), lambda i: (i, blk))
    gate_blk = GATE_COL0 // d
    return pl.pallas_call(
        _merge_kernel,
        out_shape=jax.ShapeDtypeStruct((m, d), F32),
        grid=(m // tm,),
        in_specs=[row(HY_CH), row(HY_CH), row(ATT_WIDTH), row(d, gate_blk), row(d, gate_blk + 1), row(d),
                  _const_spec((HY_CH, d)), _const_spec((ATT_WIDTH, d)), _const_spec((d, d)), _const_spec((1, d))],
        out_specs=row(d),
        compiler_params=_cparams(("parallel",), 56),
        name="merge",
    )(yconv, x0c, yatt, z, z, x2d, w_hy, w_att, w_out, g_post)


def _ffn_kernel(x_ref, p_ref, gpre_ref, wg_ref, wu_ref, wo_ref, gpost_ref, gple_ref, wpg_ref, wpi_ref,
                o_ref, h_ref, acc_ref):
    j = pl.program_id(1)

    @pl.when(j == 0)
    def _():
        _rms_rows_to(h_ref, x_ref, gpre_ref)
        acc_ref[...] = jnp.zeros_like(acc_ref)

    h = h_ref[...]
    gate = jnp.dot(h, wg_ref[...], preferred_element_type=F32)
    up = jnp.dot(h, wu_ref[...], preferred_element_type=F32)
    act = (gate * _sigmoid(gate) * up).astype(BF16)
    acc_ref[...] += jnp.dot(act, wo_ref[...], preferred_element_type=F32)

    @pl.when(j == pl.num_programs(1) - 1)
    def _():
        def body(rows):
            x2 = x_ref[rows, :] + _rms(acc_ref[rows, :], gpost_ref[...])
            e = jnp.dot(_rms(x2, gple_ref[...]).astype(BF16), wpg_ref[...], preferred_element_type=F32)
            pe = jnp.dot(p_ref[rows, :].astype(BF16), wpi_ref[...], preferred_element_type=F32)
            o_ref[rows, :] = x2 + pe * _sigmoid(e)
        _for_row_chunks(x_ref.shape[0], body, chunk=2 * ROW_CHUNK)


def _ffn_ple(x1, p2d, g_pre, w_in, w_out, g_post, g_ple, w_pg, w_pi):
    m, d = x1.shape
    tm = min(512, m)
    tf = 512
    nff = D_FF // tf
    return pl.pallas_call(
        _ffn_kernel,
        out_shape=jax.ShapeDtypeStruct((m, d), F32),
        grid=(m // tm, nff),
        in_specs=[
            pl.BlockSpec((tm, d), lambda i, j: (i, 0)),
            pl.BlockSpec((tm, PLE_DIM), lambda i, j: (i, 0)),
            _const_spec((1, d)),
            pl.BlockSpec((d, tf), lambda i, j: (0, j)),
            pl.BlockSpec((d, tf), lambda i, j: (0, nff + j)),
            pl.BlockSpec((tf, d), lambda i, j: (j, 0)),
            _const_spec((1, d)), _const_spec((1, d)),
            _const_spec((d, d)), _const_spec((PLE_DIM, d)),
        ],
        out_specs=pl.BlockSpec((tm, d), lambda i, j: (i, 0)),
        scratch_shapes=[pltpu.VMEM((tm, d), BF16), pltpu.VMEM((tm, d), F32)],
        compiler_params=_cparams(("parallel", "arbitrary"), 56),
        name="ffn_ple",
    )(x1, p2d, g_pre, w_in, w_in, w_out, g_post, g_ple, w_pg, w_pi)


def _layer(x, p, lam_init, wts):
    b, seq, d = x.shape
    x2d = x.reshape(b * seq, d)
    z = _in_proj(x2d, wts["g_mix_pre"], wts["w_in"], wts["b_gate"])
    z3 = z.reshape(b, seq, IN_COLS)

    tf = min(256, seq)
    rmat, a1, a2 = _dft_tables(seq, tf)
    ht, sums = _filt_mlp(seq, wts["filt_w1"], wts["filt_b1"], wts["filt_w2"], wts["filt_b2"],
                         wts["filt_freq"], wts["filt_w3"])
    u1, v2, kl = _filt_spec(ht, sums, rmat, a1, tf)
    x0c, wt = _hy_prep(z3, wts["conv_w"], wts["conv_b"])
    yconv = _long_conv(wt, rmat, u1, v2, a1, a2, kl, wts["hyena_d"], tf)

    yatt = _attention(z3, wts["lam"], wts["g_subln"], lam_init)

    x1 = _merge(yconv.reshape(b * seq, HY_CH), x0c.reshape(b * seq, HY_CH), yatt.reshape(b * seq, ATT_WIDTH),
                z, x2d, wts["w_hy_out"], wts["w_att_out"], wts["w_out"], wts["g_mix_post"])
    y = _ffn_ple(x1, p.reshape(b * seq, PLE_DIM), wts["g_ffn_pre"], wts["w_ffn_in"], wts["w_ffn_out"],
                 wts["g_ffn_post"], wts["g_ple"], wts["w_ple_gate"], wts["w_ple_in"])
    return y.reshape(b, seq, d)


def kernel(x_prompt, x_sample, p_prompt, p_sample, g_mix_pre, g_mix_post, g_ffn_pre, g_ffn_post, g_ple, w_in, b_gate, conv_w, conv_b, filt_w1, filt_b1, filt_w2, filt_b2, filt_freq, filt_w3, hyena_d, lam_q1, lam_k1, lam_q2, lam_k2, g_subln, w_hy_out, w_att_out, w_out, w_ffn_in, w_ffn_out, w_ple_in, w_ple_gate):
    depth = w_in.shape[0]
    xs = [x_prompt, x_sample]
    ps = [p_prompt, p_sample]
    for i in range(depth):
        lam_init = 0.8 - 0.6 * math.exp(-0.3 * i)
        wts = dict(
            g_mix_pre=g_mix_pre[i][None, :], g_mix_post=g_mix_post[i][None, :],
            g_ffn_pre=g_ffn_pre[i][None, :], g_ffn_post=g_ffn_post[i][None, :], g_ple=g_ple[i][None, :],
            w_in=w_in[i].astype(BF16), b_gate=b_gate[i].reshape(1, 2 * D_MODEL),
            conv_w=conv_w[i], conv_b=conv_b[i][None, :],
            filt_w1=filt_w1[i], filt_b1=filt_b1[i], filt_w2=filt_w2[i], filt_b2=filt_b2[i],
            filt_freq=filt_freq[i], filt_w3=filt_w3[i],
            hyena_d=jnp.broadcast_to(hyena_d[i][:, None], (HY_CH, LANES)),
            lam=jnp.stack([lam_q1[i], lam_k1[i], lam_q2[i], lam_k2[i]]),
            g_subln=g_subln[i][None, :],
            w_hy_out=w_hy_out[i].astype(BF16), w_att_out=w_att_out[i].astype(BF16),
            w_out=w_out[i].astype(BF16), w_ffn_in=w_ffn_in[i].astype(BF16),
            w_ffn_out=w_ffn_out[i].astype(BF16), w_ple_in=w_ple_in[i].astype(BF16),
            w_ple_gate=w_ple_gate[i].astype(BF16),
        )
        xs = [_layer(x, p[i], lam_init, wts) for x, p in zip(xs, ps)]
    return (xs[0], xs[1])
```

```python
import functools
import math

import jax
import jax.numpy as jnp
from jax import lax
from jax.experimental import pallas as pl
from jax.experimental.pallas import tpu as pltpu

F32 = jnp.float32
BF16 = jnp.bfloat16

D_MODEL = 2048
PLE_DIM = 256
NORM_EPS = 1e-6
HY_CH = D_MODEL // 2
FILT_BANDS = 16
FILT_EMB = 1 + 2 * FILT_BANDS
FILT_ORDER = 64
FILT_MAX_DECAY = math.log(1e-2) / 0.3
FILT_MIN_DECAY = math.log(1e-2) / 1.5
ATT_WIDTH = D_MODEL // 2
N_HEADS = 8
HEAD_DIM = ATT_WIDTH // (2 * N_HEADS)
HEAD_W = 2 * HEAD_DIM
ROPE_DIM = HEAD_DIM // 4
ROPE_THETA = 500000.0
D_FF = ((8 * D_MODEL + 3 * 256 - 1) // (3 * 256)) * 256
HY_COLS = 3 * HY_CH
GATE_COL0 = HY_COLS + 3 * ATT_WIDTH
IN_COLS = GATE_COL0 + 2 * D_MODEL
LANES = 128
LOG2E = math.log2(math.e)
MIB = 2 ** 20


def _cparams(sem, vmem_mib):
    return pltpu.CompilerParams(dimension_semantics=sem, vmem_limit_bytes=vmem_mib * MIB)


def _const_spec(shape):
    nd = len(shape)
    return pl.BlockSpec(shape, lambda *_: (0,) * nd, pipeline_mode=pl.Buffered(1))


def _rms(x, g):
    ms = jnp.mean(x * x, axis=-1, keepdims=True)
    return x * lax.rsqrt(ms + NORM_EPS) * g


def _sigmoid(x):
    return 1.0 / (1.0 + jnp.exp(-x))


ROW_CHUNK = 128


def _for_row_chunks(n_rows, body, chunk=ROW_CHUNK, unrolled=False):
    chunk = min(chunk, n_rows)
    if unrolled:
        for c in range(n_rows // chunk):
            body(pl.ds(c * chunk, chunk))
        return

    def step(c, carry):
        body(pl.ds(pl.multiple_of(c * chunk, chunk), chunk))
        return carry
    lax.fori_loop(0, n_rows // chunk, step, 0)


def _rms_rows_to(h_ref, x_ref, g_ref):
    def body(rows):
        h_ref[rows, :] = _rms(x_ref[rows, :], g_ref[...]).astype(h_ref.dtype)
    _for_row_chunks(x_ref.shape[0], body)


def _inproj_kernel(x_ref, g_ref, w_ref, b_ref, o_ref, h_ref, *, n_plain):
    j = pl.program_id(1)

    @pl.when(j == 0)
    def _():
        _rms_rows_to(h_ref, x_ref, g_ref)

    acc = jnp.dot(h_ref[...], w_ref[...], preferred_element_type=F32)
    o_ref[...] = jnp.where(j >= n_plain, _sigmoid(acc + b_ref[...]), acc).astype(BF16)


def _in_proj(x2d, g, w_bf, b_flat):
    m, d = x2d.shape
    n = w_bf.shape[1]
    tm = min(1024, m)
    tn = 1024
    n_plain = GATE_COL0 // tn
    return pl.pallas_call(
        functools.partial(_inproj_kernel, n_plain=n_plain),
        out_shape=jax.ShapeDtypeStruct((m, n), BF16),
        grid=(m // tm, n // tn),
        in_specs=[
            pl.BlockSpec((tm, d), lambda i, j: (i, 0)),
            pl.BlockSpec((1, d), lambda i, j: (0, 0)),
            pl.BlockSpec((d, tn), lambda i, j: (0, j)),
            pl.BlockSpec((1, tn), lambda i, j: (0, jnp.maximum(j - n_plain, 0))),
        ],
        out_specs=pl.BlockSpec((tm, tn), lambda i, j: (i, j)),
        scratch_shapes=[pltpu.VMEM((tm, d), BF16)],
        compiler_params=_cparams(("parallel", "arbitrary"), 48),
        name="in_proj",
    )(x2d, g, w_bf, b_flat)


def _filt_mlp_kernel(emb_ref, w1_ref, b1_ref, w2_ref, b2_ref, fr_ref, w3_ref, dl_ref, ht_ref, sum_ref):
    i = pl.program_id(0)
    hp = lax.Precision.HIGHEST
    emb = emb_ref[...]
    fr = fr_ref[...]
    h1 = jnp.sin(fr * (jnp.dot(emb, w1_ref[...], precision=hp, preferred_element_type=F32) + b1_ref[...]))
    h2 = jnp.sin(fr * (jnp.dot(h1, w2_ref[...], precision=hp, preferred_element_type=F32) + b2_ref[...]))
    h = jnp.dot(h2, w3_ref[...], precision=hp, preferred_element_type=F32)
    dec = jnp.exp(-emb[:, 0:1] * dl_ref[...])
    ht = jnp.concatenate([h[:, :HY_CH] * dec, h[:, HY_CH:] * dec], axis=1).T
    ht_ref[...] = ht
    ab = jnp.abs(ht)
    part = ab[:, 0:LANES]
    for c in range(1, ab.shape[1] // LANES):
        part = part + ab[:, c * LANES:(c + 1) * LANES]

    @pl.when(i == 0)
    def _():
        sum_ref[...] = part

    @pl.when(i > 0)
    def _():
        sum_ref[...] += part


def _filt_mlp(seq, w1, b1, w2, b2, freq, w3):
    t = jnp.linspace(0.0, 1.0, seq, dtype=F32)[:, None]
    wpos = 2.0 * math.pi * jnp.arange(seq, dtype=F32) / seq
    bands = jnp.linspace(1e-4, FILT_BANDS - 1, FILT_BANDS, dtype=F32)
    ang = wpos[:, None] * bands[None, :]
    emb = jnp.concatenate([t, jnp.cos(ang), -jnp.sin(ang)], axis=-1)
    emb = jnp.pad(emb, ((0, 0), (0, LANES - FILT_EMB)))
    pad_o = LANES - FILT_ORDER
    w1p = jnp.pad(w1, ((0, LANES - FILT_EMB), (0, pad_o)))
    w2p = jnp.pad(w2, ((0, pad_o), (0, pad_o)))
    w3p = jnp.pad(w3, ((0, pad_o), (0, 0)))
    b1p = jnp.pad(b1[None, :], ((0, 0), (0, pad_o)))
    b2p = jnp.pad(b2[None, :], ((0, 0), (0, pad_o)))
    frp = jnp.pad(freq[None, :], ((0, 0), (0, pad_o)))
    deltas = jnp.abs(jnp.linspace(FILT_MIN_DECAY, FILT_MAX_DECAY, HY_CH, dtype=F32))[None, :]
    tl = min(512, seq)
    return pl.pallas_call(
        _filt_mlp_kernel,
        out_shape=(jax.ShapeDtypeStruct((2 * HY_CH, seq), F32),
                   jax.ShapeDtypeStruct((2 * HY_CH, LANES), F32)),
        grid=(seq // tl,),
        in_specs=[
            pl.BlockSpec((tl, LANES), lambda i: (i, 0)),
            _const_spec((LANES, LANES)), _const_spec((1, LANES)),
            _const_spec((LANES, LANES)), _const_spec((1, LANES)),
            _const_spec((1, LANES)), _const_spec((LANES, 2 * HY_CH)),
            _const_spec((1, HY_CH)),
        ],
        out_specs=(pl.BlockSpec((2 * HY_CH, tl), lambda i: (0, i)),
                   pl.BlockSpec((2 * HY_CH, LANES), lambda i: (0, 0))),
        compiler_params=_cparams(("arbitrary",), 48),
        name="filt_mlp",
    )(emb, w1p, b1p, w2p, b2p, frp, w3p, deltas)


def _dot_t(a, b):
    return lax.dot_general(a, b, (((1,), (1,)), ((), ())), preferred_element_type=F32)


def _filt_spec_kernel(hf_ref, hb_ref, sf_ref, sb_ref, r_ref, u1_ref, v2_ref, kl_ref, *, tf, seq):
    f = pl.program_id(1)
    norm = jnp.sum(sf_ref[...] + sb_ref[...], axis=1, keepdims=True) + NORM_EPS
    hf = hf_ref[...] / norm
    lane = lax.broadcasted_iota(jnp.int32, hf.shape, 1)
    hb = jnp.where(lane == 0, 0.0, hb_ref[...] / norm)
    s1 = (hf + hb).astype(BF16)
    s2 = (hf - hb).astype(BF16)
    c = _dot_t(s1, r_ref[:tf, :]) * (1.0 / seq)
    col = lax.broadcasted_iota(jnp.int32, c.shape, 1) + f * tf
    u1_ref[...] = jnp.where(col == 0, 0.5 * c, c)
    v2_ref[...] = jnp.where(col == 0, 0.0, _dot_t(s2, r_ref[tf:, :]) * (1.0 / seq))

    @pl.when(f == 0)
    def _():
        nyq = _dot_t(s1, r_ref[tf:, :])[:, 0:1] * (0.5 / seq)
        kl_ref[...] = jnp.broadcast_to(nyq, kl_ref.shape)


def _filt_spec(ht, sums, rmat, tf):
    seq = ht.shape[1]
    tc = 256
    nc = HY_CH // tc
    return pl.pallas_call(
        functools.partial(_filt_spec_kernel, tf=tf, seq=seq),
        out_shape=(jax.ShapeDtypeStruct((HY_CH, seq), F32),
                   jax.ShapeDtypeStruct((HY_CH, seq), F32),
                   jax.ShapeDtypeStruct((HY_CH, LANES), F32)),
        grid=(nc, seq // tf),
        in_specs=[
            pl.BlockSpec((tc, seq), lambda c, f: (c, 0)),
            pl.BlockSpec((tc, seq), lambda c, f: (nc + c, 0)),
            pl.BlockSpec((tc, LANES), lambda c, f: (c, 0)),
            pl.BlockSpec((tc, LANES), lambda c, f: (nc + c, 0)),
            pl.BlockSpec((2 * tf, seq), lambda c, f: (f, 0)),
        ],
        out_specs=(pl.BlockSpec((tc, tf), lambda c, f: (c, f)),
                   pl.BlockSpec((tc, tf), lambda c, f: (c, f)),
                   pl.BlockSpec((tc, LANES), lambda c, f: (c, 0))),
        compiler_params=_cparams(("parallel", "arbitrary"), 48),
        name="filt_spec",
    )(ht, ht, sums, sums, rmat)


def _hyprep_kernel(u0_ref, u1_ref, u2_ref, w0_ref, w1_ref, w2_ref, b0_ref, b1_ref, b2_ref, x0_ref, wt_ref):
    seq = u0_ref.shape[0]
    row = lax.broadcasted_iota(jnp.int32, u0_ref.shape, 0)

    def sconv(u_ref, w_ref, b_ref):
        u = u_ref[...].astype(F32)
        um = jnp.where(row == 0, 0.0, pltpu.roll(u, 1, 0))
        up = jnp.where(row == seq - 1, 0.0, pltpu.roll(u, seq - 1, 0))
        w = w_ref[...]
        return b_ref[...] + um * w[0:1] + u * w[1:2] + up * w[2:3]

    x0_ref[...] = sconv(u0_ref, w0_ref, b0_ref).astype(BF16)
    prod = sconv(u1_ref, w1_ref, b1_ref) * sconv(u2_ref, w2_ref, b2_ref)
    wt_ref[...] = prod.T.astype(BF16)


def _hy_prep(z3, conv_w, conv_b):
    b, seq, _ = z3.shape
    tc = LANES
    nc = HY_CH // tc
    uspec = lambda g: pl.BlockSpec((None, seq, tc), lambda i, c: (i, 0, g * nc + c))
    wspec = lambda g: pl.BlockSpec((3, tc), lambda i, c: (0, g * nc + c))
    bspec = lambda g: pl.BlockSpec((1, tc), lambda i, c: (0, g * nc + c))
    return pl.pallas_call(
        _hyprep_kernel,
        out_shape=(jax.ShapeDtypeStruct((b, seq, HY_CH), BF16),
                   jax.ShapeDtypeStruct((b, HY_CH, seq), BF16)),
        grid=(b, nc),
        in_specs=[uspec(0), uspec(1), uspec(2), wspec(0), wspec(1), wspec(2), bspec(0), bspec(1), bspec(2)],
        out_specs=(pl.BlockSpec((None, seq, tc), lambda i, c: (i, 0, c)),
                   pl.BlockSpec((None, tc, seq), lambda i, c: (i, c, 0))),
        compiler_params=_cparams(("parallel", "parallel"), 48),
        name="hy_prep",
    )(z3, z3, z3, conv_w, conv_w, conv_w, conv_b, conv_b, conv_b)


def _conv_kernel(wt_ref, ra_ref, rb_ref, u1_ref, v2_ref, kl_ref, d_ref, o_ref, acc_ref, za_ref, zb_ref, *, tf):
    t = pl.program_id(2)
    n_f = pl.num_programs(2) - 1

    def work(z_out, z_in, first=False):
        if z_in is not None:
            acc_ref[...] += jnp.dot(z_in[...], rb_ref[...], preferred_element_type=F32)
        if z_out is not None:
            ap = _dot_t(wt_ref[...], ra_ref[...])
            a = ap[:, :tf]
            p = ap[:, tf:]
            u1 = u1_ref[...]
            v2 = v2_ref[...]
            z2 = p * u1 + a * v2
            if first:
                col = lax.broadcasted_iota(jnp.int32, z2.shape, 1)
                z2 = jnp.where(col == 0, p * kl_ref[:, 0:1], z2)
            z_out[...] = jnp.concatenate([a * u1 - p * v2, z2], axis=1).astype(BF16)

    @pl.when(t == 0)
    def _():
        acc_ref[...] = jnp.zeros_like(acc_ref)
        work(za_ref, None, first=True)

    steady = jnp.logical_and(t > 0, t < n_f)

    @pl.when(jnp.logical_and(steady, t % 2 == 1))
    def _():
        work(zb_ref, za_ref)

    @pl.when(jnp.logical_and(steady, t % 2 == 0))
    def _():
        work(za_ref, zb_ref)

    def finish(z_in):
        work(None, z_in)
        y = acc_ref[...] + d_ref[:, 0:1] * wt_ref[...].astype(F32)
        o_ref[...] = y.T.astype(BF16)

    @pl.when(jnp.logical_and(t == n_f, t % 2 == 1))
    def _():
        finish(za_ref)

    @pl.when(jnp.logical_and(t == n_f, t % 2 == 0))
    def _():
        finish(zb_ref)


def _long_conv(wt, rmat, u1, v2, kl, dcol, tf):
    b, _, seq = wt.shape
    tc = 512
    n_f = seq // tf
    return pl.pallas_call(
        functools.partial(_conv_kernel, tf=tf),
        out_shape=jax.ShapeDtypeStruct((b, seq, HY_CH), BF16),
        grid=(b, HY_CH // tc, n_f + 1),
        in_specs=[
            pl.BlockSpec((None, tc, seq), lambda i, c, t: (i, c, 0)),
            pl.BlockSpec((2 * tf, seq), lambda i, c, t: (jnp.minimum(t, n_f - 1), 0)),
            pl.BlockSpec((2 * tf, seq), lambda i, c, t: (jnp.maximum(t - 1, 0), 0)),
            pl.BlockSpec((tc, tf), lambda i, c, t: (c, jnp.minimum(t, n_f - 1))),
            pl.BlockSpec((tc, tf), lambda i, c, t: (c, jnp.minimum(t, n_f - 1))),
            pl.BlockSpec((tc, LANES), lambda i, c, t: (c, 0)),
            pl.BlockSpec((tc, LANES), lambda i, c, t: (c, 0)),
        ],
        out_specs=pl.BlockSpec((None, seq, tc), lambda i, c, t: (i, 0, c)),
        scratch_shapes=[pltpu.VMEM((tc, seq), F32), pltpu.VMEM((tc, 2 * tf), BF16), pltpu.VMEM((tc, 2 * tf), BF16)],
        compiler_params=_cparams(("parallel", "parallel", "arbitrary"), 56),
        name="long_conv",
    )(wt, rmat, rmat, u1, v2, kl, dcol)


def _dft_gen_kernel(ch_ref, sh_ref, cl_ref, sl_ref, r_ref, *, tf):
    ch, sh = ch_ref[...], sh_ref[...]
    cl, sl = cl_ref[...], sl_ref[...]
    r_ref[:tf, :] = (ch * cl - sh * sl).astype(BF16)
    sin = sh * cl + ch * sl
    row = lax.broadcasted_iota(jnp.int32, sin.shape, 0)
    lane = lax.broadcasted_iota(jnp.int32, sin.shape, 1)
    alt = (1 - 2 * (lane % 2)).astype(F32)
    nyq_row = jnp.logical_and(row == 0, pl.program_id(0) == 0)
    r_ref[tf:, :] = jnp.where(nyq_row, alt, sin).astype(BF16)


def _dft_tables(seq, tf):
    idx = jnp.arange(seq, dtype=jnp.int32)
    nf = seq // tf

    def trig(freqs):
        ang = ((freqs[:, None] * idx[None, :]) % (2 * seq)).astype(F32) * (math.pi / seq)
        return jnp.cos(ang), jnp.sin(ang)

    ch, sh = trig(jnp.arange(nf, dtype=jnp.int32) * tf)
    cl, sl = trig(jnp.arange(tf, dtype=jnp.int32))
    base_spec = pl.BlockSpec((None, 1, seq), lambda f: (f, 0, 0))
    return pl.pallas_call(
        functools.partial(_dft_gen_kernel, tf=tf),
        out_shape=jax.ShapeDtypeStruct((2 * seq, seq), BF16),
        grid=(nf,),
        in_specs=[base_spec, base_spec, _const_spec((tf, seq)), _const_spec((tf, seq))],
        out_specs=pl.BlockSpec((2 * tf, seq), lambda f: (f, 0)),
        compiler_params=_cparams(("parallel",), 32),
        name="dft_tables",
    )(ch[:, None, :], sh[:, None, :], cl, sl)


KV_CHUNK = 256


def _attn_kernel(lam_ref, q_ref, k_ref, v_ref, cos_ref, sa_ref, sb_ref, gs_ref, o_ref, krot_ref, vt_ref,
                 sa_buf, sb_buf, ma_buf, mb_buf,
                 *, tq, lam_init):
    t = pl.program_id(2)
    n_q = pl.num_programs(2) - 1
    seq = k_ref.shape[0]
    ck = min(KV_CHUNK, seq)
    n_chunks = seq // ck

    def rope(x, c, sa, sb):
        return x * c + pltpu.roll(x, LANES - ROPE_DIM // 2, 1) * sa + pltpu.roll(x, ROPE_DIM // 2, 1) * sb

    buf_a = (sa_buf, ma_buf)
    buf_b = (sb_buf, mb_buf)

    def tile_work(score, finish):
        if score is not None:
            rows = pl.ds(pl.multiple_of(t * tq, tq), tq)
            q = rope(q_ref[...].astype(F32), cos_ref[rows, :], sa_ref[rows, :], sb_ref[rows, :])
            q = q * (HEAD_DIM ** -0.5 * LOG2E)
            lane = lax.broadcasted_iota(jnp.int32, q.shape, 1)
            qq = jnp.concatenate([jnp.where(lane < HEAD_DIM, q, 0.0), jnp.where(lane >= HEAD_DIM, q, 0.0)], axis=0)
            qq = qq.astype(BF16)
            m8 = jnp.full((8, 2 * tq), -jnp.inf, F32)
        if finish is not None:
            m = jnp.max(finish[1][...], axis=0, keepdims=True)
            l8 = jnp.zeros((8, 2 * tq), F32)
            ot = jnp.zeros((HEAD_W, 2 * tq), F32)

        def score_chunk(c, m8):
            kv = slice(c * ck, (c + 1) * ck)
            s_c = _dot_t(krot_ref[kv, :], qq)
            score[0][kv, :] = s_c
            return jnp.maximum(m8, jnp.max(s_c.reshape(ck // 8, 8, 2 * tq), axis=0))

        n_scored = 0
        for c in range(n_chunks):
            kv = slice(c * ck, (c + 1) * ck)
            if finish is not None:
                p_c = jnp.exp2(finish[0][kv, :] - m)
                l8 = l8 + jnp.sum(p_c.reshape(ck // 8, 8, 2 * tq), axis=0)
                ot = ot + jnp.dot(vt_ref[:, kv], p_c.astype(BF16), preferred_element_type=F32)
            if score is not None and (finish is None or c % 3 != 2):
                m8 = score_chunk(n_scored, m8)
                n_scored += 1
        if finish is not None:
            l = jnp.sum(l8, axis=0, keepdims=True)
            lp = lam_ref[...]
            lam = (jnp.exp(jnp.sum(lp[0:1] * lp[1:2], axis=-1, keepdims=True))
                   - jnp.exp(jnp.sum(lp[2:3] * lp[3:4], axis=-1, keepdims=True)) + lam_init)
            r = 1.0 / l
            o = (ot[:, :tq] * r[:, :tq] - ot[:, tq:] * (lam * r[:, tq:])).T
            o_ref[...] = (_rms(o, gs_ref[...]) * (1.0 - lam_init)).astype(BF16)
        if score is not None:
            for c in range(n_scored, n_chunks):
                m8 = score_chunk(c, m8)
            score[1][...] = m8

    @pl.when(t == 0)
    def _():
        krot_ref[...] = rope(k_ref[...].astype(F32), cos_ref[...], sa_ref[...], sb_ref[...]).astype(BF16)
        vt_ref[...] = v_ref[...].astype(F32).T.astype(BF16)
        tile_work(buf_a, None)

    steady = jnp.logical_and(t > 0, t < n_q)

    @pl.when(jnp.logical_and(steady, t % 2 == 1))
    def _():
        tile_work(buf_b, buf_a)

    @pl.when(jnp.logical_and(steady, t % 2 == 0))
    def _():
        tile_work(buf_a, buf_b)

    @pl.when(jnp.logical_and(t == n_q, t % 2 == 1))
    def _():
        tile_work(None, buf_a)

    @pl.when(jnp.logical_and(t == n_q, t % 2 == 0))
    def _():
        tile_work(None, buf_b)


def _attention(z3, lam_params, g_subln, lam_init):
    b, seq, _ = z3.shape
    tq = min(256, seq)
    n_q = seq // tq
    inv = ROPE_THETA ** (-jnp.arange(0, ROPE_DIM, 2, dtype=F32) / ROPE_DIM)
    ang = jnp.arange(seq, dtype=F32)[:, None] * inv[None, :]
    c8, s8 = jnp.cos(ang), jnp.sin(ang)
    half = ROPE_DIM // 2
    rest = HEAD_DIM - ROPE_DIM
    one, zero = jnp.ones((seq, rest), F32), jnp.zeros((seq, rest), F32)
    z8 = jnp.zeros((seq, half), F32)
    cos_t = jnp.tile(jnp.concatenate([c8, c8, one], axis=1), (1, 2))
    sa_t = jnp.tile(jnp.concatenate([-s8, z8, zero], axis=1), (1, 2))
    sb_t = jnp.tile(jnp.concatenate([z8, s8, zero], axis=1), (1, 2))
    qb, kb, vb = (HY_COLS // HEAD_W, (HY_COLS + ATT_WIDTH) // HEAD_W, (HY_COLS + 2 * ATT_WIDTH) // HEAD_W)
    return pl.pallas_call(
        functools.partial(_attn_kernel, tq=tq, lam_init=lam_init),
        out_shape=jax.ShapeDtypeStruct((b, seq, ATT_WIDTH), BF16),
        grid=(b, N_HEADS, n_q + 1),
        in_specs=[
            _const_spec((4, HEAD_DIM)),
            pl.BlockSpec((None, tq, HEAD_W), lambda i, h, t: (i, jnp.minimum(t, n_q - 1), qb + h)),
            pl.BlockSpec((None, seq, HEAD_W), lambda i, h, t: (i, 0, kb + h)),
            pl.BlockSpec((None, seq, HEAD_W), lambda i, h, t: (i, 0, vb + h)),
            _const_spec((seq, HEAD_W)), _const_spec((seq, HEAD_W)), _const_spec((seq, HEAD_W)),
            _const_spec((1, HEAD_W)),
        ],
        out_specs=pl.BlockSpec((None, tq, HEAD_W), lambda i, h, t: (i, jnp.maximum(t - 1, 0), h)),
        scratch_shapes=[pltpu.VMEM((seq, HEAD_W), BF16), pltpu.VMEM((HEAD_W, seq), BF16),
                        pltpu.VMEM((seq, 2 * tq), F32), pltpu.VMEM((seq, 2 * tq), F32),
                        pltpu.VMEM((8, 2 * tq), F32), pltpu.VMEM((8, 2 * tq), F32)],
        compiler_params=_cparams(("parallel", "parallel", "arbitrary"), 56),
        name="attention",
    )(lam_params, z3, z3, z3, cos_t, sa_t, sb_t, g_subln)


def _merge_kernel(yc_ref, x0_ref, ya_ref, g0_ref, g1_ref, x_ref, why_ref, wat_ref, wo_ref, gp_ref, o_ref,
                  ma_ref, mb_ref):
    i = pl.program_id(0)
    n_t = pl.num_programs(0) - 1

    def work(m_out, m_in):
        if m_in is not None:
            r = jnp.dot(m_in[...], wo_ref[...], preferred_element_type=F32)
            o_ref[...] = x_ref[...] + _rms(r, gp_ref[...])
        if m_out is not None:
            yh = (x0_ref[...].astype(F32) * yc_ref[...].astype(F32)).astype(BF16)
            a = jnp.dot(yh, why_ref[...], preferred_element_type=F32)
            b = jnp.dot(ya_ref[...], wat_ref[...], preferred_element_type=F32)
            m_out[...] = (g0_ref[...].astype(F32) * a + g1_ref[...].astype(F32) * b).astype(BF16)

    @pl.when(i == 0)
    def _():
        work(ma_ref, None)

    steady = jnp.logical_and(i > 0, i < n_t)

    @pl.when(jnp.logical_and(steady, i % 2 == 1))
    def _():
        work(mb_ref, ma_ref)

    @pl.when(jnp.logical_and(steady, i % 2 == 0))
    def _():
        work(ma_ref, mb_ref)

    @pl.when(jnp.logical_and(i == n_t, i % 2 == 1))
    def _():
        work(None, ma_ref)

    @pl.when(jnp.logical_and(i == n_t, i % 2 == 0))
    def _():
        work(None, mb_ref)


def _merge(yconv, x0c, yatt, z, x2d, w_hy, w_att, w_out, g_post):
    m, d = x2d.shape
    tm = min(256, m)
    n_t = m // tm
    cur = lambda width, blk=0: pl.BlockSpec((tm, width), lambda i: (jnp.minimum(i, n_t - 1), blk))
    prev = lambda width: pl.BlockSpec((tm, width), lambda i: (jnp.maximum(i - 1, 0), 0))
    gate_blk = GATE_COL0 // d
    return pl.pallas_call(
        _merge_kernel,
        out_shape=jax.ShapeDtypeStruct((m, d), F32),
        grid=(n_t + 1,),
        in_specs=[cur(HY_CH), cur(HY_CH), cur(ATT_WIDTH), cur(d, gate_blk), cur(d, gate_blk + 1), prev(d),
                  _const_spec((HY_CH, d)), _const_spec((ATT_WIDTH, d)), _const_spec((d, d)), _const_spec((1, d))],
        out_specs=prev(d),
        scratch_shapes=[pltpu.VMEM((tm, d), BF16), pltpu.VMEM((tm, d), BF16)],
        compiler_params=_cparams(("arbitrary",), 56),
        name="merge",
    )(yconv, x0c, yatt, z, z, x2d, w_hy, w_att, w_out, g_post)


def _ffn_kernel(x_ref, p_ref, gpre_ref, wg_ref, wu_ref, wo_ref, gpost_ref, gple_ref, wpg_ref, wpi_ref,
                o_ref, h_ref, acc_ref, acta_ref, actb_ref):
    j = pl.program_id(1)
    n_ff = pl.num_programs(1) - 1

    def work(act_out, act_in):
        if act_in is not None:
            acc_ref[...] += jnp.dot(act_in[...], wo_ref[...], preferred_element_type=F32)
        if act_out is not None:
            h = h_ref[...]
            gate = jnp.dot(h, wg_ref[...], preferred_element_type=F32)
            up = jnp.dot(h, wu_ref[...], preferred_element_type=F32)
            act_out[...] = (gate * _sigmoid(gate) * up).astype(BF16)

    @pl.when(j == 0)
    def _():
        _rms_rows_to(h_ref, x_ref, gpre_ref)
        acc_ref[...] = jnp.zeros_like(acc_ref)
        work(acta_ref, None)

    steady = jnp.logical_and(j > 0, j < n_ff)

    @pl.when(jnp.logical_and(steady, j % 2 == 1))
    def _():
        work(actb_ref, acta_ref)

    @pl.when(jnp.logical_and(steady, j % 2 == 0))
    def _():
        work(acta_ref, actb_ref)

    def last(act_in):
        def body(rows):
            acc = acc_ref[rows, :] + jnp.dot(act_in[rows, :], wo_ref[...], preferred_element_type=F32)
            x2 = x_ref[rows, :] + _rms(acc, gpost_ref[...])
            e = jnp.dot(_rms(x2, gple_ref[...]).astype(BF16), wpg_ref[...], preferred_element_type=F32)
            pe = jnp.dot(p_ref[rows, :].astype(BF16), wpi_ref[...], preferred_element_type=F32)
            o_ref[rows, :] = x2 + pe * _sigmoid(e)
        _for_row_chunks(x_ref.shape[0], body, chunk=2 * ROW_CHUNK, unrolled=True)

    @pl.when(jnp.logical_and(j == n_ff, j % 2 == 1))
    def _():
        last(acta_ref)

    @pl.when(jnp.logical_and(j == n_ff, j % 2 == 0))
    def _():
        last(actb_ref)


def _ffn_ple(x1, p2d, g_pre, w_in, w_out, g_post, g_ple, w_pg, w_pi):
    m, d = x1.shape
    tm = min(512, m)
    tf = 512
    nff = D_FF // tf
    return pl.pallas_call(
        _ffn_kernel,
        out_shape=jax.ShapeDtypeStruct((m, d), F32),
        grid=(m // tm, nff + 1),
        in_specs=[
            pl.BlockSpec((tm, d), lambda i, j: (i, 0)),
            pl.BlockSpec((tm, PLE_DIM), lambda i, j: (i, 0)),
            _const_spec((1, d)),
            pl.BlockSpec((d, tf), lambda i, j: (0, jnp.minimum(j, nff - 1))),
            pl.BlockSpec((d, tf), lambda i, j: (0, nff + jnp.minimum(j, nff - 1))),
            pl.BlockSpec((tf, d), lambda i, j: (jnp.maximum(j - 1, 0), 0)),
            _const_spec((1, d)), _const_spec((1, d)),
            _const_spec((d, d)), _const_spec((PLE_DIM, d)),
        ],
        out_specs=pl.BlockSpec((tm, d), lambda i, j: (i, 0)),
        scratch_shapes=[pltpu.VMEM((tm, d), BF16), pltpu.VMEM((tm, d), F32),
                        pltpu.VMEM((tm, tf), BF16), pltpu.VMEM((tm, tf), BF16)],
        compiler_params=_cparams(("parallel", "arbitrary"), 56),
        name="ffn_ple",
    )(x1, p2d, g_pre, w_in, w_in, w_out, g_post, g_ple, w_pg, w_pi)


def _layer(x, p, lam_init, wts):
    b, seq, d = x.shape
    x2d = x.reshape(b * seq, d)
    z = _in_proj(x2d, wts["g_mix_pre"], wts["w_in"], wts["b_gate"])
    z3 = z.reshape(b, seq, IN_COLS)

    tf = min(256, seq)
    rmat = _dft_tables(seq, tf)
    ht, sums = _filt_mlp(seq, wts["filt_w1"], wts["filt_b1"], wts["filt_w2"], wts["filt_b2"],
                         wts["filt_freq"], wts["filt_w3"])
    u1, v2, kl = _filt_spec(ht, sums, rmat, tf)
    x0c, wt = _hy_prep(z3, wts["conv_w"], wts["conv_b"])
    yconv = _long_conv(wt, rmat, u1, v2, kl, wts["hyena_d"], tf)

    yatt = _attention(z3, wts["lam"], wts["g_subln"], lam_init)

    x1 = _merge(yconv.reshape(b * seq, HY_CH), x0c.reshape(b * seq, HY_CH), yatt.reshape(b * seq, ATT_WIDTH),
                z, x2d, wts["w_hy_out"], wts["w_att_out"], wts["w_out"], wts["g_mix_post"])
    y = _ffn_ple(x1, p.reshape(b * seq, PLE_DIM), wts["g_ffn_pre"], wts["w_ffn_in"], wts["w_ffn_out"],
                 wts["g_ffn_post"], wts["g_ple"], wts["w_ple_gate"], wts["w_ple_in"])
    return y.reshape(b, seq, d)


def kernel(x_prompt, x_sample, p_prompt, p_sample, g_mix_pre, g_mix_post, g_ffn_pre, g_ffn_post, g_ple, w_in, b_gate, conv_w, conv_b, filt_w1, filt_b1, filt_w2, filt_b2, filt_freq, filt_w3, hyena_d, lam_q1, lam_k1, lam_q2, lam_k2, g_subln, w_hy_out, w_att_out, w_out, w_ffn_in, w_ffn_out, w_ple_in, w_ple_gate):
    depth = w_in.shape[0]
    xs = [x_prompt, x_sample]
    ps = [p_prompt, p_sample]
    for i in range(depth):
        lam_init = 0.8 - 0.6 * math.exp(-0.3 * i)
        wts = dict(
            g_mix_pre=g_mix_pre[i][None, :], g_mix_post=g_mix_post[i][None, :],
            g_ffn_pre=g_ffn_pre[i][None, :], g_ffn_post=g_ffn_post[i][None, :], g_ple=g_ple[i][None, :],
            w_in=w_in[i].astype(BF16), b_gate=b_gate[i].reshape(1, 2 * D_MODEL),
            conv_w=conv_w[i], conv_b=conv_b[i][None, :],
            filt_w1=filt_w1[i], filt_b1=filt_b1[i], filt_w2=filt_w2[i], filt_b2=filt_b2[i],
            filt_freq=filt_freq[i], filt_w3=filt_w3[i],
            hyena_d=jnp.broadcast_to(hyena_d[i][:, None], (HY_CH, LANES)),
            lam=jnp.stack([lam_q1[i], lam_k1[i], lam_q2[i], lam_k2[i]]),
            g_subln=g_subln[i][None, :],
            w_hy_out=w_hy_out[i].astype(BF16), w_att_out=w_att_out[i].astype(BF16),
            w_out=w_out[i].astype(BF16), w_ffn_in=w_ffn_in[i].astype(BF16),
            w_ffn_out=w_ffn_out[i].astype(BF16), w_ple_in=w_ple_in[i].astype(BF16),
            w_ple_gate=w_ple_gate[i].astype(BF16),
        )
        xs = [_layer(x, p[i], lam_init, wts) for x, p in zip(xs, ps)]
    return (xs[0], xs[1])
```

```python
import functools
import math

import jax
import jax.numpy as jnp
from jax import lax
from jax.experimental import pallas as pl
from jax.experimental.pallas import tpu as pltpu

F32 = jnp.float32
BF16 = jnp.bfloat16

D_MODEL = 2048
PLE_DIM = 256
NORM_EPS = 1e-6
HY_CH = D_MODEL // 2
FILT_BANDS = 16
FILT_EMB = 1 + 2 * FILT_BANDS
FILT_ORDER = 64
FILT_MAX_DECAY = math.log(1e-2) / 0.3
FILT_MIN_DECAY = math.log(1e-2) / 1.5
ATT_WIDTH = D_MODEL // 2
N_HEADS = 8
HEAD_DIM = ATT_WIDTH // (2 * N_HEADS)
HEAD_W = 2 * HEAD_DIM
ROPE_DIM = HEAD_DIM // 4
ROPE_THETA = 500000.0
D_FF = ((8 * D_MODEL + 3 * 256 - 1) // (3 * 256)) * 256
HY_COLS = 3 * HY_CH
GATE_COL0 = HY_COLS + 3 * ATT_WIDTH
IN_COLS = GATE_COL0 + 2 * D_MODEL
LANES = 128
LOG2E = math.log2(math.e)
MIB = 2 ** 20


def _cparams(sem, vmem_mib):
    return pltpu.CompilerParams(dimension_semantics=sem, vmem_limit_bytes=vmem_mib * MIB)


def _const_spec(shape):
    nd = len(shape)
    return pl.BlockSpec(shape, lambda *_: (0,) * nd, pipeline_mode=pl.Buffered(1))


def _rms(x, g):
    ms = jnp.mean(x * x, axis=-1, keepdims=True)
    return x * lax.rsqrt(ms + NORM_EPS) * g


def _sigmoid(x):
    return 1.0 / (1.0 + jnp.exp(-x))


ROW_CHUNK = 128


def _for_row_chunks(n_rows, body, chunk=ROW_CHUNK):
    chunk = min(chunk, n_rows)

    def step(c, carry):
        body(pl.ds(pl.multiple_of(c * chunk, chunk), chunk))
        return carry
    lax.fori_loop(0, n_rows // chunk, step, 0)


def _rms_rows_to(h_ref, x_ref, g_ref):
    def body(rows):
        h_ref[rows, :] = _rms(x_ref[rows, :], g_ref[...]).astype(h_ref.dtype)
    _for_row_chunks(x_ref.shape[0], body)


def _inproj_kernel(x_ref, g_ref, w_ref, b_ref, o_ref, h_ref, *, n_plain):
    j = pl.program_id(1)

    @pl.when(j == 0)
    def _():
        _rms_rows_to(h_ref, x_ref, g_ref)

    acc = jnp.dot(h_ref[...], w_ref[...], preferred_element_type=F32)
    o_ref[...] = jnp.where(j >= n_plain, _sigmoid(acc + b_ref[...]), acc).astype(BF16)


def _in_proj(x2d, g, w_bf, b_flat):
    m, d = x2d.shape
    n = w_bf.shape[1]
    tm = min(1024, m)
    tn = 1024
    n_plain = GATE_COL0 // tn
    return pl.pallas_call(
        functools.partial(_inproj_kernel, n_plain=n_plain),
        out_shape=jax.ShapeDtypeStruct((m, n), BF16),
        grid=(m // tm, n // tn),
        in_specs=[
            pl.BlockSpec((tm, d), lambda i, j: (i, 0)),
            pl.BlockSpec((1, d), lambda i, j: (0, 0)),
            pl.BlockSpec((d, tn), lambda i, j: (0, j)),
            pl.BlockSpec((1, tn), lambda i, j: (0, jnp.maximum(j - n_plain, 0))),
        ],
        out_specs=pl.BlockSpec((tm, tn), lambda i, j: (i, j)),
        scratch_shapes=[pltpu.VMEM((tm, d), BF16)],
        compiler_params=_cparams(("parallel", "arbitrary"), 48),
        name="in_proj",
    )(x2d, g, w_bf, b_flat)


def _filt_mlp_kernel(emb_ref, w1_ref, b1_ref, w2_ref, b2_ref, fr_ref, w3_ref, dl_ref, ht_ref, sum_ref):
    i = pl.program_id(0)
    hp = lax.Precision.HIGHEST
    emb = emb_ref[...]
    fr = fr_ref[...]
    h1 = jnp.sin(fr * (jnp.dot(emb, w1_ref[...], precision=hp, preferred_element_type=F32) + b1_ref[...]))
    h2 = jnp.sin(fr * (jnp.dot(h1, w2_ref[...], precision=hp, preferred_element_type=F32) + b2_ref[...]))
    h = jnp.dot(h2, w3_ref[...], precision=hp, preferred_element_type=F32)
    dec = jnp.exp(-emb[:, 0:1] * dl_ref[...])
    ht = jnp.concatenate([h[:, :HY_CH] * dec, h[:, HY_CH:] * dec], axis=1).T
    ht_ref[...] = ht
    ab = jnp.abs(ht)
    part = ab[:, 0:LANES]
    for c in range(1, ab.shape[1] // LANES):
        part = part + ab[:, c * LANES:(c + 1) * LANES]

    @pl.when(i == 0)
    def _():
        sum_ref[...] = part

    @pl.when(i > 0)
    def _():
        sum_ref[...] += part


def _filt_mlp(seq, w1, b1, w2, b2, freq, w3):
    t = jnp.linspace(0.0, 1.0, seq, dtype=F32)[:, None]
    wpos = 2.0 * math.pi * jnp.arange(seq, dtype=F32) / seq
    bands = jnp.linspace(1e-4, FILT_BANDS - 1, FILT_BANDS, dtype=F32)
    ang = wpos[:, None] * bands[None, :]
    emb = jnp.concatenate([t, jnp.cos(ang), -jnp.sin(ang)], axis=-1)
    emb = jnp.pad(emb, ((0, 0), (0, LANES - FILT_EMB)))
    pad_o = LANES - FILT_ORDER
    w1p = jnp.pad(w1, ((0, LANES - FILT_EMB), (0, pad_o)))
    w2p = jnp.pad(w2, ((0, pad_o), (0, pad_o)))
    w3p = jnp.pad(w3, ((0, pad_o), (0, 0)))
    b1p = jnp.pad(b1[None, :], ((0, 0), (0, pad_o)))
    b2p = jnp.pad(b2[None, :], ((0, 0), (0, pad_o)))
    frp = jnp.pad(freq[None, :], ((0, 0), (0, pad_o)))
    deltas = jnp.abs(jnp.linspace(FILT_MIN_DECAY, FILT_MAX_DECAY, HY_CH, dtype=F32))[None, :]
    tl = min(512, seq)
    return pl.pallas_call(
        _filt_mlp_kernel,
        out_shape=(jax.ShapeDtypeStruct((2 * HY_CH, seq), F32),
                   jax.ShapeDtypeStruct((2 * HY_CH, LANES), F32)),
        grid=(seq // tl,),
        in_specs=[
            pl.BlockSpec((tl, LANES), lambda i: (i, 0)),
            _const_spec((LANES, LANES)), _const_spec((1, LANES)),
            _const_spec((LANES, LANES)), _const_spec((1, LANES)),
            _const_spec((1, LANES)), _const_spec((LANES, 2 * HY_CH)),
            _const_spec((1, HY_CH)),
        ],
        out_specs=(pl.BlockSpec((2 * HY_CH, tl), lambda i: (0, i)),
                   pl.BlockSpec((2 * HY_CH, LANES), lambda i: (0, 0))),
        compiler_params=_cparams(("arbitrary",), 48),
        name="filt_mlp",
    )(emb, w1p, b1p, w2p, b2p, frp, w3p, deltas)


def _dot_t(a, b):
    return lax.dot_general(a, b, (((1,), (1,)), ((), ())), preferred_element_type=F32)


def _filt_spec_kernel(hf_ref, hb_ref, sf_ref, sb_ref, r_ref, u1_ref, v2_ref, kl_ref, *, tf, seq):
    f = pl.program_id(1)
    norm = jnp.sum(sf_ref[...] + sb_ref[...], axis=1, keepdims=True) + NORM_EPS
    hf = hf_ref[...] / norm
    lane = lax.broadcasted_iota(jnp.int32, hf.shape, 1)
    hb = jnp.where(lane == 0, 0.0, hb_ref[...] / norm)
    s1 = (hf + hb).astype(BF16)
    s2 = (hf - hb).astype(BF16)
    c = _dot_t(s1, r_ref[:tf, :]) * (1.0 / seq)
    col = lax.broadcasted_iota(jnp.int32, c.shape, 1) + f * tf
    u1_ref[...] = jnp.where(col == 0, 0.5 * c, c)
    v2_ref[...] = jnp.where(col == 0, 0.0, _dot_t(s2, r_ref[tf:, :]) * (1.0 / seq))

    @pl.when(f == 0)
    def _():
        nyq = _dot_t(s1, r_ref[tf:, :])[:, 0:1] * (0.5 / seq)
        kl_ref[...] = jnp.broadcast_to(nyq, kl_ref.shape)


def _filt_spec(ht, sums, rmat, tf):
    seq = ht.shape[1]
    tc = 256
    nc = HY_CH // tc
    return pl.pallas_call(
        functools.partial(_filt_spec_kernel, tf=tf, seq=seq),
        out_shape=(jax.ShapeDtypeStruct((HY_CH, seq), F32),
                   jax.ShapeDtypeStruct((HY_CH, seq), F32),
                   jax.ShapeDtypeStruct((HY_CH, LANES), F32)),
        grid=(nc, seq // tf),
        in_specs=[
            pl.BlockSpec((tc, seq), lambda c, f: (c, 0)),
            pl.BlockSpec((tc, seq), lambda c, f: (nc + c, 0)),
            pl.BlockSpec((tc, LANES), lambda c, f: (c, 0)),
            pl.BlockSpec((tc, LANES), lambda c, f: (nc + c, 0)),
            pl.BlockSpec((2 * tf, seq), lambda c, f: (f, 0)),
        ],
        out_specs=(pl.BlockSpec((tc, tf), lambda c, f: (c, f)),
                   pl.BlockSpec((tc, tf), lambda c, f: (c, f)),
                   pl.BlockSpec((tc, LANES), lambda c, f: (c, 0))),
        compiler_params=_cparams(("parallel", "arbitrary"), 48),
        name="filt_spec",
    )(ht, ht, sums, sums, rmat)


def _hyprep_kernel(u0_ref, u1_ref, u2_ref, w0_ref, w1_ref, w2_ref, b0_ref, b1_ref, b2_ref, x0_ref, wt_ref):
    seq = u0_ref.shape[0]
    row = lax.broadcasted_iota(jnp.int32, u0_ref.shape, 0)

    def sconv(u_ref, w_ref, b_ref):
        u = u_ref[...].astype(F32)
        um = jnp.where(row == 0, 0.0, pltpu.roll(u, 1, 0))
        up = jnp.where(row == seq - 1, 0.0, pltpu.roll(u, seq - 1, 0))
        w = w_ref[...]
        return b_ref[...] + um * w[0:1] + u * w[1:2] + up * w[2:3]

    x0_ref[...] = sconv(u0_ref, w0_ref, b0_ref).astype(BF16)
    prod = sconv(u1_ref, w1_ref, b1_ref) * sconv(u2_ref, w2_ref, b2_ref)
    wt_ref[...] = prod.T.astype(BF16)


def _hy_prep(z3, conv_w, conv_b):
    b, seq, _ = z3.shape
    tc = LANES
    nc = HY_CH // tc
    uspec = lambda g: pl.BlockSpec((None, seq, tc), lambda i, c: (i, 0, g * nc + c))
    wspec = lambda g: pl.BlockSpec((3, tc), lambda i, c: (0, g * nc + c))
    bspec = lambda g: pl.BlockSpec((1, tc), lambda i, c: (0, g * nc + c))
    return pl.pallas_call(
        _hyprep_kernel,
        out_shape=(jax.ShapeDtypeStruct((b, seq, HY_CH), BF16),
                   jax.ShapeDtypeStruct((b, HY_CH, seq), BF16)),
        grid=(b, nc),
        in_specs=[uspec(0), uspec(1), uspec(2), wspec(0), wspec(1), wspec(2), bspec(0), bspec(1), bspec(2)],
        out_specs=(pl.BlockSpec((None, seq, tc), lambda i, c: (i, 0, c)),
                   pl.BlockSpec((None, tc, seq), lambda i, c: (i, c, 0))),
        compiler_params=_cparams(("parallel", "parallel"), 48),
        name="hy_prep",
    )(z3, z3, z3, conv_w, conv_w, conv_w, conv_b, conv_b, conv_b)


def _conv_kernel(wt_ref, ra_ref, rb_ref, u1_ref, v2_ref, kl_ref, d_ref, o_ref, acc_ref, za_ref, zb_ref, *, tf):
    t = pl.program_id(2)
    n_f = pl.num_programs(2) - 1

    def work(z_out, z_in, first=False):
        if z_in is not None:
            acc_ref[...] += jnp.dot(z_in[...], rb_ref[...], preferred_element_type=F32)
        if z_out is not None:
            ap = _dot_t(wt_ref[...], ra_ref[...])
            a = ap[:, :tf]
            p = ap[:, tf:]
            u1 = u1_ref[...]
            v2 = v2_ref[...]
            z2 = p * u1 + a * v2
            if first:
                col = lax.broadcasted_iota(jnp.int32, z2.shape, 1)
                z2 = jnp.where(col == 0, p * kl_ref[:, 0:1], z2)
            z_out[...] = jnp.concatenate([a * u1 - p * v2, z2], axis=1).astype(BF16)

    @pl.when(t == 0)
    def _():
        acc_ref[...] = jnp.zeros_like(acc_ref)
        work(za_ref, None, first=True)

    steady = jnp.logical_and(t > 0, t < n_f)

    @pl.when(jnp.logical_and(steady, t % 2 == 1))
    def _():
        work(zb_ref, za_ref)

    @pl.when(jnp.logical_and(steady, t % 2 == 0))
    def _():
        work(za_ref, zb_ref)

    def finish(z_in):
        work(None, z_in)
        y = acc_ref[...] + d_ref[:, 0:1] * wt_ref[...].astype(F32)
        o_ref[...] = y.T.astype(BF16)

    @pl.when(jnp.logical_and(t == n_f, t % 2 == 1))
    def _():
        finish(za_ref)

    @pl.when(jnp.logical_and(t == n_f, t % 2 == 0))
    def _():
        finish(zb_ref)


def _long_conv(wt, rmat, u1, v2, kl, dcol, tf):
    b, _, seq = wt.shape
    tc = 512
    n_f = seq // tf
    return pl.pallas_call(
        functools.partial(_conv_kernel, tf=tf),
        out_shape=jax.ShapeDtypeStruct((b, seq, HY_CH), BF16),
        grid=(b, HY_CH // tc, n_f + 1),
        in_specs=[
            pl.BlockSpec((None, tc, seq), lambda i, c, t: (i, c, 0)),
            pl.BlockSpec((2 * tf, seq), lambda i, c, t: (jnp.minimum(t, n_f - 1), 0)),
            pl.BlockSpec((2 * tf, seq), lambda i, c, t: (jnp.maximum(t - 1, 0), 0)),
            pl.BlockSpec((tc, tf), lambda i, c, t: (c, jnp.minimum(t, n_f - 1))),
            pl.BlockSpec((tc, tf), lambda i, c, t: (c, jnp.minimum(t, n_f - 1))),
            pl.BlockSpec((tc, LANES), lambda i, c, t: (c, 0)),
            pl.BlockSpec((tc, LANES), lambda i, c, t: (c, 0)),
        ],
        out_specs=pl.BlockSpec((None, seq, tc), lambda i, c, t: (i, 0, c)),
        scratch_shapes=[pltpu.VMEM((tc, seq), F32), pltpu.VMEM((tc, 2 * tf), BF16), pltpu.VMEM((tc, 2 * tf), BF16)],
        compiler_params=_cparams(("parallel", "parallel", "arbitrary"), 56),
        name="long_conv",
    )(wt, rmat, rmat, u1, v2, kl, dcol)


def _dft_gen_kernel(ch_ref, sh_ref, cl_ref, sl_ref, r_ref, *, tf):
    ch, sh = ch_ref[...], sh_ref[...]
    cl, sl = cl_ref[...], sl_ref[...]
    r_ref[:tf, :] = (ch * cl - sh * sl).astype(BF16)
    sin = sh * cl + ch * sl
    row = lax.broadcasted_iota(jnp.int32, sin.shape, 0)
    lane = lax.broadcasted_iota(jnp.int32, sin.shape, 1)
    alt = (1 - 2 * (lane % 2)).astype(F32)
    nyq_row = jnp.logical_and(row == 0, pl.program_id(0) == 0)
    r_ref[tf:, :] = jnp.where(nyq_row, alt, sin).astype(BF16)


def _dft_tables(seq, tf):
    idx = jnp.arange(seq, dtype=jnp.int32)
    nf = seq // tf

    def trig(freqs):
        ang = ((freqs[:, None] * idx[None, :]) % (2 * seq)).astype(F32) * (math.pi / seq)
        return jnp.cos(ang), jnp.sin(ang)

    ch, sh = trig(jnp.arange(nf, dtype=jnp.int32) * tf)
    cl, sl = trig(jnp.arange(tf, dtype=jnp.int32))
    base_spec = pl.BlockSpec((None, 1, seq), lambda f: (f, 0, 0))
    return pl.pallas_call(
        functools.partial(_dft_gen_kernel, tf=tf),
        out_shape=jax.ShapeDtypeStruct((2 * seq, seq), BF16),
        grid=(nf,),
        in_specs=[base_spec, base_spec, _const_spec((tf, seq)), _const_spec((tf, seq))],
        out_specs=pl.BlockSpec((2 * tf, seq), lambda f: (f, 0)),
        compiler_params=_cparams(("parallel",), 32),
        name="dft_tables",
    )(ch[:, None, :], sh[:, None, :], cl, sl)


KV_CHUNK = 256


def _attn_kernel(lam_ref, q_ref, k_ref, v_ref, cos_ref, sa_ref, sb_ref, gs_ref, o_ref, krot_ref, vt_ref,
                 sa_buf, sb_buf, ma_buf, mb_buf,
                 *, tq, lam_init):
    t = pl.program_id(2)
    n_q = pl.num_programs(2) - 1
    seq = k_ref.shape[0]
    ck = min(KV_CHUNK, seq)
    n_chunks = seq // ck

    def rope(x, c, sa, sb):
        return x * c + pltpu.roll(x, LANES - ROPE_DIM // 2, 1) * sa + pltpu.roll(x, ROPE_DIM // 2, 1) * sb

    buf_a = (sa_buf, ma_buf)
    buf_b = (sb_buf, mb_buf)

    def tile_work(score, finish):
        if score is not None:
            rows = pl.ds(pl.multiple_of(t * tq, tq), tq)
            q = rope(q_ref[...].astype(F32), cos_ref[rows, :], sa_ref[rows, :], sb_ref[rows, :])
            q = q * (HEAD_DIM ** -0.5 * LOG2E)
            lane = lax.broadcasted_iota(jnp.int32, q.shape, 1)
            qq = jnp.concatenate([jnp.where(lane < HEAD_DIM, q, 0.0), jnp.where(lane >= HEAD_DIM, q, 0.0)], axis=0)
            qq = qq.astype(BF16)
            m8 = jnp.full((8, 2 * tq), -jnp.inf, F32)
        if finish is not None:
            m = jnp.max(finish[1][...], axis=0, keepdims=True)
            l8 = jnp.zeros((8, 2 * tq), F32)
            ot = jnp.zeros((HEAD_W, 2 * tq), F32)

        for c in range(n_chunks):
            kv = slice(c * ck, (c + 1) * ck)
            if finish is not None:
                p_c = jnp.exp2(finish[0][kv, :] - m)
                l8 = l8 + jnp.sum(p_c.reshape(ck // 8, 8, 2 * tq), axis=0)
                ot = ot + jnp.dot(vt_ref[:, kv], p_c.astype(BF16), preferred_element_type=F32)
            if score is not None:
                s_c = _dot_t(krot_ref[kv, :], qq)
                score[0][kv, :] = s_c
                m8 = jnp.maximum(m8, jnp.max(s_c.reshape(ck // 8, 8, 2 * tq), axis=0))
        if score is not None:
            score[1][...] = m8
        if finish is not None:
            l = jnp.sum(l8, axis=0, keepdims=True)
            lp = lam_ref[...]
            lam = (jnp.exp(jnp.sum(lp[0:1] * lp[1:2], axis=-1, keepdims=True))
                   - jnp.exp(jnp.sum(lp[2:3] * lp[3:4], axis=-1, keepdims=True)) + lam_init)
            r = 1.0 / l
            o = (ot[:, :tq] * r[:, :tq] - ot[:, tq:] * (lam * r[:, tq:])).T
            o_ref[...] = (_rms(o, gs_ref[...]) * (1.0 - lam_init)).astype(BF16)

    @pl.when(t == 0)
    def _():
        krot_ref[...] = rope(k_ref[...].astype(F32), cos_ref[...], sa_ref[...], sb_ref[...]).astype(BF16)
        vt_ref[...] = v_ref[...].astype(F32).T.astype(BF16)
        tile_work(buf_a, None)

    steady = jnp.logical_and(t > 0, t < n_q)

    @pl.when(jnp.logical_and(steady, t % 2 == 1))
    def _():
        tile_work(buf_b, buf_a)

    @pl.when(jnp.logical_and(steady, t % 2 == 0))
    def _():
        tile_work(buf_a, buf_b)

    @pl.when(jnp.logical_and(t == n_q, t % 2 == 1))
    def _():
        tile_work(None, buf_a)

    @pl.when(jnp.logical_and(t == n_q, t % 2 == 0))
    def _():
        tile_work(None, buf_b)


def _attention(z3, lam_params, g_subln, lam_init):
    b, seq, _ = z3.shape
    tq = min(256, seq)
    n_q = seq // tq
    inv = ROPE_THETA ** (-jnp.arange(0, ROPE_DIM, 2, dtype=F32) / ROPE_DIM)
    ang = jnp.arange(seq, dtype=F32)[:, None] * inv[None, :]
    c8, s8 = jnp.cos(ang), jnp.sin(ang)
    half = ROPE_DIM // 2
    rest = HEAD_DIM - ROPE_DIM
    one, zero = jnp.ones((seq, rest), F32), jnp.zeros((seq, rest), F32)
    z8 = jnp.zeros((seq, half), F32)
    cos_t = jnp.tile(jnp.concatenate([c8, c8, one], axis=1), (1, 2))
    sa_t = jnp.tile(jnp.concatenate([-s8, z8, zero], axis=1), (1, 2))
    sb_t = jnp.tile(jnp.concatenate([z8, s8, zero], axis=1), (1, 2))
    qb, kb, vb = (HY_COLS // HEAD_W, (HY_COLS + ATT_WIDTH) // HEAD_W, (HY_COLS + 2 * ATT_WIDTH) // HEAD_W)
    return pl.pallas_call(
        functools.partial(_attn_kernel, tq=tq, lam_init=lam_init),
        out_shape=jax.ShapeDtypeStruct((b, seq, ATT_WIDTH), BF16),
        grid=(b, N_HEADS, n_q + 1),
        in_specs=[
            _const_spec((4, HEAD_DIM)),
            pl.BlockSpec((None, tq, HEAD_W), lambda i, h, t: (i, jnp.minimum(t, n_q - 1), qb + h)),
            pl.BlockSpec((None, seq, HEAD_W), lambda i, h, t: (i, 0, kb + h)),
            pl.BlockSpec((None, seq, HEAD_W), lambda i, h, t: (i, 0, vb + h)),
            _const_spec((seq, HEAD_W)), _const_spec((seq, HEAD_W)), _const_spec((seq, HEAD_W)),
            _const_spec((1, HEAD_W)),
        ],
        out_specs=pl.BlockSpec((None, tq, HEAD_W), lambda i, h, t: (i, jnp.maximum(t - 1, 0), h)),
        scratch_shapes=[pltpu.VMEM((seq, HEAD_W), BF16), pltpu.VMEM((HEAD_W, seq), BF16),
                        pltpu.VMEM((seq, 2 * tq), F32), pltpu.VMEM((seq, 2 * tq), F32),
                        pltpu.VMEM((8, 2 * tq), F32), pltpu.VMEM((8, 2 * tq), F32)],
        compiler_params=_cparams(("parallel", "parallel", "arbitrary"), 56),
        name="attention",
    )(lam_params, z3, z3, z3, cos_t, sa_t, sb_t, g_subln)


def _merge_kernel(yc_ref, x0_ref, ya_ref, g0_ref, g1_ref, x_ref, why_ref, wat_ref, wo_ref, gp_ref, o_ref):
    yh = (x0_ref[...].astype(F32) * yc_ref[...].astype(F32)).astype(BF16)
    a = jnp.dot(yh, why_ref[...], preferred_element_type=F32)
    b = jnp.dot(ya_ref[...], wat_ref[...], preferred_element_type=F32)
    m = g0_ref[...].astype(F32) * a + g1_ref[...].astype(F32) * b
    r = jnp.dot(m.astype(BF16), wo_ref[...], preferred_element_type=F32)
    o_ref[...] = x_ref[...] + _rms(r, gp_ref[...])


def _merge(yconv, x0c, yatt, z, x2d, w_hy, w_att, w_out, g_post):
    m, d = x2d.shape
    tm = min(256, m)
    row = lambda width, blk=0: pl.BlockSpec((tm, width), lambda i: (i, blk))
    gate_blk = GATE_COL0 // d
    return pl.pallas_call(
        _merge_kernel,
        out_shape=jax.ShapeDtypeStruct((m, d), F32),
        grid=(m // tm,),
        in_specs=[row(HY_CH), row(HY_CH), row(ATT_WIDTH), row(d, gate_blk), row(d, gate_blk + 1), row(d),
                  _const_spec((HY_CH, d)), _const_spec((ATT_WIDTH, d)), _const_spec((d, d)), _const_spec((1, d))],
        out_specs=row(d),
        compiler_params=_cparams(("parallel",), 56),
        name="merge",
    )(yconv, x0c, yatt, z, z, x2d, w_hy, w_att, w_out, g_post)


def _ffn_kernel(x_ref, p_ref, gpre_ref, wg_ref, wu_ref, wo_ref, gpost_ref, gple_ref, wpg_ref, wpi_ref,
                o_ref, h_ref, acc_ref):
    j = pl.program_id(1)

    @pl.when(j == 0)
    def _():
        _rms_rows_to(h_ref, x_ref, gpre_ref)
        acc_ref[...] = jnp.zeros_like(acc_ref)

    h = h_ref[...]
    gate = jnp.dot(h, wg_ref[...], preferred_element_type=F32)
    up = jnp.dot(h, wu_ref[...], preferred_element_type=F32)
    act = (gate * _sigmoid(gate) * up).astype(BF16)
    acc_ref[...] += jnp.dot(act, wo_ref[...], preferred_element_type=F32)

    @pl.when(j == pl.num_programs(1) - 1)
    def _():
        def body(rows):
            x2 = x_ref[rows, :] + _rms(acc_ref[rows, :], gpost_ref[...])
            e = jnp.dot(_rms(x2, gple_ref[...]).astype(BF16), wpg_ref[...], preferred_element_type=F32)
            pe = jnp.dot(p_ref[rows, :].astype(BF16), wpi_ref[...], preferred_element_type=F32)
            o_ref[rows, :] = x2 + pe * _sigmoid(e)
        _for_row_chunks(x_ref.shape[0], body, chunk=2 * ROW_CHUNK)


def _ffn_ple(x1, p2d, g_pre, w_in, w_out, g_post, g_ple, w_pg, w_pi):
    m, d = x1.shape
    tm = min(512, m)
    tf = 512
    nff = D_FF // tf
    return pl.pallas_call(
        _ffn_kernel,
        out_shape=jax.ShapeDtypeStruct((m, d), F32),
        grid=(m // tm, nff),
        in_specs=[
            pl.BlockSpec((tm, d), lambda i, j: (i, 0)),
            pl.BlockSpec((tm, PLE_DIM), lambda i, j: (i, 0)),
            _const_spec((1, d)),
            pl.BlockSpec((d, tf), lambda i, j: (0, j)),
            pl.BlockSpec((d, tf), lambda i, j: (0, nff + j)),
            pl.BlockSpec((tf, d), lambda i, j: (j, 0)),
            _const_spec((1, d)), _const_spec((1, d)),
            _const_spec((d, d)), _const_spec((PLE_DIM, d)),
        ],
        out_specs=pl.BlockSpec((tm, d), lambda i, j: (i, 0)),
        scratch_shapes=[pltpu.VMEM((tm, d), BF16), pltpu.VMEM((tm, d), F32)],
        compiler_params=_cparams(("parallel", "arbitrary"), 56),
        name="ffn_ple",
    )(x1, p2d, g_pre, w_in, w_in, w_out, g_post, g_ple, w_pg, w_pi)


def _layer(x, p, lam_init, wts):
    b, seq, d = x.shape
    x2d = x.reshape(b * seq, d)
    z = _in_proj(x2d, wts["g_mix_pre"], wts["w_in"], wts["b_gate"])
    z3 = z.reshape(b, seq, IN_COLS)

    tf = min(256, seq)
    rmat = _dft_tables(seq, tf)
    ht, sums = _filt_mlp(seq, wts["filt_w1"], wts["filt_b1"], wts["filt_w2"], wts["filt_b2"],
                         wts["filt_freq"], wts["filt_w3"])
    u1, v2, kl = _filt_spec(ht, sums, rmat, tf)
    x0c, wt = _hy_prep(z3, wts["conv_w"], wts["conv_b"])
    yconv = _long_conv(wt, rmat, u1, v2, kl, wts["hyena_d"], tf)

    yatt = _attention(z3, wts["lam"], wts["g_subln"], lam_init)

    x1 = _merge(yconv.reshape(b * seq, HY_CH), x0c.reshape(b * seq, HY_CH), yatt.reshape(b * seq, ATT_WIDTH),
                z, x2d, wts["w_hy_out"], wts["w_att_out"], wts["w_out"], wts["g_mix_post"])
    y = _ffn_ple(x1, p.reshape(b * seq, PLE_DIM), wts["g_ffn_pre"], wts["w_ffn_in"], wts["w_ffn_out"],
                 wts["g_ffn_post"], wts["g_ple"], wts["w_ple_gate"], wts["w_ple_in"])
    return y.reshape(b, seq, d)


def kernel(x_prompt, x_sample, p_prompt, p_sample, g_mix_pre, g_mix_post, g_ffn_pre, g_ffn_post, g_ple, w_in, b_gate, conv_w, conv_b, filt_w1, filt_b1, filt_w2, filt_b2, filt_freq, filt_w3, hyena_d, lam_q1, lam_k1, lam_q2, lam_k2, g_subln, w_hy_out, w_att_out, w_out, w_ffn_in, w_ffn_out, w_ple_in, w_ple_gate):
    depth = w_in.shape[0]
    xs = [x_prompt, x_sample]
    ps = [p_prompt, p_sample]
    for i in range(depth):
        lam_init = 0.8 - 0.6 * math.exp(-0.3 * i)
        wts = dict(
            g_mix_pre=g_mix_pre[i][None, :], g_mix_post=g_mix_post[i][None, :],
            g_ffn_pre=g_ffn_pre[i][None, :], g_ffn_post=g_ffn_post[i][None, :], g_ple=g_ple[i][None, :],
            w_in=w_in[i].astype(BF16), b_gate=b_gate[i].reshape(1, 2 * D_MODEL),
            conv_w=conv_w[i], conv_b=conv_b[i][None, :],
            filt_w1=filt_w1[i], filt_b1=filt_b1[i], filt_w2=filt_w2[i], filt_b2=filt_b2[i],
            filt_freq=filt_freq[i], filt_w3=filt_w3[i],
            hyena_d=jnp.broadcast_to(hyena_d[i][:, None], (HY_CH, LANES)),
            lam=jnp.stack([lam_q1[i], lam_k1[i], lam_q2[i], lam_k2[i]]),
            g_subln=g_subln[i][None, :],
            w_hy_out=w_hy_out[i].astype(BF16), w_att_out=w_att_out[i].astype(BF16),
            w_out=w_out[i].astype(BF16), w_ffn_in=w_ffn_in[i].astype(BF16),
            w_ffn_out=w_ffn_out[i].astype(BF16), w_ple_in=w_ple_in[i].astype(BF16),
            w_ple_gate=w_ple_gate[i].astype(BF16),
        )
        xs = [_layer(x, p[i], lam_init, wts) for x, p in zip(xs, ps)]
    return (xs[0], xs[1])
```

```python
import functools
import math

import jax
import jax.numpy as jnp
from jax import lax
from jax.experimental import pallas as pl
from jax.experimental.pallas import tpu as pltpu

F32 = jnp.float32
BF16 = jnp.bfloat16

D_MODEL = 2048
PLE_DIM = 256
NORM_EPS = 1e-6
HY_CH = D_MODEL // 2
FILT_BANDS = 16
FILT_EMB = 1 + 2 * FILT_BANDS
FILT_ORDER = 64
FILT_MAX_DECAY = math.log(1e-2) / 0.3
FILT_MIN_DECAY = math.log(1e-2) / 1.5
ATT_WIDTH = D_MODEL // 2
N_HEADS = 8
HEAD_DIM = ATT_WIDTH // (2 * N_HEADS)
HEAD_W = 2 * HEAD_DIM
ROPE_DIM = HEAD_DIM // 4
ROPE_THETA = 500000.0
D_FF = ((8 * D_MODEL + 3 * 256 - 1) // (3 * 256)) * 256
HY_COLS = 3 * HY_CH
GATE_COL0 = HY_COLS + 3 * ATT_WIDTH
IN_COLS = GATE_COL0 + 2 * D_MODEL
LANES = 128
LOG2E = math.log2(math.e)
MIB = 2 ** 20


def _cparams(sem, vmem_mib):
    return pltpu.CompilerParams(dimension_semantics=sem, vmem_limit_bytes=vmem_mib * MIB)


def _const_spec(shape):
    nd = len(shape)
    return pl.BlockSpec(shape, lambda *_: (0,) * nd, pipeline_mode=pl.Buffered(1))


def _rms(x, g):
    ms = jnp.mean(x * x, axis=-1, keepdims=True)
    return x * lax.rsqrt(ms + NORM_EPS) * g


def _sigmoid(x):
    return 1.0 / (1.0 + jnp.exp(-x))


ROW_CHUNK = 128


def _for_row_chunks(n_rows, body, chunk=ROW_CHUNK):
    chunk = min(chunk, n_rows)

    def step(c, carry):
        body(pl.ds(pl.multiple_of(c * chunk, chunk), chunk))
        return carry
    lax.fori_loop(0, n_rows // chunk, step, 0)


def _rms_rows_to(h_ref, x_ref, g_ref):
    def body(rows):
        h_ref[rows, :] = _rms(x_ref[rows, :], g_ref[...]).astype(h_ref.dtype)
    _for_row_chunks(x_ref.shape[0], body)


def _inproj_kernel(x_ref, g_ref, w_ref, b_ref, o_ref, h_ref, *, n_plain):
    j = pl.program_id(1)

    @pl.when(j == 0)
    def _():
        _rms_rows_to(h_ref, x_ref, g_ref)

    acc = jnp.dot(h_ref[...], w_ref[...], preferred_element_type=F32)
    bias = b_ref[jnp.maximum(j - n_plain, 0)]
    o_ref[...] = jnp.where(j >= n_plain, _sigmoid(acc + bias), acc).astype(BF16)


def _in_proj(x2d, g, w_bf, b_flat):
    m, d = x2d.shape
    n = w_bf.shape[1]
    tm = min(1024, m)
    tn = 1024
    n_plain = GATE_COL0 // tn
    return pl.pallas_call(
        functools.partial(_inproj_kernel, n_plain=n_plain),
        out_shape=jax.ShapeDtypeStruct((m, n), BF16),
        grid=(m // tm, n // tn),
        in_specs=[
            pl.BlockSpec((tm, d), lambda i, j: (i, 0)),
            _const_spec((1, d)),
            pl.BlockSpec((d, tn), lambda i, j: (0, j)),
            _const_spec((b_flat.shape[1] // tn, 1, tn)),
        ],
        out_specs=pl.BlockSpec((tm, tn), lambda i, j: (i, j)),
        scratch_shapes=[pltpu.VMEM((tm, d), BF16)],
        compiler_params=_cparams(("parallel", "arbitrary"), 48),
        name="in_proj",
    )(x2d, g, w_bf, b_flat.reshape(b_flat.shape[1] // tn, 1, tn))


def _filt_mlp_kernel(emb_ref, w1_ref, b1_ref, w2_ref, b2_ref, fr_ref, w3_ref, dl_ref, ht_ref, sum_ref):
    i = pl.program_id(0)
    hp = lax.Precision.HIGHEST
    emb = emb_ref[...]
    fr = fr_ref[...]
    h1 = jnp.sin(fr * (jnp.dot(emb, w1_ref[...], precision=hp, preferred_element_type=F32) + b1_ref[...]))
    h2 = jnp.sin(fr * (jnp.dot(h1, w2_ref[...], precision=hp, preferred_element_type=F32) + b2_ref[...]))
    h = jnp.dot(h2, w3_ref[...], precision=hp, preferred_element_type=F32)
    dec = jnp.exp(-emb[:, 0:1] * dl_ref[...])
    ht = jnp.concatenate([h[:, :HY_CH] * dec, h[:, HY_CH:] * dec], axis=1).T
    ht_ref[...] = ht
    ab = jnp.abs(ht)
    part = ab[:, 0:LANES]
    for c in range(1, ab.shape[1] // LANES):
        part = part + ab[:, c * LANES:(c + 1) * LANES]

    @pl.when(i == 0)
    def _():
        sum_ref[...] = part

    @pl.when(i > 0)
    def _():
        sum_ref[...] += part


def _filt_mlp(seq, w1, b1, w2, b2, freq, w3):
    t = jnp.linspace(0.0, 1.0, seq, dtype=F32)[:, None]
    wpos = 2.0 * math.pi * jnp.arange(seq, dtype=F32) / seq
    bands = jnp.linspace(1e-4, FILT_BANDS - 1, FILT_BANDS, dtype=F32)
    ang = wpos[:, None] * bands[None, :]
    emb = jnp.concatenate([t, jnp.cos(ang), -jnp.sin(ang)], axis=-1)
    emb = jnp.pad(emb, ((0, 0), (0, LANES - FILT_EMB)))
    pad_o = LANES - FILT_ORDER
    w1p = jnp.pad(w1, ((0, LANES - FILT_EMB), (0, pad_o)))
    w2p = jnp.pad(w2, ((0, pad_o), (0, pad_o)))
    w3p = jnp.pad(w3, ((0, pad_o), (0, 0)))
    b1p = jnp.pad(b1[None, :], ((0, 0), (0, pad_o)))
    b2p = jnp.pad(b2[None, :], ((0, 0), (0, pad_o)))
    frp = jnp.pad(freq[None, :], ((0, 0), (0, pad_o)))
    deltas = jnp.abs(jnp.linspace(FILT_MIN_DECAY, FILT_MAX_DECAY, HY_CH, dtype=F32))[None, :]
    tl = min(512, seq)
    return pl.pallas_call(
        _filt_mlp_kernel,
        out_shape=(jax.ShapeDtypeStruct((2 * HY_CH, seq), F32),
                   jax.ShapeDtypeStruct((2 * HY_CH, LANES), F32)),
        grid=(seq // tl,),
        in_specs=[
            pl.BlockSpec((tl, LANES), lambda i: (i, 0)),
            _const_spec((LANES, LANES)), _const_spec((1, LANES)),
            _const_spec((LANES, LANES)), _const_spec((1, LANES)),
            _const_spec((1, LANES)), _const_spec((LANES, 2 * HY_CH)),
            _const_spec((1, HY_CH)),
        ],
        out_specs=(pl.BlockSpec((2 * HY_CH, tl), lambda i: (0, i)),
                   pl.BlockSpec((2 * HY_CH, LANES), lambda i: (0, 0))),
        compiler_params=_cparams(("arbitrary",), 48),
        name="filt_mlp",
    )(emb, w1p, b1p, w2p, b2p, frp, w3p, deltas)


def _dot_t(a, b):
    return lax.dot_general(a, b, (((1,), (1,)), ((), ())), preferred_element_type=F32)


def _filt_spec_kernel(hf_ref, hb_ref, sf_ref, sb_ref, r_ref, u1_ref, v2_ref, kl_ref, *, tf, seq):
    f = pl.program_id(1)
    norm = jnp.sum(sf_ref[...] + sb_ref[...], axis=1, keepdims=True) + NORM_EPS
    hf = hf_ref[...] / norm
    lane = lax.broadcasted_iota(jnp.int32, hf.shape, 1)
    hb = jnp.where(lane == 0, 0.0, hb_ref[...] / norm)
    s1 = (hf + hb).astype(BF16)
    s2 = (hf - hb).astype(BF16)
    c = _dot_t(s1, r_ref[:tf, :]) * (1.0 / seq)
    col = lax.broadcasted_iota(jnp.int32, c.shape, 1) + f * tf
    u1_ref[...] = jnp.where(col == 0, 0.5 * c, c)
    v2_ref[...] = jnp.where(col == 0, 0.0, _dot_t(s2, r_ref[tf:, :]) * (1.0 / seq))

    @pl.when(f == 0)
    def _():
        nyq = _dot_t(s1, r_ref[tf:, :])[:, 0:1] * (0.5 / seq)
        kl_ref[...] = jnp.broadcast_to(nyq, kl_ref.shape)


def _filt_spec(ht, sums, rmat, tf):
    seq = ht.shape[1]
    tc = 256
    nc = HY_CH // tc
    return pl.pallas_call(
        functools.partial(_filt_spec_kernel, tf=tf, seq=seq),
        out_shape=(jax.ShapeDtypeStruct((HY_CH, seq), F32),
                   jax.ShapeDtypeStruct((HY_CH, seq), F32),
                   jax.ShapeDtypeStruct((HY_CH, LANES), F32)),
        grid=(nc, seq // tf),
        in_specs=[
            pl.BlockSpec((tc, seq), lambda c, f: (c, 0)),
            pl.BlockSpec((tc, seq), lambda c, f: (nc + c, 0)),
            pl.BlockSpec((tc, LANES), lambda c, f: (c, 0)),
            pl.BlockSpec((tc, LANES), lambda c, f: (nc + c, 0)),
            pl.BlockSpec((2 * tf, seq), lambda c, f: (f, 0)),
        ],
        out_specs=(pl.BlockSpec((tc, tf), lambda c, f: (c, f)),
                   pl.BlockSpec((tc, tf), lambda c, f: (c, f)),
                   pl.BlockSpec((tc, LANES), lambda c, f: (c, 0))),
        compiler_params=_cparams(("parallel", "arbitrary"), 48),
        name="filt_spec",
    )(ht, ht, sums, sums, rmat)


def _hyprep_kernel(u0_ref, u1_ref, u2_ref, w0_ref, w1_ref, w2_ref, b0_ref, b1_ref, b2_ref, x0_ref, wt_ref):
    seq = u0_ref.shape[0]
    row = lax.broadcasted_iota(jnp.int32, u0_ref.shape, 0)

    def sconv(u_ref, w_ref, b_ref):
        u = u_ref[...].astype(F32)
        um = jnp.where(row == 0, 0.0, pltpu.roll(u, 1, 0))
        up = jnp.where(row == seq - 1, 0.0, pltpu.roll(u, seq - 1, 0))
        w = w_ref[...]
        return b_ref[...] + um * w[0:1] + u * w[1:2] + up * w[2:3]

    x0_ref[...] = sconv(u0_ref, w0_ref, b0_ref).astype(BF16)
    prod = sconv(u1_ref, w1_ref, b1_ref) * sconv(u2_ref, w2_ref, b2_ref)
    wt_ref[...] = prod.T.astype(BF16)


def _hy_prep(z3, conv_w, conv_b):
    b, seq, _ = z3.shape
    tc = LANES
    nc = HY_CH // tc
    uspec = lambda g: pl.BlockSpec((None, seq, tc), lambda i, c: (i, 0, g * nc + c))
    wspec = lambda g: pl.BlockSpec((3, tc), lambda i, c: (0, g * nc + c))
    bspec = lambda g: pl.BlockSpec((1, tc), lambda i, c: (0, g * nc + c))
    return pl.pallas_call(
        _hyprep_kernel,
        out_shape=(jax.ShapeDtypeStruct((b, seq, HY_CH), BF16),
                   jax.ShapeDtypeStruct((b, HY_CH, seq), BF16)),
        grid=(b, nc),
        in_specs=[uspec(0), uspec(1), uspec(2), wspec(0), wspec(1), wspec(2), bspec(0), bspec(1), bspec(2)],
        out_specs=(pl.BlockSpec((None, seq, tc), lambda i, c: (i, 0, c)),
                   pl.BlockSpec((None, tc, seq), lambda i, c: (i, c, 0))),
        compiler_params=_cparams(("parallel", "parallel"), 48),
        name="hy_prep",
    )(z3, z3, z3, conv_w, conv_w, conv_w, conv_b, conv_b, conv_b)


def _conv_kernel(wt_ref, ra_ref, rb_ref, u1_ref, v2_ref, kl_ref, d_ref, o_ref, acc_ref, za_ref, zb_ref, *, tf):
    t = pl.program_id(2)
    n_f = pl.num_programs(2) - 1

    def work(z_out, z_in, first=False):
        if z_in is not None:
            acc_ref[...] += jnp.dot(z_in[...], rb_ref[...], preferred_element_type=F32)
        if z_out is not None:
            ap = _dot_t(wt_ref[...], ra_ref[...])
            a = ap[:, :tf]
            p = ap[:, tf:]
            u1 = u1_ref[...]
            v2 = v2_ref[...]
            z2 = p * u1 + a * v2
            if first:
                col = lax.broadcasted_iota(jnp.int32, z2.shape, 1)
                z2 = jnp.where(col == 0, p * kl_ref[:, 0:1], z2)
            z_out[...] = jnp.concatenate([a * u1 - p * v2, z2], axis=1).astype(BF16)

    @pl.when(t == 0)
    def _():
        acc_ref[...] = jnp.zeros_like(acc_ref)
        work(za_ref, None, first=True)

    steady = jnp.logical_and(t > 0, t < n_f)

    @pl.when(jnp.logical_and(steady, t % 2 == 1))
    def _():
        work(zb_ref, za_ref)

    @pl.when(jnp.logical_and(steady, t % 2 == 0))
    def _():
        work(za_ref, zb_ref)

    def finish(z_in):
        work(None, z_in)
        y = acc_ref[...] + d_ref[:, 0:1] * wt_ref[...].astype(F32)
        o_ref[...] = y.T.astype(BF16)

    @pl.when(jnp.logical_and(t == n_f, t % 2 == 1))
    def _():
        finish(za_ref)

    @pl.when(jnp.logical_and(t == n_f, t % 2 == 0))
    def _():
        finish(zb_ref)


def _long_conv(wt, rmat, u1, v2, kl, dcol, tf):
    b, _, seq = wt.shape
    tc = 512
    n_f = seq // tf
    return pl.pallas_call(
        functools.partial(_conv_kernel, tf=tf),
        out_shape=jax.ShapeDtypeStruct((b, seq, HY_CH), BF16),
        grid=(b, HY_CH // tc, n_f + 1),
        in_specs=[
            pl.BlockSpec((None, tc, seq), lambda i, c, t: (i, c, 0)),
            pl.BlockSpec((2 * tf, seq), lambda i, c, t: (jnp.minimum(t, n_f - 1), 0)),
            pl.BlockSpec((2 * tf, seq), lambda i, c, t: (jnp.maximum(t - 1, 0), 0)),
            pl.BlockSpec((tc, tf), lambda i, c, t: (c, jnp.minimum(t, n_f - 1))),
            pl.BlockSpec((tc, tf), lambda i, c, t: (c, jnp.minimum(t, n_f - 1))),
            pl.BlockSpec((tc, LANES), lambda i, c, t: (c, 0)),
            pl.BlockSpec((tc, LANES), lambda i, c, t: (c, 0)),
        ],
        out_specs=pl.BlockSpec((None, seq, tc), lambda i, c, t: (i, 0, c)),
        scratch_shapes=[pltpu.VMEM((tc, seq), F32), pltpu.VMEM((tc, 2 * tf), BF16), pltpu.VMEM((tc, 2 * tf), BF16)],
        compiler_params=_cparams(("parallel", "parallel", "arbitrary"), 56),
        name="long_conv",
    )(wt, rmat, rmat, u1, v2, kl, dcol)


def _dft_gen_kernel(ch_ref, sh_ref, cl_ref, sl_ref, r_ref, *, tf):
    ch, sh = ch_ref[...], sh_ref[...]
    cl, sl = cl_ref[...], sl_ref[...]
    r_ref[:tf, :] = (ch * cl - sh * sl).astype(BF16)
    sin = sh * cl + ch * sl
    row = lax.broadcasted_iota(jnp.int32, sin.shape, 0)
    lane = lax.broadcasted_iota(jnp.int32, sin.shape, 1)
    alt = (1 - 2 * (lane % 2)).astype(F32)
    nyq_row = jnp.logical_and(row == 0, pl.program_id(0) == 0)
    r_ref[tf:, :] = jnp.where(nyq_row, alt, sin).astype(BF16)


def _dft_tables(seq, tf):
    idx = jnp.arange(seq, dtype=jnp.int32)
    nf = seq // tf

    def trig(freqs):
        ang = ((freqs[:, None] * idx[None, :]) % (2 * seq)).astype(F32) * (math.pi / seq)
        return jnp.cos(ang), jnp.sin(ang)

    ch, sh = trig(jnp.arange(nf, dtype=jnp.int32) * tf)
    cl, sl = trig(jnp.arange(tf, dtype=jnp.int32))
    base_spec = pl.BlockSpec((None, 1, seq), lambda f: (f, 0, 0))
    return pl.pallas_call(
        functools.partial(_dft_gen_kernel, tf=tf),
        out_shape=jax.ShapeDtypeStruct((2 * seq, seq), BF16),
        grid=(nf,),
        in_specs=[base_spec, base_spec, _const_spec((tf, seq)), _const_spec((tf, seq))],
        out_specs=pl.BlockSpec((2 * tf, seq), lambda f: (f, 0)),
        compiler_params=_cparams(("parallel",), 32),
        name="dft_tables",
    )(ch[:, None, :], sh[:, None, :], cl, sl)


KV_CHUNK = 256


def _attn_kernel(par_ref, q_ref, k_ref, v_ref, tab_ref, o_ref, krot_ref, vt_ref,
                 sa_buf, sb_buf, ma_buf, mb_buf,
                 *, tq, lam_init):
    t = pl.program_id(2)
    n_q = pl.num_programs(2) - 1
    seq = k_ref.shape[0]
    ck = min(KV_CHUNK, seq)
    n_chunks = seq // ck

    def rope(x, rows):
        c, sa, sb = (tab_ref[rows, i * HEAD_W:(i + 1) * HEAD_W] for i in range(3))
        return x * c + pltpu.roll(x, LANES - ROPE_DIM // 2, 1) * sa + pltpu.roll(x, ROPE_DIM // 2, 1) * sb

    buf_a = (sa_buf, ma_buf)
    buf_b = (sb_buf, mb_buf)

    def tile_work(score, finish):
        if score is not None:
            rows = pl.ds(pl.multiple_of(t * tq, tq), tq)
            q = rope(q_ref[rows, :].astype(F32), rows)
            q = q * (HEAD_DIM ** -0.5 * LOG2E)
            lane = lax.broadcasted_iota(jnp.int32, q.shape, 1)
            qq = jnp.concatenate([jnp.where(lane < HEAD_DIM, q, 0.0), jnp.where(lane >= HEAD_DIM, q, 0.0)], axis=0)
            qq = qq.astype(BF16)
            m8 = jnp.full((8, 2 * tq), -jnp.inf, F32)
        if finish is not None:
            m = jnp.max(finish[1][...], axis=0, keepdims=True)
            l8 = jnp.zeros((8, 2 * tq), F32)
            ot = jnp.zeros((HEAD_W, 2 * tq), F32)

        for c in range(n_chunks):
            kv = slice(c * ck, (c + 1) * ck)
            if finish is not None:
                p_c = jnp.exp2(finish[0][kv, :] - m)
                l8 = l8 + jnp.sum(p_c.reshape(ck // 8, 8, 2 * tq), axis=0)
                ot = ot + jnp.dot(vt_ref[:, kv], p_c.astype(BF16), preferred_element_type=F32)
            if score is not None:
                s_c = _dot_t(krot_ref[kv, :], qq)
                score[0][kv, :] = s_c
                m8 = jnp.maximum(m8, jnp.max(s_c.reshape(ck // 8, 8, 2 * tq), axis=0))
        if score is not None:
            score[1][...] = m8
        if finish is not None:
            l = jnp.sum(l8, axis=0, keepdims=True)
            lp = par_ref[0:4, :]
            lam = (jnp.exp(jnp.sum(lp[0:1] * lp[1:2], axis=-1, keepdims=True))
                   - jnp.exp(jnp.sum(lp[2:3] * lp[3:4], axis=-1, keepdims=True)) + lam_init)
            r = 1.0 / l
            o = (ot[:, :tq] * r[:, :tq] - ot[:, tq:] * (lam * r[:, tq:])).T
            o_ref[...] = (_rms(o, par_ref[4:5, :]) * (1.0 - lam_init)).astype(BF16)

    @pl.when(t == 0)
    def _():
        krot_ref[...] = rope(k_ref[...].astype(F32), slice(None)).astype(BF16)
        vt_ref[...] = v_ref[...].astype(F32).T.astype(BF16)
        tile_work(buf_a, None)

    steady = jnp.logical_and(t > 0, t < n_q)

    @pl.when(jnp.logical_and(steady, t % 2 == 1))
    def _():
        tile_work(buf_b, buf_a)

    @pl.when(jnp.logical_and(steady, t % 2 == 0))
    def _():
        tile_work(buf_a, buf_b)

    @pl.when(jnp.logical_and(t == n_q, t % 2 == 1))
    def _():
        tile_work(None, buf_a)

    @pl.when(jnp.logical_and(t == n_q, t % 2 == 0))
    def _():
        tile_work(None, buf_b)


def _attention(z3, lam_params, g_subln, lam_init):
    b, seq, _ = z3.shape
    tq = min(256, seq)
    n_q = seq // tq
    inv = ROPE_THETA ** (-jnp.arange(0, ROPE_DIM, 2, dtype=F32) / ROPE_DIM)
    ang = jnp.arange(seq, dtype=F32)[:, None] * inv[None, :]
    c8, s8 = jnp.cos(ang), jnp.sin(ang)
    half = ROPE_DIM // 2
    rest = HEAD_DIM - ROPE_DIM
    one, zero = jnp.ones((seq, rest), F32), jnp.zeros((seq, rest), F32)
    z8 = jnp.zeros((seq, half), F32)
    cos_t = jnp.tile(jnp.concatenate([c8, c8, one], axis=1), (1, 2))
    sa_t = jnp.tile(jnp.concatenate([-s8, z8, zero], axis=1), (1, 2))
    sb_t = jnp.tile(jnp.concatenate([z8, s8, zero], axis=1), (1, 2))
    tab = jnp.concatenate([cos_t, sa_t, sb_t], axis=1)
    par = jnp.concatenate([jnp.pad(lam_params, ((0, 0), (0, HEAD_W - HEAD_DIM))), g_subln,
                           jnp.zeros((3, HEAD_W), F32)], axis=0)
    qb, kb, vb = (HY_COLS // HEAD_W, (HY_COLS + ATT_WIDTH) // HEAD_W, (HY_COLS + 2 * ATT_WIDTH) // HEAD_W)
    head_cols = lambda blk: pl.BlockSpec((None, seq, HEAD_W), lambda i, h, t: (i, 0, blk + h))
    return pl.pallas_call(
        functools.partial(_attn_kernel, tq=tq, lam_init=lam_init),
        out_shape=jax.ShapeDtypeStruct((b, seq, ATT_WIDTH), BF16),
        grid=(b, N_HEADS, n_q + 1),
        in_specs=[_const_spec((8, HEAD_W)), head_cols(qb), head_cols(kb), head_cols(vb),
                  _const_spec((seq, 3 * HEAD_W))],
        out_specs=pl.BlockSpec((None, tq, HEAD_W), lambda i, h, t: (i, jnp.maximum(t - 1, 0), h)),
        scratch_shapes=[pltpu.VMEM((seq, HEAD_W), BF16), pltpu.VMEM((HEAD_W, seq), BF16),
                        pltpu.VMEM((seq, 2 * tq), F32), pltpu.VMEM((seq, 2 * tq), F32),
                        pltpu.VMEM((8, 2 * tq), F32), pltpu.VMEM((8, 2 * tq), F32)],
        compiler_params=_cparams(("parallel", "parallel", "arbitrary"), 56),
        name="attention",
    )(par, z3, z3, z3, tab)


def _merge_kernel(yc_ref, x0_ref, ya_ref, g0_ref, g1_ref, x_ref, why_ref, wat_ref, wo_ref, gp_ref, o_ref):
    yh = (x0_ref[...].astype(F32) * yc_ref[...].astype(F32)).astype(BF16)
    a = jnp.dot(yh, why_ref[...], preferred_element_type=F32)
    b = jnp.dot(ya_ref[...], wat_ref[...], preferred_element_type=F32)
    m = g0_ref[...].astype(F32) * a + g1_ref[...].astype(F32) * b
    r = jnp.dot(m.astype(BF16), wo_ref[...], preferred_element_type=F32)
    o_ref[...] = x_ref[...] + _rms(r, gp_ref[...])


def _merge(yconv, x0c, yatt, z, x2d, w_hy, w_att, w_out, g_post):
    m, d = x2d.shape
    tm = min(256, m)
    row = lambda width, blk=0: pl.BlockSpec((tm, width), lambda i: (i, blk))
    gate_blk = GATE_COL0 // d
    return pl.pallas_call(
        _merge_kernel,
        out_shape=jax.ShapeDtypeStruct((m, d), F32),
        grid=(m // tm,),
        in_specs=[row(HY_CH), row(HY_CH), row(ATT_WIDTH), row(d, gate_blk), row(d, gate_blk + 1), row(d),
                  _const_spec((HY_CH, d)), _const_spec((ATT_WIDTH, d)), _const_spec((d, d)), _const_spec((1, d))],
        out_specs=row(d),
        compiler_params=_cparams(("parallel",), 56),
        name="merge",
    )(yconv, x0c, yatt, z, z, x2d, w_hy, w_att, w_out, g_post)


def _ffn_kernel(x_ref, p_ref, g_ref, wgu_ref, wo_ref, wpg_ref, wpi_ref, o_ref, h_ref, acc_ref, *, tf):
    j = pl.program_id(1)
    gpre_ref, gpost_ref, gple_ref = (g_ref.at[pl.ds(r, 1), :] for r in range(3))

    @pl.when(j == 0)
    def _():
        _rms_rows_to(h_ref, x_ref, gpre_ref)
        acc_ref[...] = jnp.zeros_like(acc_ref)

    gu = jnp.dot(h_ref[...], wgu_ref[...], preferred_element_type=F32)
    gate = gu[:, :tf]
    up = gu[:, tf:]
    act = (gate * _sigmoid(gate) * up).astype(BF16)
    acc_ref[...] += jnp.dot(act, wo_ref[...], preferred_element_type=F32)

    @pl.when(j == pl.num_programs(1) - 1)
    def _():
        def body(rows):
            x2 = x_ref[rows, :] + _rms(acc_ref[rows, :], gpost_ref[...])
            e = jnp.dot(_rms(x2, gple_ref[...]).astype(BF16), wpg_ref[...], preferred_element_type=F32)
            pe = jnp.dot(p_ref[rows, :].astype(BF16), wpi_ref[...], preferred_element_type=F32)
            o_ref[rows, :] = x2 + pe * _sigmoid(e)
        _for_row_chunks(x_ref.shape[0], body, chunk=2 * ROW_CHUNK)


FFN_TILE = 512


def _ffn_ple(x1, p2d, gains, w_gu, w_out, w_pg, w_pi):
    m, d = x1.shape
    tm = min(512, m)
    tf = FFN_TILE
    nff = D_FF // tf
    return pl.pallas_call(
        functools.partial(_ffn_kernel, tf=tf),
        out_shape=jax.ShapeDtypeStruct((m, d), F32),
        grid=(m // tm, nff),
        in_specs=[
            pl.BlockSpec((tm, d), lambda i, j: (i, 0)),
            pl.BlockSpec((tm, PLE_DIM), lambda i, j: (i, 0)),
            _const_spec((3, d)),
            pl.BlockSpec((d, 2 * tf), lambda i, j: (0, j)),
            pl.BlockSpec((tf, d), lambda i, j: (j, 0)),
            _const_spec((d, d)), _const_spec((PLE_DIM, d)),
        ],
        out_specs=pl.BlockSpec((tm, d), lambda i, j: (i, 0)),
        scratch_shapes=[pltpu.VMEM((tm, d), BF16), pltpu.VMEM((tm, d), F32)],
        compiler_params=_cparams(("parallel", "arbitrary"), 56),
        name="ffn_ple",
    )(x1, p2d, gains, w_gu, w_out, w_pg, w_pi)


def _layer(x, p, lam_init, wts):
    b, seq, d = x.shape
    x2d = x.reshape(b * seq, d)
    z = _in_proj(x2d, wts["g_mix_pre"], wts["w_in"], wts["b_gate"])
    z3 = z.reshape(b, seq, IN_COLS)

    tf = min(256, seq)
    rmat = _dft_tables(seq, tf)
    ht, sums = _filt_mlp(seq, wts["filt_w1"], wts["filt_b1"], wts["filt_w2"], wts["filt_b2"],
                         wts["filt_freq"], wts["filt_w3"])
    u1, v2, kl = _filt_spec(ht, sums, rmat, tf)
    x0c, wt = _hy_prep(z3, wts["conv_w"], wts["conv_b"])
    yconv = _long_conv(wt, rmat, u1, v2, kl, wts["hyena_d"], tf)

    yatt = _attention(z3, wts["lam"], wts["g_subln"], lam_init)

    x1 = _merge(yconv.reshape(b * seq, HY_CH), x0c.reshape(b * seq, HY_CH), yatt.reshape(b * seq, ATT_WIDTH),
                z, x2d, wts["w_hy_out"], wts["w_att_out"], wts["w_out"], wts["g_mix_post"])
    y = _ffn_ple(x1, p.reshape(b * seq, PLE_DIM), wts["ffn_gains"], wts["w_ffn_gu"], wts["w_ffn_out"],
                 wts["w_ple_gate"], wts["w_ple_in"])
    return y.reshape(b, seq, d)


def kernel(x_prompt, x_sample, p_prompt, p_sample, g_mix_pre, g_mix_post, g_ffn_pre, g_ffn_post, g_ple, w_in, b_gate, conv_w, conv_b, filt_w1, filt_b1, filt_w2, filt_b2, filt_freq, filt_w3, hyena_d, lam_q1, lam_k1, lam_q2, lam_k2, g_subln, w_hy_out, w_att_out, w_out, w_ffn_in, w_ffn_out, w_ple_in, w_ple_gate):
    depth = w_in.shape[0]
    xs = [x_prompt, x_sample]
    ps = [p_prompt, p_sample]
    for i in range(depth):
        lam_init = 0.8 - 0.6 * math.exp(-0.3 * i)
        wts = dict(
            g_mix_pre=g_mix_pre[i][None, :], g_mix_post=g_mix_post[i][None, :],
            ffn_gains=jnp.stack([g_ffn_pre[i], g_ffn_post[i], g_ple[i]]),
            w_in=w_in[i].astype(BF16), b_gate=b_gate[i].reshape(1, 2 * D_MODEL),
            conv_w=conv_w[i], conv_b=conv_b[i][None, :],
            filt_w1=filt_w1[i], filt_b1=filt_b1[i], filt_w2=filt_w2[i], filt_b2=filt_b2[i],
            filt_freq=filt_freq[i], filt_w3=filt_w3[i],
            hyena_d=jnp.broadcast_to(hyena_d[i][:, None], (HY_CH, LANES)),
            lam=jnp.stack([lam_q1[i], lam_k1[i], lam_q2[i], lam_k2[i]]),
            g_subln=g_subln[i][None, :],
            w_hy_out=w_hy_out[i].astype(BF16), w_att_out=w_att_out[i].astype(BF16),
            w_out=w_out[i].astype(BF16),
            w_ffn_gu=w_ffn_in[i].reshape(D_MODEL, 2, D_FF // FFN_TILE, FFN_TILE).transpose(0, 2, 1, 3)
            .reshape(D_MODEL, 2 * D_FF).astype(BF16),
            w_ffn_out=w_ffn_out[i].astype(BF16), w_ple_in=w_ple_in[i].astype(BF16),
            w_ple_gate=w_ple_gate[i].astype(BF16),
        )
        xs = [_layer(x, p[i], lam_init, wts) for x, p in zip(xs, ps)]
    return (xs[0], xs[1])
```

```python
import functools
import math

import jax
import jax.numpy as jnp
from jax import lax
from jax.experimental import pallas as pl
from jax.experimental.pallas import tpu as pltpu

F32 = jnp.float32
BF16 = jnp.bfloat16

D_MODEL = 2048
PLE_DIM = 256
NORM_EPS = 1e-6
HY_CH = D_MODEL // 2
FILT_BANDS = 16
FILT_EMB = 1 + 2 * FILT_BANDS
FILT_ORDER = 64
FILT_MAX_DECAY = math.log(1e-2) / 0.3
FILT_MIN_DECAY = math.log(1e-2) / 1.5
ATT_WIDTH = D_MODEL // 2
N_HEADS = 8
HEAD_DIM = ATT_WIDTH // (2 * N_HEADS)
HEAD_W = 2 * HEAD_DIM
ROPE_DIM = HEAD_DIM // 4
ROPE_THETA = 500000.0
D_FF = ((8 * D_MODEL + 3 * 256 - 1) // (3 * 256)) * 256
HY_COLS = 3 * HY_CH
GATE_COL0 = HY_COLS + 3 * ATT_WIDTH
IN_COLS = GATE_COL0 + 2 * D_MODEL
LANES = 128
LONG_SEQ = 2048
LOG2E = math.log2(math.e)
MIB = 2 ** 20


def _cparams(sem, vmem_mib):
    return pltpu.CompilerParams(dimension_semantics=sem, vmem_limit_bytes=vmem_mib * MIB)


def _const_spec(shape):
    nd = len(shape)
    return pl.BlockSpec(shape, lambda *_: (0,) * nd, pipeline_mode=pl.Buffered(1))


def _rms(x, g):
    ms = jnp.mean(x * x, axis=-1, keepdims=True)
    return x * lax.rsqrt(ms + NORM_EPS) * g


def _sigmoid(x):
    return 1.0 / (1.0 + jnp.exp(-x))


ROW_CHUNK = 128


def _for_row_chunks(n_rows, body, chunk=ROW_CHUNK):
    chunk = min(chunk, n_rows)

    def step(c, carry):
        body(pl.ds(pl.multiple_of(c * chunk, chunk), chunk))
        return carry
    lax.fori_loop(0, n_rows // chunk, step, 0)


def _rms_rows_to(h_ref, x_ref, g_ref):
    def body(rows):
        h_ref[rows, :] = _rms(x_ref[rows, :], g_ref[...]).astype(h_ref.dtype)
    _for_row_chunks(x_ref.shape[0], body)


def _inproj_kernel(x_ref, g_ref, w_ref, b_ref, o_ref, h_ref, *, n_plain):
    j = pl.program_id(1)

    @pl.when(j == 0)
    def _():
        _rms_rows_to(h_ref, x_ref, g_ref)

    acc = jnp.dot(h_ref[...], w_ref[...], preferred_element_type=F32)
    o_ref[...] = jnp.where(j >= n_plain, _sigmoid(acc + b_ref[...]), acc).astype(BF16)


def _in_proj(x2d, g, w_bf, b_flat):
    m, d = x2d.shape
    n = w_bf.shape[1]
    tm = min(1024, m)
    tn = 1024
    n_plain = GATE_COL0 // tn
    return pl.pallas_call(
        functools.partial(_inproj_kernel, n_plain=n_plain),
        out_shape=jax.ShapeDtypeStruct((m, n), BF16),
        grid=(m // tm, n // tn),
        in_specs=[
            pl.BlockSpec((tm, d), lambda i, j: (i, 0)),
            pl.BlockSpec((1, d), lambda i, j: (0, 0)),
            pl.BlockSpec((d, tn), lambda i, j: (0, j)),
            pl.BlockSpec((1, tn), lambda i, j: (0, jnp.maximum(j - n_plain, 0))),
        ],
        out_specs=pl.BlockSpec((tm, tn), lambda i, j: (i, j)),
        scratch_shapes=[pltpu.VMEM((tm, d), BF16)],
        compiler_params=_cparams(("parallel", "arbitrary"), 48),
        name="in_proj",
    )(x2d, g, w_bf, b_flat)


def _filt_mlp_kernel(emb_ref, w1_ref, b1_ref, w2_ref, b2_ref, fr_ref, w3_ref, dl_ref, ht_ref, sum_ref):
    i = pl.program_id(0)
    hp = lax.Precision.HIGHEST
    emb = emb_ref[...]
    fr = fr_ref[...]
    h1 = jnp.sin(fr * (jnp.dot(emb, w1_ref[...], precision=hp, preferred_element_type=F32) + b1_ref[...]))
    h2 = jnp.sin(fr * (jnp.dot(h1, w2_ref[...], precision=hp, preferred_element_type=F32) + b2_ref[...]))
    h = jnp.dot(h2, w3_ref[...], precision=hp, preferred_element_type=F32)
    dec = jnp.exp(-emb[:, 0:1] * dl_ref[...])
    ht = jnp.concatenate([h[:, :HY_CH] * dec, h[:, HY_CH:] * dec], axis=1).T
    ht_ref[...] = ht
    ab = jnp.abs(ht)
    part = ab[:, 0:LANES]
    for c in range(1, ab.shape[1] // LANES):
        part = part + ab[:, c * LANES:(c + 1) * LANES]

    @pl.when(i == 0)
    def _():
        sum_ref[...] = part

    @pl.when(i > 0)
    def _():
        sum_ref[...] += part


def _filt_mlp(seq, w1, b1, w2, b2, freq, w3):
    t = jnp.linspace(0.0, 1.0, seq, dtype=F32)[:, None]
    wpos = 2.0 * math.pi * jnp.arange(seq, dtype=F32) / seq
    bands = jnp.linspace(1e-4, FILT_BANDS - 1, FILT_BANDS, dtype=F32)
    ang = wpos[:, None] * bands[None, :]
    emb = jnp.concatenate([t, jnp.cos(ang), -jnp.sin(ang)], axis=-1)
    emb = jnp.pad(emb, ((0, 0), (0, LANES - FILT_EMB)))
    pad_o = LANES - FILT_ORDER
    w1p = jnp.pad(w1, ((0, LANES - FILT_EMB), (0, pad_o)))
    w2p = jnp.pad(w2, ((0, pad_o), (0, pad_o)))
    w3p = jnp.pad(w3, ((0, pad_o), (0, 0)))
    b1p = jnp.pad(b1[None, :], ((0, 0), (0, pad_o)))
    b2p = jnp.pad(b2[None, :], ((0, 0), (0, pad_o)))
    frp = jnp.pad(freq[None, :], ((0, 0), (0, pad_o)))
    deltas = jnp.abs(jnp.linspace(FILT_MIN_DECAY, FILT_MAX_DECAY, HY_CH, dtype=F32))[None, :]
    tl = min(512, seq)
    return pl.pallas_call(
        _filt_mlp_kernel,
        out_shape=(jax.ShapeDtypeStruct((2 * HY_CH, seq), F32),
                   jax.ShapeDtypeStruct((2 * HY_CH, LANES), F32)),
        grid=(seq // tl,),
        in_specs=[
            pl.BlockSpec((tl, LANES), lambda i: (i, 0)),
            _const_spec((LANES, LANES)), _const_spec((1, LANES)),
            _const_spec((LANES, LANES)), _const_spec((1, LANES)),
            _const_spec((1, LANES)), _const_spec((LANES, 2 * HY_CH)),
            _const_spec((1, HY_CH)),
        ],
        out_specs=(pl.BlockSpec((2 * HY_CH, tl), lambda i: (0, i)),
                   pl.BlockSpec((2 * HY_CH, LANES), lambda i: (0, 0))),
        compiler_params=_cparams(("arbitrary",), 48),
        name="filt_mlp",
    )(emb, w1p, b1p, w2p, b2p, frp, w3p, deltas)


def _dot_t(a, b):
    return lax.dot_general(a, b, (((1,), (1,)), ((), ())), preferred_element_type=F32)


def _filt_spec_kernel(hf_ref, hb_ref, sf_ref, sb_ref, r_ref, u1_ref, v2_ref, kl_ref, s1_ref, s2_ref, *, tf, seq):
    f = pl.program_id(1)

    @pl.when(f == 0)
    def _():
        norm = jnp.sum(sf_ref[...] + sb_ref[...], axis=1, keepdims=True) + NORM_EPS
        hf = hf_ref[...] / norm
        lane = lax.broadcasted_iota(jnp.int32, hf.shape, 1)
        hb = jnp.where(lane == 0, 0.0, hb_ref[...] / norm)
        s1_ref[...] = (hf + hb).astype(BF16)
        s2_ref[...] = (hf - hb).astype(BF16)

    s1 = s1_ref[...]
    s2 = s2_ref[...]
    c = _dot_t(s1, r_ref[:tf, :]) * (1.0 / seq)
    col = lax.broadcasted_iota(jnp.int32, c.shape, 1) + f * tf
    u1_ref[...] = jnp.where(col == 0, 0.5 * c, c)
    v2_ref[...] = jnp.where(col == 0, 0.0, _dot_t(s2, r_ref[tf:, :]) * (1.0 / seq))

    @pl.when(f == 0)
    def _():
        nyq = _dot_t(s1, r_ref[tf:, :])[:, 0:1] * (0.5 / seq)
        kl_ref[...] = jnp.broadcast_to(nyq, kl_ref.shape)


def _filt_spec(ht, sums, rmat, tf):
    seq = ht.shape[1]
    tc = 256
    nc = HY_CH // tc
    return pl.pallas_call(
        functools.partial(_filt_spec_kernel, tf=tf, seq=seq),
        out_shape=(jax.ShapeDtypeStruct((HY_CH, seq), F32),
                   jax.ShapeDtypeStruct((HY_CH, seq), F32),
                   jax.ShapeDtypeStruct((HY_CH, LANES), F32)),
        grid=(nc, seq // tf),
        in_specs=[
            pl.BlockSpec((tc, seq), lambda c, f: (c, 0)),
            pl.BlockSpec((tc, seq), lambda c, f: (nc + c, 0)),
            pl.BlockSpec((tc, LANES), lambda c, f: (c, 0)),
            pl.BlockSpec((tc, LANES), lambda c, f: (nc + c, 0)),
            pl.BlockSpec((2 * tf, seq), lambda c, f: (f, 0)),
        ],
        out_specs=(pl.BlockSpec((tc, tf), lambda c, f: (c, f)),
                   pl.BlockSpec((tc, tf), lambda c, f: (c, f)),
                   pl.BlockSpec((tc, LANES), lambda c, f: (c, 0))),
        scratch_shapes=[pltpu.VMEM((tc, seq), BF16), pltpu.VMEM((tc, seq), BF16)],
        compiler_params=_cparams(("parallel", "arbitrary"), 48),
        name="filt_spec",
    )(ht, ht, sums, sums, rmat)


def _hyprep_kernel(u0_ref, u1_ref, u2_ref, w0_ref, w1_ref, w2_ref, b0_ref, b1_ref, b2_ref, x0_ref, wt_ref):
    seq = u0_ref.shape[0]
    row = lax.broadcasted_iota(jnp.int32, u0_ref.shape, 0)

    def sconv(u_ref, w_ref, b_ref):
        u = u_ref[...].astype(F32)
        um = jnp.where(row == 0, 0.0, pltpu.roll(u, 1, 0))
        up = jnp.where(row == seq - 1, 0.0, pltpu.roll(u, seq - 1, 0))
        w = w_ref[...]
        return b_ref[...] + um * w[0:1] + u * w[1:2] + up * w[2:3]

    x0_ref[...] = sconv(u0_ref, w0_ref, b0_ref).astype(BF16)
    prod = sconv(u1_ref, w1_ref, b1_ref) * sconv(u2_ref, w2_ref, b2_ref)
    wt_ref[...] = prod.T.astype(BF16)


def _hy_prep(z3, conv_w, conv_b):
    b, seq, _ = z3.shape
    tc = LANES if seq > LONG_SEQ else 2 * LANES
    nc = HY_CH // tc
    uspec = lambda g: pl.BlockSpec((None, seq, tc), lambda i, c: (i, 0, g * nc + c))
    wspec = lambda g: pl.BlockSpec((3, tc), lambda i, c: (0, g * nc + c))
    bspec = lambda g: pl.BlockSpec((1, tc), lambda i, c: (0, g * nc + c))
    return pl.pallas_call(
        _hyprep_kernel,
        out_shape=(jax.ShapeDtypeStruct((b, seq, HY_CH), BF16),
                   jax.ShapeDtypeStruct((b, HY_CH, seq), BF16)),
        grid=(b, nc),
        in_specs=[uspec(0), uspec(1), uspec(2), wspec(0), wspec(1), wspec(2), bspec(0), bspec(1), bspec(2)],
        out_specs=(pl.BlockSpec((None, seq, tc), lambda i, c: (i, 0, c)),
                   pl.BlockSpec((None, tc, seq), lambda i, c: (i, c, 0))),
        compiler_params=_cparams(("parallel", "parallel"), 48),
        name="hy_prep",
    )(z3, z3, z3, conv_w, conv_w, conv_w, conv_b, conv_b, conv_b)


def _conv_kernel(wt_ref, ra_ref, rb_ref, u1_ref, v2_ref, kl_ref, d_ref, o_ref, acc_ref, za_ref, zb_ref, *, tf):
    t = pl.program_id(2)
    n_f = pl.num_programs(2) - 1

    def work(z_out, z_in, first=False):
        if z_in is not None:
            acc_ref[...] += jnp.dot(z_in[...], rb_ref[...], preferred_element_type=F32)
        if z_out is not None:
            ap = _dot_t(wt_ref[...], ra_ref[...])
            a = ap[:, :tf]
            p = ap[:, tf:]
            u1 = u1_ref[...]
            v2 = v2_ref[...]
            z2 = p * u1 + a * v2
            if first:
                col = lax.broadcasted_iota(jnp.int32, z2.shape, 1)
                z2 = jnp.where(col == 0, p * kl_ref[:, 0:1], z2)
            z_out[...] = jnp.concatenate([a * u1 - p * v2, z2], axis=1).astype(BF16)

    @pl.when(t == 0)
    def _():
        acc_ref[...] = jnp.zeros_like(acc_ref)
        work(za_ref, None, first=True)

    steady = jnp.logical_and(t > 0, t < n_f)

    @pl.when(jnp.logical_and(steady, t % 2 == 1))
    def _():
        work(zb_ref, za_ref)

    @pl.when(jnp.logical_and(steady, t % 2 == 0))
    def _():
        work(za_ref, zb_ref)

    def finish(z_in):
        work(None, z_in)
        y = acc_ref[...] + d_ref[:, 0:1] * wt_ref[...].astype(F32)
        o_ref[...] = y.T.astype(BF16)

    @pl.when(jnp.logical_and(t == n_f, t % 2 == 1))
    def _():
        finish(za_ref)

    @pl.when(jnp.logical_and(t == n_f, t % 2 == 0))
    def _():
        finish(zb_ref)


def _long_conv(wt, rmat, u1, v2, kl, dcol, tf):
    b, _, seq = wt.shape
    tc = 512
    n_f = seq // tf
    return pl.pallas_call(
        functools.partial(_conv_kernel, tf=tf),
        out_shape=jax.ShapeDtypeStruct((b, seq, HY_CH), BF16),
        grid=(b, HY_CH // tc, n_f + 1),
        in_specs=[
            pl.BlockSpec((None, tc, seq), lambda i, c, t: (i, c, 0)),
            pl.BlockSpec((2 * tf, seq), lambda i, c, t: (jnp.minimum(t, n_f - 1), 0)),
            pl.BlockSpec((2 * tf, seq), lambda i, c, t: (jnp.maximum(t - 1, 0), 0)),
            pl.BlockSpec((tc, tf), lambda i, c, t: (c, jnp.minimum(t, n_f - 1))),
            pl.BlockSpec((tc, tf), lambda i, c, t: (c, jnp.minimum(t, n_f - 1))),
            pl.BlockSpec((tc, LANES), lambda i, c, t: (c, 0)),
            pl.BlockSpec((tc, LANES), lambda i, c, t: (c, 0)),
        ],
        out_specs=pl.BlockSpec((None, seq, tc), lambda i, c, t: (i, 0, c)),
        scratch_shapes=[pltpu.VMEM((tc, seq), F32), pltpu.VMEM((tc, 2 * tf), BF16), pltpu.VMEM((tc, 2 * tf), BF16)],
        compiler_params=_cparams(("parallel", "parallel", "arbitrary"), 56),
        name="long_conv",
    )(wt, rmat, rmat, u1, v2, kl, dcol)


def _dft_gen_kernel(ch_ref, sh_ref, cl_ref, sl_ref, r_ref, *, tf):
    ch, sh = ch_ref[...], sh_ref[...]
    cl, sl = cl_ref[...], sl_ref[...]
    r_ref[:tf, :] = (ch * cl - sh * sl).astype(BF16)
    sin = sh * cl + ch * sl
    row = lax.broadcasted_iota(jnp.int32, sin.shape, 0)
    lane = lax.broadcasted_iota(jnp.int32, sin.shape, 1)
    alt = (1 - 2 * (lane % 2)).astype(F32)
    nyq_row = jnp.logical_and(row == 0, pl.program_id(0) == 0)
    r_ref[tf:, :] = jnp.where(nyq_row, alt, sin).astype(BF16)


def _dft_tables(seq, tf):
    idx = jnp.arange(seq, dtype=jnp.int32)
    nf = seq // tf

    def trig(freqs):
        ang = ((freqs[:, None] * idx[None, :]) % (2 * seq)).astype(F32) * (math.pi / seq)
        return jnp.cos(ang), jnp.sin(ang)

    ch, sh = trig(jnp.arange(nf, dtype=jnp.int32) * tf)
    cl, sl = trig(jnp.arange(tf, dtype=jnp.int32))
    base_spec = pl.BlockSpec((None, 1, seq), lambda f: (f, 0, 0))
    return pl.pallas_call(
        functools.partial(_dft_gen_kernel, tf=tf),
        out_shape=jax.ShapeDtypeStruct((2 * seq, seq), BF16),
        grid=(nf,),
        in_specs=[base_spec, base_spec, _const_spec((tf, seq)), _const_spec((tf, seq))],
        out_specs=pl.BlockSpec((2 * tf, seq), lambda f: (f, 0)),
        compiler_params=_cparams(("parallel",), 32),
        name="dft_tables",
    )(ch[:, None, :], sh[:, None, :], cl, sl)


KV_CHUNK = 256
ATT_Q_TILE = 256


def _attn_kernel(par_ref, q_ref, k_ref, v_ref, tab_ref, o_ref, krot_ref, vt_ref,
                 sa_buf, sb_buf, ma_buf, mb_buf,
                 *, tq, lam_init):
    t = pl.program_id(2)
    n_q = pl.num_programs(2) - 1
    seq = k_ref.shape[0]
    ck = min(KV_CHUNK, seq)
    n_chunks = seq // ck

    def rope(x, rows):
        c, sa, sb = (tab_ref[rows, i * HEAD_W:(i + 1) * HEAD_W] for i in range(3))
        return x * c + pltpu.roll(x, LANES - ROPE_DIM // 2, 1) * sa + pltpu.roll(x, ROPE_DIM // 2, 1) * sb

    buf_a = (sa_buf, ma_buf)
    buf_b = (sb_buf, mb_buf)

    def tile_work(score, finish):
        if score is not None:
            rows = pl.ds(pl.multiple_of(t * tq, tq), tq)
            q = rope(q_ref[rows, :].astype(F32), rows)
            q = q * (HEAD_DIM ** -0.5 * LOG2E)
            lane = lax.broadcasted_iota(jnp.int32, q.shape, 1)
            qq = jnp.concatenate([jnp.where(lane < HEAD_DIM, q, 0.0), jnp.where(lane >= HEAD_DIM, q, 0.0)], axis=0)
            qq = qq.astype(BF16)
            m8 = jnp.full((8, 2 * tq), -jnp.inf, F32)
        if finish is not None:
            m = jnp.max(finish[1][...], axis=0, keepdims=True)
            l8 = jnp.zeros((8, 2 * tq), F32)
            ot = jnp.zeros((HEAD_W, 2 * tq), F32)

        for c in range(n_chunks):
            kv = slice(c * ck, (c + 1) * ck)
            if finish is not None:
                p_c = jnp.exp2(finish[0][kv, :] - m)
                l8 = l8 + jnp.sum(p_c.reshape(ck // 8, 8, 2 * tq), axis=0)
                ot = ot + jnp.dot(vt_ref[:, kv], p_c.astype(BF16), preferred_element_type=F32)
            if score is not None:
                s_c = _dot_t(krot_ref[kv, :], qq)
                score[0][kv, :] = s_c
                m8 = jnp.maximum(m8, jnp.max(s_c.reshape(ck // 8, 8, 2 * tq), axis=0))
        if score is not None:
            score[1][...] = m8
        if finish is not None:
            l = jnp.sum(l8, axis=0, keepdims=True)
            lp = par_ref[0:4, :]
            lam = (jnp.exp(jnp.sum(lp[0:1] * lp[1:2], axis=-1, keepdims=True))
                   - jnp.exp(jnp.sum(lp[2:3] * lp[3:4], axis=-1, keepdims=True)) + lam_init)
            r = 1.0 / l
            o = (ot[:, :tq] * r[:, :tq] - ot[:, tq:] * (lam * r[:, tq:])).T
            o_ref[...] = (_rms(o, par_ref[4:5, :]) * (1.0 - lam_init)).astype(BF16)

    @pl.when(t == 0)
    def _():
        krot_ref[...] = rope(k_ref[...].astype(F32), slice(None)).astype(BF16)
        vt_ref[...] = v_ref[...].astype(F32).T.astype(BF16)
        tile_work(buf_a, None)

    steady = jnp.logical_and(t > 0, t < n_q)

    @pl.when(jnp.logical_and(steady, t % 2 == 1))
    def _():
        tile_work(buf_b, buf_a)

    @pl.when(jnp.logical_and(steady, t % 2 == 0))
    def _():
        tile_work(buf_a, buf_b)

    @pl.when(jnp.logical_and(t == n_q, t % 2 == 1))
    def _():
        tile_work(None, buf_a)

    @pl.when(jnp.logical_and(t == n_q, t % 2 == 0))
    def _():
        tile_work(None, buf_b)


def _attention(z3, lam_params, g_subln, lam_init):
    b, seq, _ = z3.shape
    tq = min(ATT_Q_TILE if seq > LONG_SEQ else 2 * ATT_Q_TILE, seq)
    n_q = seq // tq
    inv = ROPE_THETA ** (-jnp.arange(0, ROPE_DIM, 2, dtype=F32) / ROPE_DIM)
    ang = jnp.arange(seq, dtype=F32)[:, None] * inv[None, :]
    c8, s8 = jnp.cos(ang), jnp.sin(ang)
    half = ROPE_DIM // 2
    rest = HEAD_DIM - ROPE_DIM
    one, zero = jnp.ones((seq, rest), F32), jnp.zeros((seq, rest), F32)
    z8 = jnp.zeros((seq, half), F32)
    cos_t = jnp.tile(jnp.concatenate([c8, c8, one], axis=1), (1, 2))
    sa_t = jnp.tile(jnp.concatenate([-s8, z8, zero], axis=1), (1, 2))
    sb_t = jnp.tile(jnp.concatenate([z8, s8, zero], axis=1), (1, 2))
    tab = jnp.concatenate([cos_t, sa_t, sb_t], axis=1)
    par = jnp.concatenate([jnp.pad(lam_params, ((0, 0), (0, HEAD_W - HEAD_DIM))), g_subln,
                           jnp.zeros((3, HEAD_W), F32)], axis=0)
    qb, kb, vb = (HY_COLS // HEAD_W, (HY_COLS + ATT_WIDTH) // HEAD_W, (HY_COLS + 2 * ATT_WIDTH) // HEAD_W)
    head_cols = lambda blk: pl.BlockSpec((None, seq, HEAD_W), lambda i, h, t: (i, 0, blk + h))
    return pl.pallas_call(
        functools.partial(_attn_kernel, tq=tq, lam_init=lam_init),
        out_shape=jax.ShapeDtypeStruct((b, seq, ATT_WIDTH), BF16),
        grid=(b, N_HEADS, n_q + 1),
        in_specs=[_const_spec((8, HEAD_W)), head_cols(qb), head_cols(kb), head_cols(vb),
                  _const_spec((seq, 3 * HEAD_W))],
        out_specs=pl.BlockSpec((None, tq, HEAD_W), lambda i, h, t: (i, jnp.maximum(t - 1, 0), h)),
        scratch_shapes=[pltpu.VMEM((seq, HEAD_W), BF16), pltpu.VMEM((HEAD_W, seq), BF16),
                        pltpu.VMEM((seq, 2 * tq), F32), pltpu.VMEM((seq, 2 * tq), F32),
                        pltpu.VMEM((8, 2 * tq), F32), pltpu.VMEM((8, 2 * tq), F32)],
        compiler_params=_cparams(("parallel", "parallel", "arbitrary"), 56),
        name="attention",
    )(par, z3, z3, z3, tab)


def _merge_kernel(yc_ref, x0_ref, ya_ref, g0_ref, g1_ref, x_ref, why_ref, wat_ref, wo_ref, gp_ref, o_ref):
    yh = (x0_ref[...].astype(F32) * yc_ref[...].astype(F32)).astype(BF16)
    a = jnp.dot(yh, why_ref[...], preferred_element_type=F32)
    b = jnp.dot(ya_ref[...], wat_ref[...], preferred_element_type=F32)
    m = g0_ref[...].astype(F32) * a + g1_ref[...].astype(F32) * b
    r = jnp.dot(m.astype(BF16), wo_ref[...], preferred_element_type=F32)
    o_ref[...] = x_ref[...] + _rms(r, gp_ref[...])


def _merge(yconv, x0c, yatt, z, x2d, w_hy, w_att, w_out, g_post):
    m, d = x2d.shape
    tm = min(256, m)
    row = lambda width, blk=0: pl.BlockSpec((tm, width), lambda i: (i, blk))
    gate_blk = GATE_COL0 // d
    return pl.pallas_call(
        _merge_kernel,
        out_shape=jax.ShapeDtypeStruct((m, d), F32),
        grid=(m // tm,),
        in_specs=[row(HY_CH), row(HY_CH), row(ATT_WIDTH), row(d, gate_blk), row(d, gate_blk + 1), row(d),
                  _const_spec((HY_CH, d)), _const_spec((ATT_WIDTH, d)), _const_spec((d, d)), _const_spec((1, d))],
        out_specs=row(d),
        compiler_params=_cparams(("parallel",), 56),
        name="merge",
    )(yconv, x0c, yatt, z, z, x2d, w_hy, w_att, w_out, g_post)


def _ffn_kernel(x_ref, p_ref, gpre_ref, wg_ref, wu_ref, wo_ref, gpost_ref, gple_ref, wpg_ref, wpi_ref,
                o_ref, h_ref, acc_ref):
    j = pl.program_id(1)

    @pl.when(j == 0)
    def _():
        _rms_rows_to(h_ref, x_ref, gpre_ref)
        acc_ref[...] = jnp.zeros_like(acc_ref)

    h = h_ref[...]
    gate = jnp.dot(h, wg_ref[...], preferred_element_type=F32)
    up = jnp.dot(h, wu_ref[...], preferred_element_type=F32)
    act = (gate * _sigmoid(gate) * up).astype(BF16)
    acc_ref[...] += jnp.dot(act, wo_ref[...], preferred_element_type=F32)

    @pl.when(j == pl.num_programs(1) - 1)
    def _():
        def body(rows):
            x2 = x_ref[rows, :] + _rms(acc_ref[rows, :], gpost_ref[...])
            e = jnp.dot(_rms(x2, gple_ref[...]).astype(BF16), wpg_ref[...], preferred_element_type=F32)
            pe = jnp.dot(p_ref[rows, :].astype(BF16), wpi_ref[...], preferred_element_type=F32)
            o_ref[rows, :] = x2 + pe * _sigmoid(e)
        _for_row_chunks(x_ref.shape[0], body, chunk=2 * ROW_CHUNK)


def _ffn_ple(x1, p2d, g_pre, w_in, w_out, g_post, g_ple, w_pg, w_pi):
    m, d = x1.shape
    tm = min(512, m)
    tf = 512
    nff = D_FF // tf
    return pl.pallas_call(
        _ffn_kernel,
        out_shape=jax.ShapeDtypeStruct((m, d), F32),
        grid=(m // tm, nff),
        in_specs=[
            pl.BlockSpec((tm, d), lambda i, j: (i, 0)),
            pl.BlockSpec((tm, PLE_DIM), lambda i, j: (i, 0)),
            _const_spec((1, d)),
            pl.BlockSpec((d, tf), lambda i, j: (0, j)),
            pl.BlockSpec((d, tf), lambda i, j: (0, nff + j)),
            pl.BlockSpec((tf, d), lambda i, j: (j, 0)),
            _const_spec((1, d)), _const_spec((1, d)),
            _const_spec((d, d)), _const_spec((PLE_DIM, d)),
        ],
        out_specs=pl.BlockSpec((tm, d), lambda i, j: (i, 0)),
        scratch_shapes=[pltpu.VMEM((tm, d), BF16), pltpu.VMEM((tm, d), F32)],
        compiler_params=_cparams(("parallel", "arbitrary"), 56),
        name="ffn_ple",
    )(x1, p2d, g_pre, w_in, w_in, w_out, g_post, g_ple, w_pg, w_pi)


def _layer(x, p, lam_init, wts):
    b, seq, d = x.shape
    x2d = x.reshape(b * seq, d)
    z = _in_proj(x2d, wts["g_mix_pre"], wts["w_in"], wts["b_gate"])
    z3 = z.reshape(b, seq, IN_COLS)

    tf = min(256, seq)
    rmat = _dft_tables(seq, tf)
    ht, sums = _filt_mlp(seq, wts["filt_w1"], wts["filt_b1"], wts["filt_w2"], wts["filt_b2"],
                         wts["filt_freq"], wts["filt_w3"])
    u1, v2, kl = _filt_spec(ht, sums, rmat, tf)
    x0c, wt = _hy_prep(z3, wts["conv_w"], wts["conv_b"])
    yconv = _long_conv(wt, rmat, u1, v2, kl, wts["hyena_d"], tf)

    yatt = _attention(z3, wts["lam"], wts["g_subln"], lam_init)

    x1 = _merge(yconv.reshape(b * seq, HY_CH), x0c.reshape(b * seq, HY_CH), yatt.reshape(b * seq, ATT_WIDTH),
                z, x2d, wts["w_hy_out"], wts["w_att_out"], wts["w_out"], wts["g_mix_post"])
    y = _ffn_ple(x1, p.reshape(b * seq, PLE_DIM), wts["g_ffn_pre"], wts["w_ffn_in"], wts["w_ffn_out"],
                 wts["g_ffn_post"], wts["g_ple"], wts["w_ple_gate"], wts["w_ple_in"])
    return y.reshape(b, seq, d)


def kernel(x_prompt, x_sample, p_prompt, p_sample, g_mix_pre, g_mix_post, g_ffn_pre, g_ffn_post, g_ple, w_in, b_gate, conv_w, conv_b, filt_w1, filt_b1, filt_w2, filt_b2, filt_freq, filt_w3, hyena_d, lam_q1, lam_k1, lam_q2, lam_k2, g_subln, w_hy_out, w_att_out, w_out, w_ffn_in, w_ffn_out, w_ple_in, w_ple_gate):
    depth = w_in.shape[0]
    xs = [x_prompt, x_sample]
    ps = [p_prompt, p_sample]
    for i in range(depth):
        lam_init = 0.8 - 0.6 * math.exp(-0.3 * i)
        wts = dict(
            g_mix_pre=g_mix_pre[i][None, :], g_mix_post=g_mix_post[i][None, :],
            g_ffn_pre=g_ffn_pre[i][None, :], g_ffn_post=g_ffn_post[i][None, :], g_ple=g_ple[i][None, :],
            w_in=w_in[i].astype(BF16), b_gate=b_gate[i].reshape(1, 2 * D_MODEL),
            conv_w=conv_w[i], conv_b=conv_b[i][None, :],
            filt_w1=filt_w1[i], filt_b1=filt_b1[i], filt_w2=filt_w2[i], filt_b2=filt_b2[i],
            filt_freq=filt_freq[i], filt_w3=filt_w3[i],
            hyena_d=jnp.broadcast_to(hyena_d[i][:, None], (HY_CH, LANES)),
            lam=jnp.stack([lam_q1[i], lam_k1[i], lam_q2[i], lam_k2[i]]),
            g_subln=g_subln[i][None, :],
            w_hy_out=w_hy_out[i].astype(BF16), w_att_out=w_att_out[i].astype(BF16),
            w_out=w_out[i].astype(BF16), w_ffn_in=w_ffn_in[i].astype(BF16),
            w_ffn_out=w_ffn_out[i].astype(BF16), w_ple_in=w_ple_in[i].astype(BF16),
            w_ple_gate=w_ple_gate[i].astype(BF16),
        )
        xs = [_layer(x, p[i], lam_init, wts) for x, p in zip(xs, ps)]
    return (xs[0], xs[1])
```

```python
import functools
import math

import jax
import jax.numpy as jnp
from jax import lax
from jax.experimental import pallas as pl
from jax.experimental.pallas import tpu as pltpu

F32 = jnp.float32
BF16 = jnp.bfloat16

D_MODEL = 2048
PLE_DIM = 256
NORM_EPS = 1e-6
HY_CH = D_MODEL // 2
FILT_BANDS = 16
FILT_EMB = 1 + 2 * FILT_BANDS
FILT_ORDER = 64
FILT_MAX_DECAY = math.log(1e-2) / 0.3
FILT_MIN_DECAY = math.log(1e-2) / 1.5
ATT_WIDTH = D_MODEL // 2
N_HEADS = 8
HEAD_DIM = ATT_WIDTH // (2 * N_HEADS)
HEAD_W = 2 * HEAD_DIM
ROPE_DIM = HEAD_DIM // 4
ROPE_THETA = 500000.0
D_FF = ((8 * D_MODEL + 3 * 256 - 1) // (3 * 256)) * 256
HY_COLS = 3 * HY_CH
GATE_COL0 = HY_COLS + 3 * ATT_WIDTH
IN_COLS = GATE_COL0 + 2 * D_MODEL
LANES = 128
LONG_SEQ = 2048
FREQ_TILE = 256
LOG2E = math.log2(math.e)
MIB = 2 ** 20


def _cparams(sem, vmem_mib):
    return pltpu.CompilerParams(dimension_semantics=sem, vmem_limit_bytes=vmem_mib * MIB)


def _const_spec(shape):
    nd = len(shape)
    return pl.BlockSpec(shape, lambda *_: (0,) * nd, pipeline_mode=pl.Buffered(1))


def _rms(x, g):
    ms = jnp.mean(x * x, axis=-1, keepdims=True)
    return x * lax.rsqrt(ms + NORM_EPS) * g


def _sigmoid(x):
    return 1.0 / (1.0 + jnp.exp(-x))


ROW_CHUNK = 128


def _for_row_chunks(n_rows, body, chunk=ROW_CHUNK):
    chunk = min(chunk, n_rows)

    def step(c, carry):
        body(pl.ds(pl.multiple_of(c * chunk, chunk), chunk))
        return carry
    lax.fori_loop(0, n_rows // chunk, step, 0)


def _rms_rows_to(h_ref, x_ref, g_ref):
    def body(rows):
        h_ref[rows, :] = _rms(x_ref[rows, :], g_ref[...]).astype(h_ref.dtype)
    _for_row_chunks(x_ref.shape[0], body)


def _inproj_kernel(x_ref, g_ref, w_ref, b_ref, o_ref, h_ref, *, n_plain):
    j = pl.program_id(1)

    @pl.when(j == 0)
    def _():
        _rms_rows_to(h_ref, x_ref, g_ref)

    acc = jnp.dot(h_ref[...], w_ref[...], preferred_element_type=F32)
    o_ref[...] = jnp.where(j >= n_plain, _sigmoid(acc + b_ref[...]), acc).astype(BF16)


def _in_proj(x2d, g, w_bf, b_flat):
    m, d = x2d.shape
    n = w_bf.shape[1]
    tm = min(1024, m)
    tn = 1024
    n_plain = GATE_COL0 // tn
    return pl.pallas_call(
        functools.partial(_inproj_kernel, n_plain=n_plain),
        out_shape=jax.ShapeDtypeStruct((m, n), BF16),
        grid=(m // tm, n // tn),
        in_specs=[
            pl.BlockSpec((tm, d), lambda i, j: (i, 0)),
            pl.BlockSpec((1, d), lambda i, j: (0, 0)),
            pl.BlockSpec((d, tn), lambda i, j: (0, j)),
            pl.BlockSpec((1, tn), lambda i, j: (0, jnp.maximum(j - n_plain, 0))),
        ],
        out_specs=pl.BlockSpec((tm, tn), lambda i, j: (i, j)),
        scratch_shapes=[pltpu.VMEM((tm, d), BF16)],
        compiler_params=_cparams(("parallel", "arbitrary"), 48),
        name="in_proj",
    )(x2d, g, w_bf, b_flat)


def _filt_mlp_kernel(emb_ref, w1_ref, b1_ref, w2_ref, b2_ref, fr_ref, w3_ref, dl_ref, ht_ref, sum_ref):
    i = pl.program_id(0)
    hp = lax.Precision.HIGHEST
    emb = emb_ref[...]
    fr = fr_ref[...]
    h1 = jnp.sin(fr * (jnp.dot(emb, w1_ref[...], precision=hp, preferred_element_type=F32) + b1_ref[...]))
    h2 = jnp.sin(fr * (jnp.dot(h1, w2_ref[...], precision=hp, preferred_element_type=F32) + b2_ref[...]))
    h = jnp.dot(h2, w3_ref[...], precision=hp, preferred_element_type=F32)
    dec = jnp.exp(-emb[:, 0:1] * dl_ref[...])
    ht = jnp.concatenate([h[:, :HY_CH] * dec, h[:, HY_CH:] * dec], axis=1).T
    ht_ref[...] = ht
    ab = jnp.abs(ht)
    part = ab[:, 0:LANES]
    for c in range(1, ab.shape[1] // LANES):
        part = part + ab[:, c * LANES:(c + 1) * LANES]

    @pl.when(i == 0)
    def _():
        sum_ref[...] = part

    @pl.when(i > 0)
    def _():
        sum_ref[...] += part


def _filt_mlp(seq, w1, b1, w2, b2, freq, w3):
    t = jnp.linspace(0.0, 1.0, seq, dtype=F32)[:, None]
    wpos = 2.0 * math.pi * jnp.arange(seq, dtype=F32) / seq
    bands = jnp.linspace(1e-4, FILT_BANDS - 1, FILT_BANDS, dtype=F32)
    ang = wpos[:, None] * bands[None, :]
    emb = jnp.concatenate([t, jnp.cos(ang), -jnp.sin(ang)], axis=-1)
    emb = jnp.pad(emb, ((0, 0), (0, LANES - FILT_EMB)))
    pad_o = LANES - FILT_ORDER
    w1p = jnp.pad(w1, ((0, LANES - FILT_EMB), (0, pad_o)))
    w2p = jnp.pad(w2, ((0, pad_o), (0, pad_o)))
    w3p = jnp.pad(w3, ((0, pad_o), (0, 0)))
    b1p = jnp.pad(b1[None, :], ((0, 0), (0, pad_o)))
    b2p = jnp.pad(b2[None, :], ((0, 0), (0, pad_o)))
    frp = jnp.pad(freq[None, :], ((0, 0), (0, pad_o)))
    deltas = jnp.abs(jnp.linspace(FILT_MIN_DECAY, FILT_MAX_DECAY, HY_CH, dtype=F32))[None, :]
    tl = min(512, seq)
    return pl.pallas_call(
        _filt_mlp_kernel,
        out_shape=(jax.ShapeDtypeStruct((2 * HY_CH, seq), F32),
                   jax.ShapeDtypeStruct((2 * HY_CH, LANES), F32)),
        grid=(seq // tl,),
        in_specs=[
            pl.BlockSpec((tl, LANES), lambda i: (i, 0)),
            _const_spec((LANES, LANES)), _const_spec((1, LANES)),
            _const_spec((LANES, LANES)), _const_spec((1, LANES)),
            _const_spec((1, LANES)), _const_spec((LANES, 2 * HY_CH)),
            _const_spec((1, HY_CH)),
        ],
        out_specs=(pl.BlockSpec((2 * HY_CH, tl), lambda i: (0, i)),
                   pl.BlockSpec((2 * HY_CH, LANES), lambda i: (0, 0))),
        compiler_params=_cparams(("arbitrary",), 48),
        name="filt_mlp",
    )(emb, w1p, b1p, w2p, b2p, frp, w3p, deltas)


def _dot_t(a, b):
    return lax.dot_general(a, b, (((1,), (1,)), ((), ())), preferred_element_type=F32)


def _filt_spec_kernel(hf_ref, hb_ref, sf_ref, sb_ref, r_ref, u1_ref, v2_ref, kl_ref, s1_ref, s2_ref, *, tf, seq):
    f = pl.program_id(1)

    @pl.when(f == 0)
    def _():
        norm = jnp.sum(sf_ref[...] + sb_ref[...], axis=1, keepdims=True) + NORM_EPS
        hf = hf_ref[...] / norm
        lane = lax.broadcasted_iota(jnp.int32, hf.shape, 1)
        hb = jnp.where(lane == 0, 0.0, hb_ref[...] / norm)
        s1_ref[...] = (hf + hb).astype(BF16)
        s2_ref[...] = (hf - hb).astype(BF16)

    s1 = s1_ref[...]
    s2 = s2_ref[...]
    c = _dot_t(s1, r_ref[:tf, :]) * (1.0 / seq)
    col = lax.broadcasted_iota(jnp.int32, c.shape, 1) + f * tf
    u1_ref[...] = jnp.where(col == 0, 0.5 * c, c)
    v2_ref[...] = jnp.where(col == 0, 0.0, _dot_t(s2, r_ref[tf:, :]) * (1.0 / seq))

    @pl.when(f == 0)
    def _():
        nyq = _dot_t(s1, r_ref[tf:, :])[:, 0:1] * (0.5 / seq)
        kl_ref[...] = jnp.broadcast_to(nyq, kl_ref.shape)


def _filt_spec(ht, sums, rmat, tf):
    seq = ht.shape[1]
    tc = 256
    nc = HY_CH // tc
    return pl.pallas_call(
        functools.partial(_filt_spec_kernel, tf=tf, seq=seq),
        out_shape=(jax.ShapeDtypeStruct((HY_CH, seq), F32),
                   jax.ShapeDtypeStruct((HY_CH, seq), F32),
                   jax.ShapeDtypeStruct((HY_CH, LANES), F32)),
        grid=(nc, seq // tf),
        in_specs=[
            pl.BlockSpec((tc, seq), lambda c, f: (c, 0)),
            pl.BlockSpec((tc, seq), lambda c, f: (nc + c, 0)),
            pl.BlockSpec((tc, LANES), lambda c, f: (c, 0)),
            pl.BlockSpec((tc, LANES), lambda c, f: (nc + c, 0)),
            pl.BlockSpec((2 * tf, seq), lambda c, f: (f, 0)),
        ],
        out_specs=(pl.BlockSpec((tc, tf), lambda c, f: (c, f)),
                   pl.BlockSpec((tc, tf), lambda c, f: (c, f)),
                   pl.BlockSpec((tc, LANES), lambda c, f: (c, 0))),
        scratch_shapes=[pltpu.VMEM((tc, seq), BF16), pltpu.VMEM((tc, seq), BF16)],
        compiler_params=_cparams(("parallel", "arbitrary"), 48),
        name="filt_spec",
    )(ht, ht, sums, sums, rmat)


def _hyprep_kernel(u0_ref, u1_ref, u2_ref, w0_ref, w1_ref, w2_ref, b0_ref, b1_ref, b2_ref, x0_ref, wt_ref):
    seq = u0_ref.shape[0]
    row = lax.broadcasted_iota(jnp.int32, u0_ref.shape, 0)

    def sconv(u_ref, w_ref, b_ref):
        u = u_ref[...].astype(F32)
        um = jnp.where(row == 0, 0.0, pltpu.roll(u, 1, 0))
        up = jnp.where(row == seq - 1, 0.0, pltpu.roll(u, seq - 1, 0))
        w = w_ref[...]
        return b_ref[...] + um * w[0:1] + u * w[1:2] + up * w[2:3]

    x0_ref[...] = sconv(u0_ref, w0_ref, b0_ref).astype(BF16)
    prod = sconv(u1_ref, w1_ref, b1_ref) * sconv(u2_ref, w2_ref, b2_ref)
    wt_ref[...] = prod.T.astype(BF16)


def _hy_prep(z3, conv_w, conv_b):
    b, seq, _ = z3.shape
    tc = LANES if seq > LONG_SEQ else 2 * LANES
    nc = HY_CH // tc
    uspec = lambda g: pl.BlockSpec((None, seq, tc), lambda i, c: (i, 0, g * nc + c))
    wspec = lambda g: pl.BlockSpec((3, tc), lambda i, c: (0, g * nc + c))
    bspec = lambda g: pl.BlockSpec((1, tc), lambda i, c: (0, g * nc + c))
    return pl.pallas_call(
        _hyprep_kernel,
        out_shape=(jax.ShapeDtypeStruct((b, seq, HY_CH), BF16),
                   jax.ShapeDtypeStruct((b, HY_CH, seq), BF16)),
        grid=(b, nc),
        in_specs=[uspec(0), uspec(1), uspec(2), wspec(0), wspec(1), wspec(2), bspec(0), bspec(1), bspec(2)],
        out_specs=(pl.BlockSpec((None, seq, tc), lambda i, c: (i, 0, c)),
                   pl.BlockSpec((None, tc, seq), lambda i, c: (i, c, 0))),
        compiler_params=_cparams(("parallel", "parallel"), 48),
        name="hy_prep",
    )(z3, z3, z3, conv_w, conv_w, conv_w, conv_b, conv_b, conv_b)


def _conv_kernel(wt_ref, ra_ref, rb_ref, u1_ref, v2_ref, kl_ref, d_ref, o_ref, acc_ref, za_ref, zb_ref, *, tf):
    t = pl.program_id(2)
    n_f = pl.num_programs(2) - 1

    def work(z_out, z_in, first=False):
        if z_in is not None:
            acc_ref[...] += jnp.dot(z_in[...], rb_ref[...], preferred_element_type=F32)
        if z_out is not None:
            ap = _dot_t(wt_ref[...], ra_ref[...])
            a = ap[:, :tf]
            p = ap[:, tf:]
            u1 = u1_ref[...]
            v2 = v2_ref[...]
            z2 = p * u1 + a * v2
            if first:
                col = lax.broadcasted_iota(jnp.int32, z2.shape, 1)
                z2 = jnp.where(col == 0, p * kl_ref[:, 0:1], z2)
            z_out[...] = jnp.concatenate([a * u1 - p * v2, z2], axis=1).astype(BF16)

    @pl.when(t == 0)
    def _():
        acc_ref[...] = jnp.zeros_like(acc_ref)
        work(za_ref, None, first=True)

    steady = jnp.logical_and(t > 0, t < n_f)

    @pl.when(jnp.logical_and(steady, t % 2 == 1))
    def _():
        work(zb_ref, za_ref)

    @pl.when(jnp.logical_and(steady, t % 2 == 0))
    def _():
        work(za_ref, zb_ref)

    def finish(z_in):
        work(None, z_in)
        y = acc_ref[...] + d_ref[:, 0:1] * wt_ref[...].astype(F32)
        o_ref[...] = y.T.astype(BF16)

    @pl.when(jnp.logical_and(t == n_f, t % 2 == 1))
    def _():
        finish(za_ref)

    @pl.when(jnp.logical_and(t == n_f, t % 2 == 0))
    def _():
        finish(zb_ref)


def _long_conv(wt, rmat, u1, v2, kl, dcol, tf):
    b, _, seq = wt.shape
    tc = 512
    n_f = seq // tf
    return pl.pallas_call(
        functools.partial(_conv_kernel, tf=tf),
        out_shape=jax.ShapeDtypeStruct((b, seq, HY_CH), BF16),
        grid=(b, HY_CH // tc, n_f + 1),
        in_specs=[
            pl.BlockSpec((None, tc, seq), lambda i, c, t: (i, c, 0)),
            pl.BlockSpec((2 * tf, seq), lambda i, c, t: (jnp.minimum(t, n_f - 1), 0)),
            pl.BlockSpec((2 * tf, seq), lambda i, c, t: (jnp.maximum(t - 1, 0), 0)),
            pl.BlockSpec((tc, tf), lambda i, c, t: (c, jnp.minimum(t, n_f - 1))),
            pl.BlockSpec((tc, tf), lambda i, c, t: (c, jnp.minimum(t, n_f - 1))),
            pl.BlockSpec((tc, LANES), lambda i, c, t: (c, 0)),
            pl.BlockSpec((tc, LANES), lambda i, c, t: (c, 0)),
        ],
        out_specs=pl.BlockSpec((None, seq, tc), lambda i, c, t: (i, 0, c)),
        scratch_shapes=[pltpu.VMEM((tc, seq), F32), pltpu.VMEM((tc, 2 * tf), BF16), pltpu.VMEM((tc, 2 * tf), BF16)],
        compiler_params=_cparams(("parallel", "parallel", "arbitrary"), 56),
        name="long_conv",
    )(wt, rmat, rmat, u1, v2, kl, dcol)


def _dft_gen_kernel(ch_ref, sh_ref, cl_ref, sl_ref, r_ref, *, tf):
    ch, sh = ch_ref[...], sh_ref[...]
    cl, sl = cl_ref[...], sl_ref[...]
    r_ref[:tf, :] = (ch * cl - sh * sl).astype(BF16)
    sin = sh * cl + ch * sl
    row = lax.broadcasted_iota(jnp.int32, sin.shape, 0)
    lane = lax.broadcasted_iota(jnp.int32, sin.shape, 1)
    alt = (1 - 2 * (lane % 2)).astype(F32)
    nyq_row = jnp.logical_and(row == 0, pl.program_id(0) == 0)
    r_ref[tf:, :] = jnp.where(nyq_row, alt, sin).astype(BF16)


def _dft_tables(seq, tf):
    idx = jnp.arange(seq, dtype=jnp.int32)
    nf = seq // tf

    def trig(freqs):
        ang = ((freqs[:, None] * idx[None, :]) % (2 * seq)).astype(F32) * (math.pi / seq)
        return jnp.cos(ang), jnp.sin(ang)

    ch, sh = trig(jnp.arange(nf, dtype=jnp.int32) * tf)
    cl, sl = trig(jnp.arange(tf, dtype=jnp.int32))
    base_spec = pl.BlockSpec((None, 1, seq), lambda f: (f, 0, 0))
    return pl.pallas_call(
        functools.partial(_dft_gen_kernel, tf=tf),
        out_shape=jax.ShapeDtypeStruct((2 * seq, seq), BF16),
        grid=(nf,),
        in_specs=[base_spec, base_spec, _const_spec((tf, seq)), _const_spec((tf, seq))],
        out_specs=pl.BlockSpec((2 * tf, seq), lambda f: (f, 0)),
        compiler_params=_cparams(("parallel",), 32),
        name="dft_tables",
    )(ch[:, None, :], sh[:, None, :], cl, sl)


KV_CHUNK = 256
ATT_Q_TILE = 512


def _attn_kernel(par_ref, q_ref, k_ref, v_ref, tab_ref, o_ref, krot_ref, vt_ref,
                 sa_buf, sb_buf, ma_buf, mb_buf,
                 *, tq, lam_init):
    t = pl.program_id(2)
    n_q = pl.num_programs(2) - 1
    seq = k_ref.shape[0]
    ck = min(KV_CHUNK, seq)
    n_chunks = seq // ck

    def rope(x, rows):
        c, sa, sb = (tab_ref[rows, i * HEAD_W:(i + 1) * HEAD_W] for i in range(3))
        return x * c + pltpu.roll(x, LANES - ROPE_DIM // 2, 1) * sa + pltpu.roll(x, ROPE_DIM // 2, 1) * sb

    buf_a = (sa_buf, ma_buf)
    buf_b = (sb_buf, mb_buf)

    def tile_work(score, finish):
        if score is not None:
            rows = pl.ds(pl.multiple_of(t * tq, tq), tq)
            q = rope(q_ref[rows, :].astype(F32), rows)
            q = q * (HEAD_DIM ** -0.5 * LOG2E)
            lane = lax.broadcasted_iota(jnp.int32, q.shape, 1)
            qq = jnp.concatenate([jnp.where(lane < HEAD_DIM, q, 0.0), jnp.where(lane >= HEAD_DIM, q, 0.0)], axis=0)
            qq = qq.astype(BF16)
            m8 = jnp.full((8, 2 * tq), -jnp.inf, F32)
        if finish is not None:
            m = jnp.max(finish[1][...], axis=0, keepdims=True)
            l8 = jnp.zeros((8, 2 * tq), F32)
            ot = jnp.zeros((HEAD_W, 2 * tq), F32)

        for c in range(n_chunks):
            kv = slice(c * ck, (c + 1) * ck)
            if finish is not None:
                p_c = jnp.exp2(finish[0][kv, :] - m)
                l8 = l8 + jnp.sum(p_c.reshape(ck // 8, 8, 2 * tq), axis=0)
                ot = ot + jnp.dot(vt_ref[:, kv], p_c.astype(BF16), preferred_element_type=F32)
            if score is not None:
                s_c = _dot_t(krot_ref[kv, :], qq)
                score[0][kv, :] = s_c
                m8 = jnp.maximum(m8, jnp.max(s_c.reshape(ck // 8, 8, 2 * tq), axis=0))
        if score is not None:
            score[1][...] = m8
        if finish is not None:
            l = jnp.sum(l8, axis=0, keepdims=True)
            lp = par_ref[0:4, :]
            lam = (jnp.exp(jnp.sum(lp[0:1] * lp[1:2], axis=-1, keepdims=True))
                   - jnp.exp(jnp.sum(lp[2:3] * lp[3:4], axis=-1, keepdims=True)) + lam_init)
            r = 1.0 / l
            o = (ot[:, :tq] * r[:, :tq] - ot[:, tq:] * (lam * r[:, tq:])).T
            o_ref[...] = (_rms(o, par_ref[4:5, :]) * (1.0 - lam_init)).astype(BF16)

    @pl.when(t == 0)
    def _():
        krot_ref[...] = rope(k_ref[...].astype(F32), slice(None)).astype(BF16)
        vt_ref[...] = v_ref[...].astype(F32).T.astype(BF16)
        tile_work(buf_a, None)

    steady = jnp.logical_and(t > 0, t < n_q)

    @pl.when(jnp.logical_and(steady, t % 2 == 1))
    def _():
        tile_work(buf_b, buf_a)

    @pl.when(jnp.logical_and(steady, t % 2 == 0))
    def _():
        tile_work(buf_a, buf_b)

    @pl.when(jnp.logical_and(t == n_q, t % 2 == 1))
    def _():
        tile_work(None, buf_a)

    @pl.when(jnp.logical_and(t == n_q, t % 2 == 0))
    def _():
        tile_work(None, buf_b)


def _attention(z3, lam_params, g_subln, lam_init):
    b, seq, _ = z3.shape
    tq = min(ATT_Q_TILE, seq)
    n_q = seq // tq
    inv = ROPE_THETA ** (-jnp.arange(0, ROPE_DIM, 2, dtype=F32) / ROPE_DIM)
    ang = jnp.arange(seq, dtype=F32)[:, None] * inv[None, :]
    c8, s8 = jnp.cos(ang), jnp.sin(ang)
    half = ROPE_DIM // 2
    rest = HEAD_DIM - ROPE_DIM
    one, zero = jnp.ones((seq, rest), F32), jnp.zeros((seq, rest), F32)
    z8 = jnp.zeros((seq, half), F32)
    cos_t = jnp.tile(jnp.concatenate([c8, c8, one], axis=1), (1, 2))
    sa_t = jnp.tile(jnp.concatenate([-s8, z8, zero], axis=1), (1, 2))
    sb_t = jnp.tile(jnp.concatenate([z8, s8, zero], axis=1), (1, 2))
    tab = jnp.concatenate([cos_t, sa_t, sb_t], axis=1)
    par = jnp.concatenate([jnp.pad(lam_params, ((0, 0), (0, HEAD_W - HEAD_DIM))), g_subln,
                           jnp.zeros((3, HEAD_W), F32)], axis=0)
    qb, kb, vb = (HY_COLS // HEAD_W, (HY_COLS + ATT_WIDTH) // HEAD_W, (HY_COLS + 2 * ATT_WIDTH) // HEAD_W)
    head_cols = lambda blk: pl.BlockSpec((None, seq, HEAD_W), lambda i, h, t: (i, 0, blk + h))
    return pl.pallas_call(
        functools.partial(_attn_kernel, tq=tq, lam_init=lam_init),
        out_shape=jax.ShapeDtypeStruct((b, seq, ATT_WIDTH), BF16),
        grid=(b, N_HEADS, n_q + 1),
        in_specs=[_const_spec((8, HEAD_W)), head_cols(qb), head_cols(kb), head_cols(vb),
                  _const_spec((seq, 3 * HEAD_W))],
        out_specs=pl.BlockSpec((None, tq, HEAD_W), lambda i, h, t: (i, jnp.maximum(t - 1, 0), h)),
        scratch_shapes=[pltpu.VMEM((seq, HEAD_W), BF16), pltpu.VMEM((HEAD_W, seq), BF16),
                        pltpu.VMEM((seq, 2 * tq), F32), pltpu.VMEM((seq, 2 * tq), F32),
                        pltpu.VMEM((8, 2 * tq), F32), pltpu.VMEM((8, 2 * tq), F32)],
        compiler_params=_cparams(("parallel", "parallel", "arbitrary"), 56),
        name="attention",
    )(par, z3, z3, z3, tab)


def _merge_kernel(yc_ref, x0_ref, ya_ref, g0_ref, g1_ref, x_ref, why_ref, wat_ref, wo_ref, gp_ref, o_ref):
    yh = (x0_ref[...].astype(F32) * yc_ref[...].astype(F32)).astype(BF16)
    a = jnp.dot(yh, why_ref[...], preferred_element_type=F32)
    b = jnp.dot(ya_ref[...], wat_ref[...], preferred_element_type=F32)
    m = g0_ref[...].astype(F32) * a + g1_ref[...].astype(F32) * b
    r = jnp.dot(m.astype(BF16), wo_ref[...], preferred_element_type=F32)
    o_ref[...] = x_ref[...] + _rms(r, gp_ref[...])


def _merge(yconv, x0c, yatt, z, x2d, w_hy, w_att, w_out, g_post):
    m, d = x2d.shape
    tm = min(256, m)
    row = lambda width, blk=0: pl.BlockSpec((tm, width), lambda i: (i, blk))
    gate_blk = GATE_COL0 // d
    return pl.pallas_call(
        _merge_kernel,
        out_shape=jax.ShapeDtypeStruct((m, d), F32),
        grid=(m // tm,),
        in_specs=[row(HY_CH), row(HY_CH), row(ATT_WIDTH), row(d, gate_blk), row(d, gate_blk + 1), row(d),
                  _const_spec((HY_CH, d)), _const_spec((ATT_WIDTH, d)), _const_spec((d, d)), _const_spec((1, d))],
        out_specs=row(d),
        compiler_params=_cparams(("parallel",), 56),
        name="merge",
    )(yconv, x0c, yatt, z, z, x2d, w_hy, w_att, w_out, g_post)


def _ffn_kernel(x_ref, p_ref, gpre_ref, wg_ref, wu_ref, wo_ref, gpost_ref, gple_ref, wpg_ref, wpi_ref,
                o_ref, h_ref, acc_ref):
    j = pl.program_id(1)

    @pl.when(j == 0)
    def _():
        _rms_rows_to(h_ref, x_ref, gpre_ref)
        acc_ref[...] = jnp.zeros_like(acc_ref)

    h = h_ref[...]
    gate = jnp.dot(h, wg_ref[...], preferred_element_type=F32)
    up = jnp.dot(h, wu_ref[...], preferred_element_type=F32)
    act = (gate * _sigmoid(gate) * up).astype(BF16)
    acc_ref[...] += jnp.dot(act, wo_ref[...], preferred_element_type=F32)

    @pl.when(j == pl.num_programs(1) - 1)
    def _():
        def body(rows):
            x2 = x_ref[rows, :] + _rms(acc_ref[rows, :], gpost_ref[...])
            e = jnp.dot(_rms(x2, gple_ref[...]).astype(BF16), wpg_ref[...], preferred_element_type=F32)
            pe = jnp.dot(p_ref[rows, :].astype(BF16), wpi_ref[...], preferred_element_type=F32)
            o_ref[rows, :] = x2 + pe * _sigmoid(e)
        _for_row_chunks(x_ref.shape[0], body, chunk=2 * ROW_CHUNK)


def _ffn_ple(x1, p2d, g_pre, w_in, w_out, g_post, g_ple, w_pg, w_pi):
    m, d = x1.shape
    tm = min(512, m)
    tf = 512
    nff = D_FF // tf
    return pl.pallas_call(
        _ffn_kernel,
        out_shape=jax.ShapeDtypeStruct((m, d), F32),
        grid=(m // tm, nff),
        in_specs=[
            pl.BlockSpec((tm, d), lambda i, j: (i, 0)),
            pl.BlockSpec((tm, PLE_DIM), lambda i, j: (i, 0)),
            _const_spec((1, d)),
            pl.BlockSpec((d, tf), lambda i, j: (0, j)),
            pl.BlockSpec((d, tf), lambda i, j: (0, nff + j)),
            pl.BlockSpec((tf, d), lambda i, j: (j, 0)),
            _const_spec((1, d)), _const_spec((1, d)),
            _const_spec((d, d)), _const_spec((PLE_DIM, d)),
        ],
        out_specs=pl.BlockSpec((tm, d), lambda i, j: (i, 0)),
        scratch_shapes=[pltpu.VMEM((tm, d), BF16), pltpu.VMEM((tm, d), F32)],
        compiler_params=_cparams(("parallel", "arbitrary"), 56),
        name="ffn_ple",
    )(x1, p2d, g_pre, w_in, w_in, w_out, g_post, g_ple, w_pg, w_pi)


def _layer(x, p, lam_init, wts):
    b, seq, d = x.shape
    x2d = x.reshape(b * seq, d)
    z = _in_proj(x2d, wts["g_mix_pre"], wts["w_in"], wts["b_gate"])
    z3 = z.reshape(b, seq, IN_COLS)

    tf = min(FREQ_TILE if seq > LONG_SEQ else 2 * FREQ_TILE, seq)
    rmat = _dft_tables(seq, tf)
    ht, sums = _filt_mlp(seq, wts["filt_w1"], wts["filt_b1"], wts["filt_w2"], wts["filt_b2"],
                         wts["filt_freq"], wts["filt_w3"])
    u1, v2, kl = _filt_spec(ht, sums, rmat, tf)
    x0c, wt = _hy_prep(z3, wts["conv_w"], wts["conv_b"])
    yconv = _long_conv(wt, rmat, u1, v2, kl, wts["hyena_d"], tf)

    yatt = _attention(z3, wts["lam"], wts["g_subln"], lam_init)

    x1 = _merge(yconv.reshape(b * seq, HY_CH), x0c.reshape(b * seq, HY_CH), yatt.reshape(b * seq, ATT_WIDTH),
                z, x2d, wts["w_hy_out"], wts["w_att_out"], wts["w_out"], wts["g_mix_post"])
    y = _ffn_ple(x1, p.reshape(b * seq, PLE_DIM), wts["g_ffn_pre"], wts["w_ffn_in"], wts["w_ffn_out"],
                 wts["g_ffn_post"], wts["g_ple"], wts["w_ple_gate"], wts["w_ple_in"])
    return y.reshape(b, seq, d)


def kernel(x_prompt, x_sample, p_prompt, p_sample, g_mix_pre, g_mix_post, g_ffn_pre, g_ffn_post, g_ple, w_in, b_gate, conv_w, conv_b, filt_w1, filt_b1, filt_w2, filt_b2, filt_freq, filt_w3, hyena_d, lam_q1, lam_k1, lam_q2, lam_k2, g_subln, w_hy_out, w_att_out, w_out, w_ffn_in, w_ffn_out, w_ple_in, w_ple_gate):
    depth = w_in.shape[0]
    xs = [x_prompt, x_sample]
    ps = [p_prompt, p_sample]
    for i in range(depth):
        lam_init = 0.8 - 0.6 * math.exp(-0.3 * i)
        wts = dict(
            g_mix_pre=g_mix_pre[i][None, :], g_mix_post=g_mix_post[i][None, :],
            g_ffn_pre=g_ffn_pre[i][None, :], g_ffn_post=g_ffn_post[i][None, :], g_ple=g_ple[i][None, :],
            w_in=w_in[i].astype(BF16), b_gate=b_gate[i].reshape(1, 2 * D_MODEL),
            conv_w=conv_w[i], conv_b=conv_b[i][None, :],
            filt_w1=filt_w1[i], filt_b1=filt_b1[i], filt_w2=filt_w2[i], filt_b2=filt_b2[i],
            filt_freq=filt_freq[i], filt_w3=filt_w3[i],
            hyena_d=jnp.broadcast_to(hyena_d[i][:, None], (HY_CH, LANES)),
            lam=jnp.stack([lam_q1[i], lam_k1[i], lam_q2[i], lam_k2[i]]),
            g_subln=g_subln[i][None, :],
            w_hy_out=w_hy_out[i].astype(BF16), w_att_out=w_att_out[i].astype(BF16),
            w_out=w_out[i].astype(BF16), w_ffn_in=w_ffn_in[i].astype(BF16),
            w_ffn_out=w_ffn_out[i].astype(BF16), w_ple_in=w_ple_in[i].astype(BF16),
            w_ple_gate=w_ple_gate[i].astype(BF16),
        )
        xs = [_layer(x, p[i], lam_init, wts) for x, p in zip(xs, ps)]
    return (xs[0], xs[1])
```

```python
import functools
import math

import jax
import jax.numpy as jnp
from jax import lax
from jax.experimental import pallas as pl
from jax.experimental.pallas import tpu as pltpu

F32 = jnp.float32
BF16 = jnp.bfloat16

D_MODEL = 2048
PLE_DIM = 256
NORM_EPS = 1e-6
HY_CH = D_MODEL // 2
FILT_BANDS = 16
FILT_EMB = 1 + 2 * FILT_BANDS
FILT_ORDER = 64
FILT_MAX_DECAY = math.log(1e-2) / 0.3
FILT_MIN_DECAY = math.log(1e-2) / 1.5
ATT_WIDTH = D_MODEL // 2
N_HEADS = 8
HEAD_DIM = ATT_WIDTH // (2 * N_HEADS)
HEAD_W = 2 * HEAD_DIM
ROPE_DIM = HEAD_DIM // 4
ROPE_THETA = 500000.0
D_FF = ((8 * D_MODEL + 3 * 256 - 1) // (3 * 256)) * 256
HY_COLS = 3 * HY_CH
GATE_COL0 = HY_COLS + 3 * ATT_WIDTH
IN_COLS = GATE_COL0 + 2 * D_MODEL
LANES = 128
LONG_SEQ = 2048
FREQ_TILE = 256
LOG2E = math.log2(math.e)
MIB = 2 ** 20


def _cparams(sem, vmem_mib):
    return pltpu.CompilerParams(dimension_semantics=sem, vmem_limit_bytes=vmem_mib * MIB)


def _const_spec(shape):
    nd = len(shape)
    return pl.BlockSpec(shape, lambda *_: (0,) * nd, pipeline_mode=pl.Buffered(1))


def _rms(x, g):
    ms = jnp.mean(x * x, axis=-1, keepdims=True)
    return x * lax.rsqrt(ms + NORM_EPS) * g


def _sigmoid(x):
    return 1.0 / (1.0 + jnp.exp(-x))


ROW_CHUNK = 128


def _for_row_chunks(n_rows, body, chunk=ROW_CHUNK):
    chunk = min(chunk, n_rows)

    def step(c, carry):
        body(pl.ds(pl.multiple_of(c * chunk, chunk), chunk))
        return carry
    lax.fori_loop(0, n_rows // chunk, step, 0)


def _rms_rows_to(h_ref, x_ref, g_ref):
    def body(rows):
        h_ref[rows, :] = _rms(x_ref[rows, :], g_ref[...]).astype(h_ref.dtype)
    _for_row_chunks(x_ref.shape[0], body)


def _inproj_kernel(x_ref, g_ref, w_ref, b_ref, o_ref, h_ref, *, n_plain):
    j = pl.program_id(1)

    @pl.when(j == 0)
    def _():
        _rms_rows_to(h_ref, x_ref, g_ref)

    acc = jnp.dot(h_ref[...], w_ref[...], preferred_element_type=F32)
    o_ref[...] = jnp.where(j >= n_plain, _sigmoid(acc + b_ref[...]), acc).astype(BF16)


def _in_proj(x2d, g, w_bf, b_flat):
    m, d = x2d.shape
    n = w_bf.shape[1]
    tm = min(1024, m)
    tn = 1024
    n_plain = GATE_COL0 // tn
    return pl.pallas_call(
        functools.partial(_inproj_kernel, n_plain=n_plain),
        out_shape=jax.ShapeDtypeStruct((m, n), BF16),
        grid=(m // tm, n // tn),
        in_specs=[
            pl.BlockSpec((tm, d), lambda i, j: (i, 0)),
            pl.BlockSpec((1, d), lambda i, j: (0, 0)),
            pl.BlockSpec((d, tn), lambda i, j: (0, j)),
            pl.BlockSpec((1, tn), lambda i, j: (0, jnp.maximum(j - n_plain, 0))),
        ],
        out_specs=pl.BlockSpec((tm, tn), lambda i, j: (i, j)),
        scratch_shapes=[pltpu.VMEM((tm, d), BF16)],
        compiler_params=_cparams(("parallel", "arbitrary"), 48),
        name="in_proj",
    )(x2d, g, w_bf, b_flat)


def _filt_mlp_kernel(emb_ref, w1_ref, b1_ref, w2_ref, b2_ref, fr_ref, w3_ref, dl_ref, ht_ref, sum_ref):
    i = pl.program_id(0)
    hp = lax.Precision.HIGHEST
    emb = emb_ref[...]
    fr = fr_ref[...]
    h1 = jnp.sin(fr * (jnp.dot(emb, w1_ref[...], precision=hp, preferred_element_type=F32) + b1_ref[...]))
    h2 = jnp.sin(fr * (jnp.dot(h1, w2_ref[...], precision=hp, preferred_element_type=F32) + b2_ref[...]))
    h = jnp.dot(h2, w3_ref[...], precision=hp, preferred_element_type=F32)
    dec = jnp.exp(-emb[:, 0:1] * dl_ref[...])
    ht = jnp.concatenate([h[:, :HY_CH] * dec, h[:, HY_CH:] * dec], axis=1).T
    ht_ref[...] = ht
    ab = jnp.abs(ht)
    part = ab[:, 0:LANES]
    for c in range(1, ab.shape[1] // LANES):
        part = part + ab[:, c * LANES:(c + 1) * LANES]

    @pl.when(i == 0)
    def _():
        sum_ref[...] = part

    @pl.when(i > 0)
    def _():
        sum_ref[...] += part


def _filt_mlp(seq, w1, b1, w2, b2, freq, w3):
    t = jnp.linspace(0.0, 1.0, seq, dtype=F32)[:, None]
    wpos = 2.0 * math.pi * jnp.arange(seq, dtype=F32) / seq
    bands = jnp.linspace(1e-4, FILT_BANDS - 1, FILT_BANDS, dtype=F32)
    ang = wpos[:, None] * bands[None, :]
    emb = jnp.concatenate([t, jnp.cos(ang), -jnp.sin(ang)], axis=-1)
    emb = jnp.pad(emb, ((0, 0), (0, LANES - FILT_EMB)))
    pad_o = LANES - FILT_ORDER
    w1p = jnp.pad(w1, ((0, LANES - FILT_EMB), (0, pad_o)))
    w2p = jnp.pad(w2, ((0, pad_o), (0, pad_o)))
    w3p = jnp.pad(w3, ((0, pad_o), (0, 0)))
    b1p = jnp.pad(b1[None, :], ((0, 0), (0, pad_o)))
    b2p = jnp.pad(b2[None, :], ((0, 0), (0, pad_o)))
    frp = jnp.pad(freq[None, :], ((0, 0), (0, pad_o)))
    deltas = jnp.abs(jnp.linspace(FILT_MIN_DECAY, FILT_MAX_DECAY, HY_CH, dtype=F32))[None, :]
    tl = min(512, seq)
    return pl.pallas_call(
        _filt_mlp_kernel,
        out_shape=(jax.ShapeDtypeStruct((2 * HY_CH, seq), F32),
                   jax.ShapeDtypeStruct((2 * HY_CH, LANES), F32)),
        grid=(seq // tl,),
        in_specs=[
            pl.BlockSpec((tl, LANES), lambda i: (i, 0)),
            _const_spec((LANES, LANES)), _const_spec((1, LANES)),
            _const_spec((LANES, LANES)), _const_spec((1, LANES)),
            _const_spec((1, LANES)), _const_spec((LANES, 2 * HY_CH)),
            _const_spec((1, HY_CH)),
        ],
        out_specs=(pl.BlockSpec((2 * HY_CH, tl), lambda i: (0, i)),
                   pl.BlockSpec((2 * HY_CH, LANES), lambda i: (0, 0))),
        compiler_params=_cparams(("arbitrary",), 48),
        name="filt_mlp",
    )(emb, w1p, b1p, w2p, b2p, frp, w3p, deltas)


def _dot_t(a, b):
    return lax.dot_general(a, b, (((1,), (1,)), ((), ())), preferred_element_type=F32)


def _filt_spec_kernel(hf_ref, hb_ref, sf_ref, sb_ref, r_ref, u1_ref, v2_ref, kl_ref, s1_ref, s2_ref, *, tf, seq):
    f = pl.program_id(1)

    @pl.when(f == 0)
    def _():
        norm = jnp.sum(sf_ref[...] + sb_ref[...], axis=1, keepdims=True) + NORM_EPS
        hf = hf_ref[...] / norm
        lane = lax.broadcasted_iota(jnp.int32, hf.shape, 1)
        hb = jnp.where(lane == 0, 0.0, hb_ref[...] / norm)
        s1_ref[...] = (hf + hb).astype(BF16)
        s2_ref[...] = (hf - hb).astype(BF16)

    s1 = s1_ref[...]
    s2 = s2_ref[...]
    c = _dot_t(s1, r_ref[:tf, :]) * (1.0 / seq)
    col = lax.broadcasted_iota(jnp.int32, c.shape, 1) + f * tf
    u1_ref[...] = jnp.where(col == 0, 0.5 * c, c)
    v2_ref[...] = jnp.where(col == 0, 0.0, _dot_t(s2, r_ref[tf:, :]) * (1.0 / seq))

    @pl.when(f == 0)
    def _():
        nyq = _dot_t(s1, r_ref[tf:, :])[:, 0:1] * (0.5 / seq)
        kl_ref[...] = jnp.broadcast_to(nyq, kl_ref.shape)


def _filt_spec(ht, sums, rmat, tf):
    seq = ht.shape[1]
    tc = 256
    nc = HY_CH // tc
    return pl.pallas_call(
        functools.partial(_filt_spec_kernel, tf=tf, seq=seq),
        out_shape=(jax.ShapeDtypeStruct((HY_CH, seq), F32),
                   jax.ShapeDtypeStruct((HY_CH, seq), F32),
                   jax.ShapeDtypeStruct((HY_CH, LANES), F32)),
        grid=(nc, seq // tf),
        in_specs=[
            pl.BlockSpec((tc, seq), lambda c, f: (c, 0)),
            pl.BlockSpec((tc, seq), lambda c, f: (nc + c, 0)),
            pl.BlockSpec((tc, LANES), lambda c, f: (c, 0)),
            pl.BlockSpec((tc, LANES), lambda c, f: (nc + c, 0)),
            pl.BlockSpec((2 * tf, seq), lambda c, f: (f, 0)),
        ],
        out_specs=(pl.BlockSpec((tc, tf), lambda c, f: (c, f)),
                   pl.BlockSpec((tc, tf), lambda c, f: (c, f)),
                   pl.BlockSpec((tc, LANES), lambda c, f: (c, 0))),
        scratch_shapes=[pltpu.VMEM((tc, seq), BF16), pltpu.VMEM((tc, seq), BF16)],
        compiler_params=_cparams(("parallel", "arbitrary"), 48),
        name="filt_spec",
    )(ht, ht, sums, sums, rmat)


def _hyprep_kernel(u0_ref, u1_ref, u2_ref, w0_ref, w1_ref, w2_ref, b0_ref, b1_ref, b2_ref, x0_ref, wt_ref):
    seq = u0_ref.shape[0]
    row = lax.broadcasted_iota(jnp.int32, u0_ref.shape, 0)

    def sconv(u_ref, w_ref, b_ref):
        u = u_ref[...].astype(F32)
        um = jnp.where(row == 0, 0.0, pltpu.roll(u, 1, 0))
        up = jnp.where(row == seq - 1, 0.0, pltpu.roll(u, seq - 1, 0))
        w = w_ref[...]
        return b_ref[...] + um * w[0:1] + u * w[1:2] + up * w[2:3]

    x0_ref[...] = sconv(u0_ref, w0_ref, b0_ref).astype(BF16)
    prod = sconv(u1_ref, w1_ref, b1_ref) * sconv(u2_ref, w2_ref, b2_ref)
    wt_ref[...] = prod.T.astype(BF16)


def _hy_prep(z3, conv_w, conv_b):
    b, seq, _ = z3.shape
    tc = 2 * LANES
    nc = HY_CH // tc
    uspec = lambda g: pl.BlockSpec((None, seq, tc), lambda i, c: (i, 0, g * nc + c))
    wspec = lambda g: pl.BlockSpec((3, tc), lambda i, c: (0, g * nc + c))
    bspec = lambda g: pl.BlockSpec((1, tc), lambda i, c: (0, g * nc + c))
    return pl.pallas_call(
        _hyprep_kernel,
        out_shape=(jax.ShapeDtypeStruct((b, seq, HY_CH), BF16),
                   jax.ShapeDtypeStruct((b, HY_CH, seq), BF16)),
        grid=(b, nc),
        in_specs=[uspec(0), uspec(1), uspec(2), wspec(0), wspec(1), wspec(2), bspec(0), bspec(1), bspec(2)],
        out_specs=(pl.BlockSpec((None, seq, tc), lambda i, c: (i, 0, c)),
                   pl.BlockSpec((None, tc, seq), lambda i, c: (i, c, 0))),
        compiler_params=_cparams(("parallel", "parallel"), 56),
        name="hy_prep",
    )(z3, z3, z3, conv_w, conv_w, conv_w, conv_b, conv_b, conv_b)


def _conv_kernel(wt_ref, ra_ref, rb_ref, u1_ref, v2_ref, kl_ref, d_ref, o_ref, acc_ref, za_ref, zb_ref, *, tf):
    t = pl.program_id(2)
    n_f = pl.num_programs(2) - 1

    def work(z_out, z_in, first=False):
        if z_in is not None:
            acc_ref[...] += jnp.dot(z_in[...], rb_ref[...], preferred_element_type=F32)
        if z_out is not None:
            ap = _dot_t(wt_ref[...], ra_ref[...])
            a = ap[:, :tf]
            p = ap[:, tf:]
            u1 = u1_ref[...]
            v2 = v2_ref[...]
            z2 = p * u1 + a * v2
            if first:
                col = lax.broadcasted_iota(jnp.int32, z2.shape, 1)
                z2 = jnp.where(col == 0, p * kl_ref[:, 0:1], z2)
            z_out[...] = jnp.concatenate([a * u1 - p * v2, z2], axis=1).astype(BF16)

    @pl.when(t == 0)
    def _():
        acc_ref[...] = jnp.zeros_like(acc_ref)
        work(za_ref, None, first=True)

    steady = jnp.logical_and(t > 0, t < n_f)

    @pl.when(jnp.logical_and(steady, t % 2 == 1))
    def _():
        work(zb_ref, za_ref)

    @pl.when(jnp.logical_and(steady, t % 2 == 0))
    def _():
        work(za_ref, zb_ref)

    def finish(z_in):
        work(None, z_in)
        y = acc_ref[...] + d_ref[:, 0:1] * wt_ref[...].astype(F32)
        o_ref[...] = y.T.astype(BF16)

    @pl.when(jnp.logical_and(t == n_f, t % 2 == 1))
    def _():
        finish(za_ref)

    @pl.when(jnp.logical_and(t == n_f, t % 2 == 0))
    def _():
        finish(zb_ref)


def _long_conv(wt, rmat, u1, v2, kl, dcol, tf):
    b, _, seq = wt.shape
    tc = 512
    n_f = seq // tf
    return pl.pallas_call(
        functools.partial(_conv_kernel, tf=tf),
        out_shape=jax.ShapeDtypeStruct((b, seq, HY_CH), BF16),
        grid=(b, HY_CH // tc, n_f + 1),
        in_specs=[
            pl.BlockSpec((None, tc, seq), lambda i, c, t: (i, c, 0)),
            pl.BlockSpec((2 * tf, seq), lambda i, c, t: (jnp.minimum(t, n_f - 1), 0)),
            pl.BlockSpec((2 * tf, seq), lambda i, c, t: (jnp.maximum(t - 1, 0), 0)),
            pl.BlockSpec((tc, tf), lambda i, c, t: (c, jnp.minimum(t, n_f - 1))),
            pl.BlockSpec((tc, tf), lambda i, c, t: (c, jnp.minimum(t, n_f - 1))),
            pl.BlockSpec((tc, LANES), lambda i, c, t: (c, 0)),
            pl.BlockSpec((tc, LANES), lambda i, c, t: (c, 0)),
        ],
        out_specs=pl.BlockSpec((None, seq, tc), lambda i, c, t: (i, 0, c)),
        scratch_shapes=[pltpu.VMEM((tc, seq), F32), pltpu.VMEM((tc, 2 * tf), BF16), pltpu.VMEM((tc, 2 * tf), BF16)],
        compiler_params=_cparams(("parallel", "parallel", "arbitrary"), 56),
        name="long_conv",
    )(wt, rmat, rmat, u1, v2, kl, dcol)


def _dft_gen_kernel(ch_ref, sh_ref, cl_ref, sl_ref, r_ref, *, tf):
    ch, sh = ch_ref[...], sh_ref[...]
    cl, sl = cl_ref[...], sl_ref[...]
    r_ref[:tf, :] = (ch * cl - sh * sl).astype(BF16)
    sin = sh * cl + ch * sl
    row = lax.broadcasted_iota(jnp.int32, sin.shape, 0)
    lane = lax.broadcasted_iota(jnp.int32, sin.shape, 1)
    alt = (1 - 2 * (lane % 2)).astype(F32)
    nyq_row = jnp.logical_and(row == 0, pl.program_id(0) == 0)
    r_ref[tf:, :] = jnp.where(nyq_row, alt, sin).astype(BF16)


def _dft_tables(seq, tf):
    idx = jnp.arange(seq, dtype=jnp.int32)
    nf = seq // tf

    def trig(freqs):
        ang = ((freqs[:, None] * idx[None, :]) % (2 * seq)).astype(F32) * (math.pi / seq)
        return jnp.cos(ang), jnp.sin(ang)

    ch, sh = trig(jnp.arange(nf, dtype=jnp.int32) * tf)
    cl, sl = trig(jnp.arange(tf, dtype=jnp.int32))
    base_spec = pl.BlockSpec((None, 1, seq), lambda f: (f, 0, 0))
    return pl.pallas_call(
        functools.partial(_dft_gen_kernel, tf=tf),
        out_shape=jax.ShapeDtypeStruct((2 * seq, seq), BF16),
        grid=(nf,),
        in_specs=[base_spec, base_spec, _const_spec((tf, seq)), _const_spec((tf, seq))],
        out_specs=pl.BlockSpec((2 * tf, seq), lambda f: (f, 0)),
        compiler_params=_cparams(("parallel",), 32),
        name="dft_tables",
    )(ch[:, None, :], sh[:, None, :], cl, sl)


KV_CHUNK = 256
ATT_Q_TILE = 512


def _attn_kernel(par_ref, q_ref, k_ref, v_ref, tab_ref, o_ref, krot_ref, vt_ref,
                 sa_buf, sb_buf, ma_buf, mb_buf,
                 *, tq, lam_init):
    t = pl.program_id(2)
    n_q = pl.num_programs(2) - 1
    seq = k_ref.shape[0]
    ck = min(KV_CHUNK, seq)
    n_chunks = seq // ck

    def rope(x, rows):
        c, sa, sb = (tab_ref[rows, i * HEAD_W:(i + 1) * HEAD_W] for i in range(3))
        return x * c + pltpu.roll(x, LANES - ROPE_DIM // 2, 1) * sa + pltpu.roll(x, ROPE_DIM // 2, 1) * sb

    buf_a = (sa_buf, ma_buf)
    buf_b = (sb_buf, mb_buf)

    def tile_work(score, finish):
        if score is not None:
            rows = pl.ds(pl.multiple_of(t * tq, tq), tq)
            q = rope(q_ref[rows, :].astype(F32), rows)
            q = q * (HEAD_DIM ** -0.5 * LOG2E)
            lane = lax.broadcasted_iota(jnp.int32, q.shape, 1)
            qq = jnp.concatenate([jnp.where(lane < HEAD_DIM, q, 0.0), jnp.where(lane >= HEAD_DIM, q, 0.0)], axis=0)
            qq = qq.astype(BF16)
            m8 = jnp.full((8, 2 * tq), -jnp.inf, F32)
        if finish is not None:
            m = jnp.max(finish[1][...], axis=0, keepdims=True)
            l8 = jnp.zeros((8, 2 * tq), F32)
            ot = jnp.zeros((HEAD_W, 2 * tq), F32)

        for c in range(n_chunks):
            kv = slice(c * ck, (c + 1) * ck)
            if finish is not None:
                p_c = jnp.exp2(finish[0][kv, :] - m)
                l8 = l8 + jnp.sum(p_c.reshape(ck // 8, 8, 2 * tq), axis=0)
                ot = ot + jnp.dot(vt_ref[:, kv], p_c.astype(BF16), preferred_element_type=F32)
            if score is not None:
                s_c = _dot_t(krot_ref[kv, :], qq)
                score[0][kv, :] = s_c
                m8 = jnp.maximum(m8, jnp.max(s_c.reshape(ck // 8, 8, 2 * tq), axis=0))
        if score is not None:
            score[1][...] = m8
        if finish is not None:
            l = jnp.sum(l8, axis=0, keepdims=True)
            lp = par_ref[0:4, :]
            lam = (jnp.exp(jnp.sum(lp[0:1] * lp[1:2], axis=-1, keepdims=True))
                   - jnp.exp(jnp.sum(lp[2:3] * lp[3:4], axis=-1, keepdims=True)) + lam_init)
            r = 1.0 / l
            o = (ot[:, :tq] * r[:, :tq] - ot[:, tq:] * (lam * r[:, tq:])).T
            o_ref[...] = (_rms(o, par_ref[4:5, :]) * (1.0 - lam_init)).astype(BF16)

    @pl.when(t == 0)
    def _():
        krot_ref[...] = rope(k_ref[...].astype(F32), slice(None)).astype(BF16)
        vt_ref[...] = v_ref[...].astype(F32).T.astype(BF16)
        tile_work(buf_a, None)

    steady = jnp.logical_and(t > 0, t < n_q)

    @pl.when(jnp.logical_and(steady, t % 2 == 1))
    def _():
        tile_work(buf_b, buf_a)

    @pl.when(jnp.logical_and(steady, t % 2 == 0))
    def _():
        tile_work(buf_a, buf_b)

    @pl.when(jnp.logical_and(t == n_q, t % 2 == 1))
    def _():
        tile_work(None, buf_a)

    @pl.when(jnp.logical_and(t == n_q, t % 2 == 0))
    def _():
        tile_work(None, buf_b)


def _attention(z3, lam_params, g_subln, lam_init):
    b, seq, _ = z3.shape
    tq = min(ATT_Q_TILE if seq > LONG_SEQ else 2 * ATT_Q_TILE, seq // 2)
    n_q = seq // tq
    inv = ROPE_THETA ** (-jnp.arange(0, ROPE_DIM, 2, dtype=F32) / ROPE_DIM)
    ang = jnp.arange(seq, dtype=F32)[:, None] * inv[None, :]
    c8, s8 = jnp.cos(ang), jnp.sin(ang)
    half = ROPE_DIM // 2
    rest = HEAD_DIM - ROPE_DIM
    one, zero = jnp.ones((seq, rest), F32), jnp.zeros((seq, rest), F32)
    z8 = jnp.zeros((seq, half), F32)
    cos_t = jnp.tile(jnp.concatenate([c8, c8, one], axis=1), (1, 2))
    sa_t = jnp.tile(jnp.concatenate([-s8, z8, zero], axis=1), (1, 2))
    sb_t = jnp.tile(jnp.concatenate([z8, s8, zero], axis=1), (1, 2))
    tab = jnp.concatenate([cos_t, sa_t, sb_t], axis=1)
    par = jnp.concatenate([jnp.pad(lam_params, ((0, 0), (0, HEAD_W - HEAD_DIM))), g_subln,
                           jnp.zeros((3, HEAD_W), F32)], axis=0)
    qb, kb, vb = (HY_COLS // HEAD_W, (HY_COLS + ATT_WIDTH) // HEAD_W, (HY_COLS + 2 * ATT_WIDTH) // HEAD_W)
    head_cols = lambda blk: pl.BlockSpec((None, seq, HEAD_W), lambda i, h, t: (i, 0, blk + h))
    return pl.pallas_call(
        functools.partial(_attn_kernel, tq=tq, lam_init=lam_init),
        out_shape=jax.ShapeDtypeStruct((b, seq, ATT_WIDTH), BF16),
        grid=(b, N_HEADS, n_q + 1),
        in_specs=[_const_spec((8, HEAD_W)), head_cols(qb), head_cols(kb), head_cols(vb),
                  _const_spec((seq, 3 * HEAD_W))],
        out_specs=pl.BlockSpec((None, tq, HEAD_W), lambda i, h, t: (i, jnp.maximum(t - 1, 0), h)),
        scratch_shapes=[pltpu.VMEM((seq, HEAD_W), BF16), pltpu.VMEM((HEAD_W, seq), BF16),
                        pltpu.VMEM((seq, 2 * tq), F32), pltpu.VMEM((seq, 2 * tq), F32),
                        pltpu.VMEM((8, 2 * tq), F32), pltpu.VMEM((8, 2 * tq), F32)],
        compiler_params=_cparams(("parallel", "parallel", "arbitrary"), 56),
        name="attention",
    )(par, z3, z3, z3, tab)


def _merge_kernel(yc_ref, x0_ref, ya_ref, g0_ref, g1_ref, x_ref, why_ref, wat_ref, wo_ref, gp_ref, o_ref):
    yh = (x0_ref[...].astype(F32) * yc_ref[...].astype(F32)).astype(BF16)
    a = jnp.dot(yh, why_ref[...], preferred_element_type=F32)
    b = jnp.dot(ya_ref[...], wat_ref[...], preferred_element_type=F32)
    m = g0_ref[...].astype(F32) * a + g1_ref[...].astype(F32) * b
    r = jnp.dot(m.astype(BF16), wo_ref[...], preferred_element_type=F32)
    o_ref[...] = x_ref[...] + _rms(r, gp_ref[...])


def _merge(yconv, x0c, yatt, z, x2d, w_hy, w_att, w_out, g_post):
    m, d = x2d.shape
    tm = min(256, m)
    row = lambda width, blk=0: pl.BlockSpec((tm, width), lambda i: (i, blk))
    gate_blk = GATE_COL0 // d
    return pl.pallas_call(
        _merge_kernel,
        out_shape=jax.ShapeDtypeStruct((m, d), F32),
        grid=(m // tm,),
        in_specs=[row(HY_CH), row(HY_CH), row(ATT_WIDTH), row(d, gate_blk), row(d, gate_blk + 1), row(d),
                  _const_spec((HY_CH, d)), _const_spec((ATT_WIDTH, d)), _const_spec((d, d)), _const_spec((1, d))],
        out_specs=row(d),
        compiler_params=_cparams(("parallel",), 56),
        name="merge",
    )(yconv, x0c, yatt, z, z, x2d, w_hy, w_att, w_out, g_post)


def _ffn_kernel(x_ref, p_ref, gpre_ref, wg_ref, wu_ref, wo_ref, gpost_ref, gple_ref, wpg_ref, wpi_ref,
                o_ref, h_ref, acc_ref):
    j = pl.program_id(1)

    @pl.when(j == 0)
    def _():
        _rms_rows_to(h_ref, x_ref, gpre_ref)
        acc_ref[...] = jnp.zeros_like(acc_ref)

    h = h_ref[...]
    gate = jnp.dot(h, wg_ref[...], preferred_element_type=F32)
    up = jnp.dot(h, wu_ref[...], preferred_element_type=F32)
    act = (gate * _sigmoid(gate) * up).astype(BF16)
    acc_ref[...] += jnp.dot(act, wo_ref[...], preferred_element_type=F32)

    @pl.when(j == pl.num_programs(1) - 1)
    def _():
        def body(rows):
            x2 = x_ref[rows, :] + _rms(acc_ref[rows, :], gpost_ref[...])
            e = jnp.dot(_rms(x2, gple_ref[...]).astype(BF16), wpg_ref[...], preferred_element_type=F32)
            pe = jnp.dot(p_ref[rows, :].astype(BF16), wpi_ref[...], preferred_element_type=F32)
            o_ref[rows, :] = x2 + pe * _sigmoid(e)
        _for_row_chunks(x_ref.shape[0], body, chunk=2 * ROW_CHUNK)


def _ffn_ple(x1, p2d, g_pre, w_in, w_out, g_post, g_ple, w_pg, w_pi):
    m, d = x1.shape
    tm = min(512, m)
    tf = 512
    nff = D_FF // tf
    return pl.pallas_call(
        _ffn_kernel,
        out_shape=jax.ShapeDtypeStruct((m, d), F32),
        grid=(m // tm, nff),
        in_specs=[
            pl.BlockSpec((tm, d), lambda i, j: (i, 0)),
            pl.BlockSpec((tm, PLE_DIM), lambda i, j: (i, 0)),
            _const_spec((1, d)),
            pl.BlockSpec((d, tf), lambda i, j: (0, j)),
            pl.BlockSpec((d, tf), lambda i, j: (0, nff + j)),
            pl.BlockSpec((tf, d), lambda i, j: (j, 0)),
            _const_spec((1, d)), _const_spec((1, d)),
            _const_spec((d, d)), _const_spec((PLE_DIM, d)),
        ],
        out_specs=pl.BlockSpec((tm, d), lambda i, j: (i, 0)),
        scratch_shapes=[pltpu.VMEM((tm, d), BF16), pltpu.VMEM((tm, d), F32)],
        compiler_params=_cparams(("parallel", "arbitrary"), 56),
        name="ffn_ple",
    )(x1, p2d, g_pre, w_in, w_in, w_out, g_post, g_ple, w_pg, w_pi)


def _layer(x, p, lam_init, wts):
    b, seq, d = x.shape
    x2d = x.reshape(b * seq, d)
    z = _in_proj(x2d, wts["g_mix_pre"], wts["w_in"], wts["b_gate"])
    z3 = z.reshape(b, seq, IN_COLS)

    tf = min(FREQ_TILE if seq > LONG_SEQ else 2 * FREQ_TILE, seq)
    rmat = _dft_tables(seq, tf)
    ht, sums = _filt_mlp(seq, wts["filt_w1"], wts["filt_b1"], wts["filt_w2"], wts["filt_b2"],
                         wts["filt_freq"], wts["filt_w3"])
    u1, v2, kl = _filt_spec(ht, sums, rmat, tf)
    x0c, wt = _hy_prep(z3, wts["conv_w"], wts["conv_b"])
    yconv = _long_conv(wt, rmat, u1, v2, kl, wts["hyena_d"], tf)

    yatt = _attention(z3, wts["lam"], wts["g_subln"], lam_init)

    x1 = _merge(yconv.reshape(b * seq, HY_CH), x0c.reshape(b * seq, HY_CH), yatt.reshape(b * seq, ATT_WIDTH),
                z, x2d, wts["w_hy_out"], wts["w_att_out"], wts["w_out"], wts["g_mix_post"])
    y = _ffn_ple(x1, p.reshape(b * seq, PLE_DIM), wts["g_ffn_pre"], wts["w_ffn_in"], wts["w_ffn_out"],
                 wts["g_ffn_post"], wts["g_ple"], wts["w_ple_gate"], wts["w_ple_in"])
    return y.reshape(b, seq, d)


def kernel(x_prompt, x_sample, p_prompt, p_sample, g_mix_pre, g_mix_post, g_ffn_pre, g_ffn_post, g_ple, w_in, b_gate, conv_w, conv_b, filt_w1, filt_b1, filt_w2, filt_b2, filt_freq, filt_w3, hyena_d, lam_q1, lam_k1, lam_q2, lam_k2, g_subln, w_hy_out, w_att_out, w_out, w_ffn_in, w_ffn_out, w_ple_in, w_ple_gate):
    depth = w_in.shape[0]
    xs = [x_prompt, x_sample]
    ps = [p_prompt, p_sample]
    for i in range(depth):
        lam_init = 0.8 - 0.6 * math.exp(-0.3 * i)
        wts = dict(
            g_mix_pre=g_mix_pre[i][None, :], g_mix_post=g_mix_post[i][None, :],
            g_ffn_pre=g_ffn_pre[i][None, :], g_ffn_post=g_ffn_post[i][None, :], g_ple=g_ple[i][None, :],
            w_in=w_in[i].astype(BF16), b_gate=b_gate[i].reshape(1, 2 * D_MODEL),
            conv_w=conv_w[i], conv_b=conv_b[i][None, :],
            filt_w1=filt_w1[i], filt_b1=filt_b1[i], filt_w2=filt_w2[i], filt_b2=filt_b2[i],
            filt_freq=filt_freq[i], filt_w3=filt_w3[i],
            hyena_d=jnp.broadcast_to(hyena_d[i][:, None], (HY_CH, LANES)),
            lam=jnp.stack([lam_q1[i], lam_k1[i], lam_q2[i], lam_k2[i]]),
            g_subln=g_subln[i][None, :],
            w_hy_out=w_hy_out[i].astype(BF16), w_att_out=w_att_out[i].astype(BF16),
            w_out=w_out[i].astype(BF16), w_ffn_in=w_ffn_in[i].astype(BF16),
            w_ffn_out=w_ffn_out[i].astype(BF16), w_ple_in=w_ple_in[i].astype(BF16),
            w_ple_gate=w_ple_gate[i].astype(BF16),
        )
        xs = [_layer(x, p[i], lam_init, wts) for x, p in zip(xs, ps)]
    return (xs[0], xs[1])
```

```python
import functools
import math

import jax
import jax.numpy as jnp
from jax import lax
from jax.experimental import pallas as pl
from jax.experimental.pallas import tpu as pltpu

F32 = jnp.float32
BF16 = jnp.bfloat16

D_MODEL = 2048
PLE_DIM = 256
NORM_EPS = 1e-6
HY_CH = D_MODEL // 2
FILT_BANDS = 16
FILT_EMB = 1 + 2 * FILT_BANDS
FILT_ORDER = 64
FILT_MAX_DECAY = math.log(1e-2) / 0.3
FILT_MIN_DECAY = math.log(1e-2) / 1.5
ATT_WIDTH = D_MODEL // 2
N_HEADS = 8
HEAD_DIM = ATT_WIDTH // (2 * N_HEADS)
HEAD_W = 2 * HEAD_DIM
ROPE_DIM = HEAD_DIM // 4
ROPE_THETA = 500000.0
D_FF = ((8 * D_MODEL + 3 * 256 - 1) // (3 * 256)) * 256
HY_COLS = 3 * HY_CH
GATE_COL0 = HY_COLS + 3 * ATT_WIDTH
IN_COLS = GATE_COL0 + 2 * D_MODEL
LANES = 128
LONG_SEQ = 2048
FREQ_TILE = 256
LOG2E = math.log2(math.e)
MIB = 2 ** 20


def _cparams(sem, vmem_mib):
    return pltpu.CompilerParams(dimension_semantics=sem, vmem_limit_bytes=vmem_mib * MIB)


def _const_spec(shape):
    nd = len(shape)
    return pl.BlockSpec(shape, lambda *_: (0,) * nd, pipeline_mode=pl.Buffered(1))


def _rms(x, g):
    ms = jnp.mean(x * x, axis=-1, keepdims=True)
    return x * lax.rsqrt(ms + NORM_EPS) * g


def _sigmoid(x):
    return 1.0 / (1.0 + jnp.exp(-x))


ROW_CHUNK = 128


def _for_row_chunks(n_rows, body, chunk=ROW_CHUNK):
    chunk = min(chunk, n_rows)

    def step(c, carry):
        body(pl.ds(pl.multiple_of(c * chunk, chunk), chunk))
        return carry
    lax.fori_loop(0, n_rows // chunk, step, 0)


def _rms_rows_to(h_ref, x_ref, g_ref):
    def body(rows):
        h_ref[rows, :] = _rms(x_ref[rows, :], g_ref[...]).astype(h_ref.dtype)
    _for_row_chunks(x_ref.shape[0], body)


def _inproj_kernel(x_ref, g_ref, w_ref, b_ref, o_ref, h_ref, *, n_plain):
    j = pl.program_id(1)

    @pl.when(j == 0)
    def _():
        _rms_rows_to(h_ref, x_ref, g_ref)

    acc = jnp.dot(h_ref[...], w_ref[...], preferred_element_type=F32)
    o_ref[...] = jnp.where(j >= n_plain, _sigmoid(acc + b_ref[...]), acc).astype(BF16)


def _in_proj(x2d, g, w_bf, b_flat):
    m, d = x2d.shape
    n = w_bf.shape[1]
    tm = min(1024, m)
    tn = 2048
    n_plain = GATE_COL0 // tn
    return pl.pallas_call(
        functools.partial(_inproj_kernel, n_plain=n_plain),
        out_shape=jax.ShapeDtypeStruct((m, n), BF16),
        grid=(m // tm, n // tn),
        in_specs=[
            pl.BlockSpec((tm, d), lambda i, j: (i, 0)),
            pl.BlockSpec((1, d), lambda i, j: (0, 0)),
            pl.BlockSpec((d, tn), lambda i, j: (0, j)),
            pl.BlockSpec((1, tn), lambda i, j: (0, jnp.maximum(j - n_plain, 0))),
        ],
        out_specs=pl.BlockSpec((tm, tn), lambda i, j: (i, j)),
        scratch_shapes=[pltpu.VMEM((tm, d), BF16)],
        compiler_params=_cparams(("parallel", "arbitrary"), 58),
        name="in_proj",
    )(x2d, g, w_bf, b_flat)


def _filt_mlp_kernel(emb_ref, w1_ref, b1_ref, w2_ref, b2_ref, fr_ref, w3_ref, dl_ref, ht_ref, sum_ref):
    i = pl.program_id(0)
    hp = lax.Precision.HIGHEST
    emb = emb_ref[...]
    fr = fr_ref[...]
    h1 = jnp.sin(fr * (jnp.dot(emb, w1_ref[...], precision=hp, preferred_element_type=F32) + b1_ref[...]))
    h2 = jnp.sin(fr * (jnp.dot(h1, w2_ref[...], precision=hp, preferred_element_type=F32) + b2_ref[...]))
    h = jnp.dot(h2, w3_ref[...], precision=hp, preferred_element_type=F32)
    dec = jnp.exp(-emb[:, 0:1] * dl_ref[...])
    ht = jnp.concatenate([h[:, :HY_CH] * dec, h[:, HY_CH:] * dec], axis=1).T
    ht_ref[...] = ht
    ab = jnp.abs(ht)
    part = ab[:, 0:LANES]
    for c in range(1, ab.shape[1] // LANES):
        part = part + ab[:, c * LANES:(c + 1) * LANES]

    @pl.when(i == 0)
    def _():
        sum_ref[...] = part

    @pl.when(i > 0)
    def _():
        sum_ref[...] += part


def _filt_mlp(seq, w1, b1, w2, b2, freq, w3):
    t = jnp.linspace(0.0, 1.0, seq, dtype=F32)[:, None]
    wpos = 2.0 * math.pi * jnp.arange(seq, dtype=F32) / seq
    bands = jnp.linspace(1e-4, FILT_BANDS - 1, FILT_BANDS, dtype=F32)
    ang = wpos[:, None] * bands[None, :]
    emb = jnp.concatenate([t, jnp.cos(ang), -jnp.sin(ang)], axis=-1)
    emb = jnp.pad(emb, ((0, 0), (0, LANES - FILT_EMB)))
    pad_o = LANES - FILT_ORDER
    w1p = jnp.pad(w1, ((0, LANES - FILT_EMB), (0, pad_o)))
    w2p = jnp.pad(w2, ((0, pad_o), (0, pad_o)))
    w3p = jnp.pad(w3, ((0, pad_o), (0, 0)))
    b1p = jnp.pad(b1[None, :], ((0, 0), (0, pad_o)))
    b2p = jnp.pad(b2[None, :], ((0, 0), (0, pad_o)))
    frp = jnp.pad(freq[None, :], ((0, 0), (0, pad_o)))
    deltas = jnp.abs(jnp.linspace(FILT_MIN_DECAY, FILT_MAX_DECAY, HY_CH, dtype=F32))[None, :]
    tl = min(512, seq)
    return pl.pallas_call(
        _filt_mlp_kernel,
        out_shape=(jax.ShapeDtypeStruct((2 * HY_CH, seq), F32),
                   jax.ShapeDtypeStruct((2 * HY_CH, LANES), F32)),
        grid=(seq // tl,),
        in_specs=[
            pl.BlockSpec((tl, LANES), lambda i: (i, 0)),
            _const_spec((LANES, LANES)), _const_spec((1, LANES)),
            _const_spec((LANES, LANES)), _const_spec((1, LANES)),
            _const_spec((1, LANES)), _const_spec((LANES, 2 * HY_CH)),
            _const_spec((1, HY_CH)),
        ],
        out_specs=(pl.BlockSpec((2 * HY_CH, tl), lambda i: (0, i)),
                   pl.BlockSpec((2 * HY_CH, LANES), lambda i: (0, 0))),
        compiler_params=_cparams(("arbitrary",), 48),
        name="filt_mlp",
    )(emb, w1p, b1p, w2p, b2p, frp, w3p, deltas)


def _dot_t(a, b):
    return lax.dot_general(a, b, (((1,), (1,)), ((), ())), preferred_element_type=F32)


def _filt_spec_kernel(hf_ref, hb_ref, sf_ref, sb_ref, r_ref, u1_ref, v2_ref, kl_ref, s1_ref, s2_ref, *, tf, seq):
    f = pl.program_id(1)

    @pl.when(f == 0)
    def _():
        norm = jnp.sum(sf_ref[...] + sb_ref[...], axis=1, keepdims=True) + NORM_EPS
        hf = hf_ref[...] / norm
        lane = lax.broadcasted_iota(jnp.int32, hf.shape, 1)
        hb = jnp.where(lane == 0, 0.0, hb_ref[...] / norm)
        s1_ref[...] = (hf + hb).astype(BF16)
        s2_ref[...] = (hf - hb).astype(BF16)

    s1 = s1_ref[...]
    s2 = s2_ref[...]
    c = _dot_t(s1, r_ref[:tf, :]) * (1.0 / seq)
    col = lax.broadcasted_iota(jnp.int32, c.shape, 1) + f * tf
    u1_ref[...] = jnp.where(col == 0, 0.5 * c, c)
    v2_ref[...] = jnp.where(col == 0, 0.0, _dot_t(s2, r_ref[tf:, :]) * (1.0 / seq))

    @pl.when(f == 0)
    def _():
        nyq = _dot_t(s1, r_ref[tf:, :])[:, 0:1] * (0.5 / seq)
        kl_ref[...] = jnp.broadcast_to(nyq, kl_ref.shape)


def _filt_spec(ht, sums, rmat, tf):
    seq = ht.shape[1]
    tc = 256
    nc = HY_CH // tc
    return pl.pallas_call(
        functools.partial(_filt_spec_kernel, tf=tf, seq=seq),
        out_shape=(jax.ShapeDtypeStruct((HY_CH, seq), F32),
                   jax.ShapeDtypeStruct((HY_CH, seq), F32),
                   jax.ShapeDtypeStruct((HY_CH, LANES), F32)),
        grid=(nc, seq // tf),
        in_specs=[
            pl.BlockSpec((tc, seq), lambda c, f: (c, 0)),
            pl.BlockSpec((tc, seq), lambda c, f: (nc + c, 0)),
            pl.BlockSpec((tc, LANES), lambda c, f: (c, 0)),
            pl.BlockSpec((tc, LANES), lambda c, f: (nc + c, 0)),
            pl.BlockSpec((2 * tf, seq), lambda c, f: (f, 0)),
        ],
        out_specs=(pl.BlockSpec((tc, tf), lambda c, f: (c, f)),
                   pl.BlockSpec((tc, tf), lambda c, f: (c, f)),
                   pl.BlockSpec((tc, LANES), lambda c, f: (c, 0))),
        scratch_shapes=[pltpu.VMEM((tc, seq), BF16), pltpu.VMEM((tc, seq), BF16)],
        compiler_params=_cparams(("parallel", "arbitrary"), 48),
        name="filt_spec",
    )(ht, ht, sums, sums, rmat)


def _hyprep_kernel(u0_ref, u1_ref, u2_ref, w0_ref, w1_ref, w2_ref, b0_ref, b1_ref, b2_ref, x0_ref, wt_ref):
    seq = u0_ref.shape[0]
    row = lax.broadcasted_iota(jnp.int32, u0_ref.shape, 0)

    def sconv(u_ref, w_ref, b_ref):
        u = u_ref[...].astype(F32)
        um = jnp.where(row == 0, 0.0, pltpu.roll(u, 1, 0))
        up = jnp.where(row == seq - 1, 0.0, pltpu.roll(u, seq - 1, 0))
        w = w_ref[...]
        return b_ref[...] + um * w[0:1] + u * w[1:2] + up * w[2:3]

    x0_ref[...] = sconv(u0_ref, w0_ref, b0_ref).astype(BF16)
    prod = sconv(u1_ref, w1_ref, b1_ref) * sconv(u2_ref, w2_ref, b2_ref)
    wt_ref[...] = prod.T.astype(BF16)


def _hy_prep(z3, conv_w, conv_b):
    b, seq, _ = z3.shape
    tc = LANES if seq > LONG_SEQ else 2 * LANES
    nc = HY_CH // tc
    uspec = lambda g: pl.BlockSpec((None, seq, tc), lambda i, c: (i, 0, g * nc + c))
    wspec = lambda g: pl.BlockSpec((3, tc), lambda i, c: (0, g * nc + c))
    bspec = lambda g: pl.BlockSpec((1, tc), lambda i, c: (0, g * nc + c))
    return pl.pallas_call(
        _hyprep_kernel,
        out_shape=(jax.ShapeDtypeStruct((b, seq, HY_CH), BF16),
                   jax.ShapeDtypeStruct((b, HY_CH, seq), BF16)),
        grid=(b, nc),
        in_specs=[uspec(0), uspec(1), uspec(2), wspec(0), wspec(1), wspec(2), bspec(0), bspec(1), bspec(2)],
        out_specs=(pl.BlockSpec((None, seq, tc), lambda i, c: (i, 0, c)),
                   pl.BlockSpec((None, tc, seq), lambda i, c: (i, c, 0))),
        compiler_params=_cparams(("parallel", "parallel"), 48),
        name="hy_prep",
    )(z3, z3, z3, conv_w, conv_w, conv_w, conv_b, conv_b, conv_b)


def _conv_kernel(wt_ref, ra_ref, rb_ref, u1_ref, v2_ref, kl_ref, d_ref, o_ref, acc_ref, za_ref, zb_ref, *, tf):
    t = pl.program_id(2)
    n_f = pl.num_programs(2) - 1

    def work(z_out, z_in, first=False):
        if z_in is not None:
            acc_ref[...] += jnp.dot(z_in[...], rb_ref[...], preferred_element_type=F32)
        if z_out is not None:
            ap = _dot_t(wt_ref[...], ra_ref[...])
            a = ap[:, :tf]
            p = ap[:, tf:]
            u1 = u1_ref[...]
            v2 = v2_ref[...]
            z2 = p * u1 + a * v2
            if first:
                col = lax.broadcasted_iota(jnp.int32, z2.shape, 1)
                z2 = jnp.where(col == 0, p * kl_ref[:, 0:1], z2)
            z_out[...] = jnp.concatenate([a * u1 - p * v2, z2], axis=1).astype(BF16)

    @pl.when(t == 0)
    def _():
        acc_ref[...] = jnp.zeros_like(acc_ref)
        work(za_ref, None, first=True)

    steady = jnp.logical_and(t > 0, t < n_f)

    @pl.when(jnp.logical_and(steady, t % 2 == 1))
    def _():
        work(zb_ref, za_ref)

    @pl.when(jnp.logical_and(steady, t % 2 == 0))
    def _():
        work(za_ref, zb_ref)

    def finish(z_in):
        work(None, z_in)
        y = acc_ref[...] + d_ref[:, 0:1] * wt_ref[...].astype(F32)
        o_ref[...] = y.T.astype(BF16)

    @pl.when(jnp.logical_and(t == n_f, t % 2 == 1))
    def _():
        finish(za_ref)

    @pl.when(jnp.logical_and(t == n_f, t % 2 == 0))
    def _():
        finish(zb_ref)


def _long_conv(wt, rmat, u1, v2, kl, dcol, tf):
    b, _, seq = wt.shape
    tc = 512
    n_f = seq // tf
    return pl.pallas_call(
        functools.partial(_conv_kernel, tf=tf),
        out_shape=jax.ShapeDtypeStruct((b, seq, HY_CH), BF16),
        grid=(b, HY_CH // tc, n_f + 1),
        in_specs=[
            pl.BlockSpec((None, tc, seq), lambda i, c, t: (i, c, 0)),
            pl.BlockSpec((2 * tf, seq), lambda i, c, t: (jnp.minimum(t, n_f - 1), 0)),
            pl.BlockSpec((2 * tf, seq), lambda i, c, t: (jnp.maximum(t - 1, 0), 0)),
            pl.BlockSpec((tc, tf), lambda i, c, t: (c, jnp.minimum(t, n_f - 1))),
            pl.BlockSpec((tc, tf), lambda i, c, t: (c, jnp.minimum(t, n_f - 1))),
            pl.BlockSpec((tc, LANES), lambda i, c, t: (c, 0)),
            pl.BlockSpec((tc, LANES), lambda i, c, t: (c, 0)),
        ],
        out_specs=pl.BlockSpec((None, seq, tc), lambda i, c, t: (i, 0, c)),
        scratch_shapes=[pltpu.VMEM((tc, seq), F32), pltpu.VMEM((tc, 2 * tf), BF16), pltpu.VMEM((tc, 2 * tf), BF16)],
        compiler_params=_cparams(("parallel", "parallel", "arbitrary"), 56),
        name="long_conv",
    )(wt, rmat, rmat, u1, v2, kl, dcol)


def _dft_gen_kernel(ch_ref, sh_ref, cl_ref, sl_ref, r_ref, *, tf):
    ch, sh = ch_ref[...], sh_ref[...]
    cl, sl = cl_ref[...], sl_ref[...]
    r_ref[:tf, :] = (ch * cl - sh * sl).astype(BF16)
    sin = sh * cl + ch * sl
    row = lax.broadcasted_iota(jnp.int32, sin.shape, 0)
    lane = lax.broadcasted_iota(jnp.int32, sin.shape, 1)
    alt = (1 - 2 * (lane % 2)).astype(F32)
    nyq_row = jnp.logical_and(row == 0, pl.program_id(0) == 0)
    r_ref[tf:, :] = jnp.where(nyq_row, alt, sin).astype(BF16)


def _dft_tables(seq, tf):
    idx = jnp.arange(seq, dtype=jnp.int32)
    nf = seq // tf

    def trig(freqs):
        ang = ((freqs[:, None] * idx[None, :]) % (2 * seq)).astype(F32) * (math.pi / seq)
        return jnp.cos(ang), jnp.sin(ang)

    ch, sh = trig(jnp.arange(nf, dtype=jnp.int32) * tf)
    cl, sl = trig(jnp.arange(tf, dtype=jnp.int32))
    base_spec = pl.BlockSpec((None, 1, seq), lambda f: (f, 0, 0))
    return pl.pallas_call(
        functools.partial(_dft_gen_kernel, tf=tf),
        out_shape=jax.ShapeDtypeStruct((2 * seq, seq), BF16),
        grid=(nf,),
        in_specs=[base_spec, base_spec, _const_spec((tf, seq)), _const_spec((tf, seq))],
        out_specs=pl.BlockSpec((2 * tf, seq), lambda f: (f, 0)),
        compiler_params=_cparams(("parallel",), 32),
        name="dft_tables",
    )(ch[:, None, :], sh[:, None, :], cl, sl)


KV_CHUNK = 256
ATT_Q_TILE = 512


def _attn_kernel(par_ref, q_ref, k_ref, v_ref, tab_ref, o_ref, krot_ref, vt_ref,
                 sa_buf, sb_buf, ma_buf, mb_buf,
                 *, tq, lam_init):
    t = pl.program_id(2)
    n_q = pl.num_programs(2) - 1
    seq = k_ref.shape[0]
    ck = min(KV_CHUNK, seq)
    n_chunks = seq // ck

    def rope(x, rows):
        c, sa, sb = (tab_ref[rows, i * HEAD_W:(i + 1) * HEAD_W] for i in range(3))
        return x * c + pltpu.roll(x, LANES - ROPE_DIM // 2, 1) * sa + pltpu.roll(x, ROPE_DIM // 2, 1) * sb

    buf_a = (sa_buf, ma_buf)
    buf_b = (sb_buf, mb_buf)

    def tile_work(score, finish):
        if score is not None:
            rows = pl.ds(pl.multiple_of(t * tq, tq), tq)
            q = rope(q_ref[rows, :].astype(F32), rows)
            q = q * (HEAD_DIM ** -0.5 * LOG2E)
            lane = lax.broadcasted_iota(jnp.int32, q.shape, 1)
            qq = jnp.concatenate([jnp.where(lane < HEAD_DIM, q, 0.0), jnp.where(lane >= HEAD_DIM, q, 0.0)], axis=0)
            qq = qq.astype(BF16)
            m8 = jnp.full((8, 2 * tq), -jnp.inf, F32)
        if finish is not None:
            m = jnp.max(finish[1][...], axis=0, keepdims=True)
            l8 = jnp.zeros((8, 2 * tq), F32)
            ot = jnp.zeros((HEAD_W, 2 * tq), F32)

        for c in range(n_chunks):
            kv = slice(c * ck, (c + 1) * ck)
            if finish is not None:
                p_c = jnp.exp2(finish[0][kv, :] - m)
                l8 = l8 + jnp.sum(p_c.reshape(ck // 8, 8, 2 * tq), axis=0)
                ot = ot + jnp.dot(vt_ref[:, kv], p_c.astype(BF16), preferred_element_type=F32)
            if score is not None:
                s_c = _dot_t(krot_ref[kv, :], qq)
                score[0][kv, :] = s_c
                m8 = jnp.maximum(m8, jnp.max(s_c.reshape(ck // 8, 8, 2 * tq), axis=0))
        if score is not None:
            score[1][...] = m8
        if finish is not None:
            l = jnp.sum(l8, axis=0, keepdims=True)
            lp = par_ref[0:4, :]
            lam = (jnp.exp(jnp.sum(lp[0:1] * lp[1:2], axis=-1, keepdims=True))
                   - jnp.exp(jnp.sum(lp[2:3] * lp[3:4], axis=-1, keepdims=True)) + lam_init)
            r = 1.0 / l
            o = (ot[:, :tq] * r[:, :tq] - ot[:, tq:] * (lam * r[:, tq:])).T
            o_ref[...] = (_rms(o, par_ref[4:5, :]) * (1.0 - lam_init)).astype(BF16)

    @pl.when(t == 0)
    def _():
        krot_ref[...] = rope(k_ref[...].astype(F32), slice(None)).astype(BF16)
        vt_ref[...] = v_ref[...].astype(F32).T.astype(BF16)
        tile_work(buf_a, None)

    steady = jnp.logical_and(t > 0, t < n_q)

    @pl.when(jnp.logical_and(steady, t % 2 == 1))
    def _():
        tile_work(buf_b, buf_a)

    @pl.when(jnp.logical_and(steady, t % 2 == 0))
    def _():
        tile_work(buf_a, buf_b)

    @pl.when(jnp.logical_and(t == n_q, t % 2 == 1))
    def _():
        tile_work(None, buf_a)

    @pl.when(jnp.logical_and(t == n_q, t % 2 == 0))
    def _():
        tile_work(None, buf_b)


def _attention(z3, lam_params, g_subln, lam_init):
    b, seq, _ = z3.shape
    tq = min(ATT_Q_TILE, seq)
    n_q = seq // tq
    inv = ROPE_THETA ** (-jnp.arange(0, ROPE_DIM, 2, dtype=F32) / ROPE_DIM)
    ang = jnp.arange(seq, dtype=F32)[:, None] * inv[None, :]
    c8, s8 = jnp.cos(ang), jnp.sin(ang)
    half = ROPE_DIM // 2
    rest = HEAD_DIM - ROPE_DIM
    one, zero = jnp.ones((seq, rest), F32), jnp.zeros((seq, rest), F32)
    z8 = jnp.zeros((seq, half), F32)
    cos_t = jnp.tile(jnp.concatenate([c8, c8, one], axis=1), (1, 2))
    sa_t = jnp.tile(jnp.concatenate([-s8, z8, zero], axis=1), (1, 2))
    sb_t = jnp.tile(jnp.concatenate([z8, s8, zero], axis=1), (1, 2))
    tab = jnp.concatenate([cos_t, sa_t, sb_t], axis=1)
    par = jnp.concatenate([jnp.pad(lam_params, ((0, 0), (0, HEAD_W - HEAD_DIM))), g_subln,
                           jnp.zeros((3, HEAD_W), F32)], axis=0)
    qb, kb, vb = (HY_COLS // HEAD_W, (HY_COLS + ATT_WIDTH) // HEAD_W, (HY_COLS + 2 * ATT_WIDTH) // HEAD_W)
    head_cols = lambda blk: pl.BlockSpec((None, seq, HEAD_W), lambda i, h, t: (i, 0, blk + h))
    return pl.pallas_call(
        functools.partial(_attn_kernel, tq=tq, lam_init=lam_init),
        out_shape=jax.ShapeDtypeStruct((b, seq, ATT_WIDTH), BF16),
        grid=(b, N_HEADS, n_q + 1),
        in_specs=[_const_spec((8, HEAD_W)), head_cols(qb), head_cols(kb), head_cols(vb),
                  _const_spec((seq, 3 * HEAD_W))],
        out_specs=pl.BlockSpec((None, tq, HEAD_W), lambda i, h, t: (i, jnp.maximum(t - 1, 0), h)),
        scratch_shapes=[pltpu.VMEM((seq, HEAD_W), BF16), pltpu.VMEM((HEAD_W, seq), BF16),
                        pltpu.VMEM((seq, 2 * tq), F32), pltpu.VMEM((seq, 2 * tq), F32),
                        pltpu.VMEM((8, 2 * tq), F32), pltpu.VMEM((8, 2 * tq), F32)],
        compiler_params=_cparams(("parallel", "parallel", "arbitrary"), 56),
        name="attention",
    )(par, z3, z3, z3, tab)


def _merge_kernel(yc_ref, x0_ref, ya_ref, g0_ref, g1_ref, x_ref, why_ref, wat_ref, wo_ref, gp_ref, o_ref):
    def body(rows):
        yh = (x0_ref[rows, :].astype(F32) * yc_ref[rows, :].astype(F32)).astype(BF16)
        a = jnp.dot(yh, why_ref[...], preferred_element_type=F32)
        b = jnp.dot(ya_ref[rows, :], wat_ref[...], preferred_element_type=F32)
        m = g0_ref[rows, :].astype(F32) * a + g1_ref[rows, :].astype(F32) * b
        r = jnp.dot(m.astype(BF16), wo_ref[...], preferred_element_type=F32)
        o_ref[rows, :] = x_ref[rows, :] + _rms(r, gp_ref[...])
    _for_row_chunks(x_ref.shape[0], body, chunk=2 * ROW_CHUNK)


def _merge(yconv, x0c, yatt, z, x2d, w_hy, w_att, w_out, g_post):
    m, d = x2d.shape
    tm = min(512, m)
    row = lambda width, blk=0: pl.BlockSpec((tm, width), lambda i: (i, blk))
    gate_blk = GATE_COL0 // d
    return pl.pallas_call(
        _merge_kernel,
        out_shape=jax.ShapeDtypeStruct((m, d), F32),
        grid=(m // tm,),
        in_specs=[row(HY_CH), row(HY_CH), row(ATT_WIDTH), row(d, gate_blk), row(d, gate_blk + 1), row(d),
                  _const_spec((HY_CH, d)), _const_spec((ATT_WIDTH, d)), _const_spec((d, d)), _const_spec((1, d))],
        out_specs=row(d),
        compiler_params=_cparams(("parallel",), 56),
        name="merge",
    )(yconv, x0c, yatt, z, z, x2d, w_hy, w_att, w_out, g_post)


def _ffn_kernel(x_ref, p_ref, gpre_ref, wg_ref, wu_ref, wo_ref, gpost_ref, gple_ref, wpg_ref, wpi_ref,
                o_ref, h_ref, acc_ref):
    j = pl.program_id(1)

    @pl.when(j == 0)
    def _():
        _rms_rows_to(h_ref, x_ref, gpre_ref)
        acc_ref[...] = jnp.zeros_like(acc_ref)

    h = h_ref[...]
    gate = jnp.dot(h, wg_ref[...], preferred_element_type=F32)
    up = jnp.dot(h, wu_ref[...], preferred_element_type=F32)
    act = (gate * _sigmoid(gate) * up).astype(BF16)
    acc_ref[...] += jnp.dot(act, wo_ref[...], preferred_element_type=F32)

    @pl.when(j == pl.num_programs(1) - 1)
    def _():
        def body(rows):
            x2 = x_ref[rows, :] + _rms(acc_ref[rows, :], gpost_ref[...])
            e = jnp.dot(_rms(x2, gple_ref[...]).astype(BF16), wpg_ref[...], preferred_element_type=F32)
            pe = jnp.dot(p_ref[rows, :].astype(BF16), wpi_ref[...], preferred_element_type=F32)
            o_ref[rows, :] = x2 + pe * _sigmoid(e)
        _for_row_chunks(x_ref.shape[0], body, chunk=2 * ROW_CHUNK)


def _ffn_ple(x1, p2d, g_pre, w_in, w_out, g_post, g_ple, w_pg, w_pi):
    m, d = x1.shape
    tm = min(512, m)
    tf = 512
    nff = D_FF // tf
    return pl.pallas_call(
        _ffn_kernel,
        out_shape=jax.ShapeDtypeStruct((m, d), F32),
        grid=(m // tm, nff),
        in_specs=[
            pl.BlockSpec((tm, d), lambda i, j: (i, 0)),
            pl.BlockSpec((tm, PLE_DIM), lambda i, j: (i, 0)),
            _const_spec((1, d)),
            pl.BlockSpec((d, tf), lambda i, j: (0, j)),
            pl.BlockSpec((d, tf), lambda i, j: (0, nff + j)),
            pl.BlockSpec((tf, d), lambda i, j: (j, 0)),
            _const_spec((1, d)), _const_spec((1, d)),
            _const_spec((d, d)), _const_spec((PLE_DIM, d)),
        ],
        out_specs=pl.BlockSpec((tm, d), lambda i, j: (i, 0)),
        scratch_shapes=[pltpu.VMEM((tm, d), BF16), pltpu.VMEM((tm, d), F32)],
        compiler_params=_cparams(("parallel", "arbitrary"), 56),
        name="ffn_ple",
    )(x1, p2d, g_pre, w_in, w_in, w_out, g_post, g_ple, w_pg, w_pi)


def _layer(x, p, lam_init, wts):
    b, seq, d = x.shape
    x2d = x.reshape(b * seq, d)
    z = _in_proj(x2d, wts["g_mix_pre"], wts["w_in"], wts["b_gate"])
    z3 = z.reshape(b, seq, IN_COLS)

    tf = min(FREQ_TILE if seq > LONG_SEQ else 2 * FREQ_TILE, seq)
    rmat = _dft_tables(seq, tf)
    ht, sums = _filt_mlp(seq, wts["filt_w1"], wts["filt_b1"], wts["filt_w2"], wts["filt_b2"],
                         wts["filt_freq"], wts["filt_w3"])
    u1, v2, kl = _filt_spec(ht, sums, rmat, tf)
    x0c, wt = _hy_prep(z3, wts["conv_w"], wts["conv_b"])
    yconv = _long_conv(wt, rmat, u1, v2, kl, wts["hyena_d"], tf)

    yatt = _attention(z3, wts["lam"], wts["g_subln"], lam_init)

    x1 = _merge(yconv.reshape(b * seq, HY_CH), x0c.reshape(b * seq, HY_CH), yatt.reshape(b * seq, ATT_WIDTH),
                z, x2d, wts["w_hy_out"], wts["w_att_out"], wts["w_out"], wts["g_mix_post"])
    y = _ffn_ple(x1, p.reshape(b * seq, PLE_DIM), wts["g_ffn_pre"], wts["w_ffn_in"], wts["w_ffn_out"],
                 wts["g_ffn_post"], wts["g_ple"], wts["w_ple_gate"], wts["w_ple_in"])
    return y.reshape(b, seq, d)


def kernel(x_prompt, x_sample, p_prompt, p_sample, g_mix_pre, g_mix_post, g_ffn_pre, g_ffn_post, g_ple, w_in, b_gate, conv_w, conv_b, filt_w1, filt_b1, filt_w2, filt_b2, filt_freq, filt_w3, hyena_d, lam_q1, lam_k1, lam_q2, lam_k2, g_subln, w_hy_out, w_att_out, w_out, w_ffn_in, w_ffn_out, w_ple_in, w_ple_gate):
    depth = w_in.shape[0]
    xs = [x_prompt, x_sample]
    ps = [p_prompt, p_sample]
    for i in range(depth):
        lam_init = 0.8 - 0.6 * math.exp(-0.3 * i)
        wts = dict(
            g_mix_pre=g_mix_pre[i][None, :], g_mix_post=g_mix_post[i][None, :],
            g_ffn_pre=g_ffn_pre[i][None, :], g_ffn_post=g_ffn_post[i][None, :], g_ple=g_ple[i][None, :],
            w_in=w_in[i].astype(BF16), b_gate=b_gate[i].reshape(1, 2 * D_MODEL),
            conv_w=conv_w[i], conv_b=conv_b[i][None, :],
            filt_w1=filt_w1[i], filt_b1=filt_b1[i], filt_w2=filt_w2[i], filt_b2=filt_b2[i],
            filt_freq=filt_freq[i], filt_w3=filt_w3[i],
            hyena_d=jnp.broadcast_to(hyena_d[i][:, None], (HY_CH, LANES)),
            lam=jnp.stack([lam_q1[i], lam_k1[i], lam_q2[i], lam_k2[i]]),
            g_subln=g_subln[i][None, :],
            w_hy_out=w_hy_out[i].astype(BF16), w_att_out=w_att_out[i].astype(BF16),
            w_out=w_out[i].astype(BF16), w_ffn_in=w_ffn_in[i].astype(BF16),
            w_ffn_out=w_ffn_out[i].astype(BF16), w_ple_in=w_ple_in[i].astype(BF16),
            w_ple_gate=w_ple_gate[i].astype(BF16),
        )
        xs = [_layer(x, p[i], lam_init, wts) for x, p in zip(xs, ps)]
    return (xs[0], xs[1])
```

```python
import functools
import math

import jax
import jax.numpy as jnp
from jax import lax
from jax.experimental import pallas as pl
from jax.experimental.pallas import tpu as pltpu

F32 = jnp.float32
BF16 = jnp.bfloat16

D_MODEL = 2048
PLE_DIM = 256
NORM_EPS = 1e-6
HY_CH = D_MODEL // 2
FILT_BANDS = 16
FILT_EMB = 1 + 2 * FILT_BANDS
FILT_ORDER = 64
FILT_MAX_DECAY = math.log(1e-2) / 0.3
FILT_MIN_DECAY = math.log(1e-2) / 1.5
ATT_WIDTH = D_MODEL // 2
N_HEADS = 8
HEAD_DIM = ATT_WIDTH // (2 * N_HEADS)
HEAD_W = 2 * HEAD_DIM
ROPE_DIM = HEAD_DIM // 4
ROPE_THETA = 500000.0
D_FF = ((8 * D_MODEL + 3 * 256 - 1) // (3 * 256)) * 256
HY_COLS = 3 * HY_CH
GATE_COL0 = HY_COLS + 3 * ATT_WIDTH
IN_COLS = GATE_COL0 + 2 * D_MODEL
LANES = 128
LOG2E = math.log2(math.e)
MIB = 2 ** 20

LONG_SEQ = 2048
FREQ_TILE = 256
IN_PROJ_ROWS, IN_PROJ_COLS = 1024, 2048
MERGE_ROWS = 512
FFN_ROWS, FFN_TILE = 512, 512
CONV_CHANNELS = 512
FILT_CHANNELS = 256
FILT_ROWS = 512
VMEM_SMALL, VMEM_MEDIUM, VMEM_LARGE, VMEM_MAX = 32, 48, 56, 58


def _cparams(sem, vmem_mib):
    return pltpu.CompilerParams(dimension_semantics=sem, vmem_limit_bytes=vmem_mib * MIB)


def _const_spec(shape):
    nd = len(shape)
    return pl.BlockSpec(shape, lambda *_: (0,) * nd, pipeline_mode=pl.Buffered(1))


def _rms(x, g):
    ms = jnp.mean(x * x, axis=-1, keepdims=True)
    return x * lax.rsqrt(ms + NORM_EPS) * g


def _sigmoid(x):
    return 1.0 / (1.0 + jnp.exp(-x))


ROW_CHUNK = 128


def _for_row_chunks(n_rows, body, chunk=ROW_CHUNK):
    chunk = min(chunk, n_rows)

    def step(c, carry):
        body(pl.ds(pl.multiple_of(c * chunk, chunk), chunk))
        return carry
    lax.fori_loop(0, n_rows // chunk, step, 0)


def _rms_rows_to(h_ref, x_ref, g_ref):
    def body(rows):
        h_ref[rows, :] = _rms(x_ref[rows, :], g_ref[...]).astype(h_ref.dtype)
    _for_row_chunks(x_ref.shape[0], body)


def _inproj_kernel(x_ref, g_ref, w_ref, b_ref, o_ref, h_ref, *, n_plain):
    j = pl.program_id(1)

    @pl.when(j == 0)
    def _():
        _rms_rows_to(h_ref, x_ref, g_ref)

    acc = jnp.dot(h_ref[...], w_ref[...], preferred_element_type=F32)
    o_ref[...] = jnp.where(j >= n_plain, _sigmoid(acc + b_ref[...]), acc).astype(BF16)


def _in_proj(x2d, g, w_bf, b_flat):
    m, d = x2d.shape
    n = w_bf.shape[1]
    tm = min(IN_PROJ_ROWS, m)
    tn = IN_PROJ_COLS
    n_plain = GATE_COL0 // tn
    return pl.pallas_call(
        functools.partial(_inproj_kernel, n_plain=n_plain),
        out_shape=jax.ShapeDtypeStruct((m, n), BF16),
        grid=(m // tm, n // tn),
        in_specs=[
            pl.BlockSpec((tm, d), lambda i, j: (i, 0)),
            pl.BlockSpec((1, d), lambda i, j: (0, 0)),
            pl.BlockSpec((d, tn), lambda i, j: (0, j)),
            pl.BlockSpec((1, tn), lambda i, j: (0, jnp.maximum(j - n_plain, 0))),
        ],
        out_specs=pl.BlockSpec((tm, tn), lambda i, j: (i, j)),
        scratch_shapes=[pltpu.VMEM((tm, d), BF16)],
        compiler_params=_cparams(("parallel", "arbitrary"), VMEM_MAX),
        name="in_proj",
    )(x2d, g, w_bf, b_flat)


def _filt_mlp_kernel(emb_ref, w1_ref, b1_ref, w2_ref, b2_ref, fr_ref, w3_ref, dl_ref, ht_ref, sum_ref):
    i = pl.program_id(0)
    hp = lax.Precision.HIGHEST
    emb = emb_ref[...]
    fr = fr_ref[...]
    h1 = jnp.sin(fr * (jnp.dot(emb, w1_ref[...], precision=hp, preferred_element_type=F32) + b1_ref[...]))
    h2 = jnp.sin(fr * (jnp.dot(h1, w2_ref[...], precision=hp, preferred_element_type=F32) + b2_ref[...]))
    h = jnp.dot(h2, w3_ref[...], precision=hp, preferred_element_type=F32)
    dec = jnp.exp(-emb[:, 0:1] * dl_ref[...])
    ht = jnp.concatenate([h[:, :HY_CH] * dec, h[:, HY_CH:] * dec], axis=1).T
    ht_ref[...] = ht
    ab = jnp.abs(ht)
    part = ab[:, 0:LANES]
    for c in range(1, ab.shape[1] // LANES):
        part = part + ab[:, c * LANES:(c + 1) * LANES]

    @pl.when(i == 0)
    def _():
        sum_ref[...] = part

    @pl.when(i > 0)
    def _():
        sum_ref[...] += part


def _filt_mlp(seq, w1, b1, w2, b2, freq, w3):
    t = jnp.linspace(0.0, 1.0, seq, dtype=F32)[:, None]
    wpos = 2.0 * math.pi * jnp.arange(seq, dtype=F32) / seq
    bands = jnp.linspace(1e-4, FILT_BANDS - 1, FILT_BANDS, dtype=F32)
    ang = wpos[:, None] * bands[None, :]
    emb = jnp.concatenate([t, jnp.cos(ang), -jnp.sin(ang)], axis=-1)
    emb = jnp.pad(emb, ((0, 0), (0, LANES - FILT_EMB)))
    pad_o = LANES - FILT_ORDER
    w1p = jnp.pad(w1, ((0, LANES - FILT_EMB), (0, pad_o)))
    w2p = jnp.pad(w2, ((0, pad_o), (0, pad_o)))
    w3p = jnp.pad(w3, ((0, pad_o), (0, 0)))
    b1p = jnp.pad(b1[None, :], ((0, 0), (0, pad_o)))
    b2p = jnp.pad(b2[None, :], ((0, 0), (0, pad_o)))
    frp = jnp.pad(freq[None, :], ((0, 0), (0, pad_o)))
    deltas = jnp.abs(jnp.linspace(FILT_MIN_DECAY, FILT_MAX_DECAY, HY_CH, dtype=F32))[None, :]
    tl = min(FILT_ROWS, seq)
    return pl.pallas_call(
        _filt_mlp_kernel,
        out_shape=(jax.ShapeDtypeStruct((2 * HY_CH, seq), F32),
                   jax.ShapeDtypeStruct((2 * HY_CH, LANES), F32)),
        grid=(seq // tl,),
        in_specs=[
            pl.BlockSpec((tl, LANES), lambda i: (i, 0)),
            _const_spec((LANES, LANES)), _const_spec((1, LANES)),
            _const_spec((LANES, LANES)), _const_spec((1, LANES)),
            _const_spec((1, LANES)), _const_spec((LANES, 2 * HY_CH)),
            _const_spec((1, HY_CH)),
        ],
        out_specs=(pl.BlockSpec((2 * HY_CH, tl), lambda i: (0, i)),
                   pl.BlockSpec((2 * HY_CH, LANES), lambda i: (0, 0))),
        compiler_params=_cparams(("arbitrary",), VMEM_MEDIUM),
        name="filt_mlp",
    )(emb, w1p, b1p, w2p, b2p, frp, w3p, deltas)


def _dot_t(a, b):
    return lax.dot_general(a, b, (((1,), (1,)), ((), ())), preferred_element_type=F32)


def _filt_spec_kernel(hf_ref, hb_ref, sf_ref, sb_ref, r_ref, u1_ref, v2_ref, kl_ref, s1_ref, s2_ref, *, tf, seq):
    f = pl.program_id(1)

    @pl.when(f == 0)
    def _():
        norm = jnp.sum(sf_ref[...] + sb_ref[...], axis=1, keepdims=True) + NORM_EPS
        hf = hf_ref[...] / norm
        lane = lax.broadcasted_iota(jnp.int32, hf.shape, 1)
        hb = jnp.where(lane == 0, 0.0, hb_ref[...] / norm)
        s1_ref[...] = (hf + hb).astype(BF16)
        s2_ref[...] = (hf - hb).astype(BF16)

    s1 = s1_ref[...]
    s2 = s2_ref[...]
    c = _dot_t(s1, r_ref[:tf, :]) * (1.0 / seq)
    col = lax.broadcasted_iota(jnp.int32, c.shape, 1) + f * tf
    u1_ref[...] = jnp.where(col == 0, 0.5 * c, c)
    v2_ref[...] = jnp.where(col == 0, 0.0, _dot_t(s2, r_ref[tf:, :]) * (1.0 / seq))

    @pl.when(f == 0)
    def _():
        nyq = _dot_t(s1, r_ref[tf:, :])[:, 0:1] * (0.5 / seq)
        kl_ref[...] = jnp.broadcast_to(nyq, kl_ref.shape)


def _filt_spec(ht, sums, rmat, tf):
    seq = ht.shape[1]
    tc = FILT_CHANNELS
    nc = HY_CH // tc
    return pl.pallas_call(
        functools.partial(_filt_spec_kernel, tf=tf, seq=seq),
        out_shape=(jax.ShapeDtypeStruct((HY_CH, seq), F32),
                   jax.ShapeDtypeStruct((HY_CH, seq), F32),
                   jax.ShapeDtypeStruct((HY_CH, LANES), F32)),
        grid=(nc, seq // tf),
        in_specs=[
            pl.BlockSpec((tc, seq), lambda c, f: (c, 0)),
            pl.BlockSpec((tc, seq), lambda c, f: (nc + c, 0)),
            pl.BlockSpec((tc, LANES), lambda c, f: (c, 0)),
            pl.BlockSpec((tc, LANES), lambda c, f: (nc + c, 0)),
            pl.BlockSpec((2 * tf, seq), lambda c, f: (f, 0)),
        ],
        out_specs=(pl.BlockSpec((tc, tf), lambda c, f: (c, f)),
                   pl.BlockSpec((tc, tf), lambda c, f: (c, f)),
                   pl.BlockSpec((tc, LANES), lambda c, f: (c, 0))),
        scratch_shapes=[pltpu.VMEM((tc, seq), BF16), pltpu.VMEM((tc, seq), BF16)],
        compiler_params=_cparams(("parallel", "arbitrary"), VMEM_MEDIUM),
        name="filt_spec",
    )(ht, ht, sums, sums, rmat)


def _hyprep_kernel(u0_ref, u1_ref, u2_ref, w0_ref, w1_ref, w2_ref, b0_ref, b1_ref, b2_ref, x0_ref, wt_ref):
    seq = u0_ref.shape[0]
    row = lax.broadcasted_iota(jnp.int32, u0_ref.shape, 0)

    def sconv(u_ref, w_ref, b_ref):
        u = u_ref[...].astype(F32)
        um = jnp.where(row == 0, 0.0, pltpu.roll(u, 1, 0))
        up = jnp.where(row == seq - 1, 0.0, pltpu.roll(u, seq - 1, 0))
        w = w_ref[...]
        return b_ref[...] + um * w[0:1] + u * w[1:2] + up * w[2:3]

    x0_ref[...] = sconv(u0_ref, w0_ref, b0_ref).astype(BF16)
    prod = sconv(u1_ref, w1_ref, b1_ref) * sconv(u2_ref, w2_ref, b2_ref)
    wt_ref[...] = prod.T.astype(BF16)


def _hy_prep(z3, conv_w, conv_b):
    b, seq, _ = z3.shape
    tc = LANES if seq > LONG_SEQ else 2 * LANES
    nc = HY_CH // tc
    uspec = lambda g: pl.BlockSpec((None, seq, tc), lambda i, c: (i, 0, g * nc + c))
    wspec = lambda g: pl.BlockSpec((3, tc), lambda i, c: (0, g * nc + c))
    bspec = lambda g: pl.BlockSpec((1, tc), lambda i, c: (0, g * nc + c))
    return pl.pallas_call(
        _hyprep_kernel,
        out_shape=(jax.ShapeDtypeStruct((b, seq, HY_CH), BF16),
                   jax.ShapeDtypeStruct((b, HY_CH, seq), BF16)),
        grid=(b, nc),
        in_specs=[uspec(0), uspec(1), uspec(2), wspec(0), wspec(1), wspec(2), bspec(0), bspec(1), bspec(2)],
        out_specs=(pl.BlockSpec((None, seq, tc), lambda i, c: (i, 0, c)),
                   pl.BlockSpec((None, tc, seq), lambda i, c: (i, c, 0))),
        compiler_params=_cparams(("parallel", "parallel"), VMEM_MEDIUM),
        name="hy_prep",
    )(z3, z3, z3, conv_w, conv_w, conv_w, conv_b, conv_b, conv_b)


def _conv_kernel(wt_ref, ra_ref, rb_ref, u1_ref, v2_ref, kl_ref, d_ref, o_ref, acc_ref, za_ref, zb_ref, *, tf):
    t = pl.program_id(2)
    n_f = pl.num_programs(2) - 1

    def work(z_out, z_in, first=False):
        if z_in is not None:
            acc_ref[...] += jnp.dot(z_in[...], rb_ref[...], preferred_element_type=F32)
        if z_out is not None:
            ap = _dot_t(wt_ref[...], ra_ref[...])
            a = ap[:, :tf]
            p = ap[:, tf:]
            u1 = u1_ref[...]
            v2 = v2_ref[...]
            z2 = p * u1 + a * v2
            if first:
                col = lax.broadcasted_iota(jnp.int32, z2.shape, 1)
                z2 = jnp.where(col == 0, p * kl_ref[:, 0:1], z2)
            z_out[...] = jnp.concatenate([a * u1 - p * v2, z2], axis=1).astype(BF16)

    @pl.when(t == 0)
    def _():
        acc_ref[...] = jnp.zeros_like(acc_ref)
        work(za_ref, None, first=True)

    steady = jnp.logical_and(t > 0, t < n_f)

    @pl.when(jnp.logical_and(steady, t % 2 == 1))
    def _():
        work(zb_ref, za_ref)

    @pl.when(jnp.logical_and(steady, t % 2 == 0))
    def _():
        work(za_ref, zb_ref)

    def finish(z_in):
        work(None, z_in)
        y = acc_ref[...] + d_ref[:, 0:1] * wt_ref[...].astype(F32)
        o_ref[...] = y.T.astype(BF16)

    @pl.when(jnp.logical_and(t == n_f, t % 2 == 1))
    def _():
        finish(za_ref)

    @pl.when(jnp.logical_and(t == n_f, t % 2 == 0))
    def _():
        finish(zb_ref)


def _long_conv(wt, rmat, u1, v2, kl, dcol, tf):
    b, _, seq = wt.shape
    tc = CONV_CHANNELS
    n_f = seq // tf
    return pl.pallas_call(
        functools.partial(_conv_kernel, tf=tf),
        out_shape=jax.ShapeDtypeStruct((b, seq, HY_CH), BF16),
        grid=(b, HY_CH // tc, n_f + 1),
        in_specs=[
            pl.BlockSpec((None, tc, seq), lambda i, c, t: (i, c, 0)),
            pl.BlockSpec((2 * tf, seq), lambda i, c, t: (jnp.minimum(t, n_f - 1), 0)),
            pl.BlockSpec((2 * tf, seq), lambda i, c, t: (jnp.maximum(t - 1, 0), 0)),
            pl.BlockSpec((tc, tf), lambda i, c, t: (c, jnp.minimum(t, n_f - 1))),
            pl.BlockSpec((tc, tf), lambda i, c, t: (c, jnp.minimum(t, n_f - 1))),
            pl.BlockSpec((tc, LANES), lambda i, c, t: (c, 0)),
            pl.BlockSpec((tc, LANES), lambda i, c, t: (c, 0)),
        ],
        out_specs=pl.BlockSpec((None, seq, tc), lambda i, c, t: (i, 0, c)),
        scratch_shapes=[pltpu.VMEM((tc, seq), F32), pltpu.VMEM((tc, 2 * tf), BF16), pltpu.VMEM((tc, 2 * tf), BF16)],
        compiler_params=_cparams(("parallel", "parallel", "arbitrary"), VMEM_LARGE),
        name="long_conv",
    )(wt, rmat, rmat, u1, v2, kl, dcol)


def _dft_gen_kernel(ch_ref, sh_ref, cl_ref, sl_ref, r_ref, *, tf):
    ch, sh = ch_ref[...], sh_ref[...]
    cl, sl = cl_ref[...], sl_ref[...]
    r_ref[:tf, :] = (ch * cl - sh * sl).astype(BF16)
    sin = sh * cl + ch * sl
    row = lax.broadcasted_iota(jnp.int32, sin.shape, 0)
    lane = lax.broadcasted_iota(jnp.int32, sin.shape, 1)
    alt = (1 - 2 * (lane % 2)).astype(F32)
    nyq_row = jnp.logical_and(row == 0, pl.program_id(0) == 0)
    r_ref[tf:, :] = jnp.where(nyq_row, alt, sin).astype(BF16)


def _dft_tables(seq, tf):
    idx = jnp.arange(seq, dtype=jnp.int32)
    nf = seq // tf

    def trig(freqs):
        ang = ((freqs[:, None] * idx[None, :]) % (2 * seq)).astype(F32) * (math.pi / seq)
        return jnp.cos(ang), jnp.sin(ang)

    ch, sh = trig(jnp.arange(nf, dtype=jnp.int32) * tf)
    cl, sl = trig(jnp.arange(tf, dtype=jnp.int32))
    base_spec = pl.BlockSpec((None, 1, seq), lambda f: (f, 0, 0))
    return pl.pallas_call(
        functools.partial(_dft_gen_kernel, tf=tf),
        out_shape=jax.ShapeDtypeStruct((2 * seq, seq), BF16),
        grid=(nf,),
        in_specs=[base_spec, base_spec, _const_spec((tf, seq)), _const_spec((tf, seq))],
        out_specs=pl.BlockSpec((2 * tf, seq), lambda f: (f, 0)),
        compiler_params=_cparams(("parallel",), VMEM_SMALL),
        name="dft_tables",
    )(ch[:, None, :], sh[:, None, :], cl, sl)


KV_CHUNK = 256
ATT_Q_TILE = 512


def _attn_kernel(par_ref, q_ref, k_ref, v_ref, tab_ref, o_ref, krot_ref, vt_ref,
                 sa_buf, sb_buf, ma_buf, mb_buf,
                 *, tq, lam_init):
    t = pl.program_id(2)
    n_q = pl.num_programs(2) - 1
    seq = k_ref.shape[0]
    ck = min(KV_CHUNK, seq)
    n_chunks = seq // ck

    def rope(x, rows):
        c, sa, sb = (tab_ref[rows, i * HEAD_W:(i + 1) * HEAD_W] for i in range(3))
        return x * c + pltpu.roll(x, LANES - ROPE_DIM // 2, 1) * sa + pltpu.roll(x, ROPE_DIM // 2, 1) * sb

    buf_a = (sa_buf, ma_buf)
    buf_b = (sb_buf, mb_buf)

    def tile_work(score, finish):
        if score is not None:
            rows = pl.ds(pl.multiple_of(t * tq, tq), tq)
            q = rope(q_ref[rows, :].astype(F32), rows)
            q = q * (HEAD_DIM ** -0.5 * LOG2E)
            lane = lax.broadcasted_iota(jnp.int32, q.shape, 1)
            qq = jnp.concatenate([jnp.where(lane < HEAD_DIM, q, 0.0), jnp.where(lane >= HEAD_DIM, q, 0.0)], axis=0)
            qq = qq.astype(BF16)
            m8 = jnp.full((8, 2 * tq), -jnp.inf, F32)
        if finish is not None:
            m = jnp.max(finish[1][...], axis=0, keepdims=True)
            l8 = jnp.zeros((8, 2 * tq), F32)
            ot = jnp.zeros((HEAD_W, 2 * tq), F32)

        for c in range(n_chunks):
            kv = slice(c * ck, (c + 1) * ck)
            if finish is not None:
                p_c = jnp.exp2(finish[0][kv, :] - m)
                l8 = l8 + jnp.sum(p_c.reshape(ck // 8, 8, 2 * tq), axis=0)
                ot = ot + jnp.dot(vt_ref[:, kv], p_c.astype(BF16), preferred_element_type=F32)
            if score is not None:
                s_c = _dot_t(krot_ref[kv, :], qq)
                score[0][kv, :] = s_c
                m8 = jnp.maximum(m8, jnp.max(s_c.reshape(ck // 8, 8, 2 * tq), axis=0))
        if score is not None:
            score[1][...] = m8
        if finish is not None:
            l = jnp.sum(l8, axis=0, keepdims=True)
            lp = par_ref[0:4, :]
            lam = (jnp.exp(jnp.sum(lp[0:1] * lp[1:2], axis=-1, keepdims=True))
                   - jnp.exp(jnp.sum(lp[2:3] * lp[3:4], axis=-1, keepdims=True)) + lam_init)
            r = 1.0 / l
            o = (ot[:, :tq] * r[:, :tq] - ot[:, tq:] * (lam * r[:, tq:])).T
            o_ref[...] = (_rms(o, par_ref[4:5, :]) * (1.0 - lam_init)).astype(BF16)

    @pl.when(t == 0)
    def _():
        krot_ref[...] = rope(k_ref[...].astype(F32), slice(None)).astype(BF16)
        vt_ref[...] = v_ref[...].astype(F32).T.astype(BF16)
        tile_work(buf_a, None)

    steady = jnp.logical_and(t > 0, t < n_q)

    @pl.when(jnp.logical_and(steady, t % 2 == 1))
    def _():
        tile_work(buf_b, buf_a)

    @pl.when(jnp.logical_and(steady, t % 2 == 0))
    def _():
        tile_work(buf_a, buf_b)

    @pl.when(jnp.logical_and(t == n_q, t % 2 == 1))
    def _():
        tile_work(None, buf_a)

    @pl.when(jnp.logical_and(t == n_q, t % 2 == 0))
    def _():
        tile_work(None, buf_b)


def _attention(z3, lam_params, g_subln, lam_init):
    b, seq, _ = z3.shape
    tq = min(ATT_Q_TILE, seq)
    n_q = seq // tq
    inv = ROPE_THETA ** (-jnp.arange(0, ROPE_DIM, 2, dtype=F32) / ROPE_DIM)
    ang = jnp.arange(seq, dtype=F32)[:, None] * inv[None, :]
    c8, s8 = jnp.cos(ang), jnp.sin(ang)
    half = ROPE_DIM // 2
    rest = HEAD_DIM - ROPE_DIM
    one, zero = jnp.ones((seq, rest), F32), jnp.zeros((seq, rest), F32)
    z8 = jnp.zeros((seq, half), F32)
    cos_t = jnp.tile(jnp.concatenate([c8, c8, one], axis=1), (1, 2))
    sa_t = jnp.tile(jnp.concatenate([-s8, z8, zero], axis=1), (1, 2))
    sb_t = jnp.tile(jnp.concatenate([z8, s8, zero], axis=1), (1, 2))
    tab = jnp.concatenate([cos_t, sa_t, sb_t], axis=1)
    par = jnp.concatenate([jnp.pad(lam_params, ((0, 0), (0, HEAD_W - HEAD_DIM))), g_subln,
                           jnp.zeros((3, HEAD_W), F32)], axis=0)
    qb, kb, vb = (HY_COLS // HEAD_W, (HY_COLS + ATT_WIDTH) // HEAD_W, (HY_COLS + 2 * ATT_WIDTH) // HEAD_W)
    head_cols = lambda blk: pl.BlockSpec((None, seq, HEAD_W), lambda i, h, t: (i, 0, blk + h))
    return pl.pallas_call(
        functools.partial(_attn_kernel, tq=tq, lam_init=lam_init),
        out_shape=jax.ShapeDtypeStruct((b, seq, ATT_WIDTH), BF16),
        grid=(b, N_HEADS, n_q + 1),
        in_specs=[_const_spec((8, HEAD_W)), head_cols(qb), head_cols(kb), head_cols(vb),
                  _const_spec((seq, 3 * HEAD_W))],
        out_specs=pl.BlockSpec((None, tq, HEAD_W), lambda i, h, t: (i, jnp.maximum(t - 1, 0), h)),
        scratch_shapes=[pltpu.VMEM((seq, HEAD_W), BF16), pltpu.VMEM((HEAD_W, seq), BF16),
                        pltpu.VMEM((seq, 2 * tq), F32), pltpu.VMEM((seq, 2 * tq), F32),
                        pltpu.VMEM((8, 2 * tq), F32), pltpu.VMEM((8, 2 * tq), F32)],
        compiler_params=_cparams(("parallel", "parallel", "arbitrary"), VMEM_LARGE),
        name="attention",
    )(par, z3, z3, z3, tab)


def _merge_kernel(yc_ref, x0_ref, ya_ref, g0_ref, g1_ref, x_ref, why_ref, wat_ref, wo_ref, gp_ref, o_ref):
    def body(rows):
        yh = (x0_ref[rows, :].astype(F32) * yc_ref[rows, :].astype(F32)).astype(BF16)
        a = jnp.dot(yh, why_ref[...], preferred_element_type=F32)
        b = jnp.dot(ya_ref[rows, :], wat_ref[...], preferred_element_type=F32)
        m = g0_ref[rows, :].astype(F32) * a + g1_ref[rows, :].astype(F32) * b
        r = jnp.dot(m.astype(BF16), wo_ref[...], preferred_element_type=F32)
        o_ref[rows, :] = x_ref[rows, :] + _rms(r, gp_ref[...])
    _for_row_chunks(x_ref.shape[0], body, chunk=2 * ROW_CHUNK)


def _merge(yconv, x0c, yatt, z, x2d, w_hy, w_att, w_out, g_post):
    m, d = x2d.shape
    tm = min(MERGE_ROWS, m)
    row = lambda width, blk=0: pl.BlockSpec((tm, width), lambda i: (i, blk))
    gate_blk = GATE_COL0 // d
    return pl.pallas_call(
        _merge_kernel,
        out_shape=jax.ShapeDtypeStruct((m, d), F32),
        grid=(m // tm,),
        in_specs=[row(HY_CH), row(HY_CH), row(ATT_WIDTH), row(d, gate_blk), row(d, gate_blk + 1), row(d),
                  _const_spec((HY_CH, d)), _const_spec((ATT_WIDTH, d)), _const_spec((d, d)), _const_spec((1, d))],
        out_specs=row(d),
        compiler_params=_cparams(("parallel",), VMEM_LARGE),
        name="merge",
    )(yconv, x0c, yatt, z, z, x2d, w_hy, w_att, w_out, g_post)


def _ffn_kernel(x_ref, p_ref, gpre_ref, wg_ref, wu_ref, wo_ref, gpost_ref, gple_ref, wpg_ref, wpi_ref,
                o_ref, h_ref, acc_ref):
    j = pl.program_id(1)

    @pl.when(j == 0)
    def _():
        _rms_rows_to(h_ref, x_ref, gpre_ref)
        acc_ref[...] = jnp.zeros_like(acc_ref)

    h = h_ref[...]
    gate = jnp.dot(h, wg_ref[...], preferred_element_type=F32)
    up = jnp.dot(h, wu_ref[...], preferred_element_type=F32)
    act = (gate * _sigmoid(gate) * up).astype(BF16)
    acc_ref[...] += jnp.dot(act, wo_ref[...], preferred_element_type=F32)

    @pl.when(j == pl.num_programs(1) - 1)
    def _():
        def body(rows):
            x2 = x_ref[rows, :] + _rms(acc_ref[rows, :], gpost_ref[...])
            e = jnp.dot(_rms(x2, gple_ref[...]).astype(BF16), wpg_ref[...], preferred_element_type=F32)
            pe = jnp.dot(p_ref[rows, :].astype(BF16), wpi_ref[...], preferred_element_type=F32)
            o_ref[rows, :] = x2 + pe * _sigmoid(e)
        _for_row_chunks(x_ref.shape[0], body, chunk=2 * ROW_CHUNK)


def _ffn_ple(x1, p2d, g_pre, w_in, w_out, g_post, g_ple, w_pg, w_pi):
    m, d = x1.shape
    tm = min(FFN_ROWS, m)
    tf = FFN_TILE
    nff = D_FF // tf
    return pl.pallas_call(
        _ffn_kernel,
        out_shape=jax.ShapeDtypeStruct((m, d), F32),
        grid=(m // tm, nff),
        in_specs=[
            pl.BlockSpec((tm, d), lambda i, j: (i, 0)),
            pl.BlockSpec((tm, PLE_DIM), lambda i, j: (i, 0)),
            _const_spec((1, d)),
            pl.BlockSpec((d, tf), lambda i, j: (0, j)),
            pl.BlockSpec((d, tf), lambda i, j: (0, nff + j)),
            pl.BlockSpec((tf, d), lambda i, j: (j, 0)),
            _const_spec((1, d)), _const_spec((1, d)),
            _const_spec((d, d)), _const_spec((PLE_DIM, d)),
        ],
        out_specs=pl.BlockSpec((tm, d), lambda i, j: (i, 0)),
        scratch_shapes=[pltpu.VMEM((tm, d), BF16), pltpu.VMEM((tm, d), F32)],
        compiler_params=_cparams(("parallel", "arbitrary"), VMEM_LARGE),
        name="ffn_ple",
    )(x1, p2d, g_pre, w_in, w_in, w_out, g_post, g_ple, w_pg, w_pi)


def _layer(x, p, lam_init, wts):
    b, seq, d = x.shape
    x2d = x.reshape(b * seq, d)
    z = _in_proj(x2d, wts["g_mix_pre"], wts["w_in"], wts["b_gate"])
    z3 = z.reshape(b, seq, IN_COLS)

    tf = min(FREQ_TILE if seq > LONG_SEQ else 2 * FREQ_TILE, seq)
    rmat = _dft_tables(seq, tf)
    ht, sums = _filt_mlp(seq, wts["filt_w1"], wts["filt_b1"], wts["filt_w2"], wts["filt_b2"],
                         wts["filt_freq"], wts["filt_w3"])
    u1, v2, kl = _filt_spec(ht, sums, rmat, tf)
    x0c, wt = _hy_prep(z3, wts["conv_w"], wts["conv_b"])
    yconv = _long_conv(wt, rmat, u1, v2, kl, wts["hyena_d"], tf)

    yatt = _attention(z3, wts["lam"], wts["g_subln"], lam_init)

    x1 = _merge(yconv.reshape(b * seq, HY_CH), x0c.reshape(b * seq, HY_CH), yatt.reshape(b * seq, ATT_WIDTH),
                z, x2d, wts["w_hy_out"], wts["w_att_out"], wts["w_out"], wts["g_mix_post"])
    y = _ffn_ple(x1, p.reshape(b * seq, PLE_DIM), wts["g_ffn_pre"], wts["w_ffn_in"], wts["w_ffn_out"],
                 wts["g_ffn_post"], wts["g_ple"], wts["w_ple_gate"], wts["w_ple_in"])
    return y.reshape(b, seq, d)


def kernel(x_prompt, x_sample, p_prompt, p_sample, g_mix_pre, g_mix_post, g_ffn_pre, g_ffn_post, g_ple, w_in, b_gate, conv_w, conv_b, filt_w1, filt_b1, filt_w2, filt_b2, filt_freq, filt_w3, hyena_d, lam_q1, lam_k1, lam_q2, lam_k2, g_subln, w_hy_out, w_att_out, w_out, w_ffn_in, w_ffn_out, w_ple_in, w_ple_gate):
    depth = w_in.shape[0]
    xs = [x_prompt, x_sample]
    ps = [p_prompt, p_sample]
    for i in range(depth):
        lam_init = 0.8 - 0.6 * math.exp(-0.3 * i)
        wts = dict(
            g_mix_pre=g_mix_pre[i][None, :], g_mix_post=g_mix_post[i][None, :],
            g_ffn_pre=g_ffn_pre[i][None, :], g_ffn_post=g_ffn_post[i][None, :], g_ple=g_ple[i][None, :],
            w_in=w_in[i].astype(BF16), b_gate=b_gate[i].reshape(1, 2 * D_MODEL),
            conv_w=conv_w[i], conv_b=conv_b[i][None, :],
            filt_w1=filt_w1[i], filt_b1=filt_b1[i], filt_w2=filt_w2[i], filt_b2=filt_b2[i],
            filt_freq=filt_freq[i], filt_w3=filt_w3[i],
            hyena_d=jnp.broadcast_to(hyena_d[i][:, None], (HY_CH, LANES)),
            lam=jnp.stack([lam_q1[i], lam_k1[i], lam_q2[i], lam_k2[i]]),
            g_subln=g_subln[i][None, :],
            w_hy_out=w_hy_out[i].astype(BF16), w_att_out=w_att_out[i].astype(BF16),
            w_out=w_out[i].astype(BF16), w_ffn_in=w_ffn_in[i].astype(BF16),
            w_ffn_out=w_ffn_out[i].astype(BF16), w_ple_in=w_ple_in[i].astype(BF16),
            w_ple_gate=w_ple_gate[i].astype(BF16),
        )
        xs = [_layer(x, p[i], lam_init, wts) for x, p in zip(xs, ps)]
    return (xs[0], xs[1])
```

```python
import functools
import math

import jax
import jax.numpy as jnp
from jax import lax
from jax.experimental import pallas as pl
from jax.experimental.pallas import tpu as pltpu

F32 = jnp.float32
BF16 = jnp.bfloat16

D_MODEL = 2048
PLE_DIM = 256
NORM_EPS = 1e-6
HY_CH = D_MODEL // 2
FILT_BANDS = 16
FILT_EMB = 1 + 2 * FILT_BANDS
FILT_ORDER = 64
FILT_MAX_DECAY = math.log(1e-2) / 0.3
FILT_MIN_DECAY = math.log(1e-2) / 1.5
ATT_WIDTH = D_MODEL // 2
N_HEADS = 8
HEAD_DIM = ATT_WIDTH // (2 * N_HEADS)
HEAD_W = 2 * HEAD_DIM
ROPE_DIM = HEAD_DIM // 4
ROPE_THETA = 500000.0
D_FF = ((8 * D_MODEL + 3 * 256 - 1) // (3 * 256)) * 256
HY_COLS = 3 * HY_CH
GATE_COL0 = HY_COLS + 3 * ATT_WIDTH
IN_COLS = GATE_COL0 + 2 * D_MODEL
LANES = 128
LOG2E = math.log2(math.e)
MIB = 2 ** 20

LONG_SEQ = 2048
FREQ_TILE = 256
IN_PROJ_ROWS, IN_PROJ_COLS = 1024, 2048
MERGE_ROWS = 512
FFN_ROWS, FFN_TILE = 512, 512
CONV_CHANNELS = 512
FILT_CHANNELS = 256
FILT_ROWS = 512
VMEM_SMALL, VMEM_MEDIUM, VMEM_LARGE, VMEM_MAX = 32, 48, 56, 58


def _cparams(sem, vmem_mib):
    return pltpu.CompilerParams(dimension_semantics=sem, vmem_limit_bytes=vmem_mib * MIB)


def _const_spec(shape):
    nd = len(shape)
    return pl.BlockSpec(shape, lambda *_: (0,) * nd, pipeline_mode=pl.Buffered(1))


def _rms(x, g):
    ms = jnp.mean(x * x, axis=-1, keepdims=True)
    return x * lax.rsqrt(ms + NORM_EPS) * g


def _sigmoid(x):
    return 1.0 / (1.0 + jnp.exp(-x))


ROW_CHUNK = 128


def _for_row_chunks(n_rows, body, chunk=ROW_CHUNK):
    chunk = min(chunk, n_rows)

    def step(c, carry):
        body(pl.ds(pl.multiple_of(c * chunk, chunk), chunk))
        return carry
    lax.fori_loop(0, n_rows // chunk, step, 0)


def _rms_rows_to(h_ref, x_ref, g_ref):
    def body(rows):
        h_ref[rows, :] = _rms(x_ref[rows, :], g_ref[...]).astype(h_ref.dtype)
    _for_row_chunks(x_ref.shape[0], body)


def _inproj_kernel(x_ref, g_ref, w_ref, b_ref, o_ref, h_ref, *, n_plain):
    j = pl.program_id(1)

    @pl.when(j == 0)
    def _():
        _rms_rows_to(h_ref, x_ref, g_ref)

    acc = jnp.dot(h_ref[...], w_ref[...], preferred_element_type=F32)
    o_ref[...] = jnp.where(j >= n_plain, _sigmoid(acc + b_ref[...]), acc).astype(BF16)


def _in_proj(x2d, g, w_bf, b_flat):
    m, d = x2d.shape
    n = w_bf.shape[1]
    tm = min(IN_PROJ_ROWS, m)
    tn = IN_PROJ_COLS
    n_plain = GATE_COL0 // tn
    return pl.pallas_call(
        functools.partial(_inproj_kernel, n_plain=n_plain),
        out_shape=jax.ShapeDtypeStruct((m, n), BF16),
        grid=(m // tm, n // tn),
        in_specs=[
            pl.BlockSpec((tm, d), lambda i, j: (i, 0)),
            pl.BlockSpec((1, d), lambda i, j: (0, 0)),
            pl.BlockSpec((d, tn), lambda i, j: (0, j)),
            pl.BlockSpec((1, tn), lambda i, j: (0, jnp.maximum(j - n_plain, 0))),
        ],
        out_specs=pl.BlockSpec((tm, tn), lambda i, j: (i, j)),
        scratch_shapes=[pltpu.VMEM((tm, d), BF16)],
        compiler_params=_cparams(("parallel", "arbitrary"), VMEM_MAX),
        name="in_proj",
    )(x2d, g, w_bf, b_flat)


def _filt_mlp_kernel(emb_ref, w1_ref, b1_ref, w2_ref, b2_ref, fr_ref, w3_ref, dl_ref, ht_ref, sum_ref):
    i = pl.program_id(0)
    hp = lax.Precision.HIGHEST
    emb = emb_ref[...]
    fr = fr_ref[...]
    h1 = jnp.sin(fr * (jnp.dot(emb, w1_ref[...], precision=hp, preferred_element_type=F32) + b1_ref[...]))
    h2 = jnp.sin(fr * (jnp.dot(h1, w2_ref[...], precision=hp, preferred_element_type=F32) + b2_ref[...]))
    h = jnp.dot(h2, w3_ref[...], precision=hp, preferred_element_type=F32)
    dec = jnp.exp(-emb[:, 0:1] * dl_ref[...])
    ht = jnp.concatenate([h[:, :HY_CH] * dec, h[:, HY_CH:] * dec], axis=1).T
    ht_ref[...] = ht
    ab = jnp.abs(ht)
    part = ab[:, 0:LANES]
    for c in range(1, ab.shape[1] // LANES):
        part = part + ab[:, c * LANES:(c + 1) * LANES]

    @pl.when(i == 0)
    def _():
        sum_ref[...] = part

    @pl.when(i > 0)
    def _():
        sum_ref[...] += part


def _filt_mlp(seq, w1, b1, w2, b2, freq, w3):
    t = jnp.linspace(0.0, 1.0, seq, dtype=F32)[:, None]
    wpos = 2.0 * math.pi * jnp.arange(seq, dtype=F32) / seq
    bands = jnp.linspace(1e-4, FILT_BANDS - 1, FILT_BANDS, dtype=F32)
    ang = wpos[:, None] * bands[None, :]
    emb = jnp.concatenate([t, jnp.cos(ang), -jnp.sin(ang)], axis=-1)
    emb = jnp.pad(emb, ((0, 0), (0, LANES - FILT_EMB)))
    pad_o = LANES - FILT_ORDER
    w1p = jnp.pad(w1, ((0, LANES - FILT_EMB), (0, pad_o)))
    w2p = jnp.pad(w2, ((0, pad_o), (0, pad_o)))
    w3p = jnp.pad(w3, ((0, pad_o), (0, 0)))
    b1p = jnp.pad(b1[None, :], ((0, 0), (0, pad_o)))
    b2p = jnp.pad(b2[None, :], ((0, 0), (0, pad_o)))
    frp = jnp.pad(freq[None, :], ((0, 0), (0, pad_o)))
    deltas = jnp.abs(jnp.linspace(FILT_MIN_DECAY, FILT_MAX_DECAY, HY_CH, dtype=F32))[None, :]
    tl = min(FILT_ROWS, seq)
    return pl.pallas_call(
        _filt_mlp_kernel,
        out_shape=(jax.ShapeDtypeStruct((2 * HY_CH, seq), F32),
                   jax.ShapeDtypeStruct((2 * HY_CH, LANES), F32)),
        grid=(seq // tl,),
        in_specs=[
            pl.BlockSpec((tl, LANES), lambda i: (i, 0)),
            _const_spec((LANES, LANES)), _const_spec((1, LANES)),
            _const_spec((LANES, LANES)), _const_spec((1, LANES)),
            _const_spec((1, LANES)), _const_spec((LANES, 2 * HY_CH)),
            _const_spec((1, HY_CH)),
        ],
        out_specs=(pl.BlockSpec((2 * HY_CH, tl), lambda i: (0, i)),
                   pl.BlockSpec((2 * HY_CH, LANES), lambda i: (0, 0))),
        compiler_params=_cparams(("arbitrary",), VMEM_MEDIUM),
        name="filt_mlp",
    )(emb, w1p, b1p, w2p, b2p, frp, w3p, deltas)


def _dot_t(a, b):
    return lax.dot_general(a, b, (((1,), (1,)), ((), ())), preferred_element_type=F32)


def _filt_spec_kernel(hf_ref, hb_ref, sf_ref, sb_ref, r_ref, u1_ref, v2_ref, kl_ref, s1_ref, s2_ref, *, tf, seq):
    f = pl.program_id(1)

    @pl.when(f == 0)
    def _():
        norm = jnp.sum(sf_ref[...] + sb_ref[...], axis=1, keepdims=True) + NORM_EPS
        hf = hf_ref[...] / norm
        lane = lax.broadcasted_iota(jnp.int32, hf.shape, 1)
        hb = jnp.where(lane == 0, 0.0, hb_ref[...] / norm)
        s1_ref[...] = (hf + hb).astype(BF16)
        s2_ref[...] = (hf - hb).astype(BF16)

    s1 = s1_ref[...]
    s2 = s2_ref[...]
    c = _dot_t(s1, r_ref[:tf, :]) * (1.0 / seq)
    col = lax.broadcasted_iota(jnp.int32, c.shape, 1) + f * tf
    u1_ref[...] = jnp.where(col == 0, 0.5 * c, c)
    v2_ref[...] = jnp.where(col == 0, 0.0, _dot_t(s2, r_ref[tf:, :]) * (1.0 / seq))

    @pl.when(f == 0)
    def _():
        nyq = _dot_t(s1, r_ref[tf:, :])[:, 0:1] * (0.5 / seq)
        kl_ref[...] = jnp.broadcast_to(nyq, kl_ref.shape)


def _filt_spec(ht, sums, rmat, tf):
    seq = ht.shape[1]
    tc = FILT_CHANNELS
    nc = HY_CH // tc
    return pl.pallas_call(
        functools.partial(_filt_spec_kernel, tf=tf, seq=seq),
        out_shape=(jax.ShapeDtypeStruct((HY_CH, seq), F32),
                   jax.ShapeDtypeStruct((HY_CH, seq), F32),
                   jax.ShapeDtypeStruct((HY_CH, LANES), F32)),
        grid=(nc, seq // tf),
        in_specs=[
            pl.BlockSpec((tc, seq), lambda c, f: (c, 0)),
            pl.BlockSpec((tc, seq), lambda c, f: (nc + c, 0)),
            pl.BlockSpec((tc, LANES), lambda c, f: (c, 0)),
            pl.BlockSpec((tc, LANES), lambda c, f: (nc + c, 0)),
            pl.BlockSpec((2 * tf, seq), lambda c, f: (f, 0)),
        ],
        out_specs=(pl.BlockSpec((tc, tf), lambda c, f: (c, f)),
                   pl.BlockSpec((tc, tf), lambda c, f: (c, f)),
                   pl.BlockSpec((tc, LANES), lambda c, f: (c, 0))),
        scratch_shapes=[pltpu.VMEM((tc, seq), BF16), pltpu.VMEM((tc, seq), BF16)],
        compiler_params=_cparams(("parallel", "arbitrary"), VMEM_MEDIUM),
        name="filt_spec",
    )(ht, ht, sums, sums, rmat)


def _hyprep_kernel(u0_ref, u1_ref, u2_ref, w0_ref, w1_ref, w2_ref, b0_ref, b1_ref, b2_ref, x0_ref, wt_ref):
    seq = u0_ref.shape[0]
    row = lax.broadcasted_iota(jnp.int32, u0_ref.shape, 0)

    def sconv(u_ref, w_ref, b_ref):
        u = u_ref[...].astype(F32)
        um = jnp.where(row == 0, 0.0, pltpu.roll(u, 1, 0))
        up = jnp.where(row == seq - 1, 0.0, pltpu.roll(u, seq - 1, 0))
        w = w_ref[...]
        return b_ref[...] + um * w[0:1] + u * w[1:2] + up * w[2:3]

    x0_ref[...] = sconv(u0_ref, w0_ref, b0_ref).astype(BF16)
    prod = sconv(u1_ref, w1_ref, b1_ref) * sconv(u2_ref, w2_ref, b2_ref)
    wt_ref[...] = prod.T.astype(BF16)


def _hy_prep(z3, conv_w, conv_b):
    b, seq, _ = z3.shape
    tc = LANES if seq > LONG_SEQ else 2 * LANES
    nc = HY_CH // tc
    uspec = lambda g: pl.BlockSpec((None, seq, tc), lambda i, c: (i, 0, g * nc + c))
    wspec = lambda g: pl.BlockSpec((3, tc), lambda i, c: (0, g * nc + c))
    bspec = lambda g: pl.BlockSpec((1, tc), lambda i, c: (0, g * nc + c))
    return pl.pallas_call(
        _hyprep_kernel,
        out_shape=(jax.ShapeDtypeStruct((b, seq, HY_CH), BF16),
                   jax.ShapeDtypeStruct((b, HY_CH, seq), BF16)),
        grid=(b, nc),
        in_specs=[uspec(0), uspec(1), uspec(2), wspec(0), wspec(1), wspec(2), bspec(0), bspec(1), bspec(2)],
        out_specs=(pl.BlockSpec((None, seq, tc), lambda i, c: (i, 0, c)),
                   pl.BlockSpec((None, tc, seq), lambda i, c: (i, c, 0))),
        compiler_params=_cparams(("parallel", "parallel"), VMEM_MEDIUM),
        name="hy_prep",
    )(z3, z3, z3, conv_w, conv_w, conv_w, conv_b, conv_b, conv_b)


def _conv_kernel(wt_ref, ra_ref, rb_ref, u1_ref, v2_ref, kl_ref, d_ref, o_ref, acc_ref, za_ref, zb_ref, *, tf):
    t = pl.program_id(2)
    n_f = pl.num_programs(2) - 1

    def work(z_out, z_in, first=False):
        if z_in is not None:
            acc_ref[...] += jnp.dot(z_in[...], rb_ref[...], preferred_element_type=F32)
        if z_out is not None:
            ap = _dot_t(wt_ref[...], ra_ref[...])
            a = ap[:, :tf]
            p = ap[:, tf:]
            u1 = u1_ref[...]
            v2 = v2_ref[...]
            z2 = p * u1 + a * v2
            if first:
                col = lax.broadcasted_iota(jnp.int32, z2.shape, 1)
                z2 = jnp.where(col == 0, p * kl_ref[:, 0:1], z2)
            z_out[...] = jnp.concatenate([a * u1 - p * v2, z2], axis=1).astype(BF16)

    @pl.when(t == 0)
    def _():
        acc_ref[...] = jnp.zeros_like(acc_ref)
        work(za_ref, None, first=True)

    steady = jnp.logical_and(t > 0, t < n_f)

    @pl.when(jnp.logical_and(steady, t % 2 == 1))
    def _():
        work(zb_ref, za_ref)

    @pl.when(jnp.logical_and(steady, t % 2 == 0))
    def _():
        work(za_ref, zb_ref)

    def finish(z_in):
        work(None, z_in)
        y = acc_ref[...] + d_ref[:, 0:1] * wt_ref[...].astype(F32)
        o_ref[...] = y.T.astype(BF16)

    @pl.when(jnp.logical_and(t == n_f, t % 2 == 1))
    def _():
        finish(za_ref)

    @pl.when(jnp.logical_and(t == n_f, t % 2 == 0))
    def _():
        finish(zb_ref)


def _long_conv(wt, rmat, u1, v2, kl, dcol, tf):
    b, _, seq = wt.shape
    tc = CONV_CHANNELS
    n_f = seq // tf
    return pl.pallas_call(
        functools.partial(_conv_kernel, tf=tf),
        out_shape=jax.ShapeDtypeStruct((b, seq, HY_CH), BF16),
        grid=(b, HY_CH // tc, n_f + 1),
        in_specs=[
            pl.BlockSpec((None, tc, seq), lambda i, c, t: (i, c, 0)),
            pl.BlockSpec((2 * tf, seq), lambda i, c, t: (jnp.minimum(t, n_f - 1), 0)),
            pl.BlockSpec((2 * tf, seq), lambda i, c, t: (jnp.maximum(t - 1, 0), 0)),
            pl.BlockSpec((tc, tf), lambda i, c, t: (c, jnp.minimum(t, n_f - 1))),
            pl.BlockSpec((tc, tf), lambda i, c, t: (c, jnp.minimum(t, n_f - 1))),
            pl.BlockSpec((tc, LANES), lambda i, c, t: (c, 0)),
            pl.BlockSpec((tc, LANES), lambda i, c, t: (c, 0)),
        ],
        out_specs=pl.BlockSpec((None, seq, tc), lambda i, c, t: (i, 0, c)),
        scratch_shapes=[pltpu.VMEM((tc, seq), F32), pltpu.VMEM((tc, 2 * tf), BF16), pltpu.VMEM((tc, 2 * tf), BF16)],
        compiler_params=_cparams(("parallel", "parallel", "arbitrary"), VMEM_LARGE),
        name="long_conv",
    )(wt, rmat, rmat, u1, v2, kl, dcol)


def _dft_gen_kernel(ch_ref, sh_ref, cl_ref, sl_ref, r_ref, *, tf):
    ch, sh = ch_ref[...], sh_ref[...]
    cl, sl = cl_ref[...], sl_ref[...]
    r_ref[:tf, :] = (ch * cl - sh * sl).astype(BF16)
    sin = sh * cl + ch * sl
    row = lax.broadcasted_iota(jnp.int32, sin.shape, 0)
    lane = lax.broadcasted_iota(jnp.int32, sin.shape, 1)
    alt = (1 - 2 * (lane % 2)).astype(F32)
    nyq_row = jnp.logical_and(row == 0, pl.program_id(0) == 0)
    r_ref[tf:, :] = jnp.where(nyq_row, alt, sin).astype(BF16)


def _dft_tables(seq, tf):
    idx = jnp.arange(seq, dtype=jnp.int32)
    nf = seq // tf

    def trig(freqs):
        ang = ((freqs[:, None] * idx[None, :]) % (2 * seq)).astype(F32) * (math.pi / seq)
        return jnp.cos(ang), jnp.sin(ang)

    ch, sh = trig(jnp.arange(nf, dtype=jnp.int32) * tf)
    cl, sl = trig(jnp.arange(tf, dtype=jnp.int32))
    base_spec = pl.BlockSpec((None, 1, seq), lambda f: (f, 0, 0))
    return pl.pallas_call(
        functools.partial(_dft_gen_kernel, tf=tf),
        out_shape=jax.ShapeDtypeStruct((2 * seq, seq), BF16),
        grid=(nf,),
        in_specs=[base_spec, base_spec, _const_spec((tf, seq)), _const_spec((tf, seq))],
        out_specs=pl.BlockSpec((2 * tf, seq), lambda f: (f, 0)),
        compiler_params=_cparams(("parallel",), VMEM_SMALL),
        name="dft_tables",
    )(ch[:, None, :], sh[:, None, :], cl, sl)


KV_CHUNK = 256
ATT_Q_TILE = 512


def _attn_kernel(par_ref, q_ref, k_ref, v_ref, tab_ref, o_ref, krot_ref, vt_ref,
                 sa_buf, sb_buf, ma_buf, mb_buf,
                 *, tq, lam_init):
    t = pl.program_id(2)
    n_q = pl.num_programs(2) - 1
    seq = k_ref.shape[0]
    ck = min(KV_CHUNK, seq)
    n_chunks = seq // ck

    def rope(x, rows):
        c, sa, sb = (tab_ref[rows, i * HEAD_W:(i + 1) * HEAD_W] for i in range(3))
        return x * c + pltpu.roll(x, LANES - ROPE_DIM // 2, 1) * sa + pltpu.roll(x, ROPE_DIM // 2, 1) * sb

    buf_a = (sa_buf, ma_buf)
    buf_b = (sb_buf, mb_buf)

    def tile_work(score, finish):
        if score is not None:
            rows = pl.ds(pl.multiple_of(t * tq, tq), tq)
            q = rope(q_ref[rows, :].astype(F32), rows)
            q = q * (HEAD_DIM ** -0.5 * LOG2E)
            lane = lax.broadcasted_iota(jnp.int32, q.shape, 1)
            qq = jnp.concatenate([jnp.where(lane < HEAD_DIM, q, 0.0), jnp.where(lane >= HEAD_DIM, q, 0.0)], axis=0)
            qq = qq.astype(BF16)
            m8 = jnp.full((8, 2 * tq), -jnp.inf, F32)
        if finish is not None:
            m = jnp.max(finish[1][...], axis=0, keepdims=True)
            l8 = jnp.zeros((8, 2 * tq), F32)
            ot = jnp.zeros((HEAD_W, 2 * tq), F32)

        for c in range(n_chunks):
            kv = slice(c * ck, (c + 1) * ck)
            if finish is not None:
                p_c = jnp.exp2(finish[0][kv, :] - m)
                l8 = l8 + jnp.sum(p_c.reshape(ck // 8, 8, 2 * tq), axis=0)
                ot = ot + jnp.dot(vt_ref[:, kv], p_c.astype(BF16), preferred_element_type=F32)
            if score is not None:
                s_c = _dot_t(krot_ref[kv, :], qq)
                score[0][kv, :] = s_c
                m8 = jnp.maximum(m8, jnp.max(s_c.reshape(ck // 8, 8, 2 * tq), axis=0))
        if score is not None:
            score[1][...] = m8
        if finish is not None:
            l = jnp.sum(l8, axis=0, keepdims=True)
            lp = par_ref[0:4, :]
            lam = (jnp.exp(jnp.sum(lp[0:1] * lp[1:2], axis=-1, keepdims=True))
                   - jnp.exp(jnp.sum(lp[2:3] * lp[3:4], axis=-1, keepdims=True)) + lam_init)
            r = 1.0 / l
            o = (ot[:, :tq] * r[:, :tq] - ot[:, tq:] * (lam * r[:, tq:])).T
            o_ref[...] = (_rms(o, par_ref[4:5, :]) * (1.0 - lam_init)).astype(BF16)

    @pl.when(t == 0)
    def _():
        krot_ref[...] = rope(k_ref[...].astype(F32), slice(None)).astype(BF16)
        vt_ref[...] = v_ref[...].astype(F32).T.astype(BF16)
        tile_work(buf_a, None)

    steady = jnp.logical_and(t > 0, t < n_q)

    @pl.when(jnp.logical_and(steady, t % 2 == 1))
    def _():
        tile_work(buf_b, buf_a)

    @pl.when(jnp.logical_and(steady, t % 2 == 0))
    def _():
        tile_work(buf_a, buf_b)

    @pl.when(jnp.logical_and(t == n_q, t % 2 == 1))
    def _():
        tile_work(None, buf_a)

    @pl.when(jnp.logical_and(t == n_q, t % 2 == 0))
    def _():
        tile_work(None, buf_b)


def _attention(z3, lam_params, g_subln, lam_init):
    b, seq, _ = z3.shape
    tq = min(ATT_Q_TILE, seq)
    n_q = seq // tq
    inv = ROPE_THETA ** (-jnp.arange(0, ROPE_DIM, 2, dtype=F32) / ROPE_DIM)
    ang = jnp.arange(seq, dtype=F32)[:, None] * inv[None, :]
    c8, s8 = jnp.cos(ang), jnp.sin(ang)
    half = ROPE_DIM // 2
    rest = HEAD_DIM - ROPE_DIM
    one, zero = jnp.ones((seq, rest), F32), jnp.zeros((seq, rest), F32)
    z8 = jnp.zeros((seq, half), F32)
    cos_t = jnp.tile(jnp.concatenate([c8, c8, one], axis=1), (1, 2))
    sa_t = jnp.tile(jnp.concatenate([-s8, z8, zero], axis=1), (1, 2))
    sb_t = jnp.tile(jnp.concatenate([z8, s8, zero], axis=1), (1, 2))
    tab = jnp.concatenate([cos_t, sa_t, sb_t], axis=1)
    par = jnp.concatenate([jnp.pad(lam_params, ((0, 0), (0, HEAD_W - HEAD_DIM))), g_subln,
                           jnp.zeros((3, HEAD_W), F32)], axis=0)
    qb, kb, vb = (HY_COLS // HEAD_W, (HY_COLS + ATT_WIDTH) // HEAD_W, (HY_COLS + 2 * ATT_WIDTH) // HEAD_W)
    head_cols = lambda blk: pl.BlockSpec((None, seq, HEAD_W), lambda i, h, t: (i, 0, blk + h))
    return pl.pallas_call(
        functools.partial(_attn_kernel, tq=tq, lam_init=lam_init),
        out_shape=jax.ShapeDtypeStruct((b, seq, ATT_WIDTH), BF16),
        grid=(b, N_HEADS, n_q + 1),
        in_specs=[_const_spec((8, HEAD_W)), head_cols(qb), head_cols(kb), head_cols(vb),
                  _const_spec((seq, 3 * HEAD_W))],
        out_specs=pl.BlockSpec((None, tq, HEAD_W), lambda i, h, t: (i, jnp.maximum(t - 1, 0), h)),
        scratch_shapes=[pltpu.VMEM((seq, HEAD_W), BF16), pltpu.VMEM((HEAD_W, seq), BF16),
                        pltpu.VMEM((seq, 2 * tq), F32), pltpu.VMEM((seq, 2 * tq), F32),
                        pltpu.VMEM((8, 2 * tq), F32), pltpu.VMEM((8, 2 * tq), F32)],
        compiler_params=_cparams(("parallel", "parallel", "arbitrary"), VMEM_LARGE),
        name="attention",
    )(par, z3, z3, z3, tab)


def _merge_kernel(yc_ref, x0_ref, ya_ref, g0_ref, g1_ref, x_ref, why_ref, wat_ref, wo_ref, gp_ref, gn_ref,
                  o_ref, hn_ref):
    def body(rows):
        yh = (x0_ref[rows, :].astype(F32) * yc_ref[rows, :].astype(F32)).astype(BF16)
        a = jnp.dot(yh, why_ref[...], preferred_element_type=F32)
        b = jnp.dot(ya_ref[rows, :], wat_ref[...], preferred_element_type=F32)
        m = g0_ref[rows, :].astype(F32) * a + g1_ref[rows, :].astype(F32) * b
        r = jnp.dot(m.astype(BF16), wo_ref[...], preferred_element_type=F32)
        x1 = x_ref[rows, :] + _rms(r, gp_ref[...])
        o_ref[rows, :] = x1
        hn_ref[rows, :] = _rms(x1, gn_ref[...]).astype(BF16)
    _for_row_chunks(x_ref.shape[0], body, chunk=2 * ROW_CHUNK)


def _merge(yconv, x0c, yatt, z, x2d, w_hy, w_att, w_out, g_post, g_next):
    m, d = x2d.shape
    tm = min(MERGE_ROWS, m)
    row = lambda width, blk=0: pl.BlockSpec((tm, width), lambda i: (i, blk))
    gate_blk = GATE_COL0 // d
    return pl.pallas_call(
        _merge_kernel,
        out_shape=(jax.ShapeDtypeStruct((m, d), F32), jax.ShapeDtypeStruct((m, d), BF16)),
        grid=(m // tm,),
        in_specs=[row(HY_CH), row(HY_CH), row(ATT_WIDTH), row(d, gate_blk), row(d, gate_blk + 1), row(d),
                  _const_spec((HY_CH, d)), _const_spec((ATT_WIDTH, d)), _const_spec((d, d)), _const_spec((1, d)),
                  _const_spec((1, d))],
        out_specs=(row(d), row(d)),
        compiler_params=_cparams(("parallel",), VMEM_LARGE),
        name="merge",
    )(yconv, x0c, yatt, z, z, x2d, w_hy, w_att, w_out, g_post, g_next)


def _ffn_kernel(x_ref, h_ref, p_ref, wg_ref, wu_ref, wo_ref, gpost_ref, gple_ref, wpg_ref, wpi_ref,
                o_ref, acc_ref):
    j = pl.program_id(1)

    @pl.when(j == 0)
    def _():
        acc_ref[...] = jnp.zeros_like(acc_ref)

    h = h_ref[...]
    gate = jnp.dot(h, wg_ref[...], preferred_element_type=F32)
    up = jnp.dot(h, wu_ref[...], preferred_element_type=F32)
    act = (gate * _sigmoid(gate) * up).astype(BF16)
    acc_ref[...] += jnp.dot(act, wo_ref[...], preferred_element_type=F32)

    @pl.when(j == pl.num_programs(1) - 1)
    def _():
        def body(rows):
            x2 = x_ref[rows, :] + _rms(acc_ref[rows, :], gpost_ref[...])
            e = jnp.dot(_rms(x2, gple_ref[...]).astype(BF16), wpg_ref[...], preferred_element_type=F32)
            pe = jnp.dot(p_ref[rows, :].astype(BF16), wpi_ref[...], preferred_element_type=F32)
            o_ref[rows, :] = x2 + pe * _sigmoid(e)
        _for_row_chunks(x_ref.shape[0], body, chunk=2 * ROW_CHUNK)


def _ffn_ple(x1, h1, p2d, w_in, w_out, g_post, g_ple, w_pg, w_pi):
    m, d = x1.shape
    tm = min(FFN_ROWS, m)
    tf = FFN_TILE
    nff = D_FF // tf
    return pl.pallas_call(
        _ffn_kernel,
        out_shape=jax.ShapeDtypeStruct((m, d), F32),
        grid=(m // tm, nff),
        in_specs=[
            pl.BlockSpec((tm, d), lambda i, j: (i, 0)),
            pl.BlockSpec((tm, d), lambda i, j: (i, 0)),
            pl.BlockSpec((tm, PLE_DIM), lambda i, j: (i, 0)),
            pl.BlockSpec((d, tf), lambda i, j: (0, j)),
            pl.BlockSpec((d, tf), lambda i, j: (0, nff + j)),
            pl.BlockSpec((tf, d), lambda i, j: (j, 0)),
            _const_spec((1, d)), _const_spec((1, d)),
            _const_spec((d, d)), _const_spec((PLE_DIM, d)),
        ],
        out_specs=pl.BlockSpec((tm, d), lambda i, j: (i, 0)),
        scratch_shapes=[pltpu.VMEM((tm, d), F32)],
        compiler_params=_cparams(("parallel", "arbitrary"), VMEM_LARGE),
        name="ffn_ple",
    )(x1, h1, p2d, w_in, w_in, w_out, g_post, g_ple, w_pg, w_pi)


def _layer(x, p, lam_init, wts):
    b, seq, d = x.shape
    x2d = x.reshape(b * seq, d)
    z = _in_proj(x2d, wts["g_mix_pre"], wts["w_in"], wts["b_gate"])
    z3 = z.reshape(b, seq, IN_COLS)

    tf = min(FREQ_TILE if seq > LONG_SEQ else 2 * FREQ_TILE, seq)
    rmat = _dft_tables(seq, tf)
    ht, sums = _filt_mlp(seq, wts["filt_w1"], wts["filt_b1"], wts["filt_w2"], wts["filt_b2"],
                         wts["filt_freq"], wts["filt_w3"])
    u1, v2, kl = _filt_spec(ht, sums, rmat, tf)
    x0c, wt = _hy_prep(z3, wts["conv_w"], wts["conv_b"])
    yconv = _long_conv(wt, rmat, u1, v2, kl, wts["hyena_d"], tf)

    yatt = _attention(z3, wts["lam"], wts["g_subln"], lam_init)

    x1, h1 = _merge(yconv.reshape(b * seq, HY_CH), x0c.reshape(b * seq, HY_CH), yatt.reshape(b * seq, ATT_WIDTH),
                    z, x2d, wts["w_hy_out"], wts["w_att_out"], wts["w_out"], wts["g_mix_post"], wts["g_ffn_pre"])
    y = _ffn_ple(x1, h1, p.reshape(b * seq, PLE_DIM), wts["w_ffn_in"], wts["w_ffn_out"],
                 wts["g_ffn_post"], wts["g_ple"], wts["w_ple_gate"], wts["w_ple_in"])
    return y.reshape(b, seq, d)


def kernel(x_prompt, x_sample, p_prompt, p_sample, g_mix_pre, g_mix_post, g_ffn_pre, g_ffn_post, g_ple, w_in, b_gate, conv_w, conv_b, filt_w1, filt_b1, filt_w2, filt_b2, filt_freq, filt_w3, hyena_d, lam_q1, lam_k1, lam_q2, lam_k2, g_subln, w_hy_out, w_att_out, w_out, w_ffn_in, w_ffn_out, w_ple_in, w_ple_gate):
    depth = w_in.shape[0]
    xs = [x_prompt, x_sample]
    ps = [p_prompt, p_sample]
    for i in range(depth):
        lam_init = 0.8 - 0.6 * math.exp(-0.3 * i)
        wts = dict(
            g_mix_pre=g_mix_pre[i][None, :], g_mix_post=g_mix_post[i][None, :],
            g_ffn_pre=g_ffn_pre[i][None, :], g_ffn_post=g_ffn_post[i][None, :], g_ple=g_ple[i][None, :],
            w_in=w_in[i].astype(BF16), b_gate=b_gate[i].reshape(1, 2 * D_MODEL),
            conv_w=conv_w[i], conv_b=conv_b[i][None, :],
            filt_w1=filt_w1[i], filt_b1=filt_b1[i], filt_w2=filt_w2[i], filt_b2=filt_b2[i],
            filt_freq=filt_freq[i], filt_w3=filt_w3[i],
            hyena_d=jnp.broadcast_to(hyena_d[i][:, None], (HY_CH, LANES)),
            lam=jnp.stack([lam_q1[i], lam_k1[i], lam_q2[i], lam_k2[i]]),
            g_subln=g_subln[i][None, :],
            w_hy_out=w_hy_out[i].astype(BF16), w_att_out=w_att_out[i].astype(BF16),
            w_out=w_out[i].astype(BF16), w_ffn_in=w_ffn_in[i].astype(BF16),
            w_ffn_out=w_ffn_out[i].astype(BF16), w_ple_in=w_ple_in[i].astype(BF16),
            w_ple_gate=w_ple_gate[i].astype(BF16),
        )
        xs = [_layer(x, p[i], lam_init, wts) for x, p in zip(xs, ps)]
    return (xs[0], xs[1])
```

```python
import functools
import math

import jax
import jax.numpy as jnp
from jax import lax
from jax.experimental import pallas as pl
from jax.experimental.pallas import tpu as pltpu

F32 = jnp.float32
BF16 = jnp.bfloat16

D_MODEL = 2048
PLE_DIM = 256
NORM_EPS = 1e-6
HY_CH = D_MODEL // 2
FILT_BANDS = 16
FILT_EMB = 1 + 2 * FILT_BANDS
FILT_ORDER = 64
FILT_MAX_DECAY = math.log(1e-2) / 0.3
FILT_MIN_DECAY = math.log(1e-2) / 1.5
ATT_WIDTH = D_MODEL // 2
N_HEADS = 8
HEAD_DIM = ATT_WIDTH // (2 * N_HEADS)
HEAD_W = 2 * HEAD_DIM
ROPE_DIM = HEAD_DIM // 4
ROPE_THETA = 500000.0
D_FF = ((8 * D_MODEL + 3 * 256 - 1) // (3 * 256)) * 256
HY_COLS = 3 * HY_CH
GATE_COL0 = HY_COLS + 3 * ATT_WIDTH
IN_COLS = GATE_COL0 + 2 * D_MODEL
LANES = 128
LOG2E = math.log2(math.e)
MIB = 2 ** 20

LONG_SEQ = 2048
FREQ_TILE = 256
IN_PROJ_ROWS, IN_PROJ_COLS = 1024, 2048
MERGE_ROWS = 512
FFN_ROWS, FFN_TILE = 512, 512
CONV_CHANNELS = 512
FILT_CHANNELS = 256
FILT_ROWS = 512
VMEM_SMALL, VMEM_MEDIUM, VMEM_LARGE, VMEM_MAX = 32, 48, 56, 58


def _cparams(sem, vmem_mib):
    return pltpu.CompilerParams(dimension_semantics=sem, vmem_limit_bytes=vmem_mib * MIB)


def _const_spec(shape):
    nd = len(shape)
    return pl.BlockSpec(shape, lambda *_: (0,) * nd, pipeline_mode=pl.Buffered(1))


def _rms(x, g):
    ms = jnp.mean(x * x, axis=-1, keepdims=True)
    return x * lax.rsqrt(ms + NORM_EPS) * g


def _sigmoid(x):
    return 1.0 / (1.0 + jnp.exp(-x))


ROW_CHUNK = 128


def _for_row_chunks(n_rows, body, chunk=ROW_CHUNK):
    chunk = min(chunk, n_rows)

    def step(c, carry):
        body(pl.ds(pl.multiple_of(c * chunk, chunk), chunk))
        return carry
    lax.fori_loop(0, n_rows // chunk, step, 0)


def _rms_rows_to(h_ref, x_ref, g_ref):
    def body(rows):
        h_ref[rows, :] = _rms(x_ref[rows, :], g_ref[...]).astype(h_ref.dtype)
    _for_row_chunks(x_ref.shape[0], body)


def _inproj_kernel(x_ref, g_ref, w_ref, b_ref, o_ref, h_ref, *, n_plain):
    j = pl.program_id(1)

    @pl.when(j == 0)
    def _():
        _rms_rows_to(h_ref, x_ref, g_ref)

    acc = jnp.dot(h_ref[...], w_ref[...], preferred_element_type=F32)
    o_ref[...] = jnp.where(j >= n_plain, _sigmoid(acc + b_ref[...]), acc).astype(BF16)


def _in_proj(x2d, g, w_bf, b_flat):
    m, d = x2d.shape
    n = w_bf.shape[1]
    tm = min(IN_PROJ_ROWS, m)
    tn = IN_PROJ_COLS
    n_plain = GATE_COL0 // tn
    return pl.pallas_call(
        functools.partial(_inproj_kernel, n_plain=n_plain),
        out_shape=jax.ShapeDtypeStruct((m, n), BF16),
        grid=(m // tm, n // tn),
        in_specs=[
            pl.BlockSpec((tm, d), lambda i, j: (i, 0)),
            pl.BlockSpec((1, d), lambda i, j: (0, 0)),
            pl.BlockSpec((d, tn), lambda i, j: (0, j)),
            pl.BlockSpec((1, tn), lambda i, j: (0, jnp.maximum(j - n_plain, 0))),
        ],
        out_specs=pl.BlockSpec((tm, tn), lambda i, j: (i, j)),
        scratch_shapes=[pltpu.VMEM((tm, d), BF16)],
        compiler_params=_cparams(("parallel", "arbitrary"), VMEM_MAX),
        name="in_proj",
    )(x2d, g, w_bf, b_flat)


def _filt_mlp_kernel(emb_ref, w1_ref, b1_ref, w2_ref, b2_ref, fr_ref, w3_ref, dl_ref, ht_ref, sum_ref):
    i = pl.program_id(0)
    hp = lax.Precision.HIGHEST
    emb = emb_ref[...]
    fr = fr_ref[...]
    h1 = jnp.sin(fr * (jnp.dot(emb, w1_ref[...], precision=hp, preferred_element_type=F32) + b1_ref[...]))
    h2 = jnp.sin(fr * (jnp.dot(h1, w2_ref[...], precision=hp, preferred_element_type=F32) + b2_ref[...]))
    h = jnp.dot(h2, w3_ref[...], precision=hp, preferred_element_type=F32)
    dec = jnp.exp(-emb[:, 0:1] * dl_ref[...])
    ht = jnp.concatenate([h[:, :HY_CH] * dec, h[:, HY_CH:] * dec], axis=1).T
    ht_ref[...] = ht
    ab = jnp.abs(ht)
    part = ab[:, 0:LANES]
    for c in range(1, ab.shape[1] // LANES):
        part = part + ab[:, c * LANES:(c + 1) * LANES]

    @pl.when(i == 0)
    def _():
        sum_ref[...] = part

    @pl.when(i > 0)
    def _():
        sum_ref[...] += part


def _filt_mlp(seq, w1, b1, w2, b2, freq, w3):
    t = jnp.linspace(0.0, 1.0, seq, dtype=F32)[:, None]
    wpos = 2.0 * math.pi * jnp.arange(seq, dtype=F32) / seq
    bands = jnp.linspace(1e-4, FILT_BANDS - 1, FILT_BANDS, dtype=F32)
    ang = wpos[:, None] * bands[None, :]
    emb = jnp.concatenate([t, jnp.cos(ang), -jnp.sin(ang)], axis=-1)
    emb = jnp.pad(emb, ((0, 0), (0, LANES - FILT_EMB)))
    pad_o = LANES - FILT_ORDER
    w1p = jnp.pad(w1, ((0, LANES - FILT_EMB), (0, pad_o)))
    w2p = jnp.pad(w2, ((0, pad_o), (0, pad_o)))
    w3p = jnp.pad(w3, ((0, pad_o), (0, 0)))
    b1p = jnp.pad(b1[None, :], ((0, 0), (0, pad_o)))
    b2p = jnp.pad(b2[None, :], ((0, 0), (0, pad_o)))
    frp = jnp.pad(freq[None, :], ((0, 0), (0, pad_o)))
    deltas = jnp.abs(jnp.linspace(FILT_MIN_DECAY, FILT_MAX_DECAY, HY_CH, dtype=F32))[None, :]
    tl = min(FILT_ROWS, seq)
    return pl.pallas_call(
        _filt_mlp_kernel,
        out_shape=(jax.ShapeDtypeStruct((2 * HY_CH, seq), F32),
                   jax.ShapeDtypeStruct((2 * HY_CH, LANES), F32)),
        grid=(seq // tl,),
        in_specs=[
            pl.BlockSpec((tl, LANES), lambda i: (i, 0)),
            _const_spec((LANES, LANES)), _const_spec((1, LANES)),
            _const_spec((LANES, LANES)), _const_spec((1, LANES)),
            _const_spec((1, LANES)), _const_spec((LANES, 2 * HY_CH)),
            _const_spec((1, HY_CH)),
        ],
        out_specs=(pl.BlockSpec((2 * HY_CH, tl), lambda i: (0, i)),
                   pl.BlockSpec((2 * HY_CH, LANES), lambda i: (0, 0))),
        compiler_params=_cparams(("arbitrary",), VMEM_MEDIUM),
        name="filt_mlp",
    )(emb, w1p, b1p, w2p, b2p, frp, w3p, deltas)


def _dot_t(a, b):
    return lax.dot_general(a, b, (((1,), (1,)), ((), ())), preferred_element_type=F32)


def _filt_spec_kernel(hf_ref, hb_ref, sf_ref, sb_ref, r_ref, u1_ref, v2_ref, kl_ref, s1_ref, s2_ref, *, tf, seq):
    f = pl.program_id(1)

    @pl.when(f == 0)
    def _():
        norm = jnp.sum(sf_ref[...] + sb_ref[...], axis=1, keepdims=True) + NORM_EPS
        hf = hf_ref[...] / norm
        lane = lax.broadcasted_iota(jnp.int32, hf.shape, 1)
        hb = jnp.where(lane == 0, 0.0, hb_ref[...] / norm)
        s1_ref[...] = (hf + hb).astype(BF16)
        s2_ref[...] = (hf - hb).astype(BF16)

    s1 = s1_ref[...]
    s2 = s2_ref[...]
    c = _dot_t(s1, r_ref[:tf, :]) * (1.0 / seq)
    col = lax.broadcasted_iota(jnp.int32, c.shape, 1) + f * tf
    u1_ref[...] = jnp.where(col == 0, 0.5 * c, c)
    v2_ref[...] = jnp.where(col == 0, 0.0, _dot_t(s2, r_ref[tf:, :]) * (1.0 / seq))

    @pl.when(f == 0)
    def _():
        nyq = _dot_t(s1, r_ref[tf:, :])[:, 0:1] * (0.5 / seq)
        kl_ref[...] = jnp.broadcast_to(nyq, kl_ref.shape)


def _filt_spec(ht, sums, rmat, tf):
    seq = ht.shape[1]
    tc = FILT_CHANNELS
    nc = HY_CH // tc
    return pl.pallas_call(
        functools.partial(_filt_spec_kernel, tf=tf, seq=seq),
        out_shape=(jax.ShapeDtypeStruct((HY_CH, seq), F32),
                   jax.ShapeDtypeStruct((HY_CH, seq), F32),
                   jax.ShapeDtypeStruct((HY_CH, LANES), F32)),
        grid=(nc, seq // tf),
        in_specs=[
            pl.BlockSpec((tc, seq), lambda c, f: (c, 0)),
            pl.BlockSpec((tc, seq), lambda c, f: (nc + c, 0)),
            pl.BlockSpec((tc, LANES), lambda c, f: (c, 0)),
            pl.BlockSpec((tc, LANES), lambda c, f: (nc + c, 0)),
            pl.BlockSpec((2 * tf, seq), lambda c, f: (f, 0)),
        ],
        out_specs=(pl.BlockSpec((tc, tf), lambda c, f: (c, f)),
                   pl.BlockSpec((tc, tf), lambda c, f: (c, f)),
                   pl.BlockSpec((tc, LANES), lambda c, f: (c, 0))),
        scratch_shapes=[pltpu.VMEM((tc, seq), BF16), pltpu.VMEM((tc, seq), BF16)],
        compiler_params=_cparams(("parallel", "arbitrary"), VMEM_MEDIUM),
        name="filt_spec",
    )(ht, ht, sums, sums, rmat)


def _hyprep_kernel(u0_ref, u1_ref, u2_ref, w0_ref, w1_ref, w2_ref, b0_ref, b1_ref, b2_ref, x0_ref, wt_ref):
    seq = u0_ref.shape[0]
    row = lax.broadcasted_iota(jnp.int32, u0_ref.shape, 0)

    def sconv(u_ref, w_ref, b_ref):
        u = u_ref[...].astype(F32)
        um = jnp.where(row == 0, 0.0, pltpu.roll(u, 1, 0))
        up = jnp.where(row == seq - 1, 0.0, pltpu.roll(u, seq - 1, 0))
        w = w_ref[...]
        return b_ref[...] + um * w[0:1] + u * w[1:2] + up * w[2:3]

    x0_ref[...] = sconv(u0_ref, w0_ref, b0_ref).astype(BF16)
    prod = sconv(u1_ref, w1_ref, b1_ref) * sconv(u2_ref, w2_ref, b2_ref)
    wt_ref[...] = prod.T.astype(BF16)


def _hy_prep(z3, conv_w, conv_b):
    b, seq, _ = z3.shape
    tc = LANES if seq > LONG_SEQ else 2 * LANES
    nc = HY_CH // tc
    uspec = lambda g: pl.BlockSpec((None, seq, tc), lambda i, c: (i, 0, g * nc + c))
    wspec = lambda g: pl.BlockSpec((3, tc), lambda i, c: (0, g * nc + c))
    bspec = lambda g: pl.BlockSpec((1, tc), lambda i, c: (0, g * nc + c))
    return pl.pallas_call(
        _hyprep_kernel,
        out_shape=(jax.ShapeDtypeStruct((b, seq, HY_CH), BF16),
                   jax.ShapeDtypeStruct((b, HY_CH, seq), BF16)),
        grid=(b, nc),
        in_specs=[uspec(0), uspec(1), uspec(2), wspec(0), wspec(1), wspec(2), bspec(0), bspec(1), bspec(2)],
        out_specs=(pl.BlockSpec((None, seq, tc), lambda i, c: (i, 0, c)),
                   pl.BlockSpec((None, tc, seq), lambda i, c: (i, c, 0))),
        compiler_params=_cparams(("parallel", "parallel"), VMEM_MEDIUM),
        name="hy_prep",
    )(z3, z3, z3, conv_w, conv_w, conv_w, conv_b, conv_b, conv_b)


def _conv_kernel(wt_ref, ra_ref, rb_ref, u1_ref, v2_ref, kl_ref, d_ref, o_ref, acc_ref, za_ref, zb_ref, *, tf):
    t = pl.program_id(2)
    n_f = pl.num_programs(2) - 1

    def work(z_out, z_in, first=False):
        if z_in is not None:
            acc_ref[...] += jnp.dot(z_in[...], rb_ref[...], preferred_element_type=F32)
        if z_out is not None:
            ap = _dot_t(wt_ref[...], ra_ref[...])
            a = ap[:, :tf]
            p = ap[:, tf:]
            u1 = u1_ref[...]
            v2 = v2_ref[...]
            z2 = p * u1 + a * v2
            if first:
                col = lax.broadcasted_iota(jnp.int32, z2.shape, 1)
                z2 = jnp.where(col == 0, p * kl_ref[:, 0:1], z2)
            z_out[...] = jnp.concatenate([a * u1 - p * v2, z2], axis=1).astype(BF16)

    @pl.when(t == 0)
    def _():
        acc_ref[...] = jnp.zeros_like(acc_ref)
        work(za_ref, None, first=True)

    steady = jnp.logical_and(t > 0, t < n_f)

    @pl.when(jnp.logical_and(steady, t % 2 == 1))
    def _():
        work(zb_ref, za_ref)

    @pl.when(jnp.logical_and(steady, t % 2 == 0))
    def _():
        work(za_ref, zb_ref)

    def finish(z_in):
        work(None, z_in)
        y = acc_ref[...] + d_ref[:, 0:1] * wt_ref[...].astype(F32)
        o_ref[...] = y.T.astype(BF16)

    @pl.when(jnp.logical_and(t == n_f, t % 2 == 1))
    def _():
        finish(za_ref)

    @pl.when(jnp.logical_and(t == n_f, t % 2 == 0))
    def _():
        finish(zb_ref)


def _long_conv(wt, rmat, u1, v2, kl, dcol, tf):
    b, _, seq = wt.shape
    tc = CONV_CHANNELS
    n_f = seq // tf
    return pl.pallas_call(
        functools.partial(_conv_kernel, tf=tf),
        out_shape=jax.ShapeDtypeStruct((b, seq, HY_CH), BF16),
        grid=(b, HY_CH // tc, n_f + 1),
        in_specs=[
            pl.BlockSpec((None, tc, seq), lambda i, c, t: (i, c, 0)),
            pl.BlockSpec((2 * tf, seq), lambda i, c, t: (jnp.minimum(t, n_f - 1), 0)),
            pl.BlockSpec((2 * tf, seq), lambda i, c, t: (jnp.maximum(t - 1, 0), 0)),
            pl.BlockSpec((tc, tf), lambda i, c, t: (c, jnp.minimum(t, n_f - 1))),
            pl.BlockSpec((tc, tf), lambda i, c, t: (c, jnp.minimum(t, n_f - 1))),
            pl.BlockSpec((tc, LANES), lambda i, c, t: (c, 0)),
            pl.BlockSpec((tc, LANES), lambda i, c, t: (c, 0)),
        ],
        out_specs=pl.BlockSpec((None, seq, tc), lambda i, c, t: (i, 0, c)),
        scratch_shapes=[pltpu.VMEM((tc, seq), F32), pltpu.VMEM((tc, 2 * tf), BF16), pltpu.VMEM((tc, 2 * tf), BF16)],
        compiler_params=_cparams(("parallel", "parallel", "arbitrary"), VMEM_LARGE),
        name="long_conv",
    )(wt, rmat, rmat, u1, v2, kl, dcol)


def _dft_gen_kernel(ch_ref, sh_ref, cl_ref, sl_ref, r_ref, *, tf):
    ch, sh = ch_ref[...], sh_ref[...]
    cl, sl = cl_ref[...], sl_ref[...]
    r_ref[:tf, :] = (ch * cl - sh * sl).astype(BF16)
    sin = sh * cl + ch * sl
    row = lax.broadcasted_iota(jnp.int32, sin.shape, 0)
    lane = lax.broadcasted_iota(jnp.int32, sin.shape, 1)
    alt = (1 - 2 * (lane % 2)).astype(F32)
    nyq_row = jnp.logical_and(row == 0, pl.program_id(0) == 0)
    r_ref[tf:, :] = jnp.where(nyq_row, alt, sin).astype(BF16)


def _dft_tables(seq, tf):
    idx = jnp.arange(seq, dtype=jnp.int32)
    nf = seq // tf

    def trig(freqs):
        ang = ((freqs[:, None] * idx[None, :]) % (2 * seq)).astype(F32) * (math.pi / seq)
        return jnp.cos(ang), jnp.sin(ang)

    ch, sh = trig(jnp.arange(nf, dtype=jnp.int32) * tf)
    cl, sl = trig(jnp.arange(tf, dtype=jnp.int32))
    base_spec = pl.BlockSpec((None, 1, seq), lambda f: (f, 0, 0))
    return pl.pallas_call(
        functools.partial(_dft_gen_kernel, tf=tf),
        out_shape=jax.ShapeDtypeStruct((2 * seq, seq), BF16),
        grid=(nf,),
        in_specs=[base_spec, base_spec, _const_spec((tf, seq)), _const_spec((tf, seq))],
        out_specs=pl.BlockSpec((2 * tf, seq), lambda f: (f, 0)),
        compiler_params=_cparams(("parallel",), VMEM_SMALL),
        name="dft_tables",
    )(ch[:, None, :], sh[:, None, :], cl, sl)


KV_CHUNK = 256
ATT_Q_TILE = 512


def _attn_kernel(par_ref, q_ref, k_ref, v_ref, tab_ref, o_ref, krot_ref, vt_ref,
                 sa_buf, sb_buf, ma_buf, mb_buf,
                 *, tq, n_q, lam_init):
    g = pl.program_id(0)
    n_tiles = pl.num_programs(0) - 1
    t = g % n_q
    head_par = (g // n_q) % 2
    prev_par = (jnp.maximum(g - 1, 0) // n_q) % 2
    seq = k_ref.shape[0]
    ck = min(KV_CHUNK, seq)
    n_chunks = seq // ck

    def rope(x, rows):
        c, sa, sb = (tab_ref[rows, i * HEAD_W:(i + 1) * HEAD_W] for i in range(3))
        return x * c + pltpu.roll(x, LANES - ROPE_DIM // 2, 1) * sa + pltpu.roll(x, ROPE_DIM // 2, 1) * sb

    bufs = ((sa_buf, ma_buf), (sb_buf, mb_buf))

    def tile_work(score, finish, new_head=False):
        if new_head:
            krot_ref[...] = rope(k_ref[...].astype(F32), slice(None)).astype(BF16)
            vt_ref[head_par] = v_ref[...].astype(F32).T.astype(BF16)
        if score is not None:
            rows = pl.ds(pl.multiple_of(t * tq, tq), tq)
            q = rope(q_ref[rows, :].astype(F32), rows)
            q = q * (HEAD_DIM ** -0.5 * LOG2E)
            lane = lax.broadcasted_iota(jnp.int32, q.shape, 1)
            qq = jnp.concatenate([jnp.where(lane < HEAD_DIM, q, 0.0), jnp.where(lane >= HEAD_DIM, q, 0.0)], axis=0)
            qq = qq.astype(BF16)
            m8 = jnp.full((8, 2 * tq), -jnp.inf, F32)
        if finish is not None:
            m = jnp.max(finish[1][...], axis=0, keepdims=True)
            l8 = jnp.zeros((8, 2 * tq), F32)
            ot = jnp.zeros((HEAD_W, 2 * tq), F32)

        for c in range(n_chunks):
            kv = slice(c * ck, (c + 1) * ck)
            if finish is not None:
                p_c = jnp.exp2(finish[0][kv, :] - m)
                l8 = l8 + jnp.sum(p_c.reshape(ck // 8, 8, 2 * tq), axis=0)
                ot = ot + jnp.dot(vt_ref[prev_par, :, kv], p_c.astype(BF16), preferred_element_type=F32)
            if score is not None:
                s_c = _dot_t(krot_ref[kv, :], qq)
                score[0][kv, :] = s_c
                m8 = jnp.maximum(m8, jnp.max(s_c.reshape(ck // 8, 8, 2 * tq), axis=0))
        if score is not None:
            score[1][...] = m8
        if finish is not None:
            l = jnp.sum(l8, axis=0, keepdims=True)
            lp = par_ref[0:4, :]
            lam = (jnp.exp(jnp.sum(lp[0:1] * lp[1:2], axis=-1, keepdims=True))
                   - jnp.exp(jnp.sum(lp[2:3] * lp[3:4], axis=-1, keepdims=True)) + lam_init)
            r = 1.0 / l
            o = (ot[:, :tq] * r[:, :tq] - ot[:, tq:] * (lam * r[:, tq:])).T
            o_ref[...] = (_rms(o, par_ref[4:5, :]) * (1.0 - lam_init)).astype(BF16)

    @pl.when(g == 0)
    def _():
        tile_work(bufs[0], None, new_head=True)

    steady = jnp.logical_and(g > 0, g < n_tiles)
    for par in (0, 1):
        for opens_head in (False, True):
            if opens_head and par == 1 and n_q % 2 == 0:
                continue
            cond = jnp.logical_and(jnp.logical_and(steady, g % 2 == par), (t == 0) == opens_head)
            pl.when(cond)(functools.partial(tile_work, bufs[par], bufs[1 - par], new_head=opens_head))

    for par in (0, 1):
        pl.when(jnp.logical_and(g == n_tiles, g % 2 == par))(functools.partial(tile_work, None, bufs[1 - par]))


def _attention(z3, lam_params, g_subln, lam_init):
    b, seq, _ = z3.shape
    tq = min(ATT_Q_TILE, seq)
    n_q = seq // tq
    n_tiles = b * N_HEADS * n_q
    inv = ROPE_THETA ** (-jnp.arange(0, ROPE_DIM, 2, dtype=F32) / ROPE_DIM)
    ang = jnp.arange(seq, dtype=F32)[:, None] * inv[None, :]
    c8, s8 = jnp.cos(ang), jnp.sin(ang)
    half = ROPE_DIM // 2
    rest = HEAD_DIM - ROPE_DIM
    one, zero = jnp.ones((seq, rest), F32), jnp.zeros((seq, rest), F32)
    z8 = jnp.zeros((seq, half), F32)
    cos_t = jnp.tile(jnp.concatenate([c8, c8, one], axis=1), (1, 2))
    sa_t = jnp.tile(jnp.concatenate([-s8, z8, zero], axis=1), (1, 2))
    sb_t = jnp.tile(jnp.concatenate([z8, s8, zero], axis=1), (1, 2))
    tab = jnp.concatenate([cos_t, sa_t, sb_t], axis=1)
    par = jnp.concatenate([jnp.pad(lam_params, ((0, 0), (0, HEAD_W - HEAD_DIM))), g_subln,
                           jnp.zeros((3, HEAD_W), F32)], axis=0)
    qb, kb, vb = (HY_COLS // HEAD_W, (HY_COLS + ATT_WIDTH) // HEAD_W, (HY_COLS + 2 * ATT_WIDTH) // HEAD_W)

    def head_cols(blk):
        def index(g):
            head = jnp.minimum(g, n_tiles - 1) // n_q
            return head // N_HEADS, 0, blk + head % N_HEADS
        return pl.BlockSpec((None, seq, HEAD_W), index)

    def out_index(g):
        done = jnp.maximum(g - 1, 0)
        head = done // n_q
        return head // N_HEADS, done % n_q, head % N_HEADS

    return pl.pallas_call(
        functools.partial(_attn_kernel, tq=tq, n_q=n_q, lam_init=lam_init),
        out_shape=jax.ShapeDtypeStruct((b, seq, ATT_WIDTH), BF16),
        grid=(n_tiles + 1,),
        in_specs=[_const_spec((8, HEAD_W)), head_cols(qb), head_cols(kb), head_cols(vb),
                  _const_spec((seq, 3 * HEAD_W))],
        out_specs=pl.BlockSpec((None, tq, HEAD_W), out_index),
        scratch_shapes=[pltpu.VMEM((seq, HEAD_W), BF16), pltpu.VMEM((2, HEAD_W, seq), BF16),
                        pltpu.VMEM((seq, 2 * tq), F32), pltpu.VMEM((seq, 2 * tq), F32),
                        pltpu.VMEM((8, 2 * tq), F32), pltpu.VMEM((8, 2 * tq), F32)],
        compiler_params=_cparams(("arbitrary",), VMEM_LARGE),
        name="attention",
    )(par, z3, z3, z3, tab)


def _merge_kernel(yc_ref, x0_ref, ya_ref, g0_ref, g1_ref, x_ref, why_ref, wat_ref, wo_ref, gp_ref, o_ref):
    def body(rows):
        yh = (x0_ref[rows, :].astype(F32) * yc_ref[rows, :].astype(F32)).astype(BF16)
        a = jnp.dot(yh, why_ref[...], preferred_element_type=F32)
        b = jnp.dot(ya_ref[rows, :], wat_ref[...], preferred_element_type=F32)
        m = g0_ref[rows, :].astype(F32) * a + g1_ref[rows, :].astype(F32) * b
        r = jnp.dot(m.astype(BF16), wo_ref[...], preferred_element_type=F32)
        o_ref[rows, :] = x_ref[rows, :] + _rms(r, gp_ref[...])
    _for_row_chunks(x_ref.shape[0], body, chunk=2 * ROW_CHUNK)


def _merge(yconv, x0c, yatt, z, x2d, w_hy, w_att, w_out, g_post):
    m, d = x2d.shape
    tm = min(MERGE_ROWS, m)
    row = lambda width, blk=0: pl.BlockSpec((tm, width), lambda i: (i, blk))
    gate_blk = GATE_COL0 // d
    return pl.pallas_call(
        _merge_kernel,
        out_shape=jax.ShapeDtypeStruct((m, d), F32),
        grid=(m // tm,),
        in_specs=[row(HY_CH), row(HY_CH), row(ATT_WIDTH), row(d, gate_blk), row(d, gate_blk + 1), row(d),
                  _const_spec((HY_CH, d)), _const_spec((ATT_WIDTH, d)), _const_spec((d, d)), _const_spec((1, d))],
        out_specs=row(d),
        compiler_params=_cparams(("parallel",), VMEM_LARGE),
        name="merge",
    )(yconv, x0c, yatt, z, z, x2d, w_hy, w_att, w_out, g_post)


def _ffn_kernel(x_ref, p_ref, gpre_ref, wg_ref, wu_ref, wo_ref, gpost_ref, gple_ref, wpg_ref, wpi_ref,
                o_ref, h_ref, acc_ref):
    j = pl.program_id(1)

    @pl.when(j == 0)
    def _():
        _rms_rows_to(h_ref, x_ref, gpre_ref)
        acc_ref[...] = jnp.zeros_like(acc_ref)

    h = h_ref[...]
    gate = jnp.dot(h, wg_ref[...], preferred_element_type=F32)
    up = jnp.dot(h, wu_ref[...], preferred_element_type=F32)
    act = (gate * _sigmoid(gate) * up).astype(BF16)
    acc_ref[...] += jnp.dot(act, wo_ref[...], preferred_element_type=F32)

    @pl.when(j == pl.num_programs(1) - 1)
    def _():
        def body(rows):
            x2 = x_ref[rows, :] + _rms(acc_ref[rows, :], gpost_ref[...])
            e = jnp.dot(_rms(x2, gple_ref[...]).astype(BF16), wpg_ref[...], preferred_element_type=F32)
            pe = jnp.dot(p_ref[rows, :].astype(BF16), wpi_ref[...], preferred_element_type=F32)
            o_ref[rows, :] = x2 + pe * _sigmoid(e)
        _for_row_chunks(x_ref.shape[0], body, chunk=2 * ROW_CHUNK)


def _ffn_ple(x1, p2d, g_pre, w_in, w_out, g_post, g_ple, w_pg, w_pi):
    m, d = x1.shape
    tm = min(FFN_ROWS, m)
    tf = FFN_TILE
    nff = D_FF // tf
    return pl.pallas_call(
        _ffn_kernel,
        out_shape=jax.ShapeDtypeStruct((m, d), F32),
        grid=(m // tm, nff),
        in_specs=[
            pl.BlockSpec((tm, d), lambda i, j: (i, 0)),
            pl.BlockSpec((tm, PLE_DIM), lambda i, j: (i, 0)),
            _const_spec((1, d)),
            pl.BlockSpec((d, tf), lambda i, j: (0, j)),
            pl.BlockSpec((d, tf), lambda i, j: (0, nff + j)),
            pl.BlockSpec((tf, d), lambda i, j: (j, 0)),
            _const_spec((1, d)), _const_spec((1, d)),
            _const_spec((d, d)), _const_spec((PLE_DIM, d)),
        ],
        out_specs=pl.BlockSpec((tm, d), lambda i, j: (i, 0)),
        scratch_shapes=[pltpu.VMEM((tm, d), BF16), pltpu.VMEM((tm, d), F32)],
        compiler_params=_cparams(("parallel", "arbitrary"), VMEM_LARGE),
        name="ffn_ple",
    )(x1, p2d, g_pre, w_in, w_in, w_out, g_post, g_ple, w_pg, w_pi)


def _layer(x, p, lam_init, wts):
    b, seq, d = x.shape
    x2d = x.reshape(b * seq, d)
    z = _in_proj(x2d, wts["g_mix_pre"], wts["w_in"], wts["b_gate"])
    z3 = z.reshape(b, seq, IN_COLS)

    tf = min(FREQ_TILE if seq > LONG_SEQ else 2 * FREQ_TILE, seq)
    rmat = _dft_tables(seq, tf)
    ht, sums = _filt_mlp(seq, wts["filt_w1"], wts["filt_b1"], wts["filt_w2"], wts["filt_b2"],
                         wts["filt_freq"], wts["filt_w3"])
    u1, v2, kl = _filt_spec(ht, sums, rmat, tf)
    x0c, wt = _hy_prep(z3, wts["conv_w"], wts["conv_b"])
    yconv = _long_conv(wt, rmat, u1, v2, kl, wts["hyena_d"], tf)

    yatt = _attention(z3, wts["lam"], wts["g_subln"], lam_init)

    x1 = _merge(yconv.reshape(b * seq, HY_CH), x0c.reshape(b * seq, HY_CH), yatt.reshape(b * seq, ATT_WIDTH),
                z, x2d, wts["w_hy_out"], wts["w_att_out"], wts["w_out"], wts["g_mix_post"])
    y = _ffn_ple(x1, p.reshape(b * seq, PLE_DIM), wts["g_ffn_pre"], wts["w_ffn_in"], wts["w_ffn_out"],
                 wts["g_ffn_post"], wts["g_ple"], wts["w_ple_gate"], wts["w_ple_in"])
    return y.reshape(b, seq, d)


def kernel(x_prompt, x_sample, p_prompt, p_sample, g_mix_pre, g_mix_post, g_ffn_pre, g_ffn_post, g_ple, w_in, b_gate, conv_w, conv_b, filt_w1, filt_b1, filt_w2, filt_b2, filt_freq, filt_w3, hyena_d, lam_q1, lam_k1, lam_q2, lam_k2, g_subln, w_hy_out, w_att_out, w_out, w_ffn_in, w_ffn_out, w_ple_in, w_ple_gate):
    depth = w_in.shape[0]
    xs = [x_prompt, x_sample]
    ps = [p_prompt, p_sample]
    for i in range(depth):
        lam_init = 0.8 - 0.6 * math.exp(-0.3 * i)
        wts = dict(
            g_mix_pre=g_mix_pre[i][None, :], g_mix_post=g_mix_post[i][None, :],
            g_ffn_pre=g_ffn_pre[i][None, :], g_ffn_post=g_ffn_post[i][None, :], g_ple=g_ple[i][None, :],
            w_in=w_in[i].astype(BF16), b_gate=b_gate[i].reshape(1, 2 * D_MODEL),
            conv_w=conv_w[i], conv_b=conv_b[i][None, :],
            filt_w1=filt_w1[i], filt_b1=filt_b1[i], filt_w2=filt_w2[i], filt_b2=filt_b2[i],
            filt_freq=filt_freq[i], filt_w3=filt_w3[i],
            hyena_d=jnp.broadcast_to(hyena_d[i][:, None], (HY_CH, LANES)),
            lam=jnp.stack([lam_q1[i], lam_k1[i], lam_q2[i], lam_k2[i]]),
            g_subln=g_subln[i][None, :],
            w_hy_out=w_hy_out[i].astype(BF16), w_att_out=w_att_out[i].astype(BF16),
            w_out=w_out[i].astype(BF16), w_ffn_in=w_ffn_in[i].astype(BF16),
            w_ffn_out=w_ffn_out[i].astype(BF16), w_ple_in=w_ple_in[i].astype(BF16),
            w_ple_gate=w_ple_gate[i].astype(BF16),
        )
        xs = [_layer(x, p[i], lam_init, wts) for x, p in zip(xs, ps)]
    return (xs[0], xs[1])
```

```python
import functools
import math

import jax
import jax.numpy as jnp
from jax import lax
from jax.experimental import pallas as pl
from jax.experimental.pallas import tpu as pltpu

F32 = jnp.float32
BF16 = jnp.bfloat16

D_MODEL = 2048
PLE_DIM = 256
NORM_EPS = 1e-6
HY_CH = D_MODEL // 2
FILT_BANDS = 16
FILT_EMB = 1 + 2 * FILT_BANDS
FILT_ORDER = 64
FILT_MAX_DECAY = math.log(1e-2) / 0.3
FILT_MIN_DECAY = math.log(1e-2) / 1.5
ATT_WIDTH = D_MODEL // 2
N_HEADS = 8
HEAD_DIM = ATT_WIDTH // (2 * N_HEADS)
HEAD_W = 2 * HEAD_DIM
ROPE_DIM = HEAD_DIM // 4
ROPE_THETA = 500000.0
D_FF = ((8 * D_MODEL + 3 * 256 - 1) // (3 * 256)) * 256
HY_COLS = 3 * HY_CH
GATE_COL0 = HY_COLS + 3 * ATT_WIDTH
IN_COLS = GATE_COL0 + 2 * D_MODEL
LANES = 128
LOG2E = math.log2(math.e)
MIB = 2 ** 20

LONG_SEQ = 2048
FREQ_TILE = 256
IN_PROJ_ROWS, IN_PROJ_COLS = 1024, 2048
MERGE_ROWS = 512
FFN_ROWS, FFN_TILE = 512, 512
CONV_CHANNELS = 512
FILT_CHANNELS = 256
FILT_ROWS = 512
VMEM_SMALL, VMEM_MEDIUM, VMEM_LARGE, VMEM_MAX = 32, 48, 56, 58


def _cparams(sem, vmem_mib):
    return pltpu.CompilerParams(dimension_semantics=sem, vmem_limit_bytes=vmem_mib * MIB)


def _const_spec(shape):
    nd = len(shape)
    return pl.BlockSpec(shape, lambda *_: (0,) * nd, pipeline_mode=pl.Buffered(1))


def _rms(x, g):
    ms = jnp.mean(x * x, axis=-1, keepdims=True)
    return x * lax.rsqrt(ms + NORM_EPS) * g


def _sigmoid(x):
    return 1.0 / (1.0 + jnp.exp(-x))


ROW_CHUNK = 128


def _for_row_chunks(n_rows, body, chunk=ROW_CHUNK):
    chunk = min(chunk, n_rows)

    def step(c, carry):
        body(pl.ds(pl.multiple_of(c * chunk, chunk), chunk))
        return carry
    lax.fori_loop(0, n_rows // chunk, step, 0)


def _rms_rows_to(h_ref, x_ref, g_ref):
    def body(rows):
        h_ref[rows, :] = _rms(x_ref[rows, :], g_ref[...]).astype(h_ref.dtype)
    _for_row_chunks(x_ref.shape[0], body)


def _inproj_kernel(x_ref, g_ref, w_ref, b_ref, o_ref, h_ref, *, n_plain):
    j = pl.program_id(1)

    @pl.when(j == 0)
    def _():
        _rms_rows_to(h_ref, x_ref, g_ref)

    acc = jnp.dot(h_ref[...], w_ref[...], preferred_element_type=F32)
    o_ref[...] = jnp.where(j >= n_plain, _sigmoid(acc + b_ref[...]), acc).astype(BF16)


def _in_proj(x2d, g, w_bf, b_flat):
    m, d = x2d.shape
    n = w_bf.shape[1]
    tm = min(IN_PROJ_ROWS, m)
    tn = IN_PROJ_COLS
    n_plain = GATE_COL0 // tn
    return pl.pallas_call(
        functools.partial(_inproj_kernel, n_plain=n_plain),
        out_shape=jax.ShapeDtypeStruct((m, n), BF16),
        grid=(m // tm, n // tn),
        in_specs=[
            pl.BlockSpec((tm, d), lambda i, j: (i, 0)),
            pl.BlockSpec((1, d), lambda i, j: (0, 0)),
            pl.BlockSpec((d, tn), lambda i, j: (0, j)),
            pl.BlockSpec((1, tn), lambda i, j: (0, jnp.maximum(j - n_plain, 0))),
        ],
        out_specs=pl.BlockSpec((tm, tn), lambda i, j: (i, j)),
        scratch_shapes=[pltpu.VMEM((tm, d), BF16)],
        compiler_params=_cparams(("parallel", "arbitrary"), VMEM_MAX),
        name="in_proj",
    )(x2d, g, w_bf, b_flat)


def _filt_mlp_kernel(emb_ref, w1_ref, b1_ref, w2_ref, b2_ref, fr_ref, w3_ref, dl_ref, ht_ref, sum_ref):
    i = pl.program_id(0)
    hp = lax.Precision.HIGHEST
    emb = emb_ref[...]
    fr = fr_ref[...]
    h1 = jnp.sin(fr * (jnp.dot(emb, w1_ref[...], precision=hp, preferred_element_type=F32) + b1_ref[...]))
    h2 = jnp.sin(fr * (jnp.dot(h1, w2_ref[...], precision=hp, preferred_element_type=F32) + b2_ref[...]))
    h = jnp.dot(h2, w3_ref[...], precision=hp, preferred_element_type=F32)
    dec = jnp.exp(-emb[:, 0:1] * dl_ref[...])
    ht = jnp.concatenate([h[:, :HY_CH] * dec, h[:, HY_CH:] * dec], axis=1).T
    ht_ref[...] = ht
    ab = jnp.abs(ht)
    part = ab[:, 0:LANES]
    for c in range(1, ab.shape[1] // LANES):
        part = part + ab[:, c * LANES:(c + 1) * LANES]

    @pl.when(i == 0)
    def _():
        sum_ref[...] = part

    @pl.when(i > 0)
    def _():
        sum_ref[...] += part


def _filt_mlp(seq, w1, b1, w2, b2, freq, w3):
    t = jnp.linspace(0.0, 1.0, seq, dtype=F32)[:, None]
    wpos = 2.0 * math.pi * jnp.arange(seq, dtype=F32) / seq
    bands = jnp.linspace(1e-4, FILT_BANDS - 1, FILT_BANDS, dtype=F32)
    ang = wpos[:, None] * bands[None, :]
    emb = jnp.concatenate([t, jnp.cos(ang), -jnp.sin(ang)], axis=-1)
    emb = jnp.pad(emb, ((0, 0), (0, LANES - FILT_EMB)))
    pad_o = LANES - FILT_ORDER
    w1p = jnp.pad(w1, ((0, LANES - FILT_EMB), (0, pad_o)))
    w2p = jnp.pad(w2, ((0, pad_o), (0, pad_o)))
    w3p = jnp.pad(w3, ((0, pad_o), (0, 0)))
    b1p = jnp.pad(b1[None, :], ((0, 0), (0, pad_o)))
    b2p = jnp.pad(b2[None, :], ((0, 0), (0, pad_o)))
    frp = jnp.pad(freq[None, :], ((0, 0), (0, pad_o)))
    deltas = jnp.abs(jnp.linspace(FILT_MIN_DECAY, FILT_MAX_DECAY, HY_CH, dtype=F32))[None, :]
    tl = min(FILT_ROWS, seq)
    return pl.pallas_call(
        _filt_mlp_kernel,
        out_shape=(jax.ShapeDtypeStruct((2 * HY_CH, seq), F32),
                   jax.ShapeDtypeStruct((2 * HY_CH, LANES), F32)),
        grid=(seq // tl,),
        in_specs=[
            pl.BlockSpec((tl, LANES), lambda i: (i, 0)),
            _const_spec((LANES, LANES)), _const_spec((1, LANES)),
            _const_spec((LANES, LANES)), _const_spec((1, LANES)),
            _const_spec((1, LANES)), _const_spec((LANES, 2 * HY_CH)),
            _const_spec((1, HY_CH)),
        ],
        out_specs=(pl.BlockSpec((2 * HY_CH, tl), lambda i: (0, i)),
                   pl.BlockSpec((2 * HY_CH, LANES), lambda i: (0, 0))),
        compiler_params=_cparams(("arbitrary",), VMEM_MEDIUM),
        name="filt_mlp",
    )(emb, w1p, b1p, w2p, b2p, frp, w3p, deltas)


def _dot_t(a, b):
    return lax.dot_general(a, b, (((1,), (1,)), ((), ())), preferred_element_type=F32)


def _filt_spec_kernel(hf_ref, hb_ref, sf_ref, sb_ref, r_ref, u1_ref, v2_ref, kl_ref, s1_ref, s2_ref, *, tf, seq):
    f = pl.program_id(1)

    @pl.when(f == 0)
    def _():
        norm = jnp.sum(sf_ref[...] + sb_ref[...], axis=1, keepdims=True) + NORM_EPS
        hf = hf_ref[...] / norm
        lane = lax.broadcasted_iota(jnp.int32, hf.shape, 1)
        hb = jnp.where(lane == 0, 0.0, hb_ref[...] / norm)
        s1_ref[...] = (hf + hb).astype(BF16)
        s2_ref[...] = (hf - hb).astype(BF16)

    s1 = s1_ref[...]
    s2 = s2_ref[...]
    c = _dot_t(s1, r_ref[:tf, :]) * (1.0 / seq)
    col = lax.broadcasted_iota(jnp.int32, c.shape, 1) + f * tf
    u1_ref[...] = jnp.where(col == 0, 0.5 * c, c)
    v2_ref[...] = jnp.where(col == 0, 0.0, _dot_t(s2, r_ref[tf:, :]) * (1.0 / seq))

    @pl.when(f == 0)
    def _():
        nyq = _dot_t(s1, r_ref[tf:, :])[:, 0:1] * (0.5 / seq)
        kl_ref[...] = jnp.broadcast_to(nyq, kl_ref.shape)


def _filt_spec(ht, sums, rmat, tf):
    seq = ht.shape[1]
    tc = FILT_CHANNELS
    nc = HY_CH // tc
    return pl.pallas_call(
        functools.partial(_filt_spec_kernel, tf=tf, seq=seq),
        out_shape=(jax.ShapeDtypeStruct((HY_CH, seq), F32),
                   jax.ShapeDtypeStruct((HY_CH, seq), F32),
                   jax.ShapeDtypeStruct((HY_CH, LANES), F32)),
        grid=(nc, seq // tf),
        in_specs=[
            pl.BlockSpec((tc, seq), lambda c, f: (c, 0)),
            pl.BlockSpec((tc, seq), lambda c, f: (nc + c, 0)),
            pl.BlockSpec((tc, LANES), lambda c, f: (c, 0)),
            pl.BlockSpec((tc, LANES), lambda c, f: (nc + c, 0)),
            pl.BlockSpec((2 * tf, seq), lambda c, f: (f, 0)),
        ],
        out_specs=(pl.BlockSpec((tc, tf), lambda c, f: (c, f)),
                   pl.BlockSpec((tc, tf), lambda c, f: (c, f)),
                   pl.BlockSpec((tc, LANES), lambda c, f: (c, 0))),
        scratch_shapes=[pltpu.VMEM((tc, seq), BF16), pltpu.VMEM((tc, seq), BF16)],
        compiler_params=_cparams(("parallel", "arbitrary"), VMEM_MEDIUM),
        name="filt_spec",
    )(ht, ht, sums, sums, rmat)


def _hyprep_kernel(u0_ref, u1_ref, u2_ref, w0_ref, w1_ref, w2_ref, b0_ref, b1_ref, b2_ref, x0_ref, wt_ref):
    seq = u0_ref.shape[0]
    row = lax.broadcasted_iota(jnp.int32, u0_ref.shape, 0)

    def sconv(u_ref, w_ref, b_ref):
        u = u_ref[...].astype(F32)
        um = jnp.where(row == 0, 0.0, pltpu.roll(u, 1, 0))
        up = jnp.where(row == seq - 1, 0.0, pltpu.roll(u, seq - 1, 0))
        w = w_ref[...]
        return b_ref[...] + um * w[0:1] + u * w[1:2] + up * w[2:3]

    x0_ref[...] = sconv(u0_ref, w0_ref, b0_ref).astype(BF16)
    prod = sconv(u1_ref, w1_ref, b1_ref) * sconv(u2_ref, w2_ref, b2_ref)
    wt_ref[...] = prod.T.astype(BF16)


def _hy_prep(z3, conv_w, conv_b):
    b, seq, _ = z3.shape
    tc = LANES if seq > LONG_SEQ else 2 * LANES
    nc = HY_CH // tc
    uspec = lambda g: pl.BlockSpec((None, seq, tc), lambda i, c: (i, 0, g * nc + c))
    wspec = lambda g: pl.BlockSpec((3, tc), lambda i, c: (0, g * nc + c))
    bspec = lambda g: pl.BlockSpec((1, tc), lambda i, c: (0, g * nc + c))
    return pl.pallas_call(
        _hyprep_kernel,
        out_shape=(jax.ShapeDtypeStruct((b, seq, HY_CH), BF16),
                   jax.ShapeDtypeStruct((b, HY_CH, seq), BF16)),
        grid=(b, nc),
        in_specs=[uspec(0), uspec(1), uspec(2), wspec(0), wspec(1), wspec(2), bspec(0), bspec(1), bspec(2)],
        out_specs=(pl.BlockSpec((None, seq, tc), lambda i, c: (i, 0, c)),
                   pl.BlockSpec((None, tc, seq), lambda i, c: (i, c, 0))),
        compiler_params=_cparams(("parallel", "parallel"), VMEM_MEDIUM),
        name="hy_prep",
    )(z3, z3, z3, conv_w, conv_w, conv_w, conv_b, conv_b, conv_b)


def _conv_kernel(wt_ref, ra_ref, rb_ref, u1_ref, v2_ref, kl_ref, d_ref, o_ref, acc_ref, za_ref, zb_ref, *, tf):
    t = pl.program_id(2)
    n_f = pl.num_programs(2) - 1

    def work(z_out, z_in, first=False):
        if z_in is not None:
            acc_ref[...] += jnp.dot(z_in[...], rb_ref[...], preferred_element_type=F32)
        if z_out is not None:
            ap = _dot_t(wt_ref[...], ra_ref[...])
            a = ap[:, :tf]
            p = ap[:, tf:]
            u1 = u1_ref[...]
            v2 = v2_ref[...]
            z2 = p * u1 + a * v2
            if first:
                col = lax.broadcasted_iota(jnp.int32, z2.shape, 1)
                z2 = jnp.where(col == 0, p * kl_ref[:, 0:1], z2)
            z_out[...] = jnp.concatenate([a * u1 - p * v2, z2], axis=1).astype(BF16)

    @pl.when(t == 0)
    def _():
        acc_ref[...] = jnp.zeros_like(acc_ref)
        work(za_ref, None, first=True)

    steady = jnp.logical_and(t > 0, t < n_f)

    @pl.when(jnp.logical_and(steady, t % 2 == 1))
    def _():
        work(zb_ref, za_ref)

    @pl.when(jnp.logical_and(steady, t % 2 == 0))
    def _():
        work(za_ref, zb_ref)

    def finish(z_in):
        work(None, z_in)
        y = acc_ref[...] + d_ref[:, 0:1] * wt_ref[...].astype(F32)
        o_ref[...] = y.T.astype(BF16)

    @pl.when(jnp.logical_and(t == n_f, t % 2 == 1))
    def _():
        finish(za_ref)

    @pl.when(jnp.logical_and(t == n_f, t % 2 == 0))
    def _():
        finish(zb_ref)


def _long_conv(wt, rmat, u1, v2, kl, dcol, tf):
    b, _, seq = wt.shape
    tc = CONV_CHANNELS
    n_f = seq // tf
    return pl.pallas_call(
        functools.partial(_conv_kernel, tf=tf),
        out_shape=jax.ShapeDtypeStruct((b, seq, HY_CH), BF16),
        grid=(b, HY_CH // tc, n_f + 1),
        in_specs=[
            pl.BlockSpec((None, tc, seq), lambda i, c, t: (i, c, 0)),
            pl.BlockSpec((2 * tf, seq), lambda i, c, t: (jnp.minimum(t, n_f - 1), 0)),
            pl.BlockSpec((2 * tf, seq), lambda i, c, t: (jnp.maximum(t - 1, 0), 0)),
            pl.BlockSpec((tc, tf), lambda i, c, t: (c, jnp.minimum(t, n_f - 1))),
            pl.BlockSpec((tc, tf), lambda i, c, t: (c, jnp.minimum(t, n_f - 1))),
            pl.BlockSpec((tc, LANES), lambda i, c, t: (c, 0)),
            pl.BlockSpec((tc, LANES), lambda i, c, t: (c, 0)),
        ],
        out_specs=pl.BlockSpec((None, seq, tc), lambda i, c, t: (i, 0, c)),
        scratch_shapes=[pltpu.VMEM((tc, seq), F32), pltpu.VMEM((tc, 2 * tf), BF16), pltpu.VMEM((tc, 2 * tf), BF16)],
        compiler_params=_cparams(("parallel", "parallel", "arbitrary"), VMEM_LARGE),
        name="long_conv",
    )(wt, rmat, rmat, u1, v2, kl, dcol)


def _dft_gen_kernel(ch_ref, sh_ref, cl_ref, sl_ref, r_ref, *, tf):
    ch, sh = ch_ref[...], sh_ref[...]
    cl, sl = cl_ref[...], sl_ref[...]
    r_ref[:tf, :] = (ch * cl - sh * sl).astype(BF16)
    sin = sh * cl + ch * sl
    row = lax.broadcasted_iota(jnp.int32, sin.shape, 0)
    lane = lax.broadcasted_iota(jnp.int32, sin.shape, 1)
    alt = (1 - 2 * (lane % 2)).astype(F32)
    nyq_row = jnp.logical_and(row == 0, pl.program_id(0) == 0)
    r_ref[tf:, :] = jnp.where(nyq_row, alt, sin).astype(BF16)


def _dft_tables(seq, tf):
    idx = jnp.arange(seq, dtype=jnp.int32)
    nf = seq // tf

    def trig(freqs):
        ang = ((freqs[:, None] * idx[None, :]) % (2 * seq)).astype(F32) * (math.pi / seq)
        return jnp.cos(ang), jnp.sin(ang)

    ch, sh = trig(jnp.arange(nf, dtype=jnp.int32) * tf)
    cl, sl = trig(jnp.arange(tf, dtype=jnp.int32))
    base_spec = pl.BlockSpec((None, 1, seq), lambda f: (f, 0, 0))
    return pl.pallas_call(
        functools.partial(_dft_gen_kernel, tf=tf),
        out_shape=jax.ShapeDtypeStruct((2 * seq, seq), BF16),
        grid=(nf,),
        in_specs=[base_spec, base_spec, _const_spec((tf, seq)), _const_spec((tf, seq))],
        out_specs=pl.BlockSpec((2 * tf, seq), lambda f: (f, 0)),
        compiler_params=_cparams(("parallel",), VMEM_SMALL),
        name="dft_tables",
    )(ch[:, None, :], sh[:, None, :], cl, sl)


KV_CHUNK = 256
ATT_Q_TILE = 512


def _attn_kernel(par_ref, q_ref, k_ref, v_ref, tab_ref, o_ref, krot_ref, vt_ref,
                 sa_buf, sb_buf, ma_buf, mb_buf,
                 *, tq, n_q, lam_init):
    g = pl.program_id(0)
    n_tiles = pl.num_programs(0) - 1
    t = g % n_q
    head_par = (g // n_q) % 2
    prev_par = (jnp.maximum(g - 1, 0) // n_q) % 2
    seq = k_ref.shape[0]
    ck = min(KV_CHUNK, seq)
    n_chunks = seq // ck

    def rope(x, rows):
        c, sa, sb = (tab_ref[rows, i * HEAD_W:(i + 1) * HEAD_W] for i in range(3))
        return x * c + pltpu.roll(x, LANES - ROPE_DIM // 2, 1) * sa + pltpu.roll(x, ROPE_DIM // 2, 1) * sb

    bufs = ((sa_buf, ma_buf), (sb_buf, mb_buf))

    def tile_work(score, finish, new_head=False):
        if new_head:
            krot_ref[...] = rope(k_ref[...].astype(F32), slice(None)).astype(BF16)
            vt_ref[head_par] = v_ref[...].astype(F32).T.astype(BF16)
        if score is not None:
            rows = pl.ds(pl.multiple_of(t * tq, tq), tq)
            q = rope(q_ref[rows, :].astype(F32), rows)
            q = q * (HEAD_DIM ** -0.5 * LOG2E)
            lane = lax.broadcasted_iota(jnp.int32, q.shape, 1)
            qq = jnp.concatenate([jnp.where(lane < HEAD_DIM, q, 0.0), jnp.where(lane >= HEAD_DIM, q, 0.0)], axis=0)
            qq = qq.astype(BF16)
            m8 = jnp.full((8, 2 * tq), -jnp.inf, F32)
        if finish is not None:
            m = jnp.max(finish[1][...], axis=0, keepdims=True)
            l8 = jnp.zeros((8, 2 * tq), F32)
            ot = jnp.zeros((HEAD_W, 2 * tq), F32)

        for c in range(n_chunks):
            kv = slice(c * ck, (c + 1) * ck)
            if finish is not None:
                p_c = jnp.exp2(finish[0][kv, :] - m)
                l8 = l8 + jnp.sum(p_c.reshape(ck // 8, 8, 2 * tq), axis=0)
                ot = ot + jnp.dot(vt_ref[prev_par, :, kv], p_c.astype(BF16), preferred_element_type=F32)
            if score is not None:
                s_c = _dot_t(krot_ref[kv, :], qq)
                score[0][kv, :] = s_c
                m8 = jnp.maximum(m8, jnp.max(s_c.reshape(ck // 8, 8, 2 * tq), axis=0))
        if score is not None:
            score[1][...] = m8
        if finish is not None:
            l = jnp.sum(l8, axis=0, keepdims=True)
            lp = par_ref[0:4, :]
            lam = (jnp.exp(jnp.sum(lp[0:1] * lp[1:2], axis=-1, keepdims=True))
                   - jnp.exp(jnp.sum(lp[2:3] * lp[3:4], axis=-1, keepdims=True)) + lam_init)
            r = 1.0 / l
            o = (ot[:, :tq] * r[:, :tq] - ot[:, tq:] * (lam * r[:, tq:])).T
            o_ref[...] = (_rms(o, par_ref[4:5, :]) * (1.0 - lam_init)).astype(BF16)

    @pl.when(g == 0)
    def _():
        tile_work(bufs[0], None, new_head=True)

    steady = jnp.logical_and(g > 0, g < n_tiles)
    for par in (0, 1):
        for opens_head in (False, True):
            if opens_head and par == 1 and n_q % 2 == 0:
                continue
            cond = jnp.logical_and(jnp.logical_and(steady, g % 2 == par), (t == 0) == opens_head)
            pl.when(cond)(functools.partial(tile_work, bufs[par], bufs[1 - par], new_head=opens_head))

    for par in (0, 1):
        pl.when(jnp.logical_and(g == n_tiles, g % 2 == par))(functools.partial(tile_work, None, bufs[1 - par]))


def _attention(z3, lam_params, g_subln, lam_init):
    b, seq, _ = z3.shape
    tq = min(ATT_Q_TILE, seq)
    n_q = seq // tq
    n_tiles = b * N_HEADS * n_q
    inv = ROPE_THETA ** (-jnp.arange(0, ROPE_DIM, 2, dtype=F32) / ROPE_DIM)
    ang = jnp.arange(seq, dtype=F32)[:, None] * inv[None, :]
    c8, s8 = jnp.cos(ang), jnp.sin(ang)
    half = ROPE_DIM // 2
    rest = HEAD_DIM - ROPE_DIM
    one, zero = jnp.ones((seq, rest), F32), jnp.zeros((seq, rest), F32)
    z8 = jnp.zeros((seq, half), F32)
    cos_t = jnp.tile(jnp.concatenate([c8, c8, one], axis=1), (1, 2))
    sa_t = jnp.tile(jnp.concatenate([-s8, z8, zero], axis=1), (1, 2))
    sb_t = jnp.tile(jnp.concatenate([z8, s8, zero], axis=1), (1, 2))
    tab = jnp.concatenate([cos_t, sa_t, sb_t], axis=1)
    par = jnp.concatenate([jnp.pad(lam_params, ((0, 0), (0, HEAD_W - HEAD_DIM))), g_subln,
                           jnp.zeros((3, HEAD_W), F32)], axis=0)
    qb, kb, vb = (HY_COLS // HEAD_W, (HY_COLS + ATT_WIDTH) // HEAD_W, (HY_COLS + 2 * ATT_WIDTH) // HEAD_W)

    def head_cols(blk):
        def index(g):
            head = jnp.minimum(g, n_tiles - 1) // n_q
            return head // N_HEADS, 0, blk + head % N_HEADS
        return pl.BlockSpec((None, seq, HEAD_W), index)

    def out_index(g):
        done = jnp.maximum(g - 1, 0)
        head = done // n_q
        return head // N_HEADS, done % n_q, head % N_HEADS

    return pl.pallas_call(
        functools.partial(_attn_kernel, tq=tq, n_q=n_q, lam_init=lam_init),
        out_shape=jax.ShapeDtypeStruct((b, seq, ATT_WIDTH), BF16),
        grid=(n_tiles + 1,),
        in_specs=[_const_spec((8, HEAD_W)), head_cols(qb), head_cols(kb), head_cols(vb),
                  _const_spec((seq, 3 * HEAD_W))],
        out_specs=pl.BlockSpec((None, tq, HEAD_W), out_index),
        scratch_shapes=[pltpu.VMEM((seq, HEAD_W), BF16), pltpu.VMEM((2, HEAD_W, seq), BF16),
                        pltpu.VMEM((seq, 2 * tq), F32), pltpu.VMEM((seq, 2 * tq), F32),
                        pltpu.VMEM((8, 2 * tq), F32), pltpu.VMEM((8, 2 * tq), F32)],
        compiler_params=_cparams(("arbitrary",), VMEM_LARGE),
        name="attention",
    )(par, z3, z3, z3, tab)


def _merge_kernel(yc_ref, x0_ref, ya_ref, g0_ref, g1_ref, x_ref, why_ref, wat_ref, wo_ref, gp_ref, o_ref):
    def body(rows):
        yh = (x0_ref[rows, :].astype(F32) * yc_ref[rows, :].astype(F32)).astype(BF16)
        a = jnp.dot(yh, why_ref[...], preferred_element_type=F32)
        b = jnp.dot(ya_ref[rows, :], wat_ref[...], preferred_element_type=F32)
        m = g0_ref[rows, :].astype(F32) * a + g1_ref[rows, :].astype(F32) * b
        r = jnp.dot(m.astype(BF16), wo_ref[...], preferred_element_type=F32)
        o_ref[rows, :] = x_ref[rows, :] + _rms(r, gp_ref[...])
    _for_row_chunks(x_ref.shape[0], body, chunk=2 * ROW_CHUNK)


def _merge(yconv, x0c, yatt, z, x2d, w_hy, w_att, w_out, g_post):
    m, d = x2d.shape
    tm = min(MERGE_ROWS, m)
    row = lambda width, blk=0: pl.BlockSpec((tm, width), lambda i: (i, blk))
    gate_blk = GATE_COL0 // d
    return pl.pallas_call(
        _merge_kernel,
        out_shape=jax.ShapeDtypeStruct((m, d), F32),
        grid=(m // tm,),
        in_specs=[row(HY_CH), row(HY_CH), row(ATT_WIDTH), row(d, gate_blk), row(d, gate_blk + 1), row(d),
                  _const_spec((HY_CH, d)), _const_spec((ATT_WIDTH, d)), _const_spec((d, d)), _const_spec((1, d))],
        out_specs=row(d),
        compiler_params=_cparams(("parallel",), VMEM_LARGE),
        name="merge",
    )(yconv, x0c, yatt, z, z, x2d, w_hy, w_att, w_out, g_post)


def _ffn_kernel(x_ref, p_ref, gpre_ref, wg_ref, wu_ref, wo_ref, gpost_ref, gple_ref, wpg_ref, wpi_ref,
                o_ref, h_ref, acc_ref):
    j = pl.program_id(1)

    @pl.when(j == 0)
    def _():
        _rms_rows_to(h_ref, x_ref, gpre_ref)
        acc_ref[...] = jnp.zeros_like(acc_ref)

    h = h_ref[...]
    half = wg_ref.shape[1] // 2
    part = None
    for c in range(2):
        cols = slice(c * half, (c + 1) * half)
        gate = jnp.dot(h, wg_ref[:, cols], preferred_element_type=F32)
        up = jnp.dot(h, wu_ref[:, cols], preferred_element_type=F32)
        act = (gate * _sigmoid(gate) * up).astype(BF16)
        down = jnp.dot(act, wo_ref[cols, :], preferred_element_type=F32)
        part = down if part is None else part + down
    acc_ref[...] += part

    @pl.when(j == pl.num_programs(1) - 1)
    def _():
        def body(rows):
            x2 = x_ref[rows, :] + _rms(acc_ref[rows, :], gpost_ref[...])
            e = jnp.dot(_rms(x2, gple_ref[...]).astype(BF16), wpg_ref[...], preferred_element_type=F32)
            pe = jnp.dot(p_ref[rows, :].astype(BF16), wpi_ref[...], preferred_element_type=F32)
            o_ref[rows, :] = x2 + pe * _sigmoid(e)
        _for_row_chunks(x_ref.shape[0], body, chunk=2 * ROW_CHUNK)


def _ffn_ple(x1, p2d, g_pre, w_in, w_out, g_post, g_ple, w_pg, w_pi):
    m, d = x1.shape
    tm = min(FFN_ROWS, m)
    tf = FFN_TILE
    nff = D_FF // tf
    return pl.pallas_call(
        _ffn_kernel,
        out_shape=jax.ShapeDtypeStruct((m, d), F32),
        grid=(m // tm, nff),
        in_specs=[
            pl.BlockSpec((tm, d), lambda i, j: (i, 0)),
            pl.BlockSpec((tm, PLE_DIM), lambda i, j: (i, 0)),
            _const_spec((1, d)),
            pl.BlockSpec((d, tf), lambda i, j: (0, j)),
            pl.BlockSpec((d, tf), lambda i, j: (0, nff + j)),
            pl.BlockSpec((tf, d), lambda i, j: (j, 0)),
            _const_spec((1, d)), _const_spec((1, d)),
            _const_spec((d, d)), _const_spec((PLE_DIM, d)),
        ],
        out_specs=pl.BlockSpec((tm, d), lambda i, j: (i, 0)),
        scratch_shapes=[pltpu.VMEM((tm, d), BF16), pltpu.VMEM((tm, d), F32)],
        compiler_params=_cparams(("parallel", "arbitrary"), VMEM_LARGE),
        name="ffn_ple",
    )(x1, p2d, g_pre, w_in, w_in, w_out, g_post, g_ple, w_pg, w_pi)


def _layer(x, p, lam_init, wts):
    b, seq, d = x.shape
    x2d = x.reshape(b * seq, d)
    z = _in_proj(x2d, wts["g_mix_pre"], wts["w_in"], wts["b_gate"])
    z3 = z.reshape(b, seq, IN_COLS)

    tf = min(FREQ_TILE if seq > LONG_SEQ else 2 * FREQ_TILE, seq)
    rmat = _dft_tables(seq, tf)
    ht, sums = _filt_mlp(seq, wts["filt_w1"], wts["filt_b1"], wts["filt_w2"], wts["filt_b2"],
                         wts["filt_freq"], wts["filt_w3"])
    u1, v2, kl = _filt_spec(ht, sums, rmat, tf)
    x0c, wt = _hy_prep(z3, wts["conv_w"], wts["conv_b"])
    yconv = _long_conv(wt, rmat, u1, v2, kl, wts["hyena_d"], tf)

    yatt = _attention(z3, wts["lam"], wts["g_subln"], lam_init)

    x1 = _merge(yconv.reshape(b * seq, HY_CH), x0c.reshape(b * seq, HY_CH), yatt.reshape(b * seq, ATT_WIDTH),
                z, x2d, wts["w_hy_out"], wts["w_att_out"], wts["w_out"], wts["g_mix_post"])
    y = _ffn_ple(x1, p.reshape(b * seq, PLE_DIM), wts["g_ffn_pre"], wts["w_ffn_in"], wts["w_ffn_out"],
                 wts["g_ffn_post"], wts["g_ple"], wts["w_ple_gate"], wts["w_ple_in"])
    return y.reshape(b, seq, d)


def kernel(x_prompt, x_sample, p_prompt, p_sample, g_mix_pre, g_mix_post, g_ffn_pre, g_ffn_post, g_ple, w_in, b_gate, conv_w, conv_b, filt_w1, filt_b1, filt_w2, filt_b2, filt_freq, filt_w3, hyena_d, lam_q1, lam_k1, lam_q2, lam_k2, g_subln, w_hy_out, w_att_out, w_out, w_ffn_in, w_ffn_out, w_ple_in, w_ple_gate):
    depth = w_in.shape[0]
    xs = [x_prompt, x_sample]
    ps = [p_prompt, p_sample]
    for i in range(depth):
        lam_init = 0.8 - 0.6 * math.exp(-0.3 * i)
        wts = dict(
            g_mix_pre=g_mix_pre[i][None, :], g_mix_post=g_mix_post[i][None, :],
            g_ffn_pre=g_ffn_pre[i][None, :], g_ffn_post=g_ffn_post[i][None, :], g_ple=g_ple[i][None, :],
            w_in=w_in[i].astype(BF16), b_gate=b_gate[i].reshape(1, 2 * D_MODEL),
            conv_w=conv_w[i], conv_b=conv_b[i][None, :],
            filt_w1=filt_w1[i], filt_b1=filt_b1[i], filt_w2=filt_w2[i], filt_b2=filt_b2[i],
            filt_freq=filt_freq[i], filt_w3=filt_w3[i],
            hyena_d=jnp.broadcast_to(hyena_d[i][:, None], (HY_CH, LANES)),
            lam=jnp.stack([lam_q1[i], lam_k1[i], lam_q2[i], lam_k2[i]]),
            g_subln=g_subln[i][None, :],
            w_hy_out=w_hy_out[i].astype(BF16), w_att_out=w_att_out[i].astype(BF16),
            w_out=w_out[i].astype(BF16), w_ffn_in=w_ffn_in[i].astype(BF16),
            w_ffn_out=w_ffn_out[i].astype(BF16), w_ple_in=w_ple_in[i].astype(BF16),
            w_ple_gate=w_ple_gate[i].astype(BF16),
        )
        xs = [_layer(x, p[i], lam_init, wts) for x, p in zip(xs, ps)]
    return (xs[0], xs[1])
```

```python
import functools
import math

import jax
import jax.numpy as jnp
from jax import lax
from jax.experimental import pallas as pl
from jax.experimental.pallas import tpu as pltpu

F32 = jnp.float32
BF16 = jnp.bfloat16

D_MODEL = 2048
PLE_DIM = 256
NORM_EPS = 1e-6
HY_CH = D_MODEL // 2
FILT_BANDS = 16
FILT_EMB = 1 + 2 * FILT_BANDS
FILT_ORDER = 64
FILT_MAX_DECAY = math.log(1e-2) / 0.3
FILT_MIN_DECAY = math.log(1e-2) / 1.5
ATT_WIDTH = D_MODEL // 2
N_HEADS = 8
HEAD_DIM = ATT_WIDTH // (2 * N_HEADS)
HEAD_W = 2 * HEAD_DIM
ROPE_DIM = HEAD_DIM // 4
ROPE_THETA = 500000.0
D_FF = ((8 * D_MODEL + 3 * 256 - 1) // (3 * 256)) * 256
HY_COLS = 3 * HY_CH
GATE_COL0 = HY_COLS + 3 * ATT_WIDTH
IN_COLS = GATE_COL0 + 2 * D_MODEL
LANES = 128
LOG2E = math.log2(math.e)
MIB = 2 ** 20

LONG_SEQ = 2048
FREQ_TILE = 256
IN_PROJ_ROWS, IN_PROJ_COLS = 1024, 2048
MERGE_ROWS = 512
FFN_ROWS, FFN_TILE = 512, 512
CONV_CHANNELS = 512
FILT_CHANNELS = 256
FILT_ROWS = 512
VMEM_SMALL, VMEM_MEDIUM, VMEM_LARGE, VMEM_MAX = 32, 48, 56, 58


def _cparams(sem, vmem_mib):
    return pltpu.CompilerParams(dimension_semantics=sem, vmem_limit_bytes=vmem_mib * MIB)


def _const_spec(shape):
    nd = len(shape)
    return pl.BlockSpec(shape, lambda *_: (0,) * nd, pipeline_mode=pl.Buffered(1))


def _rms(x, g):
    ms = jnp.mean(x * x, axis=-1, keepdims=True)
    return x * lax.rsqrt(ms + NORM_EPS) * g


def _sigmoid(x):
    return 1.0 / (1.0 + jnp.exp(-x))


ROW_CHUNK = 128


def _for_row_chunks(n_rows, body, chunk=ROW_CHUNK):
    chunk = min(chunk, n_rows)

    def step(c, carry):
        body(pl.ds(pl.multiple_of(c * chunk, chunk), chunk))
        return carry
    lax.fori_loop(0, n_rows // chunk, step, 0)


def _rms_rows_to(h_ref, x_ref, g_ref):
    def body(rows):
        h_ref[rows, :] = _rms(x_ref[rows, :], g_ref[...]).astype(h_ref.dtype)
    _for_row_chunks(x_ref.shape[0], body)


def _inproj_kernel(x_ref, g_ref, w_ref, b_ref, o_ref, h_ref, *, n_plain):
    j = pl.program_id(1)

    @pl.when(j == 0)
    def _():
        _rms_rows_to(h_ref, x_ref, g_ref)

    acc = jnp.dot(h_ref[...], w_ref[...], preferred_element_type=F32)
    o_ref[...] = jnp.where(j >= n_plain, _sigmoid(acc + b_ref[...]), acc).astype(BF16)


def _in_proj(x2d, g, w_bf, b_flat):
    m, d = x2d.shape
    n = w_bf.shape[1]
    tm = min(IN_PROJ_ROWS, m)
    tn = IN_PROJ_COLS
    n_plain = GATE_COL0 // tn
    return pl.pallas_call(
        functools.partial(_inproj_kernel, n_plain=n_plain),
        out_shape=jax.ShapeDtypeStruct((m, n), BF16),
        grid=(m // tm, n // tn),
        in_specs=[
            pl.BlockSpec((tm, d), lambda i, j: (i, 0)),
            pl.BlockSpec((1, d), lambda i, j: (0, 0)),
            pl.BlockSpec((d, tn), lambda i, j: (0, j)),
            pl.BlockSpec((1, tn), lambda i, j: (0, jnp.maximum(j - n_plain, 0))),
        ],
        out_specs=pl.BlockSpec((tm, tn), lambda i, j: (i, j)),
        scratch_shapes=[pltpu.VMEM((tm, d), BF16)],
        compiler_params=_cparams(("parallel", "arbitrary"), VMEM_MAX),
        name="in_proj",
    )(x2d, g, w_bf, b_flat)


def _filt_mlp_kernel(emb_ref, w1_ref, b1_ref, w2_ref, b2_ref, fr_ref, w3_ref, dl_ref, ht_ref, sum_ref):
    i = pl.program_id(0)
    hp = lax.Precision.HIGHEST
    emb = emb_ref[...]
    fr = fr_ref[...]
    h1 = jnp.sin(fr * (jnp.dot(emb, w1_ref[...], precision=hp, preferred_element_type=F32) + b1_ref[...]))
    h2 = jnp.sin(fr * (jnp.dot(h1, w2_ref[...], precision=hp, preferred_element_type=F32) + b2_ref[...]))
    h = jnp.dot(h2, w3_ref[...], precision=hp, preferred_element_type=F32)
    dec = jnp.exp(-emb[:, 0:1] * dl_ref[...])
    ht = jnp.concatenate([h[:, :HY_CH] * dec, h[:, HY_CH:] * dec], axis=1).T
    ht_ref[...] = ht
    ab = jnp.abs(ht)
    part = ab[:, 0:LANES]
    for c in range(1, ab.shape[1] // LANES):
        part = part + ab[:, c * LANES:(c + 1) * LANES]

    @pl.when(i == 0)
    def _():
        sum_ref[...] = part

    @pl.when(i > 0)
    def _():
        sum_ref[...] += part


def _filt_mlp(seq, w1, b1, w2, b2, freq, w3):
    t = jnp.linspace(0.0, 1.0, seq, dtype=F32)[:, None]
    wpos = 2.0 * math.pi * jnp.arange(seq, dtype=F32) / seq
    bands = jnp.linspace(1e-4, FILT_BANDS - 1, FILT_BANDS, dtype=F32)
    ang = wpos[:, None] * bands[None, :]
    emb = jnp.concatenate([t, jnp.cos(ang), -jnp.sin(ang)], axis=-1)
    emb = jnp.pad(emb, ((0, 0), (0, LANES - FILT_EMB)))
    pad_o = LANES - FILT_ORDER
    w1p = jnp.pad(w1, ((0, LANES - FILT_EMB), (0, pad_o)))
    w2p = jnp.pad(w2, ((0, pad_o), (0, pad_o)))
    w3p = jnp.pad(w3, ((0, pad_o), (0, 0)))
    b1p = jnp.pad(b1[None, :], ((0, 0), (0, pad_o)))
    b2p = jnp.pad(b2[None, :], ((0, 0), (0, pad_o)))
    frp = jnp.pad(freq[None, :], ((0, 0), (0, pad_o)))
    deltas = jnp.abs(jnp.linspace(FILT_MIN_DECAY, FILT_MAX_DECAY, HY_CH, dtype=F32))[None, :]
    tl = min(FILT_ROWS, seq)
    return pl.pallas_call(
        _filt_mlp_kernel,
        out_shape=(jax.ShapeDtypeStruct((2 * HY_CH, seq), F32),
                   jax.ShapeDtypeStruct((2 * HY_CH, LANES), F32)),
        grid=(seq // tl,),
        in_specs=[
            pl.BlockSpec((tl, LANES), lambda i: (i, 0)),
            _const_spec((LANES, LANES)), _const_spec((1, LANES)),
            _const_spec((LANES, LANES)), _const_spec((1, LANES)),
            _const_spec((1, LANES)), _const_spec((LANES, 2 * HY_CH)),
            _const_spec((1, HY_CH)),
        ],
        out_specs=(pl.BlockSpec((2 * HY_CH, tl), lambda i: (0, i)),
                   pl.BlockSpec((2 * HY_CH, LANES), lambda i: (0, 0))),
        compiler_params=_cparams(("arbitrary",), VMEM_MEDIUM),
        name="filt_mlp",
    )(emb, w1p, b1p, w2p, b2p, frp, w3p, deltas)


def _dot_t(a, b):
    return lax.dot_general(a, b, (((1,), (1,)), ((), ())), preferred_element_type=F32)


def _filt_spec_kernel(hf_ref, hb_ref, sf_ref, sb_ref, r_ref, u1_ref, v2_ref, kl_ref, s1_ref, s2_ref, *, tf, seq):
    f = pl.program_id(1)

    @pl.when(f == 0)
    def _():
        norm = jnp.sum(sf_ref[...] + sb_ref[...], axis=1, keepdims=True) + NORM_EPS
        hf = hf_ref[...] / norm
        lane = lax.broadcasted_iota(jnp.int32, hf.shape, 1)
        hb = jnp.where(lane == 0, 0.0, hb_ref[...] / norm)
        s1_ref[...] = (hf + hb).astype(BF16)
        s2_ref[...] = (hf - hb).astype(BF16)

    s1 = s1_ref[...]
    s2 = s2_ref[...]
    c = _dot_t(s1, r_ref[:tf, :]) * (1.0 / seq)
    col = lax.broadcasted_iota(jnp.int32, c.shape, 1) + f * tf
    u1_ref[...] = jnp.where(col == 0, 0.5 * c, c)
    v2_ref[...] = jnp.where(col == 0, 0.0, _dot_t(s2, r_ref[tf:, :]) * (1.0 / seq))

    @pl.when(f == 0)
    def _():
        nyq = _dot_t(s1, r_ref[tf:, :])[:, 0:1] * (0.5 / seq)
        kl_ref[...] = jnp.broadcast_to(nyq, kl_ref.shape)


def _filt_spec(ht, sums, rmat, tf):
    seq = ht.shape[1]
    tc = FILT_CHANNELS
    nc = HY_CH // tc
    return pl.pallas_call(
        functools.partial(_filt_spec_kernel, tf=tf, seq=seq),
        out_shape=(jax.ShapeDtypeStruct((HY_CH, seq), F32),
                   jax.ShapeDtypeStruct((HY_CH, seq), F32),
                   jax.ShapeDtypeStruct((HY_CH, LANES), F32)),
        grid=(nc, seq // tf),
        in_specs=[
            pl.BlockSpec((tc, seq), lambda c, f: (c, 0)),
            pl.BlockSpec((tc, seq), lambda c, f: (nc + c, 0)),
            pl.BlockSpec((tc, LANES), lambda c, f: (c, 0)),
            pl.BlockSpec((tc, LANES), lambda c, f: (nc + c, 0)),
            pl.BlockSpec((2 * tf, seq), lambda c, f: (f, 0)),
        ],
        out_specs=(pl.BlockSpec((tc, tf), lambda c, f: (c, f)),
                   pl.BlockSpec((tc, tf), lambda c, f: (c, f)),
                   pl.BlockSpec((tc, LANES), lambda c, f: (c, 0))),
        scratch_shapes=[pltpu.VMEM((tc, seq), BF16), pltpu.VMEM((tc, seq), BF16)],
        compiler_params=_cparams(("parallel", "arbitrary"), VMEM_MEDIUM),
        name="filt_spec",
    )(ht, ht, sums, sums, rmat)


def _hyprep_kernel(u0_ref, u1_ref, u2_ref, w0_ref, w1_ref, w2_ref, b0_ref, b1_ref, b2_ref, x0_ref, wt_ref):
    seq = u0_ref.shape[0]
    row = lax.broadcasted_iota(jnp.int32, u0_ref.shape, 0)

    def sconv(u_ref, w_ref, b_ref):
        u = u_ref[...].astype(F32)
        um = jnp.where(row == 0, 0.0, pltpu.roll(u, 1, 0))
        up = jnp.where(row == seq - 1, 0.0, pltpu.roll(u, seq - 1, 0))
        w = w_ref[...]
        return b_ref[...] + um * w[0:1] + u * w[1:2] + up * w[2:3]

    x0_ref[...] = sconv(u0_ref, w0_ref, b0_ref).astype(BF16)
    prod = sconv(u1_ref, w1_ref, b1_ref) * sconv(u2_ref, w2_ref, b2_ref)
    wt_ref[...] = prod.T.astype(BF16)


def _hy_prep(z3, conv_w, conv_b):
    b, seq, _ = z3.shape
    tc = LANES if seq > LONG_SEQ else 2 * LANES
    nc = HY_CH // tc
    uspec = lambda g: pl.BlockSpec((None, seq, tc), lambda i, c: (i, 0, g * nc + c))
    wspec = lambda g: pl.BlockSpec((3, tc), lambda i, c: (0, g * nc + c))
    bspec = lambda g: pl.BlockSpec((1, tc), lambda i, c: (0, g * nc + c))
    return pl.pallas_call(
        _hyprep_kernel,
        out_shape=(jax.ShapeDtypeStruct((b, seq, HY_CH), BF16),
                   jax.ShapeDtypeStruct((b, HY_CH, seq), BF16)),
        grid=(b, nc),
        in_specs=[uspec(0), uspec(1), uspec(2), wspec(0), wspec(1), wspec(2), bspec(0), bspec(1), bspec(2)],
        out_specs=(pl.BlockSpec((None, seq, tc), lambda i, c: (i, 0, c)),
                   pl.BlockSpec((None, tc, seq), lambda i, c: (i, c, 0))),
        compiler_params=_cparams(("parallel", "parallel"), VMEM_MEDIUM),
        name="hy_prep",
    )(z3, z3, z3, conv_w, conv_w, conv_w, conv_b, conv_b, conv_b)


def _conv_kernel(wt_ref, ra_ref, rb_ref, u1_ref, v2_ref, kl_ref, d_ref, o_ref, acc_ref, za_ref, zb_ref, *, tf):
    t = pl.program_id(2)
    n_f = pl.num_programs(2) - 1

    def work(z_out, z_in, first=False):
        if z_in is not None:
            acc_ref[...] += jnp.dot(z_in[...], rb_ref[...], preferred_element_type=F32)
        if z_out is not None:
            ap = _dot_t(wt_ref[...], ra_ref[...])
            a = ap[:, :tf]
            p = ap[:, tf:]
            u1 = u1_ref[...]
            v2 = v2_ref[...]
            z2 = p * u1 + a * v2
            if first:
                col = lax.broadcasted_iota(jnp.int32, z2.shape, 1)
                z2 = jnp.where(col == 0, p * kl_ref[:, 0:1], z2)
            z_out[...] = jnp.concatenate([a * u1 - p * v2, z2], axis=1).astype(BF16)

    @pl.when(t == 0)
    def _():
        acc_ref[...] = jnp.zeros_like(acc_ref)
        work(za_ref, None, first=True)

    steady = jnp.logical_and(t > 0, t < n_f)

    @pl.when(jnp.logical_and(steady, t % 2 == 1))
    def _():
        work(zb_ref, za_ref)

    @pl.when(jnp.logical_and(steady, t % 2 == 0))
    def _():
        work(za_ref, zb_ref)

    def finish(z_in):
        work(None, z_in)
        y = acc_ref[...] + d_ref[:, 0:1] * wt_ref[...].astype(F32)
        o_ref[...] = y.T.astype(BF16)

    @pl.when(jnp.logical_and(t == n_f, t % 2 == 1))
    def _():
        finish(za_ref)

    @pl.when(jnp.logical_and(t == n_f, t % 2 == 0))
    def _():
        finish(zb_ref)


def _long_conv(wt, rmat, u1, v2, kl, dcol, tf):
    b, _, seq = wt.shape
    tc = CONV_CHANNELS
    n_f = seq // tf
    return pl.pallas_call(
        functools.partial(_conv_kernel, tf=tf),
        out_shape=jax.ShapeDtypeStruct((b, seq, HY_CH), BF16),
        grid=(b, HY_CH // tc, n_f + 1),
        in_specs=[
            pl.BlockSpec((None, tc, seq), lambda i, c, t: (i, c, 0)),
            pl.BlockSpec((2 * tf, seq), lambda i, c, t: (jnp.minimum(t, n_f - 1), 0)),
            pl.BlockSpec((2 * tf, seq), lambda i, c, t: (jnp.maximum(t - 1, 0), 0)),
            pl.BlockSpec((tc, tf), lambda i, c, t: (c, jnp.minimum(t, n_f - 1))),
            pl.BlockSpec((tc, tf), lambda i, c, t: (c, jnp.minimum(t, n_f - 1))),
            pl.BlockSpec((tc, LANES), lambda i, c, t: (c, 0)),
            pl.BlockSpec((tc, LANES), lambda i, c, t: (c, 0)),
        ],
        out_specs=pl.BlockSpec((None, seq, tc), lambda i, c, t: (i, 0, c)),
        scratch_shapes=[pltpu.VMEM((tc, seq), F32), pltpu.VMEM((tc, 2 * tf), BF16), pltpu.VMEM((tc, 2 * tf), BF16)],
        compiler_params=_cparams(("parallel", "parallel", "arbitrary"), VMEM_LARGE),
        name="long_conv",
    )(wt, rmat, rmat, u1, v2, kl, dcol)


def _dft_gen_kernel(ch_ref, sh_ref, cl_ref, sl_ref, r_ref, *, tf):
    ch, sh = ch_ref[...], sh_ref[...]
    cl, sl = cl_ref[...], sl_ref[...]
    r_ref[:tf, :] = (ch * cl - sh * sl).astype(BF16)
    sin = sh * cl + ch * sl
    row = lax.broadcasted_iota(jnp.int32, sin.shape, 0)
    lane = lax.broadcasted_iota(jnp.int32, sin.shape, 1)
    alt = (1 - 2 * (lane % 2)).astype(F32)
    nyq_row = jnp.logical_and(row == 0, pl.program_id(0) == 0)
    r_ref[tf:, :] = jnp.where(nyq_row, alt, sin).astype(BF16)


def _dft_tables(seq, tf):
    idx = jnp.arange(seq, dtype=jnp.int32)
    nf = seq // tf

    def trig(freqs):
        ang = ((freqs[:, None] * idx[None, :]) % (2 * seq)).astype(F32) * (math.pi / seq)
        return jnp.cos(ang), jnp.sin(ang)

    ch, sh = trig(jnp.arange(nf, dtype=jnp.int32) * tf)
    cl, sl = trig(jnp.arange(tf, dtype=jnp.int32))
    base_spec = pl.BlockSpec((None, 1, seq), lambda f: (f, 0, 0))
    return pl.pallas_call(
        functools.partial(_dft_gen_kernel, tf=tf),
        out_shape=jax.ShapeDtypeStruct((2 * seq, seq), BF16),
        grid=(nf,),
        in_specs=[base_spec, base_spec, _const_spec((tf, seq)), _const_spec((tf, seq))],
        out_specs=pl.BlockSpec((2 * tf, seq), lambda f: (f, 0)),
        compiler_params=_cparams(("parallel",), VMEM_SMALL),
        name="dft_tables",
    )(ch[:, None, :], sh[:, None, :], cl, sl)


KV_CHUNK = 256
ATT_Q_TILE = 512


def _attn_kernel(par_ref, q_ref, k_ref, v_ref, tab_ref, o_ref, krot_ref, vt_ref,
                 sa_buf, sb_buf, ma_buf, mb_buf,
                 *, tq, n_q, lam_init):
    g = pl.program_id(0)
    n_tiles = pl.num_programs(0) - 1
    t = g % n_q
    head_par = (g // n_q) % 2
    prev_par = (jnp.maximum(g - 1, 0) // n_q) % 2
    seq = k_ref.shape[0]
    ck = min(KV_CHUNK, seq)
    n_chunks = seq // ck

    def rope(x, rows):
        c, sa, sb = (tab_ref[rows, i * HEAD_W:(i + 1) * HEAD_W] for i in range(3))
        return x * c + pltpu.roll(x, LANES - ROPE_DIM // 2, 1) * sa + pltpu.roll(x, ROPE_DIM // 2, 1) * sb

    bufs = ((sa_buf, ma_buf), (sb_buf, mb_buf))

    def tile_work(score, finish, new_head=False):
        if new_head:
            krot_ref[...] = rope(k_ref[...].astype(F32), slice(None)).astype(BF16)
            vt_ref[head_par] = v_ref[...].astype(F32).T.astype(BF16)
        if score is not None:
            rows = pl.ds(pl.multiple_of(t * tq, tq), tq)
            q = rope(q_ref[rows, :].astype(F32), rows)
            q = q * (HEAD_DIM ** -0.5 * LOG2E)
            lane = lax.broadcasted_iota(jnp.int32, q.shape, 1)
            qq = jnp.concatenate([jnp.where(lane < HEAD_DIM, q, 0.0), jnp.where(lane >= HEAD_DIM, q, 0.0)], axis=0)
            qq = qq.astype(BF16)
            m8 = jnp.full((8, 2 * tq), -jnp.inf, F32)
        if finish is not None:
            m = jnp.max(finish[1][...], axis=0, keepdims=True)
            l8 = jnp.zeros((8, 2 * tq), F32)
            ot = jnp.zeros((HEAD_W, 2 * tq), F32)

        for c in range(n_chunks):
            kv = slice(c * ck, (c + 1) * ck)
            if finish is not None:
                p_c = jnp.exp2(finish[0][kv, :] - m)
                l8 = l8 + jnp.sum(p_c.reshape(ck // 8, 8, 2 * tq), axis=0)
                ot = ot + jnp.dot(vt_ref[prev_par, :, kv], p_c.astype(BF16), preferred_element_type=F32)
            if score is not None:
                s_c = _dot_t(krot_ref[kv, :], qq)
                score[0][kv, :] = s_c
                m8 = jnp.maximum(m8, jnp.max(s_c.reshape(ck // 8, 8, 2 * tq), axis=0))
        if score is not None:
            score[1][...] = m8
        if finish is not None:
            l = jnp.sum(l8, axis=0, keepdims=True)
            lp = par_ref[0:4, :]
            lam = (jnp.exp(jnp.sum(lp[0:1] * lp[1:2], axis=-1, keepdims=True))
                   - jnp.exp(jnp.sum(lp[2:3] * lp[3:4], axis=-1, keepdims=True)) + lam_init)
            r = 1.0 / l
            o = (ot[:, :tq] * r[:, :tq] - ot[:, tq:] * (lam * r[:, tq:])).T
            o_ref[...] = (_rms(o, par_ref[4:5, :]) * (1.0 - lam_init)).astype(BF16)

    @pl.when(g == 0)
    def _():
        tile_work(bufs[0], None, new_head=True)

    steady = jnp.logical_and(g > 0, g < n_tiles)
    for par in (0, 1):
        for opens_head in (False, True):
            if opens_head and par == 1 and n_q % 2 == 0:
                continue
            cond = jnp.logical_and(jnp.logical_and(steady, g % 2 == par), (t == 0) == opens_head)
            pl.when(cond)(functools.partial(tile_work, bufs[par], bufs[1 - par], new_head=opens_head))

    for par in (0, 1):
        pl.when(jnp.logical_and(g == n_tiles, g % 2 == par))(functools.partial(tile_work, None, bufs[1 - par]))


def _attention(z3, lam_params, g_subln, lam_init):
    b, seq, _ = z3.shape
    tq = min(ATT_Q_TILE, seq)
    n_q = seq // tq
    n_tiles = b * N_HEADS * n_q
    inv = ROPE_THETA ** (-jnp.arange(0, ROPE_DIM, 2, dtype=F32) / ROPE_DIM)
    ang = jnp.arange(seq, dtype=F32)[:, None] * inv[None, :]
    c8, s8 = jnp.cos(ang), jnp.sin(ang)
    half = ROPE_DIM // 2
    rest = HEAD_DIM - ROPE_DIM
    one, zero = jnp.ones((seq, rest), F32), jnp.zeros((seq, rest), F32)
    z8 = jnp.zeros((seq, half), F32)
    cos_t = jnp.tile(jnp.concatenate([c8, c8, one], axis=1), (1, 2))
    sa_t = jnp.tile(jnp.concatenate([-s8, z8, zero], axis=1), (1, 2))
    sb_t = jnp.tile(jnp.concatenate([z8, s8, zero], axis=1), (1, 2))
    tab = jnp.concatenate([cos_t, sa_t, sb_t], axis=1)
    par = jnp.concatenate([jnp.pad(lam_params, ((0, 0), (0, HEAD_W - HEAD_DIM))), g_subln,
                           jnp.zeros((3, HEAD_W), F32)], axis=0)
    qb, kb, vb = (HY_COLS // HEAD_W, (HY_COLS + ATT_WIDTH) // HEAD_W, (HY_COLS + 2 * ATT_WIDTH) // HEAD_W)

    def head_cols(blk):
        def index(g):
            head = jnp.minimum(g, n_tiles - 1) // n_q
            return head // N_HEADS, 0, blk + head % N_HEADS
        return pl.BlockSpec((None, seq, HEAD_W), index)

    def out_index(g):
        done = jnp.maximum(g - 1, 0)
        head = done // n_q
        return head // N_HEADS, done % n_q, head % N_HEADS

    return pl.pallas_call(
        functools.partial(_attn_kernel, tq=tq, n_q=n_q, lam_init=lam_init),
        out_shape=jax.ShapeDtypeStruct((b, seq, ATT_WIDTH), BF16),
        grid=(n_tiles + 1,),
        in_specs=[_const_spec((8, HEAD_W)), head_cols(qb), head_cols(kb), head_cols(vb),
                  _const_spec((seq, 3 * HEAD_W))],
        out_specs=pl.BlockSpec((None, tq, HEAD_W), out_index),
        scratch_shapes=[pltpu.VMEM((seq, HEAD_W), BF16), pltpu.VMEM((2, HEAD_W, seq), BF16),
                        pltpu.VMEM((seq, 2 * tq), F32), pltpu.VMEM((seq, 2 * tq), F32),
                        pltpu.VMEM((8, 2 * tq), F32), pltpu.VMEM((8, 2 * tq), F32)],
        compiler_params=_cparams(("arbitrary",), VMEM_LARGE),
        name="attention",
    )(par, z3, z3, z3, tab)


def _merge_kernel(yc_ref, x0_ref, ya_ref, g0_ref, g1_ref, x_ref, why_ref, wat_ref, wo_ref, gp_ref, o_ref):
    def body(rows):
        yh = (x0_ref[rows, :].astype(F32) * yc_ref[rows, :].astype(F32)).astype(BF16)
        ya = ya_ref[rows, :]
        half = wo_ref.shape[0] // 2
        r = None
        for c in range(2):
            cols = slice(c * half, (c + 1) * half)
            a = jnp.dot(yh, why_ref[:, cols], preferred_element_type=F32)
            b = jnp.dot(ya, wat_ref[:, cols], preferred_element_type=F32)
            m = g0_ref[rows, cols].astype(F32) * a + g1_ref[rows, cols].astype(F32) * b
            rc = jnp.dot(m.astype(BF16), wo_ref[cols, :], preferred_element_type=F32)
            r = rc if r is None else r + rc
        o_ref[rows, :] = x_ref[rows, :] + _rms(r, gp_ref[...])
    _for_row_chunks(x_ref.shape[0], body, chunk=2 * ROW_CHUNK)


def _merge(yconv, x0c, yatt, z, x2d, w_hy, w_att, w_out, g_post):
    m, d = x2d.shape
    tm = min(MERGE_ROWS, m)
    row = lambda width, blk=0: pl.BlockSpec((tm, width), lambda i: (i, blk))
    gate_blk = GATE_COL0 // d
    return pl.pallas_call(
        _merge_kernel,
        out_shape=jax.ShapeDtypeStruct((m, d), F32),
        grid=(m // tm,),
        in_specs=[row(HY_CH), row(HY_CH), row(ATT_WIDTH), row(d, gate_blk), row(d, gate_blk + 1), row(d),
                  _const_spec((HY_CH, d)), _const_spec((ATT_WIDTH, d)), _const_spec((d, d)), _const_spec((1, d))],
        out_specs=row(d),
        compiler_params=_cparams(("parallel",), VMEM_LARGE),
        name="merge",
    )(yconv, x0c, yatt, z, z, x2d, w_hy, w_att, w_out, g_post)


def _ffn_kernel(x_ref, p_ref, gpre_ref, wg_ref, wu_ref, wo_ref, gpost_ref, gple_ref, wpg_ref, wpi_ref,
                o_ref, h_ref, acc_ref):
    j = pl.program_id(1)

    @pl.when(j == 0)
    def _():
        _rms_rows_to(h_ref, x_ref, gpre_ref)
        acc_ref[...] = jnp.zeros_like(acc_ref)

    h = h_ref[...]
    half = wg_ref.shape[1] // 2
    part = None
    for c in range(2):
        cols = slice(c * half, (c + 1) * half)
        gate = jnp.dot(h, wg_ref[:, cols], preferred_element_type=F32)
        up = jnp.dot(h, wu_ref[:, cols], preferred_element_type=F32)
        act = (gate * _sigmoid(gate) * up).astype(BF16)
        down = jnp.dot(act, wo_ref[cols, :], preferred_element_type=F32)
        part = down if part is None else part + down
    acc_ref[...] += part

    @pl.when(j == pl.num_programs(1) - 1)
    def _():
        def body(rows):
            x2 = x_ref[rows, :] + _rms(acc_ref[rows, :], gpost_ref[...])
            hn = _rms(x2, gple_ref[...]).astype(BF16)
            pb = p_ref[rows, :].astype(BF16)
            half = wpg_ref.shape[1] // 2
            for c in range(2):
                cols = slice(c * half, (c + 1) * half)
                e = jnp.dot(hn, wpg_ref[:, cols], preferred_element_type=F32)
                pe = jnp.dot(pb, wpi_ref[:, cols], preferred_element_type=F32)
                o_ref[rows, cols] = x2[:, cols] + pe * _sigmoid(e)
        _for_row_chunks(x_ref.shape[0], body, chunk=2 * ROW_CHUNK)


def _ffn_ple(x1, p2d, g_pre, w_in, w_out, g_post, g_ple, w_pg, w_pi):
    m, d = x1.shape
    tm = min(FFN_ROWS, m)
    tf = FFN_TILE
    nff = D_FF // tf
    return pl.pallas_call(
        _ffn_kernel,
        out_shape=jax.ShapeDtypeStruct((m, d), F32),
        grid=(m // tm, nff),
        in_specs=[
            pl.BlockSpec((tm, d), lambda i, j: (i, 0)),
            pl.BlockSpec((tm, PLE_DIM), lambda i, j: (i, 0)),
            _const_spec((1, d)),
            pl.BlockSpec((d, tf), lambda i, j: (0, j)),
            pl.BlockSpec((d, tf), lambda i, j: (0, nff + j)),
            pl.BlockSpec((tf, d), lambda i, j: (j, 0)),
            _const_spec((1, d)), _const_spec((1, d)),
            _const_spec((d, d)), _const_spec((PLE_DIM, d)),
        ],
        out_specs=pl.BlockSpec((tm, d), lambda i, j: (i, 0)),
        scratch_shapes=[pltpu.VMEM((tm, d), BF16), pltpu.VMEM((tm, d), F32)],
        compiler_params=_cparams(("parallel", "arbitrary"), VMEM_LARGE),
        name="ffn_ple",
    )(x1, p2d, g_pre, w_in, w_in, w_out, g_post, g_ple, w_pg, w_pi)


def _layer(x, p, lam_init, wts):
    b, seq, d = x.shape
    x2d = x.reshape(b * seq, d)
    z = _in_proj(x2d, wts["g_mix_pre"], wts["w_in"], wts["b_gate"])
    z3 = z.reshape(b, seq, IN_COLS)

    tf = min(FREQ_TILE if seq > LONG_SEQ else 2 * FREQ_TILE, seq)
    rmat = _dft_tables(seq, tf)
    ht, sums = _filt_mlp(seq, wts["filt_w1"], wts["filt_b1"], wts["filt_w2"], wts["filt_b2"],
                         wts["filt_freq"], wts["filt_w3"])
    u1, v2, kl = _filt_spec(ht, sums, rmat, tf)
    x0c, wt = _hy_prep(z3, wts["conv_w"], wts["conv_b"])
    yconv = _long_conv(wt, rmat, u1, v2, kl, wts["hyena_d"], tf)

    yatt = _attention(z3, wts["lam"], wts["g_subln"], lam_init)

    x1 = _merge(yconv.reshape(b * seq, HY_CH), x0c.reshape(b * seq, HY_CH), yatt.reshape(b * seq, ATT_WIDTH),
                z, x2d, wts["w_hy_out"], wts["w_att_out"], wts["w_out"], wts["g_mix_post"])
    y = _ffn_ple(x1, p.reshape(b * seq, PLE_DIM), wts["g_ffn_pre"], wts["w_ffn_in"], wts["w_ffn_out"],
                 wts["g_ffn_post"], wts["g_ple"], wts["w_ple_gate"], wts["w_ple_in"])
    return y.reshape(b, seq, d)


def kernel(x_prompt, x_sample, p_prompt, p_sample, g_mix_pre, g_mix_post, g_ffn_pre, g_ffn_post, g_ple, w_in, b_gate, conv_w, conv_b, filt_w1, filt_b1, filt_w2, filt_b2, filt_freq, filt_w3, hyena_d, lam_q1, lam_k1, lam_q2, lam_k2, g_subln, w_hy_out, w_att_out, w_out, w_ffn_in, w_ffn_out, w_ple_in, w_ple_gate):
    depth = w_in.shape[0]
    xs = [x_prompt, x_sample]
    ps = [p_prompt, p_sample]
    for i in range(depth):
        lam_init = 0.8 - 0.6 * math.exp(-0.3 * i)
        wts = dict(
            g_mix_pre=g_mix_pre[i][None, :], g_mix_post=g_mix_post[i][None, :],
            g_ffn_pre=g_ffn_pre[i][None, :], g_ffn_post=g_ffn_post[i][None, :], g_ple=g_ple[i][None, :],
            w_in=w_in[i].astype(BF16), b_gate=b_gate[i].reshape(1, 2 * D_MODEL),
            conv_w=conv_w[i], conv_b=conv_b[i][None, :],
            filt_w1=filt_w1[i], filt_b1=filt_b1[i], filt_w2=filt_w2[i], filt_b2=filt_b2[i],
            filt_freq=filt_freq[i], filt_w3=filt_w3[i],
            hyena_d=jnp.broadcast_to(hyena_d[i][:, None], (HY_CH, LANES)),
            lam=jnp.stack([lam_q1[i], lam_k1[i], lam_q2[i], lam_k2[i]]),
            g_subln=g_subln[i][None, :],
            w_hy_out=w_hy_out[i].astype(BF16), w_att_out=w_att_out[i].astype(BF16),
            w_out=w_out[i].astype(BF16), w_ffn_in=w_ffn_in[i].astype(BF16),
            w_ffn_out=w_ffn_out[i].astype(BF16), w_ple_in=w_ple_in[i].astype(BF16),
            w_ple_gate=w_ple_gate[i].astype(BF16),
        )
        xs = [_layer(x, p[i], lam_init, wts) for x, p in zip(xs, ps)]
    return (xs[0], xs[1])
```

```python
import functools
import math

import jax
import jax.numpy as jnp
from jax import lax
from jax.experimental import pallas as pl
from jax.experimental.pallas import tpu as pltpu

F32 = jnp.float32
BF16 = jnp.bfloat16

D_MODEL = 2048
PLE_DIM = 256
NORM_EPS = 1e-6
HY_CH = D_MODEL // 2
FILT_BANDS = 16
FILT_EMB = 1 + 2 * FILT_BANDS
FILT_ORDER = 64
FILT_MAX_DECAY = math.log(1e-2) / 0.3
FILT_MIN_DECAY = math.log(1e-2) / 1.5
ATT_WIDTH = D_MODEL // 2
N_HEADS = 8
HEAD_DIM = ATT_WIDTH // (2 * N_HEADS)
HEAD_W = 2 * HEAD_DIM
ROPE_DIM = HEAD_DIM // 4
ROPE_THETA = 500000.0
D_FF = ((8 * D_MODEL + 3 * 256 - 1) // (3 * 256)) * 256
HY_COLS = 3 * HY_CH
GATE_COL0 = HY_COLS + 3 * ATT_WIDTH
IN_COLS = GATE_COL0 + 2 * D_MODEL
LANES = 128
LOG2E = math.log2(math.e)
MIB = 2 ** 20

LONG_SEQ = 2048
FREQ_TILE = 256
IN_PROJ_ROWS, IN_PROJ_COLS = 1024, 2048
MERGE_ROWS = 512
FFN_ROWS, FFN_TILE = 512, 512
CONV_CHANNELS = 512
FILT_CHANNELS = 256
FILT_ROWS = 512
VMEM_SMALL, VMEM_MEDIUM, VMEM_LARGE, VMEM_MAX = 32, 48, 56, 58


def _cparams(sem, vmem_mib):
    return pltpu.CompilerParams(dimension_semantics=sem, vmem_limit_bytes=vmem_mib * MIB)


def _const_spec(shape):
    nd = len(shape)
    return pl.BlockSpec(shape, lambda *_: (0,) * nd, pipeline_mode=pl.Buffered(1))


def _rms(x, g):
    ms = jnp.mean(x * x, axis=-1, keepdims=True)
    return x * lax.rsqrt(ms + NORM_EPS) * g


def _sigmoid(x):
    return 1.0 / (1.0 + jnp.exp(-x))


ROW_CHUNK = 128


def _for_row_chunks(n_rows, body, chunk=ROW_CHUNK):
    chunk = min(chunk, n_rows)

    def step(c, carry):
        body(pl.ds(pl.multiple_of(c * chunk, chunk), chunk))
        return carry
    lax.fori_loop(0, n_rows // chunk, step, 0)


def _rms_rows_to(h_ref, x_ref, g_ref):
    def body(rows):
        h_ref[rows, :] = _rms(x_ref[rows, :], g_ref[...]).astype(h_ref.dtype)
    _for_row_chunks(x_ref.shape[0], body)


def _inproj_kernel(x_ref, g_ref, w_ref, b_ref, o_ref, h_ref, *, n_plain):
    j = pl.program_id(1)

    @pl.when(j == 0)
    def _():
        _rms_rows_to(h_ref, x_ref, g_ref)

    acc = jnp.dot(h_ref[...], w_ref[...], preferred_element_type=F32)
    o_ref[...] = jnp.where(j >= n_plain, _sigmoid(acc + b_ref[...]), acc).astype(BF16)


def _in_proj(x2d, g, w_bf, b_flat):
    m, d = x2d.shape
    n = w_bf.shape[1]
    tm = min(IN_PROJ_ROWS, m)
    tn = IN_PROJ_COLS
    n_plain = GATE_COL0 // tn
    return pl.pallas_call(
        functools.partial(_inproj_kernel, n_plain=n_plain),
        out_shape=jax.ShapeDtypeStruct((m, n), BF16),
        grid=(m // tm, n // tn),
        in_specs=[
            pl.BlockSpec((tm, d), lambda i, j: (i, 0)),
            pl.BlockSpec((1, d), lambda i, j: (0, 0)),
            pl.BlockSpec((d, tn), lambda i, j: (0, j)),
            pl.BlockSpec((1, tn), lambda i, j: (0, jnp.maximum(j - n_plain, 0))),
        ],
        out_specs=pl.BlockSpec((tm, tn), lambda i, j: (i, j)),
        scratch_shapes=[pltpu.VMEM((tm, d), BF16)],
        compiler_params=_cparams(("parallel", "arbitrary"), VMEM_MAX),
        name="in_proj",
    )(x2d, g, w_bf, b_flat)


def _filt_mlp_kernel(emb_ref, w1_ref, b1_ref, w2_ref, b2_ref, fr_ref, w3_ref, dl_ref, ht_ref, sum_ref):
    i = pl.program_id(0)
    hp = lax.Precision.HIGHEST
    emb = emb_ref[...]
    fr = fr_ref[...]
    h1 = jnp.sin(fr * (jnp.dot(emb, w1_ref[...], precision=hp, preferred_element_type=F32) + b1_ref[...]))
    h2 = jnp.sin(fr * (jnp.dot(h1, w2_ref[...], precision=hp, preferred_element_type=F32) + b2_ref[...]))
    h = jnp.dot(h2, w3_ref[...], precision=hp, preferred_element_type=F32)
    dec = jnp.exp(-emb[:, 0:1] * dl_ref[...])
    ht = jnp.concatenate([h[:, :HY_CH] * dec, h[:, HY_CH:] * dec], axis=1).T
    ht_ref[...] = ht
    ab = jnp.abs(ht)
    part = ab[:, 0:LANES]
    for c in range(1, ab.shape[1] // LANES):
        part = part + ab[:, c * LANES:(c + 1) * LANES]

    @pl.when(i == 0)
    def _():
        sum_ref[...] = part

    @pl.when(i > 0)
    def _():
        sum_ref[...] += part


def _filt_mlp(seq, w1, b1, w2, b2, freq, w3):
    t = jnp.linspace(0.0, 1.0, seq, dtype=F32)[:, None]
    wpos = 2.0 * math.pi * jnp.arange(seq, dtype=F32) / seq
    bands = jnp.linspace(1e-4, FILT_BANDS - 1, FILT_BANDS, dtype=F32)
    ang = wpos[:, None] * bands[None, :]
    emb = jnp.concatenate([t, jnp.cos(ang), -jnp.sin(ang)], axis=-1)
    emb = jnp.pad(emb, ((0, 0), (0, LANES - FILT_EMB)))
    pad_o = LANES - FILT_ORDER
    w1p = jnp.pad(w1, ((0, LANES - FILT_EMB), (0, pad_o)))
    w2p = jnp.pad(w2, ((0, pad_o), (0, pad_o)))
    w3p = jnp.pad(w3, ((0, pad_o), (0, 0)))
    b1p = jnp.pad(b1[None, :], ((0, 0), (0, pad_o)))
    b2p = jnp.pad(b2[None, :], ((0, 0), (0, pad_o)))
    frp = jnp.pad(freq[None, :], ((0, 0), (0, pad_o)))
    deltas = jnp.abs(jnp.linspace(FILT_MIN_DECAY, FILT_MAX_DECAY, HY_CH, dtype=F32))[None, :]
    tl = min(FILT_ROWS, seq)
    return pl.pallas_call(
        _filt_mlp_kernel,
        out_shape=(jax.ShapeDtypeStruct((2 * HY_CH, seq), F32),
                   jax.ShapeDtypeStruct((2 * HY_CH, LANES), F32)),
        grid=(seq // tl,),
        in_specs=[
            pl.BlockSpec((tl, LANES), lambda i: (i, 0)),
            _const_spec((LANES, LANES)), _const_spec((1, LANES)),
            _const_spec((LANES, LANES)), _const_spec((1, LANES)),
            _const_spec((1, LANES)), _const_spec((LANES, 2 * HY_CH)),
            _const_spec((1, HY_CH)),
        ],
        out_specs=(pl.BlockSpec((2 * HY_CH, tl), lambda i: (0, i)),
                   pl.BlockSpec((2 * HY_CH, LANES), lambda i: (0, 0))),
        compiler_params=_cparams(("arbitrary",), VMEM_MEDIUM),
        name="filt_mlp",
    )(emb, w1p, b1p, w2p, b2p, frp, w3p, deltas)


def _dot_t(a, b):
    return lax.dot_general(a, b, (((1,), (1,)), ((), ())), preferred_element_type=F32)


def _filt_spec_kernel(hf_ref, hb_ref, sf_ref, sb_ref, r_ref, u1_ref, v2_ref, kl_ref, s1_ref, s2_ref, *, tf, seq):
    f = pl.program_id(1)

    @pl.when(f == 0)
    def _():
        norm = jnp.sum(sf_ref[...] + sb_ref[...], axis=1, keepdims=True) + NORM_EPS
        hf = hf_ref[...] / norm
        lane = lax.broadcasted_iota(jnp.int32, hf.shape, 1)
        hb = jnp.where(lane == 0, 0.0, hb_ref[...] / norm)
        s1_ref[...] = (hf + hb).astype(BF16)
        s2_ref[...] = (hf - hb).astype(BF16)

    s1 = s1_ref[...]
    s2 = s2_ref[...]
    c = _dot_t(s1, r_ref[:tf, :]) * (1.0 / seq)
    col = lax.broadcasted_iota(jnp.int32, c.shape, 1) + f * tf
    u1_ref[...] = jnp.where(col == 0, 0.5 * c, c)
    v2_ref[...] = jnp.where(col == 0, 0.0, _dot_t(s2, r_ref[tf:, :]) * (1.0 / seq))

    @pl.when(f == 0)
    def _():
        nyq = _dot_t(s1, r_ref[tf:, :])[:, 0:1] * (0.5 / seq)
        kl_ref[...] = jnp.broadcast_to(nyq, kl_ref.shape)


def _filt_spec(ht, sums, rmat, tf):
    seq = ht.shape[1]
    tc = FILT_CHANNELS
    nc = HY_CH // tc
    return pl.pallas_call(
        functools.partial(_filt_spec_kernel, tf=tf, seq=seq),
        out_shape=(jax.ShapeDtypeStruct((HY_CH, seq), F32),
                   jax.ShapeDtypeStruct((HY_CH, seq), F32),
                   jax.ShapeDtypeStruct((HY_CH, LANES), F32)),
        grid=(nc, seq // tf),
        in_specs=[
            pl.BlockSpec((tc, seq), lambda c, f: (c, 0)),
            pl.BlockSpec((tc, seq), lambda c, f: (nc + c, 0)),
            pl.BlockSpec((tc, LANES), lambda c, f: (c, 0)),
            pl.BlockSpec((tc, LANES), lambda c, f: (nc + c, 0)),
            pl.BlockSpec((2 * tf, seq), lambda c, f: (f, 0)),
        ],
        out_specs=(pl.BlockSpec((tc, tf), lambda c, f: (c, f)),
                   pl.BlockSpec((tc, tf), lambda c, f: (c, f)),
                   pl.BlockSpec((tc, LANES), lambda c, f: (c, 0))),
        scratch_shapes=[pltpu.VMEM((tc, seq), BF16), pltpu.VMEM((tc, seq), BF16)],
        compiler_params=_cparams(("parallel", "arbitrary"), VMEM_MEDIUM),
        name="filt_spec",
    )(ht, ht, sums, sums, rmat)


def _hyprep_kernel(u0_ref, u1_ref, u2_ref, w0_ref, w1_ref, w2_ref, b0_ref, b1_ref, b2_ref, x0_ref, wt_ref):
    seq = u0_ref.shape[0]
    row = lax.broadcasted_iota(jnp.int32, u0_ref.shape, 0)

    def sconv(u_ref, w_ref, b_ref):
        u = u_ref[...].astype(F32)
        um = jnp.where(row == 0, 0.0, pltpu.roll(u, 1, 0))
        up = jnp.where(row == seq - 1, 0.0, pltpu.roll(u, seq - 1, 0))
        w = w_ref[...]
        return b_ref[...] + um * w[0:1] + u * w[1:2] + up * w[2:3]

    x0_ref[...] = sconv(u0_ref, w0_ref, b0_ref).astype(BF16)
    prod = sconv(u1_ref, w1_ref, b1_ref) * sconv(u2_ref, w2_ref, b2_ref)
    wt_ref[...] = prod.T.astype(BF16)


def _hy_prep(z3, conv_w, conv_b):
    b, seq, _ = z3.shape
    tc = LANES if seq > LONG_SEQ else 2 * LANES
    nc = HY_CH // tc
    uspec = lambda g: pl.BlockSpec((None, seq, tc), lambda i, c: (i, 0, g * nc + c))
    wspec = lambda g: pl.BlockSpec((3, tc), lambda i, c: (0, g * nc + c))
    bspec = lambda g: pl.BlockSpec((1, tc), lambda i, c: (0, g * nc + c))
    return pl.pallas_call(
        _hyprep_kernel,
        out_shape=(jax.ShapeDtypeStruct((b, seq, HY_CH), BF16),
                   jax.ShapeDtypeStruct((b, HY_CH, seq), BF16)),
        grid=(b, nc),
        in_specs=[uspec(0), uspec(1), uspec(2), wspec(0), wspec(1), wspec(2), bspec(0), bspec(1), bspec(2)],
        out_specs=(pl.BlockSpec((None, seq, tc), lambda i, c: (i, 0, c)),
                   pl.BlockSpec((None, tc, seq), lambda i, c: (i, c, 0))),
        compiler_params=_cparams(("parallel", "parallel"), VMEM_MEDIUM),
        name="hy_prep",
    )(z3, z3, z3, conv_w, conv_w, conv_w, conv_b, conv_b, conv_b)


def _conv_kernel(wt_ref, ra_ref, rb_ref, u1_ref, v2_ref, kl_ref, d_ref, o_ref, acc_ref, za_ref, zb_ref, *, tf):
    t = pl.program_id(2)
    n_f = pl.num_programs(2) - 1

    def work(z_out, z_in, first=False):
        if z_in is not None:
            acc_ref[...] += jnp.dot(z_in[...], rb_ref[...], preferred_element_type=F32)
        if z_out is not None:
            ap = _dot_t(wt_ref[...], ra_ref[...])
            a = ap[:, :tf]
            p = ap[:, tf:]
            u1 = u1_ref[...]
            v2 = v2_ref[...]
            z2 = p * u1 + a * v2
            if first:
                col = lax.broadcasted_iota(jnp.int32, z2.shape, 1)
                z2 = jnp.where(col == 0, p * kl_ref[:, 0:1], z2)
            z_out[...] = jnp.concatenate([a * u1 - p * v2, z2], axis=1).astype(BF16)

    @pl.when(t == 0)
    def _():
        acc_ref[...] = jnp.zeros_like(acc_ref)
        work(za_ref, None, first=True)

    steady = jnp.logical_and(t > 0, t < n_f)

    @pl.when(jnp.logical_and(steady, t % 2 == 1))
    def _():
        work(zb_ref, za_ref)

    @pl.when(jnp.logical_and(steady, t % 2 == 0))
    def _():
        work(za_ref, zb_ref)

    def finish(z_in):
        work(None, z_in)
        y = acc_ref[...] + d_ref[:, 0:1] * wt_ref[...].astype(F32)
        o_ref[...] = y.T.astype(BF16)

    @pl.when(jnp.logical_and(t == n_f, t % 2 == 1))
    def _():
        finish(za_ref)

    @pl.when(jnp.logical_and(t == n_f, t % 2 == 0))
    def _():
        finish(zb_ref)


def _long_conv(wt, rmat, u1, v2, kl, dcol, tf):
    b, _, seq = wt.shape
    tc = CONV_CHANNELS
    n_f = seq // tf
    return pl.pallas_call(
        functools.partial(_conv_kernel, tf=tf),
        out_shape=jax.ShapeDtypeStruct((b, seq, HY_CH), BF16),
        grid=(b, HY_CH // tc, n_f + 1),
        in_specs=[
            pl.BlockSpec((None, tc, seq), lambda i, c, t: (i, c, 0)),
            pl.BlockSpec((2 * tf, seq), lambda i, c, t: (jnp.minimum(t, n_f - 1), 0)),
            pl.BlockSpec((2 * tf, seq), lambda i, c, t: (jnp.maximum(t - 1, 0), 0)),
            pl.BlockSpec((tc, tf), lambda i, c, t: (c, jnp.minimum(t, n_f - 1))),
            pl.BlockSpec((tc, tf), lambda i, c, t: (c, jnp.minimum(t, n_f - 1))),
            pl.BlockSpec((tc, LANES), lambda i, c, t: (c, 0)),
            pl.BlockSpec((tc, LANES), lambda i, c, t: (c, 0)),
        ],
        out_specs=pl.BlockSpec((None, seq, tc), lambda i, c, t: (i, 0, c)),
        scratch_shapes=[pltpu.VMEM((tc, seq), F32), pltpu.VMEM((tc, 2 * tf), BF16), pltpu.VMEM((tc, 2 * tf), BF16)],
        compiler_params=_cparams(("parallel", "parallel", "arbitrary"), VMEM_LARGE),
        name="long_conv",
    )(wt, rmat, rmat, u1, v2, kl, dcol)


def _dft_gen_kernel(ch_ref, sh_ref, cl_ref, sl_ref, r_ref, *, tf):
    ch, sh = ch_ref[...], sh_ref[...]
    cl, sl = cl_ref[...], sl_ref[...]
    r_ref[:tf, :] = (ch * cl - sh * sl).astype(BF16)
    sin = sh * cl + ch * sl
    row = lax.broadcasted_iota(jnp.int32, sin.shape, 0)
    lane = lax.broadcasted_iota(jnp.int32, sin.shape, 1)
    alt = (1 - 2 * (lane % 2)).astype(F32)
    nyq_row = jnp.logical_and(row == 0, pl.program_id(0) == 0)
    r_ref[tf:, :] = jnp.where(nyq_row, alt, sin).astype(BF16)


def _dft_tables(seq, tf):
    idx = jnp.arange(seq, dtype=jnp.int32)
    nf = seq // tf

    def trig(freqs):
        ang = ((freqs[:, None] * idx[None, :]) % (2 * seq)).astype(F32) * (math.pi / seq)
        return jnp.cos(ang), jnp.sin(ang)

    ch, sh = trig(jnp.arange(nf, dtype=jnp.int32) * tf)
    cl, sl = trig(jnp.arange(tf, dtype=jnp.int32))
    base_spec = pl.BlockSpec((None, 1, seq), lambda f: (f, 0, 0))
    return pl.pallas_call(
        functools.partial(_dft_gen_kernel, tf=tf),
        out_shape=jax.ShapeDtypeStruct((2 * seq, seq), BF16),
        grid=(nf,),
        in_specs=[base_spec, base_spec, _const_spec((tf, seq)), _const_spec((tf, seq))],
        out_specs=pl.BlockSpec((2 * tf, seq), lambda f: (f, 0)),
        compiler_params=_cparams(("parallel",), VMEM_SMALL),
        name="dft_tables",
    )(ch[:, None, :], sh[:, None, :], cl, sl)


KV_CHUNK = 256
ATT_Q_TILE = 512


def _attn_kernel(par_ref, q_ref, k_ref, v_ref, tab_ref, o_ref, krot_ref, vt_ref,
                 sa_buf, sb_buf, ma_buf, mb_buf,
                 *, tq, n_q, lam_init):
    g = pl.program_id(0)
    n_tiles = pl.num_programs(0) - 1
    t = g % n_q
    head_par = (g // n_q) % 2
    prev_par = (jnp.maximum(g - 1, 0) // n_q) % 2
    seq = k_ref.shape[0]
    ck = min(KV_CHUNK, seq)
    n_chunks = seq // ck

    def rope(x, rows):
        c, sa, sb = (tab_ref[rows, i * HEAD_W:(i + 1) * HEAD_W] for i in range(3))
        return x * c + pltpu.roll(x, LANES - ROPE_DIM // 2, 1) * sa + pltpu.roll(x, ROPE_DIM // 2, 1) * sb

    bufs = ((sa_buf, ma_buf), (sb_buf, mb_buf))

    def tile_work(score, finish, new_head=False):
        if new_head:
            krot_ref[...] = rope(k_ref[...].astype(F32), slice(None)).astype(BF16)
            vt_ref[head_par] = v_ref[...].astype(F32).T.astype(BF16)
        if score is not None:
            rows = pl.ds(pl.multiple_of(t * tq, tq), tq)
            q = rope(q_ref[rows, :].astype(F32), rows)
            q = q * (HEAD_DIM ** -0.5 * LOG2E)
            lane = lax.broadcasted_iota(jnp.int32, q.shape, 1)
            qq = jnp.concatenate([jnp.where(lane < HEAD_DIM, q, 0.0), jnp.where(lane >= HEAD_DIM, q, 0.0)], axis=0)
            qq = qq.astype(BF16)
            m8 = jnp.full((8, 2 * tq), -jnp.inf, F32)
        if finish is not None:
            m = jnp.max(finish[1][...], axis=0, keepdims=True)
            l8 = jnp.zeros((8, 2 * tq), F32)
            ot = jnp.zeros((HEAD_W, 2 * tq), F32)

        for c in range(n_chunks):
            kv = slice(c * ck, (c + 1) * ck)
            if finish is not None:
                p_c = jnp.exp2(finish[0][kv, :] - m)
                l8 = l8 + jnp.sum(p_c.reshape(ck // 8, 8, 2 * tq), axis=0)
                ot = ot + jnp.dot(vt_ref[prev_par, :, kv], p_c.astype(BF16), preferred_element_type=F32)
            if score is not None:
                s_c = _dot_t(krot_ref[kv, :], qq)
                score[0][kv, :] = s_c
                m8 = jnp.maximum(m8, jnp.max(s_c.reshape(ck // 8, 8, 2 * tq), axis=0))
        if score is not None:
            score[1][...] = m8
        if finish is not None:
            l = jnp.sum(l8, axis=0, keepdims=True)
            lp = par_ref[0:4, :]
            lam = (jnp.exp(jnp.sum(lp[0:1] * lp[1:2], axis=-1, keepdims=True))
                   - jnp.exp(jnp.sum(lp[2:3] * lp[3:4], axis=-1, keepdims=True)) + lam_init)
            r = 1.0 / l
            o = (ot[:, :tq] * r[:, :tq] - ot[:, tq:] * (lam * r[:, tq:])).T
            o_ref[...] = (_rms(o, par_ref[4:5, :]) * (1.0 - lam_init)).astype(BF16)

    @pl.when(g == 0)
    def _():
        tile_work(bufs[0], None, new_head=True)

    steady = jnp.logical_and(g > 0, g < n_tiles)
    for par in (0, 1):
        for opens_head in (False, True):
            if opens_head and par == 1 and n_q % 2 == 0:
                continue
            cond = jnp.logical_and(jnp.logical_and(steady, g % 2 == par), (t == 0) == opens_head)
            pl.when(cond)(functools.partial(tile_work, bufs[par], bufs[1 - par], new_head=opens_head))

    for par in (0, 1):
        pl.when(jnp.logical_and(g == n_tiles, g % 2 == par))(functools.partial(tile_work, None, bufs[1 - par]))


def _attention(z3, lam_params, g_subln, lam_init):
    b, seq, _ = z3.shape
    tq = min(ATT_Q_TILE, seq)
    n_q = seq // tq
    n_tiles = b * N_HEADS * n_q
    inv = ROPE_THETA ** (-jnp.arange(0, ROPE_DIM, 2, dtype=F32) / ROPE_DIM)
    ang = jnp.arange(seq, dtype=F32)[:, None] * inv[None, :]
    c8, s8 = jnp.cos(ang), jnp.sin(ang)
    half = ROPE_DIM // 2
    rest = HEAD_DIM - ROPE_DIM
    one, zero = jnp.ones((seq, rest), F32), jnp.zeros((seq, rest), F32)
    z8 = jnp.zeros((seq, half), F32)
    cos_t = jnp.tile(jnp.concatenate([c8, c8, one], axis=1), (1, 2))
    sa_t = jnp.tile(jnp.concatenate([-s8, z8, zero], axis=1), (1, 2))
    sb_t = jnp.tile(jnp.concatenate([z8, s8, zero], axis=1), (1, 2))
    tab = jnp.concatenate([cos_t, sa_t, sb_t], axis=1)
    par = jnp.concatenate([jnp.pad(lam_params, ((0, 0), (0, HEAD_W - HEAD_DIM))), g_subln,
                           jnp.zeros((3, HEAD_W), F32)], axis=0)
    qb, kb, vb = (HY_COLS // HEAD_W, (HY_COLS + ATT_WIDTH) // HEAD_W, (HY_COLS + 2 * ATT_WIDTH) // HEAD_W)

    def head_cols(blk):
        def index(g):
            head = jnp.minimum(g, n_tiles - 1) // n_q
            return head // N_HEADS, 0, blk + head % N_HEADS
        return pl.BlockSpec((None, seq, HEAD_W), index)

    def out_index(g):
        done = jnp.maximum(g - 1, 0)
        head = done // n_q
        return head // N_HEADS, done % n_q, head % N_HEADS

    return pl.pallas_call(
        functools.partial(_attn_kernel, tq=tq, n_q=n_q, lam_init=lam_init),
        out_shape=jax.ShapeDtypeStruct((b, seq, ATT_WIDTH), BF16),
        grid=(n_tiles + 1,),
        in_specs=[_const_spec((8, HEAD_W)), head_cols(qb), head_cols(kb), head_cols(vb),
                  _const_spec((seq, 3 * HEAD_W))],
        out_specs=pl.BlockSpec((None, tq, HEAD_W), out_index),
        scratch_shapes=[pltpu.VMEM((seq, HEAD_W), BF16), pltpu.VMEM((2, HEAD_W, seq), BF16),
                        pltpu.VMEM((seq, 2 * tq), F32), pltpu.VMEM((seq, 2 * tq), F32),
                        pltpu.VMEM((8, 2 * tq), F32), pltpu.VMEM((8, 2 * tq), F32)],
        compiler_params=_cparams(("arbitrary",), VMEM_LARGE),
        name="attention",
    )(par, z3, z3, z3, tab)


def _merge_kernel(yc_ref, x0_ref, ya_ref, g0_ref, g1_ref, x_ref, why_ref, wat_ref, wo_ref, gp_ref, o_ref):
    def body(rows):
        yh = (x0_ref[rows, :].astype(F32) * yc_ref[rows, :].astype(F32)).astype(BF16)
        a = jnp.dot(yh, why_ref[...], preferred_element_type=F32)
        b = jnp.dot(ya_ref[rows, :], wat_ref[...], preferred_element_type=F32)
        m = g0_ref[rows, :].astype(F32) * a + g1_ref[rows, :].astype(F32) * b
        r = jnp.dot(m.astype(BF16), wo_ref[...], preferred_element_type=F32)
        o_ref[rows, :] = x_ref[rows, :] + _rms(r, gp_ref[...])
    chunk = min(2 * ROW_CHUNK, x_ref.shape[0])
    for c in range(x_ref.shape[0] // chunk):
        body(pl.ds(c * chunk, chunk))


def _merge(yconv, x0c, yatt, z, x2d, w_hy, w_att, w_out, g_post):
    m, d = x2d.shape
    tm = min(MERGE_ROWS, m)
    row = lambda width, blk=0: pl.BlockSpec((tm, width), lambda i: (i, blk))
    gate_blk = GATE_COL0 // d
    return pl.pallas_call(
        _merge_kernel,
        out_shape=jax.ShapeDtypeStruct((m, d), F32),
        grid=(m // tm,),
        in_specs=[row(HY_CH), row(HY_CH), row(ATT_WIDTH), row(d, gate_blk), row(d, gate_blk + 1), row(d),
                  _const_spec((HY_CH, d)), _const_spec((ATT_WIDTH, d)), _const_spec((d, d)), _const_spec((1, d))],
        out_specs=row(d),
        compiler_params=_cparams(("parallel",), VMEM_LARGE),
        name="merge",
    )(yconv, x0c, yatt, z, z, x2d, w_hy, w_att, w_out, g_post)


def _ffn_kernel(x_ref, p_ref, gpre_ref, wg_ref, wu_ref, wo_ref, gpost_ref, gple_ref, wpg_ref, wpi_ref,
                o_ref, h_ref, acc_ref):
    j = pl.program_id(1)

    @pl.when(j == 0)
    def _():
        _rms_rows_to(h_ref, x_ref, gpre_ref)
        acc_ref[...] = jnp.zeros_like(acc_ref)

    h = h_ref[...]
    half = wg_ref.shape[1] // 2
    part = None
    for c in range(2):
        cols = slice(c * half, (c + 1) * half)
        gate = jnp.dot(h, wg_ref[:, cols], preferred_element_type=F32)
        up = jnp.dot(h, wu_ref[:, cols], preferred_element_type=F32)
        act = (gate * _sigmoid(gate) * up).astype(BF16)
        down = jnp.dot(act, wo_ref[cols, :], preferred_element_type=F32)
        part = down if part is None else part + down
    acc_ref[...] += part

    @pl.when(j == pl.num_programs(1) - 1)
    def _():
        def body(rows):
            x2 = x_ref[rows, :] + _rms(acc_ref[rows, :], gpost_ref[...])
            e = jnp.dot(_rms(x2, gple_ref[...]).astype(BF16), wpg_ref[...], preferred_element_type=F32)
            pe = jnp.dot(p_ref[rows, :].astype(BF16), wpi_ref[...], preferred_element_type=F32)
            o_ref[rows, :] = x2 + pe * _sigmoid(e)
        _for_row_chunks(x_ref.shape[0], body, chunk=2 * ROW_CHUNK)


def _ffn_ple(x1, p2d, g_pre, w_in, w_out, g_post, g_ple, w_pg, w_pi):
    m, d = x1.shape
    tm = min(FFN_ROWS, m)
    tf = FFN_TILE
    nff = D_FF // tf
    return pl.pallas_call(
        _ffn_kernel,
        out_shape=jax.ShapeDtypeStruct((m, d), F32),
        grid=(m // tm, nff),
        in_specs=[
            pl.BlockSpec((tm, d), lambda i, j: (i, 0)),
            pl.BlockSpec((tm, PLE_DIM), lambda i, j: (i, 0)),
            _const_spec((1, d)),
            pl.BlockSpec((d, tf), lambda i, j: (0, j)),
            pl.BlockSpec((d, tf), lambda i, j: (0, nff + j)),
            pl.BlockSpec((tf, d), lambda i, j: (j, 0)),
            _const_spec((1, d)), _const_spec((1, d)),
            _const_spec((d, d)), _const_spec((PLE_DIM, d)),
        ],
        out_specs=pl.BlockSpec((tm, d), lambda i, j: (i, 0)),
        scratch_shapes=[pltpu.VMEM((tm, d), BF16), pltpu.VMEM((tm, d), F32)],
        compiler_params=_cparams(("parallel", "arbitrary"), VMEM_LARGE),
        name="ffn_ple",
    )(x1, p2d, g_pre, w_in, w_in, w_out, g_post, g_ple, w_pg, w_pi)


def _layer(x, p, lam_init, wts):
    b, seq, d = x.shape
    x2d = x.reshape(b * seq, d)
    z = _in_proj(x2d, wts["g_mix_pre"], wts["w_in"], wts["b_gate"])
    z3 = z.reshape(b, seq, IN_COLS)

    tf = min(FREQ_TILE if seq > LONG_SEQ else 2 * FREQ_TILE, seq)
    rmat = _dft_tables(seq, tf)
    ht, sums = _filt_mlp(seq, wts["filt_w1"], wts["filt_b1"], wts["filt_w2"], wts["filt_b2"],
                         wts["filt_freq"], wts["filt_w3"])
    u1, v2, kl = _filt_spec(ht, sums, rmat, tf)
    x0c, wt = _hy_prep(z3, wts["conv_w"], wts["conv_b"])
    yconv = _long_conv(wt, rmat, u1, v2, kl, wts["hyena_d"], tf)

    yatt = _attention(z3, wts["lam"], wts["g_subln"], lam_init)

    x1 = _merge(yconv.reshape(b * seq, HY_CH), x0c.reshape(b * seq, HY_CH), yatt.reshape(b * seq, ATT_WIDTH),
                z, x2d, wts["w_hy_out"], wts["w_att_out"], wts["w_out"], wts["g_mix_post"])
    y = _ffn_ple(x1, p.reshape(b * seq, PLE_DIM), wts["g_ffn_pre"], wts["w_ffn_in"], wts["w_ffn_out"],
                 wts["g_ffn_post"], wts["g_ple"], wts["w_ple_gate"], wts["w_ple_in"])
    return y.reshape(b, seq, d)


def kernel(x_prompt, x_sample, p_prompt, p_sample, g_mix_pre, g_mix_post, g_ffn_pre, g_ffn_post, g_ple, w_in, b_gate, conv_w, conv_b, filt_w1, filt_b1, filt_w2, filt_b2, filt_freq, filt_w3, hyena_d, lam_q1, lam_k1, lam_q2, lam_k2, g_subln, w_hy_out, w_att_out, w_out, w_ffn_in, w_ffn_out, w_ple_in, w_ple_gate):
    depth = w_in.shape[0]
    xs = [x_prompt, x_sample]
    ps = [p_prompt, p_sample]
    for i in range(depth):
        lam_init = 0.8 - 0.6 * math.exp(-0.3 * i)
        wts = dict(
            g_mix_pre=g_mix_pre[i][None, :], g_mix_post=g_mix_post[i][None, :],
            g_ffn_pre=g_ffn_pre[i][None, :], g_ffn_post=g_ffn_post[i][None, :], g_ple=g_ple[i][None, :],
            w_in=w_in[i].astype(BF16), b_gate=b_gate[i].reshape(1, 2 * D_MODEL),
            conv_w=conv_w[i], conv_b=conv_b[i][None, :],
            filt_w1=filt_w1[i], filt_b1=filt_b1[i], filt_w2=filt_w2[i], filt_b2=filt_b2[i],
            filt_freq=filt_freq[i], filt_w3=filt_w3[i],
            hyena_d=jnp.broadcast_to(hyena_d[i][:, None], (HY_CH, LANES)),
            lam=jnp.stack([lam_q1[i], lam_k1[i], lam_q2[i], lam_k2[i]]),
            g_subln=g_subln[i][None, :],
            w_hy_out=w_hy_out[i].astype(BF16), w_att_out=w_att_out[i].astype(BF16),
            w_out=w_out[i].astype(BF16), w_ffn_in=w_ffn_in[i].astype(BF16),
            w_ffn_out=w_ffn_out[i].astype(BF16), w_ple_in=w_ple_in[i].astype(BF16),
            w_ple_gate=w_ple_gate[i].astype(BF16),
        )
        xs = [_layer(x, p[i], lam_init, wts) for x, p in zip(xs, ps)]
    return (xs[0], xs[1])
```

```python
import functools
import math

import jax
import jax.numpy as jnp
from jax import lax
from jax.experimental import pallas as pl
from jax.experimental.pallas import tpu as pltpu

F32 = jnp.float32
BF16 = jnp.bfloat16

D_MODEL = 2048
PLE_DIM = 256
NORM_EPS = 1e-6
HY_CH = D_MODEL // 2
FILT_BANDS = 16
FILT_EMB = 1 + 2 * FILT_BANDS
FILT_ORDER = 64
FILT_MAX_DECAY = math.log(1e-2) / 0.3
FILT_MIN_DECAY = math.log(1e-2) / 1.5
ATT_WIDTH = D_MODEL // 2
N_HEADS = 8
HEAD_DIM = ATT_WIDTH // (2 * N_HEADS)
HEAD_W = 2 * HEAD_DIM
ROPE_DIM = HEAD_DIM // 4
ROPE_THETA = 500000.0
D_FF = ((8 * D_MODEL + 3 * 256 - 1) // (3 * 256)) * 256
HY_COLS = 3 * HY_CH
GATE_COL0 = HY_COLS + 3 * ATT_WIDTH
IN_COLS = GATE_COL0 + 2 * D_MODEL
LANES = 128
LOG2E = math.log2(math.e)
MIB = 2 ** 20

LONG_SEQ = 2048
FREQ_TILE = 256
IN_PROJ_ROWS, IN_PROJ_COLS = 1024, 2048
MERGE_ROWS = 512
FFN_ROWS, FFN_TILE = 512, 512
CONV_CHANNELS = 512
FILT_CHANNELS = 256
FILT_ROWS = 512
VMEM_SMALL, VMEM_MEDIUM, VMEM_LARGE, VMEM_MAX = 32, 48, 56, 58


def _cparams(sem, vmem_mib):
    return pltpu.CompilerParams(dimension_semantics=sem, vmem_limit_bytes=vmem_mib * MIB)


def _const_spec(shape):
    nd = len(shape)
    return pl.BlockSpec(shape, lambda *_: (0,) * nd, pipeline_mode=pl.Buffered(1))


def _rms(x, g):
    ms = jnp.mean(x * x, axis=-1, keepdims=True)
    return x * lax.rsqrt(ms + NORM_EPS) * g


def _sigmoid(x):
    return 1.0 / (1.0 + jnp.exp(-x))


ROW_CHUNK = 128


def _for_row_chunks(n_rows, body, chunk=ROW_CHUNK):
    chunk = min(chunk, n_rows)

    def step(c, carry):
        body(pl.ds(pl.multiple_of(c * chunk, chunk), chunk))
        return carry
    lax.fori_loop(0, n_rows // chunk, step, 0)


def _rms_rows_to(h_ref, x_ref, g_ref):
    def body(rows):
        h_ref[rows, :] = _rms(x_ref[rows, :], g_ref[...]).astype(h_ref.dtype)
    _for_row_chunks(x_ref.shape[0], body)


def _inproj_kernel(x_ref, g_ref, w_ref, b_ref, o_ref, h_ref, *, n_plain):
    j = pl.program_id(1)

    @pl.when(j == 0)
    def _():
        _rms_rows_to(h_ref, x_ref, g_ref)

    acc = jnp.dot(h_ref[...], w_ref[...], preferred_element_type=F32)
    o_ref[...] = jnp.where(j >= n_plain, _sigmoid(acc + b_ref[...]), acc).astype(BF16)


def _in_proj(x2d, g, w_bf, b_flat):
    m, d = x2d.shape
    n = w_bf.shape[1]
    tm = min(IN_PROJ_ROWS, m)
    tn = IN_PROJ_COLS
    n_plain = GATE_COL0 // tn
    return pl.pallas_call(
        functools.partial(_inproj_kernel, n_plain=n_plain),
        out_shape=jax.ShapeDtypeStruct((m, n), BF16),
        grid=(m // tm, n // tn),
        in_specs=[
            pl.BlockSpec((tm, d), lambda i, j: (i, 0)),
            pl.BlockSpec((1, d), lambda i, j: (0, 0)),
            pl.BlockSpec((d, tn), lambda i, j: (0, j)),
            pl.BlockSpec((1, tn), lambda i, j: (0, jnp.maximum(j - n_plain, 0))),
        ],
        out_specs=pl.BlockSpec((tm, tn), lambda i, j: (i, j)),
        scratch_shapes=[pltpu.VMEM((tm, d), BF16)],
        compiler_params=_cparams(("parallel", "arbitrary"), VMEM_MAX),
        name="in_proj",
    )(x2d, g, w_bf, b_flat)


def _filt_mlp_kernel(emb_ref, w1_ref, b1_ref, w2_ref, b2_ref, fr_ref, w3_ref, dl_ref, ht_ref, sum_ref):
    i = pl.program_id(0)
    hp = lax.Precision.HIGHEST
    emb = emb_ref[...]
    fr = fr_ref[...]
    h1 = jnp.sin(fr * (jnp.dot(emb, w1_ref[...], precision=hp, preferred_element_type=F32) + b1_ref[...]))
    h2 = jnp.sin(fr * (jnp.dot(h1, w2_ref[...], precision=hp, preferred_element_type=F32) + b2_ref[...]))
    h = jnp.dot(h2, w3_ref[...], precision=hp, preferred_element_type=F32)
    dec = jnp.exp(-emb[:, 0:1] * dl_ref[...])
    ht = jnp.concatenate([h[:, :HY_CH] * dec, h[:, HY_CH:] * dec], axis=1).T
    ht_ref[...] = ht
    ab = jnp.abs(ht)
    part = ab[:, 0:LANES]
    for c in range(1, ab.shape[1] // LANES):
        part = part + ab[:, c * LANES:(c + 1) * LANES]

    @pl.when(i == 0)
    def _():
        sum_ref[...] = part

    @pl.when(i > 0)
    def _():
        sum_ref[...] += part


def _filt_mlp(seq, w1, b1, w2, b2, freq, w3):
    t = jnp.linspace(0.0, 1.0, seq, dtype=F32)[:, None]
    wpos = 2.0 * math.pi * jnp.arange(seq, dtype=F32) / seq
    bands = jnp.linspace(1e-4, FILT_BANDS - 1, FILT_BANDS, dtype=F32)
    ang = wpos[:, None] * bands[None, :]
    emb = jnp.concatenate([t, jnp.cos(ang), -jnp.sin(ang)], axis=-1)
    emb = jnp.pad(emb, ((0, 0), (0, LANES - FILT_EMB)))
    pad_o = LANES - FILT_ORDER
    w1p = jnp.pad(w1, ((0, LANES - FILT_EMB), (0, pad_o)))
    w2p = jnp.pad(w2, ((0, pad_o), (0, pad_o)))
    w3p = jnp.pad(w3, ((0, pad_o), (0, 0)))
    b1p = jnp.pad(b1[None, :], ((0, 0), (0, pad_o)))
    b2p = jnp.pad(b2[None, :], ((0, 0), (0, pad_o)))
    frp = jnp.pad(freq[None, :], ((0, 0), (0, pad_o)))
    deltas = jnp.abs(jnp.linspace(FILT_MIN_DECAY, FILT_MAX_DECAY, HY_CH, dtype=F32))[None, :]
    tl = min(FILT_ROWS, seq)
    return pl.pallas_call(
        _filt_mlp_kernel,
        out_shape=(jax.ShapeDtypeStruct((2 * HY_CH, seq), F32),
                   jax.ShapeDtypeStruct((2 * HY_CH, LANES), F32)),
        grid=(seq // tl,),
        in_specs=[
            pl.BlockSpec((tl, LANES), lambda i: (i, 0)),
            _const_spec((LANES, LANES)), _const_spec((1, LANES)),
            _const_spec((LANES, LANES)), _const_spec((1, LANES)),
            _const_spec((1, LANES)), _const_spec((LANES, 2 * HY_CH)),
            _const_spec((1, HY_CH)),
        ],
        out_specs=(pl.BlockSpec((2 * HY_CH, tl), lambda i: (0, i)),
                   pl.BlockSpec((2 * HY_CH, LANES), lambda i: (0, 0))),
        compiler_params=_cparams(("arbitrary",), VMEM_MEDIUM),
        name="filt_mlp",
    )(emb, w1p, b1p, w2p, b2p, frp, w3p, deltas)


def _dot_t(a, b):
    return lax.dot_general(a, b, (((1,), (1,)), ((), ())), preferred_element_type=F32)


def _filt_spec_kernel(hf_ref, hb_ref, sf_ref, sb_ref, r_ref, d_ref, u1_ref, v2_ref, kl_ref, s1_ref, s2_ref,
                      *, tf, seq):
    f = pl.program_id(1)
    d = d_ref[:, 0:1]

    @pl.when(f == 0)
    def _():
        norm = jnp.sum(sf_ref[...] + sb_ref[...], axis=1, keepdims=True) + NORM_EPS
        hf = hf_ref[...] / norm
        lane = lax.broadcasted_iota(jnp.int32, hf.shape, 1)
        hb = jnp.where(lane == 0, 0.0, hb_ref[...] / norm)
        s1_ref[...] = (hf + hb).astype(BF16)
        s2_ref[...] = (hf - hb).astype(BF16)

    s1 = s1_ref[...]
    s2 = s2_ref[...]
    c = (_dot_t(s1, r_ref[:tf, :]) + d) * (1.0 / seq)
    col = lax.broadcasted_iota(jnp.int32, c.shape, 1) + f * tf
    u1_ref[...] = jnp.where(col == 0, 0.5 * c, c)
    v2_ref[...] = jnp.where(col == 0, 0.0, _dot_t(s2, r_ref[tf:, :]) * (1.0 / seq))

    @pl.when(f == 0)
    def _():
        nyq = (_dot_t(s1, r_ref[tf:, :])[:, 0:1] + d) * (0.5 / seq)
        kl_ref[...] = jnp.broadcast_to(nyq, kl_ref.shape)


def _filt_spec(ht, sums, rmat, dcol, tf):
    seq = ht.shape[1]
    tc = FILT_CHANNELS
    nc = HY_CH // tc
    return pl.pallas_call(
        functools.partial(_filt_spec_kernel, tf=tf, seq=seq),
        out_shape=(jax.ShapeDtypeStruct((HY_CH, seq), F32),
                   jax.ShapeDtypeStruct((HY_CH, seq), F32),
                   jax.ShapeDtypeStruct((HY_CH, LANES), F32)),
        grid=(nc, seq // tf),
        in_specs=[
            pl.BlockSpec((tc, seq), lambda c, f: (c, 0)),
            pl.BlockSpec((tc, seq), lambda c, f: (nc + c, 0)),
            pl.BlockSpec((tc, LANES), lambda c, f: (c, 0)),
            pl.BlockSpec((tc, LANES), lambda c, f: (nc + c, 0)),
            pl.BlockSpec((2 * tf, seq), lambda c, f: (f, 0)),
            pl.BlockSpec((tc, LANES), lambda c, f: (c, 0)),
        ],
        out_specs=(pl.BlockSpec((tc, tf), lambda c, f: (c, f)),
                   pl.BlockSpec((tc, tf), lambda c, f: (c, f)),
                   pl.BlockSpec((tc, LANES), lambda c, f: (c, 0))),
        scratch_shapes=[pltpu.VMEM((tc, seq), BF16), pltpu.VMEM((tc, seq), BF16)],
        compiler_params=_cparams(("parallel", "arbitrary"), VMEM_MEDIUM),
        name="filt_spec",
    )(ht, ht, sums, sums, rmat, dcol)


def _hyprep_kernel(u0_ref, u1_ref, u2_ref, w0_ref, w1_ref, w2_ref, b0_ref, b1_ref, b2_ref, x0_ref, wt_ref):
    seq = u0_ref.shape[0]
    row = lax.broadcasted_iota(jnp.int32, u0_ref.shape, 0)

    def sconv(u_ref, w_ref, b_ref):
        u = u_ref[...].astype(F32)
        um = jnp.where(row == 0, 0.0, pltpu.roll(u, 1, 0))
        up = jnp.where(row == seq - 1, 0.0, pltpu.roll(u, seq - 1, 0))
        w = w_ref[...]
        return b_ref[...] + um * w[0:1] + u * w[1:2] + up * w[2:3]

    x0_ref[...] = sconv(u0_ref, w0_ref, b0_ref).astype(BF16)
    prod = sconv(u1_ref, w1_ref, b1_ref) * sconv(u2_ref, w2_ref, b2_ref)
    wt_ref[...] = prod.T.astype(BF16)


def _hy_prep(z3, conv_w, conv_b):
    b, seq, _ = z3.shape
    tc = LANES if seq > LONG_SEQ else 2 * LANES
    nc = HY_CH // tc
    uspec = lambda g: pl.BlockSpec((None, seq, tc), lambda i, c: (i, 0, g * nc + c))
    wspec = lambda g: pl.BlockSpec((3, tc), lambda i, c: (0, g * nc + c))
    bspec = lambda g: pl.BlockSpec((1, tc), lambda i, c: (0, g * nc + c))
    return pl.pallas_call(
        _hyprep_kernel,
        out_shape=(jax.ShapeDtypeStruct((b, seq, HY_CH), BF16),
                   jax.ShapeDtypeStruct((b, HY_CH, seq), BF16)),
        grid=(b, nc),
        in_specs=[uspec(0), uspec(1), uspec(2), wspec(0), wspec(1), wspec(2), bspec(0), bspec(1), bspec(2)],
        out_specs=(pl.BlockSpec((None, seq, tc), lambda i, c: (i, 0, c)),
                   pl.BlockSpec((None, tc, seq), lambda i, c: (i, c, 0))),
        compiler_params=_cparams(("parallel", "parallel"), VMEM_MEDIUM),
        name="hy_prep",
    )(z3, z3, z3, conv_w, conv_w, conv_w, conv_b, conv_b, conv_b)


def _conv_kernel(wt_ref, ra_ref, rb_ref, u1_ref, v2_ref, kl_ref, o_ref, acc_ref, za_ref, zb_ref, *, tf):
    t = pl.program_id(2)
    n_f = pl.num_programs(2) - 1

    def work(z_out, z_in, first=False):
        if z_in is not None:
            acc_ref[...] += jnp.dot(z_in[...], rb_ref[...], preferred_element_type=F32)
        if z_out is not None:
            ap = _dot_t(wt_ref[...], ra_ref[...])
            a = ap[:, :tf]
            p = ap[:, tf:]
            u1 = u1_ref[...]
            v2 = v2_ref[...]
            z2 = p * u1 + a * v2
            if first:
                col = lax.broadcasted_iota(jnp.int32, z2.shape, 1)
                z2 = jnp.where(col == 0, p * kl_ref[:, 0:1], z2)
            z_out[...] = jnp.concatenate([a * u1 - p * v2, z2], axis=1).astype(BF16)

    @pl.when(t == 0)
    def _():
        acc_ref[...] = jnp.zeros_like(acc_ref)
        work(za_ref, None, first=True)

    steady = jnp.logical_and(t > 0, t < n_f)

    @pl.when(jnp.logical_and(steady, t % 2 == 1))
    def _():
        work(zb_ref, za_ref)

    @pl.when(jnp.logical_and(steady, t % 2 == 0))
    def _():
        work(za_ref, zb_ref)

    def finish(z_in):
        work(None, z_in)
        o_ref[...] = acc_ref[...].T.astype(BF16)

    @pl.when(jnp.logical_and(t == n_f, t % 2 == 1))
    def _():
        finish(za_ref)

    @pl.when(jnp.logical_and(t == n_f, t % 2 == 0))
    def _():
        finish(zb_ref)


def _long_conv(wt, rmat, u1, v2, kl, tf):
    b, _, seq = wt.shape
    tc = CONV_CHANNELS
    n_f = seq // tf
    return pl.pallas_call(
        functools.partial(_conv_kernel, tf=tf),
        out_shape=jax.ShapeDtypeStruct((b, seq, HY_CH), BF16),
        grid=(b, HY_CH // tc, n_f + 1),
        in_specs=[
            pl.BlockSpec((None, tc, seq), lambda i, c, t: (i, c, 0)),
            pl.BlockSpec((2 * tf, seq), lambda i, c, t: (jnp.minimum(t, n_f - 1), 0)),
            pl.BlockSpec((2 * tf, seq), lambda i, c, t: (jnp.maximum(t - 1, 0), 0)),
            pl.BlockSpec((tc, tf), lambda i, c, t: (c, jnp.minimum(t, n_f - 1))),
            pl.BlockSpec((tc, tf), lambda i, c, t: (c, jnp.minimum(t, n_f - 1))),
            pl.BlockSpec((tc, LANES), lambda i, c, t: (c, 0)),
        ],
        out_specs=pl.BlockSpec((None, seq, tc), lambda i, c, t: (i, 0, c)),
        scratch_shapes=[pltpu.VMEM((tc, seq), F32), pltpu.VMEM((tc, 2 * tf), BF16), pltpu.VMEM((tc, 2 * tf), BF16)],
        compiler_params=_cparams(("parallel", "parallel", "arbitrary"), VMEM_LARGE),
        name="long_conv",
    )(wt, rmat, rmat, u1, v2, kl)


def _dft_gen_kernel(ch_ref, sh_ref, cl_ref, sl_ref, r_ref, *, tf):
    ch, sh = ch_ref[...], sh_ref[...]
    cl, sl = cl_ref[...], sl_ref[...]
    r_ref[:tf, :] = (ch * cl - sh * sl).astype(BF16)
    sin = sh * cl + ch * sl
    row = lax.broadcasted_iota(jnp.int32, sin.shape, 0)
    lane = lax.broadcasted_iota(jnp.int32, sin.shape, 1)
    alt = (1 - 2 * (lane % 2)).astype(F32)
    nyq_row = jnp.logical_and(row == 0, pl.program_id(0) == 0)
    r_ref[tf:, :] = jnp.where(nyq_row, alt, sin).astype(BF16)


def _dft_tables(seq, tf):
    idx = jnp.arange(seq, dtype=jnp.int32)
    nf = seq // tf

    def trig(freqs):
        ang = ((freqs[:, None] * idx[None, :]) % (2 * seq)).astype(F32) * (math.pi / seq)
        return jnp.cos(ang), jnp.sin(ang)

    ch, sh = trig(jnp.arange(nf, dtype=jnp.int32) * tf)
    cl, sl = trig(jnp.arange(tf, dtype=jnp.int32))
    base_spec = pl.BlockSpec((None, 1, seq), lambda f: (f, 0, 0))
    return pl.pallas_call(
        functools.partial(_dft_gen_kernel, tf=tf),
        out_shape=jax.ShapeDtypeStruct((2 * seq, seq), BF16),
        grid=(nf,),
        in_specs=[base_spec, base_spec, _const_spec((tf, seq)), _const_spec((tf, seq))],
        out_specs=pl.BlockSpec((2 * tf, seq), lambda f: (f, 0)),
        compiler_params=_cparams(("parallel",), VMEM_SMALL),
        name="dft_tables",
    )(ch[:, None, :], sh[:, None, :], cl, sl)


KV_CHUNK = 256
ATT_Q_TILE = 512


def _attn_kernel(par_ref, q_ref, k_ref, v_ref, tab_ref, o_ref, krot_ref, vt_ref,
                 sa_buf, sb_buf, ma_buf, mb_buf,
                 *, tq, n_q, lam_init):
    g = pl.program_id(0)
    n_tiles = pl.num_programs(0) - 1
    t = g % n_q
    head_par = (g // n_q) % 2
    prev_par = (jnp.maximum(g - 1, 0) // n_q) % 2
    seq = k_ref.shape[0]
    ck = min(KV_CHUNK, seq)
    n_chunks = seq // ck

    def rope(x, rows):
        c, sa, sb = (tab_ref[rows, i * HEAD_W:(i + 1) * HEAD_W] for i in range(3))
        return x * c + pltpu.roll(x, LANES - ROPE_DIM // 2, 1) * sa + pltpu.roll(x, ROPE_DIM // 2, 1) * sb

    bufs = ((sa_buf, ma_buf), (sb_buf, mb_buf))

    def tile_work(score, finish, new_head=False):
        if new_head:
            krot_ref[...] = rope(k_ref[...].astype(F32), slice(None)).astype(BF16)
            vt_ref[head_par] = v_ref[...].astype(F32).T.astype(BF16)
        if score is not None:
            rows = pl.ds(pl.multiple_of(t * tq, tq), tq)
            q = rope(q_ref[rows, :].astype(F32), rows)
            q = q * (HEAD_DIM ** -0.5 * LOG2E)
            lane = lax.broadcasted_iota(jnp.int32, q.shape, 1)
            qq = jnp.concatenate([jnp.where(lane < HEAD_DIM, q, 0.0), jnp.where(lane >= HEAD_DIM, q, 0.0)], axis=0)
            qq = qq.astype(BF16)
            m8 = jnp.full((8, 2 * tq), -jnp.inf, F32)
        if finish is not None:
            m = jnp.max(finish[1][...], axis=0, keepdims=True)
            l8 = jnp.zeros((8, 2 * tq), F32)
            ot = jnp.zeros((HEAD_W, 2 * tq), F32)

        for c in range(n_chunks):
            kv = slice(c * ck, (c + 1) * ck)
            if finish is not None:
                p_c = jnp.exp2(finish[0][kv, :] - m)
                l8 = l8 + jnp.sum(p_c.reshape(ck // 8, 8, 2 * tq), axis=0)
                ot = ot + jnp.dot(vt_ref[prev_par, :, kv], p_c.astype(BF16), preferred_element_type=F32)
            if score is not None:
                s_c = _dot_t(krot_ref[kv, :], qq)
                score[0][kv, :] = s_c
                m8 = jnp.maximum(m8, jnp.max(s_c.reshape(ck // 8, 8, 2 * tq), axis=0))
        if score is not None:
            score[1][...] = m8
        if finish is not None:
            l = jnp.sum(l8, axis=0, keepdims=True)
            lp = par_ref[0:4, :]
            lam = (jnp.exp(jnp.sum(lp[0:1] * lp[1:2], axis=-1, keepdims=True))
                   - jnp.exp(jnp.sum(lp[2:3] * lp[3:4], axis=-1, keepdims=True)) + lam_init)
            r = 1.0 / l
            o = (ot[:, :tq] * r[:, :tq] - ot[:, tq:] * (lam * r[:, tq:])).T
            o_ref[...] = (_rms(o, par_ref[4:5, :]) * (1.0 - lam_init)).astype(BF16)

    @pl.when(g == 0)
    def _():
        tile_work(bufs[0], None, new_head=True)

    steady = jnp.logical_and(g > 0, g < n_tiles)
    for par in (0, 1):
        for opens_head in (False, True):
            if opens_head and par == 1 and n_q % 2 == 0:
                continue
            cond = jnp.logical_and(jnp.logical_and(steady, g % 2 == par), (t == 0) == opens_head)
            pl.when(cond)(functools.partial(tile_work, bufs[par], bufs[1 - par], new_head=opens_head))

    for par in (0, 1):
        pl.when(jnp.logical_and(g == n_tiles, g % 2 == par))(functools.partial(tile_work, None, bufs[1 - par]))


def _attention(z3, lam_params, g_subln, lam_init):
    b, seq, _ = z3.shape
    tq = min(ATT_Q_TILE, seq)
    n_q = seq // tq
    n_tiles = b * N_HEADS * n_q
    inv = ROPE_THETA ** (-jnp.arange(0, ROPE_DIM, 2, dtype=F32) / ROPE_DIM)
    ang = jnp.arange(seq, dtype=F32)[:, None] * inv[None, :]
    c8, s8 = jnp.cos(ang), jnp.sin(ang)
    half = ROPE_DIM // 2
    rest = HEAD_DIM - ROPE_DIM
    one, zero = jnp.ones((seq, rest), F32), jnp.zeros((seq, rest), F32)
    z8 = jnp.zeros((seq, half), F32)
    cos_t = jnp.tile(jnp.concatenate([c8, c8, one], axis=1), (1, 2))
    sa_t = jnp.tile(jnp.concatenate([-s8, z8, zero], axis=1), (1, 2))
    sb_t = jnp.tile(jnp.concatenate([z8, s8, zero], axis=1), (1, 2))
    tab = jnp.concatenate([cos_t, sa_t, sb_t], axis=1)
    par = jnp.concatenate([jnp.pad(lam_params, ((0, 0), (0, HEAD_W - HEAD_DIM))), g_subln,
                           jnp.zeros((3, HEAD_W), F32)], axis=0)
    qb, kb, vb = (HY_COLS // HEAD_W, (HY_COLS + ATT_WIDTH) // HEAD_W, (HY_COLS + 2 * ATT_WIDTH) // HEAD_W)

    def head_cols(blk):
        def index(g):
            head = jnp.minimum(g, n_tiles - 1) // n_q
            return head // N_HEADS, 0, blk + head % N_HEADS
        return pl.BlockSpec((None, seq, HEAD_W), index)

    def out_index(g):
        done = jnp.maximum(g - 1, 0)
        head = done // n_q
        return head // N_HEADS, done % n_q, head % N_HEADS

    return pl.pallas_call(
        functools.partial(_attn_kernel, tq=tq, n_q=n_q, lam_init=lam_init),
        out_shape=jax.ShapeDtypeStruct((b, seq, ATT_WIDTH), BF16),
        grid=(n_tiles + 1,),
        in_specs=[_const_spec((8, HEAD_W)), head_cols(qb), head_cols(kb), head_cols(vb),
                  _const_spec((seq, 3 * HEAD_W))],
        out_specs=pl.BlockSpec((None, tq, HEAD_W), out_index),
        scratch_shapes=[pltpu.VMEM((seq, HEAD_W), BF16), pltpu.VMEM((2, HEAD_W, seq), BF16),
                        pltpu.VMEM((seq, 2 * tq), F32), pltpu.VMEM((seq, 2 * tq), F32),
                        pltpu.VMEM((8, 2 * tq), F32), pltpu.VMEM((8, 2 * tq), F32)],
        compiler_params=_cparams(("arbitrary",), VMEM_LARGE),
        name="attention",
    )(par, z3, z3, z3, tab)


def _merge_kernel(yc_ref, x0_ref, ya_ref, g0_ref, g1_ref, x_ref, why_ref, wat_ref, wo_ref, gp_ref, o_ref):
    def body(rows):
        yh = (x0_ref[rows, :].astype(F32) * yc_ref[rows, :].astype(F32)).astype(BF16)
        a = jnp.dot(yh, why_ref[...], preferred_element_type=F32)
        b = jnp.dot(ya_ref[rows, :], wat_ref[...], preferred_element_type=F32)
        m = g0_ref[rows, :].astype(F32) * a + g1_ref[rows, :].astype(F32) * b
        r = jnp.dot(m.astype(BF16), wo_ref[...], preferred_element_type=F32)
        o_ref[rows, :] = x_ref[rows, :] + _rms(r, gp_ref[...])
    _for_row_chunks(x_ref.shape[0], body, chunk=2 * ROW_CHUNK)


def _merge(yconv, x0c, yatt, z, x2d, w_hy, w_att, w_out, g_post):
    m, d = x2d.shape
    tm = min(MERGE_ROWS, m)
    row = lambda width, blk=0: pl.BlockSpec((tm, width), lambda i: (i, blk))
    gate_blk = GATE_COL0 // d
    return pl.pallas_call(
        _merge_kernel,
        out_shape=jax.ShapeDtypeStruct((m, d), F32),
        grid=(m // tm,),
        in_specs=[row(HY_CH), row(HY_CH), row(ATT_WIDTH), row(d, gate_blk), row(d, gate_blk + 1), row(d),
                  _const_spec((HY_CH, d)), _const_spec((ATT_WIDTH, d)), _const_spec((d, d)), _const_spec((1, d))],
        out_specs=row(d),
        compiler_params=_cparams(("parallel",), VMEM_LARGE),
        name="merge",
    )(yconv, x0c, yatt, z, z, x2d, w_hy, w_att, w_out, g_post)


def _ffn_kernel(x_ref, p_ref, gpre_ref, wg_ref, wu_ref, wo_ref, gpost_ref, gple_ref, wpg_ref, wpi_ref,
                o_ref, h_ref, acc_ref):
    j = pl.program_id(1)

    @pl.when(j == 0)
    def _():
        _rms_rows_to(h_ref, x_ref, gpre_ref)
        acc_ref[...] = jnp.zeros_like(acc_ref)

    h = h_ref[...]
    half = wg_ref.shape[1] // 2
    part = None
    for c in range(2):
        cols = slice(c * half, (c + 1) * half)
        gate = jnp.dot(h, wg_ref[:, cols], preferred_element_type=F32)
        up = jnp.dot(h, wu_ref[:, cols], preferred_element_type=F32)
        act = (gate * _sigmoid(gate) * up).astype(BF16)
        down = jnp.dot(act, wo_ref[cols, :], preferred_element_type=F32)
        part = down if part is None else part + down
    acc_ref[...] += part

    @pl.when(j == pl.num_programs(1) - 1)
    def _():
        def body(rows):
            x2 = x_ref[rows, :] + _rms(acc_ref[rows, :], gpost_ref[...])
            e = jnp.dot(_rms(x2, gple_ref[...]).astype(BF16), wpg_ref[...], preferred_element_type=F32)
            pe = jnp.dot(p_ref[rows, :].astype(BF16), wpi_ref[...], preferred_element_type=F32)
            o_ref[rows, :] = x2 + pe * _sigmoid(e)
        _for_row_chunks(x_ref.shape[0], body, chunk=2 * ROW_CHUNK)


def _ffn_ple(x1, p2d, g_pre, w_in, w_out, g_post, g_ple, w_pg, w_pi):
    m, d = x1.shape
    tm = min(FFN_ROWS, m)
    tf = FFN_TILE
    nff = D_FF // tf
    return pl.pallas_call(
        _ffn_kernel,
        out_shape=jax.ShapeDtypeStruct((m, d), F32),
        grid=(m // tm, nff),
        in_specs=[
            pl.BlockSpec((tm, d), lambda i, j: (i, 0)),
            pl.BlockSpec((tm, PLE_DIM), lambda i, j: (i, 0)),
            _const_spec((1, d)),
            pl.BlockSpec((d, tf), lambda i, j: (0, j)),
            pl.BlockSpec((d, tf), lambda i, j: (0, nff + j)),
            pl.BlockSpec((tf, d), lambda i, j: (j, 0)),
            _const_spec((1, d)), _const_spec((1, d)),
            _const_spec((d, d)), _const_spec((PLE_DIM, d)),
        ],
        out_specs=pl.BlockSpec((tm, d), lambda i, j: (i, 0)),
        scratch_shapes=[pltpu.VMEM((tm, d), BF16), pltpu.VMEM((tm, d), F32)],
        compiler_params=_cparams(("parallel", "arbitrary"), VMEM_LARGE),
        name="ffn_ple",
    )(x1, p2d, g_pre, w_in, w_in, w_out, g_post, g_ple, w_pg, w_pi)


def _layer(x, p, lam_init, wts):
    b, seq, d = x.shape
    x2d = x.reshape(b * seq, d)
    z = _in_proj(x2d, wts["g_mix_pre"], wts["w_in"], wts["b_gate"])
    z3 = z.reshape(b, seq, IN_COLS)

    tf = min(FREQ_TILE if seq > LONG_SEQ else 2 * FREQ_TILE, seq)
    rmat = _dft_tables(seq, tf)
    ht, sums = _filt_mlp(seq, wts["filt_w1"], wts["filt_b1"], wts["filt_w2"], wts["filt_b2"],
                         wts["filt_freq"], wts["filt_w3"])
    u1, v2, kl = _filt_spec(ht, sums, rmat, wts["hyena_d"], tf)
    x0c, wt = _hy_prep(z3, wts["conv_w"], wts["conv_b"])
    yconv = _long_conv(wt, rmat, u1, v2, kl, tf)

    yatt = _attention(z3, wts["lam"], wts["g_subln"], lam_init)

    x1 = _merge(yconv.reshape(b * seq, HY_CH), x0c.reshape(b * seq, HY_CH), yatt.reshape(b * seq, ATT_WIDTH),
                z, x2d, wts["w_hy_out"], wts["w_att_out"], wts["w_out"], wts["g_mix_post"])
    y = _ffn_ple(x1, p.reshape(b * seq, PLE_DIM), wts["g_ffn_pre"], wts["w_ffn_in"], wts["w_ffn_out"],
                 wts["g_ffn_post"], wts["g_ple"], wts["w_ple_gate"], wts["w_ple_in"])
    return y.reshape(b, seq, d)


def kernel(x_prompt, x_sample, p_prompt, p_sample, g_mix_pre, g_mix_post, g_ffn_pre, g_ffn_post, g_ple, w_in, b_gate, conv_w, conv_b, filt_w1, filt_b1, filt_w2, filt_b2, filt_freq, filt_w3, hyena_d, lam_q1, lam_k1, lam_q2, lam_k2, g_subln, w_hy_out, w_att_out, w_out, w_ffn_in, w_ffn_out, w_ple_in, w_ple_gate):
    depth = w_in.shape[0]
    xs = [x_prompt, x_sample]
    ps = [p_prompt, p_sample]
    for i in range(depth):
        lam_init = 0.8 - 0.6 * math.exp(-0.3 * i)
        wts = dict(
            g_mix_pre=g_mix_pre[i][None, :], g_mix_post=g_mix_post[i][None, :],
            g_ffn_pre=g_ffn_pre[i][None, :], g_ffn_post=g_ffn_post[i][None, :], g_ple=g_ple[i][None, :],
            w_in=w_in[i].astype(BF16), b_gate=b_gate[i].reshape(1, 2 * D_MODEL),
            conv_w=conv_w[i], conv_b=conv_b[i][None, :],
            filt_w1=filt_w1[i], filt_b1=filt_b1[i], filt_w2=filt_w2[i], filt_b2=filt_b2[i],
            filt_freq=filt_freq[i], filt_w3=filt_w3[i],
            hyena_d=jnp.broadcast_to(hyena_d[i][:, None], (HY_CH, LANES)),
            lam=jnp.stack([lam_q1[i], lam_k1[i], lam_q2[i], lam_k2[i]]),
            g_subln=g_subln[i][None, :],
            w_hy_out=w_hy_out[i].astype(BF16), w_att_out=w_att_out[i].astype(BF16),
            w_out=w_out[i].astype(BF16), w_ffn_in=w_ffn_in[i].astype(BF16),
            w_ffn_out=w_ffn_out[i].astype(BF16), w_ple_in=w_ple_in[i].astype(BF16),
            w_ple_gate=w_ple_gate[i].astype(BF16),
        )
        xs = [_layer(x, p[i], lam_init, wts) for x, p in zip(xs, ps)]
    return (xs[0], xs[1])
```
